```python
import jax, jax.numpy as jnp
from jax import lax
import numpy as np

D_MODEL = 1024
BATCH = 4
SEQ = 4096
DEPTH = 1

CTX_LEN = 256
GRID_W = 64
N_MOD = 6
EPS = 1e-6

GLA_HEADS = 4
GLA_DK = 64
GLA_DV = 128
GLA_LR = 16
GLA_TAU = 16.0
GLA_CHUNK = 64

ML_HEADS = 4
ML_DH = 128
ML_CHUNK = 64
CONV_K = 3

N_GROUPS = 4
EXP_PER_GROUP = 8
N_EXPERTS = N_GROUPS * EXP_PER_GROUP
TOP_K = 2
D_EXPERT = 512
MOE_BLOCK = 128

GLA_QK_W = GLA_HEADS * GLA_DK
GLA_V_W = GLA_HEADS * GLA_DV
ML_W = ML_HEADS * ML_DH
MIX_W = GLA_V_W + ML_W

IN_SPLITS = (GLA_QK_W, GLA_QK_W, GLA_V_W, GLA_V_W, 2 * GLA_LR, 2 * ML_W, ML_W, ML_W, 2 * ML_HEADS, 2 * ML_HEADS)
IN_W = sum(IN_SPLITS)

kernel_name = "hybrid_gla_mlstm_hmoe_ctxprefix_layer"


def rmsnorm(x, g):
    xf = x.astype(jnp.float32)
    y = xf * lax.rsqrt(jnp.mean(xf * xf, axis=-1, keepdims=True) + EPS)
    return (y * g.astype(jnp.float32)).astype(x.dtype)


def modulate(h, shift, scale):
    return h * (1 + scale) + shift


def to_heads(t, n_heads):
    b, l, w = t.shape
    return t.reshape(b, l, n_heads, w // n_heads).transpose(0, 2, 1, 3)


def from_heads(t):
    b, h, l, d = t.shape
    return t.transpose(0, 2, 1, 3).reshape(b, l, h * d)


def head_rmsnorm(t, g):
    y = t * lax.rsqrt(jnp.mean(t * t, axis=-1, keepdims=True) + EPS)
    return from_heads(y) * g.astype(jnp.float32)


def to_chunks(t, c):
    b, h, l = t.shape[:3]
    return jnp.moveaxis(t.reshape((b, h, l // c, c) + t.shape[3:]), 2, 0)


def from_chunks(t):
    n, b, h, c = t.shape[:4]
    return jnp.moveaxis(t, 0, 2).reshape((b, h, n * c) + t.shape[4:])


def flip_time(t):
    return jnp.flip(t, axis=2)


def gla_chunked(q, k, v, log_a, s0):
    mask = jnp.tril(jnp.ones((GLA_CHUNK, GLA_CHUNK), dtype=bool))

    def step(s, inp):
        qc, kc, vc, gc = inp
        b = jnp.cumsum(gc, axis=2)
        q_t = qc * jnp.exp(b)
        k_t = kc * jnp.exp(-b)
        att = jnp.where(mask, jnp.einsum('bhtd,bhsd->bhts', q_t, k_t), 0.0)
        o = jnp.einsum('bhts,bhsv->bhtv', att, vc) + jnp.einsum('bhtd,bhdv->bhtv', q_t, s)
        b_end = b[:, :, -1]
        s_new = jnp.exp(b_end)[..., None] * s + jnp.einsum(
            'bhsd,bhsv->bhdv', kc * jnp.exp(b_end[:, :, None] - b), vc)
        return s_new, o

    s_fin, o = lax.scan(step, s0, tuple(to_chunks(t, GLA_CHUNK) for t in (q, k, v, log_a)))
    return from_chunks(o), s_fin


def mlstm_chunked(q, k, v, ig, lf, state0):
    mask = jnp.tril(jnp.ones((ML_CHUNK, ML_CHUNK), dtype=bool))

    def step(state, inp):
        s, n, m = state
        qc, kc, vc, ic, fc = inp
        a = jnp.cumsum(fc, axis=-1)
        dmat = jnp.where(mask, a[..., :, None] - a[..., None, :] + ic[..., None, :], -jnp.inf)
        inter = a + m[..., None]
        m_t = jnp.maximum(inter, jnp.max(dmat, axis=-1))
        w_inter = jnp.exp(inter - m_t)
        qk = jnp.einsum('bhtd,bhsd->bhts', qc, kc) * jnp.exp(dmat - m_t[..., None])
        num = jnp.einsum('bhts,bhsv->bhtv', qk, vc) + w_inter[..., None] * jnp.einsum('bhtd,bhdv->bhtv', qc, s)
        den = jnp.sum(qk, axis=-1) + w_inter * jnp.einsum('bhtd,bhd->bht', qc, n)
        h = num / jnp.maximum(jnp.abs(den), jnp.exp(-m_t))[..., None]
        a_end = a[..., -1]
        g = a_end[..., None] - a + ic
        m_new = jnp.maximum(a_end + m, jnp.max(g, axis=-1))
        decay = jnp.exp(a_end + m - m_new)
        kw = kc * jnp.exp(g - m_new[..., None])[..., None]
        s_new = decay[..., None, None] * s + jnp.einsum('bhsd,bhsv->bhdv', kw, vc)
        n_new = decay[..., None] * n + jnp.sum(kw, axis=2)
        return (s_new, n_new, m_new), h

    state_fin, h = lax.scan(step, state0, tuple(to_chunks(t, ML_CHUNK) for t in (q, k, v, ig, lf)))
    return from_chunks(h), state_fin


def bidir(scan_fn, ctx_dirs, lat_dirs, init):
    outs_c, outs_l = [], []
    for direction in range(2):
        fl = flip_time if direction == 1 else (lambda t: t)
        oc, st = scan_fn(*[fl(t) for t in ctx_dirs[direction]], init)
        ol, _ = scan_fn(*[fl(t) for t in lat_dirs[direction]], st)
        outs_c.append(fl(oc))
        outs_l.append(fl(ol))
    return outs_c[0] + outs_c[1], outs_l[0] + outs_l[1]


def grid_conv(t, w, b, grid_w):
    bsz, l, ch = t.shape
    rows = l // grid_w
    img = t.reshape(bsz, rows, grid_w, ch)
    y = lax.conv_general_dilated(img, w.astype(t.dtype), (1, 1), 'SAME',
                                 dimension_numbers=('NHWC', 'HWIO', 'NHWC'), feature_group_count=ch)
    return y.reshape(bsz, l, ch) + b


def prepare_stream(z, grid_w, gla_up_w, gla_up_b, conv_w, conv_b, ml_i_b, ml_f_b):
    bsz, l, _ = z.shape
    zf = z.astype(jnp.float32)
    idx = np.cumsum(IN_SPLITS)[:-1].tolist()
    gq, gk, gv, gg, glr, mqk, mv, mo, mi, mf = jnp.split(zf, idx, axis=-1)
    gla_q = to_heads(gq, GLA_HEADS) * (GLA_DK ** -0.5)
    gla_k = to_heads(gk, GLA_HEADS)
    gla_v = to_heads(gv, GLA_HEADS)
    gate_logits = jnp.einsum('blrk,rkn->blrn', glr.reshape(bsz, l, 2, GLA_LR), gla_up_w) + gla_up_b
    log_a = jax.nn.log_sigmoid(gate_logits) / GLA_TAU
    gla_dirs = tuple((gla_q, gla_k, gla_v, to_heads(log_a[:, :, d], GLA_HEADS)) for d in range(2))
    qk = jax.nn.silu(grid_conv(mqk, conv_w, conv_b, grid_w))
    mq, mk = jnp.split(qk, 2, axis=-1)
    ml_q = to_heads(mq, ML_HEADS)
    ml_k = to_heads(mk, ML_HEADS) * (ML_DH ** -0.5)
    ml_v = to_heads(mv, ML_HEADS)
    ig = (mi.reshape(bsz, l, 2, ML_HEADS) + ml_i_b).transpose(2, 0, 3, 1)
    lf = jax.nn.log_sigmoid(mf.reshape(bsz, l, 2, ML_HEADS) + ml_f_b).transpose(2, 0, 3, 1)
    ml_dirs = tuple((ml_q, ml_k, ml_v, ig[d], lf[d]) for d in range(2))
    return gla_dirs, ml_dirs, gg, mo


def mixer_output(gla_h, ml_h, gg, mo, gla_norm_g, ml_norm_g, w_out, dtype):
    gla_o = head_rmsnorm(gla_h, gla_norm_g) * jax.nn.silu(gg)
    ml_o = jax.nn.sigmoid(mo) * head_rmsnorm(ml_h, ml_norm_g)
    return jnp.concatenate([gla_o, ml_o], axis=-1).astype(dtype) @ w_out


def hier_moe(h, rg_w, rg_b, re_w, re_b, e_w_in, e_w_out):
    bsz, l, d = h.shape
    t = bsz * l
    tok = h.reshape(t, d)
    g_logits = (tok @ rg_w + rg_b).astype(jnp.float32)
    p_group = jax.nn.softmax(g_logits, axis=-1)
    g_idx = jnp.argmax(g_logits, axis=-1)
    g_w = jnp.take_along_axis(p_group, g_idx[:, None], axis=-1)
    e_logits = (tok @ re_w + re_b).astype(jnp.float32).reshape(t, N_GROUPS, EXP_PER_GROUP)
    e_in_group = jnp.take_along_axis(e_logits, g_idx[:, None, None], axis=1)[:, 0]
    top_v, top_i = lax.top_k(e_in_group, TOP_K)
    weights = g_w * jax.nn.softmax(top_v, axis=-1)
    expert_ids = (g_idx[:, None] * EXP_PER_GROUP + top_i).reshape(-1)
    n_assign = t * TOP_K
    order = jnp.argsort(expert_ids)
    sorted_e = expert_ids[order]
    token_of = order // TOP_K
    counts = jnp.bincount(expert_ids, length=N_EXPERTS)
    padded = (counts + MOE_BLOCK - 1) // MOE_BLOCK * MOE_BLOCK
    pad_end = jnp.cumsum(padded)
    pad_start = pad_end - padded
    start = jnp.cumsum(counts) - counts
    dest = pad_start[sorted_e] + jnp.arange(n_assign) - start[sorted_e]
    n_blocks = (n_assign + MOE_BLOCK - 1) // MOE_BLOCK + N_EXPERTS
    buf = jnp.zeros((n_blocks * MOE_BLOCK, d), tok.dtype).at[dest].set(tok[token_of])
    block_e = jnp.minimum(jnp.searchsorted(pad_end, jnp.arange(n_blocks) * MOE_BLOCK, side='right'), N_EXPERTS - 1)

    def expert_block(args):
        xb, e = args
        gate, up = jnp.split(xb @ e_w_in[e], 2, axis=-1)
        return (jax.nn.silu(gate) * up) @ e_w_out[e]

    yb = lax.map(expert_block, (buf.reshape(n_blocks, MOE_BLOCK, d), block_e)).reshape(-1, d)
    y = yb[dest] * weights.reshape(-1)[order][:, None].astype(yb.dtype)
    out = jax.ops.segment_sum(y, token_of, num_segments=t)
    return out.reshape(bsz, l, d)


def hybrid_layer(x, ctx, c, c_ctx, ada_w, ada_b, norm1_g, w_in, gla_up_w, gla_up_b, gla_norm_g,
                 ml_conv_w, ml_conv_b, ml_i_b, ml_f_b, ml_norm_g, w_out, norm2_g,
                 rg_w, rg_b, re_w, re_b, e_w_in, e_w_out, update_ctx):
    bsz = x.shape[0]
    mod_x = (jax.nn.silu(c) @ ada_w + ada_b)[:, None, :]
    mod_c = (jax.nn.silu(c_ctx) @ ada_w + ada_b)[None, None, :]
    sh1x, sc1x, g1x, sh2x, sc2x, g2x = jnp.split(mod_x, N_MOD, axis=-1)
    sh1c, sc1c, g1c, sh2c, sc2c, g2c = jnp.split(mod_c, N_MOD, axis=-1)

    zx = modulate(rmsnorm(x, norm1_g), sh1x, sc1x) @ w_in
    zc = modulate(rmsnorm(ctx, norm1_g), sh1c, sc1c) @ w_in
    gla_c, ml_c, gg_c, mo_c = prepare_stream(zc, zc.shape[1], gla_up_w, gla_up_b, ml_conv_w, ml_conv_b, ml_i_b, ml_f_b)
    gla_x, ml_x, gg_x, mo_x = prepare_stream(zx, GRID_W, gla_up_w, gla_up_b, ml_conv_w, ml_conv_b, ml_i_b, ml_f_b)
    gla_init = jnp.zeros((bsz, GLA_HEADS, GLA_DK, GLA_DV), jnp.float32)
    ml_init = (jnp.zeros((bsz, ML_HEADS, ML_DH, ML_DH), jnp.float32),
               jnp.zeros((bsz, ML_HEADS, ML_DH), jnp.float32),
               jnp.zeros((bsz, ML_HEADS), jnp.float32))
    gla_hc, gla_hx = bidir(gla_chunked, gla_c, gla_x, gla_init)
    ml_hc, ml_hx = bidir(mlstm_chunked, ml_c, ml_x, ml_init)
    x = x + g1x * mixer_output(gla_hx, ml_hx, gg_x, mo_x, gla_norm_g, ml_norm_g, w_out, x.dtype)

    x = x + g2x * hier_moe(modulate(rmsnorm(x, norm2_g), sh2x, sc2x), rg_w, rg_b, re_w, re_b, e_w_in, e_w_out)

    if update_ctx:
        ctx = ctx + g1c * mixer_output(gla_hc, ml_hc, gg_c, mo_c, gla_norm_g, ml_norm_g, w_out, ctx.dtype)
        ctx = ctx + g2c * hier_moe(modulate(rmsnorm(ctx, norm2_g), sh2c, sc2c), rg_w, rg_b, re_w, re_b, e_w_in, e_w_out)
    return x, ctx


def setup_inputs(seed: int = 0) -> dict:
    key = jax.random.key(seed)
    ks = jax.random.split(key, 26)
    nrm = lambda k, shape, s: jax.random.normal(k, shape, jnp.float32) * s
    L = DEPTH
    return {
        "x": nrm(ks[0], (BATCH, SEQ, D_MODEL), 1.0),
        "c": nrm(ks[1], (BATCH, D_MODEL), 1.0),
        "ctx": nrm(ks[2], (BATCH, CTX_LEN, D_MODEL), 1.0),
        "c_ctx": nrm(ks[3], (D_MODEL,), 1.0),
        "ada_w": nrm(ks[4], (L, D_MODEL, N_MOD * D_MODEL), D_MODEL ** -0.5),
        "ada_b": nrm(ks[5], (L, N_MOD * D_MODEL), 0.02),
        "norm1_g": 1.0 + nrm(ks[6], (L, D_MODEL), 0.02),
        "w_in": nrm(ks[7], (L, D_MODEL, IN_W), D_MODEL ** -0.5),
        "gla_up_w": nrm(ks[8], (L, 2, GLA_LR, GLA_QK_W), GLA_LR ** -0.5),
        "gla_up_b": nrm(ks[9], (L, 2, GLA_QK_W), 0.1),
        "gla_norm_g": 1.0 + nrm(ks[10], (L, GLA_V_W), 0.02),
        "ml_conv_w": nrm(ks[11], (L, CONV_K, CONV_K, 1, 2 * ML_W), 1.0 / CONV_K),
        "ml_conv_b": nrm(ks[12], (L, 2 * ML_W), 0.02),
        "ml_i_b": nrm(ks[13], (L, 2, ML_HEADS), 0.1),
        "ml_f_b": 3.0 + nrm(ks[14], (L, 2, ML_HEADS), 0.5),
        "ml_norm_g": 1.0 + nrm(ks[15], (L, ML_W), 0.02),
        "w_out": nrm(ks[16], (L, MIX_W, D_MODEL), MIX_W ** -0.5),
        "norm2_g": 1.0 + nrm(ks[17], (L, D_MODEL), 0.02),
        "router_group_w": nrm(ks[18], (L, D_MODEL, N_GROUPS), D_MODEL ** -0.5),
        "router_group_b": nrm(ks[19], (L, N_GROUPS), 0.01),
        "router_expert_w": nrm(ks[20], (L, D_MODEL, N_EXPERTS), D_MODEL ** -0.5),
        "router_expert_b": nrm(ks[21], (L, N_EXPERTS), 0.01),
        "expert_w_in": nrm(ks[22], (L, N_EXPERTS, D_MODEL, 2 * D_EXPERT), D_MODEL ** -0.5),
        "expert_w_out": nrm(ks[23], (L, N_EXPERTS, D_EXPERT, D_MODEL), D_EXPERT ** -0.5),
        "final_norm_g": 1.0 + nrm(ks[24], (D_MODEL,), 0.02),
    }


def reference(x, c, ctx, c_ctx, ada_w, ada_b, norm1_g, w_in, gla_up_w, gla_up_b, gla_norm_g,
              ml_conv_w, ml_conv_b, ml_i_b, ml_f_b, ml_norm_g, w_out, norm2_g,
              router_group_w, router_group_b, router_expert_w, router_expert_b,
              expert_w_in, expert_w_out, final_norm_g):
    for layer in range(DEPTH):
        x, ctx = hybrid_layer(
            x, ctx, c, c_ctx, ada_w[layer], ada_b[layer], norm1_g[layer], w_in[layer],
            gla_up_w[layer], gla_up_b[layer], gla_norm_g[layer],
            ml_conv_w[layer], ml_conv_b[layer], ml_i_b[layer], ml_f_b[layer], ml_norm_g[layer],
            w_out[layer], norm2_g[layer],
            router_group_w[layer], router_group_b[layer], router_expert_w[layer], router_expert_b[layer],
            expert_w_in[layer], expert_w_out[layer], update_ctx=(layer < DEPTH - 1))
    return rmsnorm(x, final_norm_g)
```

```python
import functools

import jax
import jax.numpy as jnp
from jax import lax
from jax.experimental import pallas as pl
from jax.experimental.pallas import tpu as pltpu

F32 = jnp.float32
BF16 = jnp.bfloat16

D_MODEL = 1024
GRID_W = 64
N_MOD = 6
EPS = 1e-6

GLA_HEADS = 4
GLA_DK = 64
GLA_DV = 128
GLA_LR = 16
GLA_TAU = 16.0
GLA_C = 64

ML_HEADS = 4
ML_DH = 128
ML_C = 128

N_GROUPS = 4
EXP_PER_GROUP = 8
N_EXPERTS = N_GROUPS * EXP_PER_GROUP
D_EXPERT = 512
MOE_BLK = 256

GLA_QK_W = GLA_HEADS * GLA_DK
GLA_V_W = GLA_HEADS * GLA_DV
ML_W = ML_HEADS * ML_DH
LANES = 128
VMEM_LIMIT = 56 * 1024 * 1024

_LR0 = 0
_MI0 = 2 * GLA_LR
_MF0 = _MI0 + 2 * ML_HEADS


def _cparams(sem):
    return pltpu.CompilerParams(dimension_semantics=sem, vmem_limit_bytes=VMEM_LIMIT)


def _sigmoid(x):
    return 1.0 / (1.0 + jnp.exp(-x))


def _silu(x):
    return x * _sigmoid(x)


def _log_sigmoid(x):
    return jnp.minimum(x, 0.0) - jnp.log1p(jnp.exp(-jnp.abs(x)))


def _split_dot(a_bf16_exact, x, dims=None):
    x_hi = x.astype(BF16)
    x_lo = (x - x_hi.astype(F32)).astype(BF16)
    if dims is None:
        f = lambda u: jnp.dot(a_bf16_exact, u, preferred_element_type=F32)
    else:
        f = lambda u: lax.dot_general(u, a_bf16_exact, dims, preferred_element_type=F32)
    return f(x_hi) + f(x_lo)


def _mod_kernel(c_ref, w_ref, b_ref, o_ref):
    c = c_ref[...]
    s = _silu(c).astype(BF16)
    o_ref[...] = jnp.dot(s, w_ref[...].astype(BF16), preferred_element_type=F32) + b_ref[...]


def _modulation(cc, ada_w, ada_b):
    rows, d = cc.shape
    n = ada_w.shape[1]
    tn = 1536
    return pl.pallas_call(
        _mod_kernel,
        grid=(n // tn,),
        in_specs=[pl.BlockSpec((rows, d), lambda j: (0, 0)),
                  pl.BlockSpec((d, tn), lambda j: (0, j)),
                  pl.BlockSpec((1, tn), lambda j: (0, j))],
        out_specs=pl.BlockSpec((rows, tn), lambda j: (0, j)),
        out_shape=jax.ShapeDtypeStruct((rows, n), F32),
        compiler_params=_cparams(("arbitrary",)),
        name="mod",
    )(cc, ada_w, ada_b.reshape(1, n))


def _inproj_kernel(x_ref, mod_ref, g_ref, wg_ref, wm_ref, ws_ref, wst_ref, bcol_ref, brow_ref,
                   zg_ref, zm_ref, zs_ref, gcol_ref, grow_ref):
    tm = x_ref.shape[1]
    x = x_ref[0]
    y = x * lax.rsqrt(jnp.mean(x * x, axis=-1, keepdims=True) + EPS) * g_ref[...]
    h = (y * (1.0 + mod_ref[0, 1:2, :]) + mod_ref[0, 0:1, :]).astype(BF16)
    zg_ref[0] = jnp.dot(h, wg_ref[...], preferred_element_type=F32)
    zm_ref[0] = jnp.dot(h, wm_ref[...], preferred_element_type=F32)
    zs = jnp.dot(h, ws_ref[...], preferred_element_type=F32) + bcol_ref[...]
    zst = lax.dot_general(wst_ref[...], h, (((1,), (1,)), ((), ())),
                          preferred_element_type=F32) + brow_ref[...]
    zs_ref[0] = zs

    r = lax.broadcasted_iota(jnp.int32, (tm, tm), 0)
    c = lax.broadcasted_iota(jnp.int32, (tm, tm), 1)
    shift = ML_C.bit_length() - 1
    same = jnp.right_shift(r, shift) == jnp.right_shift(c, shift)
    lower = jnp.where(same & (c <= r), 1.0, 0.0).astype(BF16)
    upper = jnp.where(same & (c >= r), 1.0, 0.0).astype(BF16)
    lsf = _log_sigmoid(zs)
    a_pre = _split_dot(lower, lsf)
    a_suf = _split_dot(upper, lsf)
    lsft = _log_sigmoid(zst)
    dims = (((1,), (0,)), ((), ()))
    a_pre_t = _split_dot(upper, lsft, dims)
    a_suf_t = _split_dot(lower, lsft, dims)

    lane = lax.broadcasted_iota(jnp.int32, (tm, LANES), 1)
    for hd in range(ML_HEADS):
        cols = (a_pre[:, _MF0 + hd:_MF0 + hd + 1],
                a_suf[:, _MF0 + ML_HEADS + hd:_MF0 + ML_HEADS + hd + 1],
                zs[:, _MI0 + hd:_MI0 + hd + 1],
                zs[:, _MI0 + ML_HEADS + hd:_MI0 + ML_HEADS + hd + 1])
        slab = jnp.zeros((tm, LANES), F32)
        for j, col in enumerate(cols):
            slab = jnp.where(lane == j, col, slab)
        gcol_ref[0, :, hd * LANES:(hd + 1) * LANES] = slab
        rows = (a_pre_t[_MF0 + hd:_MF0 + hd + 1, :],
                a_suf_t[_MF0 + ML_HEADS + hd:_MF0 + ML_HEADS + hd + 1, :],
                zst[_MI0 + hd:_MI0 + hd + 1, :],
                zst[_MI0 + ML_HEADS + hd:_MI0 + ML_HEADS + hd + 1, :])
        for j, row in enumerate(rows):
            grow_ref[0, hd, j:j + 1, :] = row
        grow_ref[0, hd, 4:8, :] = jnp.zeros((4, tm), F32)


def _inproj(x, mods, mod_row_of_batch, norm_g, wg, wm, ws, wst, bcol, brow, tm):
    bsz, l, d = x.shape
    assert l % tm == 0 and tm % ML_C == 0
    const = lambda shape: pl.BlockSpec(shape, lambda b, i: (0,) * len(shape))
    return pl.pallas_call(
        _inproj_kernel,
        grid=(bsz, l // tm),
        in_specs=[pl.BlockSpec((1, tm, d), lambda b, i: (b, i, 0)),
                  pl.BlockSpec((1, N_MOD, d), lambda b, i: (mod_row_of_batch(b), 0, 0)),
                  const((1, d)), const(wg.shape), const(wm.shape), const(ws.shape), const(wst.shape),
                  const((1, LANES)), const((LANES, 1))],
        out_specs=[pl.BlockSpec((1, tm, wg.shape[1]), lambda b, i: (b, i, 0)),
                   pl.BlockSpec((1, tm, wm.shape[1]), lambda b, i: (b, i, 0)),
                   pl.BlockSpec((1, tm, LANES), lambda b, i: (b, i, 0)),
                   pl.BlockSpec((1, tm, ML_HEADS * LANES), lambda b, i: (b, i, 0)),
                   pl.BlockSpec((1, ML_HEADS, 8, tm), lambda b, i: (b, 0, 0, i))],
        out_shape=[jax.ShapeDtypeStruct((bsz, l, wg.shape[1]), F32),
                   jax.ShapeDtypeStruct((bsz, l, wm.shape[1]), F32),
                   jax.ShapeDtypeStruct((bsz, l, LANES), F32),
                   jax.ShapeDtypeStruct((bsz, l, ML_HEADS * LANES), F32),
                   jax.ShapeDtypeStruct((bsz, ML_HEADS, 8, l), F32)],
        compiler_params=_cparams(("arbitrary", "arbitrary")),
        name="inproj",
    )(x, mods, norm_g, wg, wm, ws, wst, bcol, brow)


def _gla_chunk(q, k, v, zs, wup, bup, s_ref, direction, want_out):
    c = k.shape[0]
    logits = jnp.dot(zs.astype(BF16), wup, preferred_element_type=F32) + bup
    g = _log_sigmoid(logits) * (1.0 / GLA_TAU)
    r = lax.broadcasted_iota(jnp.int32, (c, c), 0)
    cc = lax.broadcasted_iota(jnp.int32, (c, c), 1)
    causal = (cc <= r) if direction == 0 else (cc >= r)
    b = _split_dot(jnp.where(causal, 1.0, 0.0).astype(BF16), g)
    b_end = b[c - 1:c, :] if direction == 0 else b[0:1, :]
    s = s_ref[...]
    out = None
    if want_out:
        b_mid = b[c // 2:c // 2 + 1, :]
        q_in = (q * jnp.exp(b - b_mid)).astype(BF16)
        k_in = (k * jnp.exp(b_mid - b)).astype(BF16)
        att = lax.dot_general(q_in, k_in, (((1,), (1,)), ((), ())), preferred_element_type=F32)
        att = jnp.where(causal, att, 0.0)
        out = jnp.dot(att.astype(BF16), v.astype(BF16), preferred_element_type=F32)
        out = out + lax.dot_general((q * jnp.exp(b)).astype(BF16), s.astype(BF16),
                                    (((1,), (1,)), ((), ())), preferred_element_type=F32)
    kd = (k * jnp.exp(b_end - b)).astype(BF16)
    upd = lax.dot_general(v.astype(BF16), kd, (((0,), (0,)), ((), ())), preferred_element_type=F32)
    s_ref[...] = jnp.exp(b_end) * s + upd
    return out


def _gla_kernel(q_ref, k_ref, v_ref, gg_ref, zs_ref, kc_ref, vc_ref, zsc_ref,
                wup_ref, bup_ref, ng_ref, o_ref, s_ref, acc_ref):
    seq = q_ref.shape[1]
    ctx = kc_ref.shape[1]
    n = seq // GLA_C
    nc = ctx // GLA_C
    s_ref[...] = jnp.zeros_like(s_ref)

    def rows(i):
        return pl.ds(pl.multiple_of(i * GLA_C, GLA_C), GLA_C)

    def ctx_step(j, carry):
        for d in range(2):
            i = j if d == 0 else nc - 1 - j
            _gla_chunk(None, kc_ref[0, rows(i), :], vc_ref[0, rows(i), :],
                       zsc_ref[0, rows(i), :], wup_ref[0, d], bup_ref[0, d], s_ref.at[d], d, False)
        return carry

    lax.fori_loop(0, nc, ctx_step, 0)

    def finish(i, total):
        y = total * lax.rsqrt(jnp.mean(total * total, axis=-1, keepdims=True) + EPS) * ng_ref[...]
        o_ref[0, rows(i), :] = (y * _silu(gg_ref[0, rows(i), :])).astype(o_ref.dtype)

    def lat_step(j, carry, second):
        for d in range(2):
            i = j if d == 0 else n - 1 - j
            out = _gla_chunk(q_ref[0, rows(i), :] * (GLA_DK ** -0.5), k_ref[0, rows(i), :],
                             v_ref[0, rows(i), :], zs_ref[0, rows(i), :], wup_ref[0, d], bup_ref[0, d],
                             s_ref.at[d], d, True)
            if second:
                finish(i, acc_ref[rows(i), :] + out)
            else:
                acc_ref[rows(i), :] = out
        return carry

    lax.fori_loop(0, n // 2, functools.partial(lat_step, second=False), 0)
    lax.fori_loop(n // 2, n, functools.partial(lat_step, second=True), 0)


def _gla(zg_x, zs_x, zg_c, zs_c, wup, bup, norm_g):
    bsz, seq, _ = zg_x.shape
    ctx = zg_c.shape[1]
    assert seq % (2 * GLA_C) == 0 and ctx % GLA_C == 0
    h = GLA_HEADS

    def col(l, off):
        return pl.BlockSpec((1, l, LANES), lambda b, hd: (b, 0, off + hd))

    return pl.pallas_call(
        _gla_kernel,
        grid=(bsz, h),
        in_specs=[col(seq, 0), col(seq, h), col(seq, 2 * h), col(seq, 3 * h),
                  pl.BlockSpec((1, seq, LANES), lambda b, hd: (b, 0, 0)),
                  col(ctx, h), col(ctx, 2 * h),
                  pl.BlockSpec((1, ctx, LANES), lambda b, hd: (b, 0, 0)),
                  pl.BlockSpec((1, 2, LANES, LANES), lambda b, hd: (hd, 0, 0, 0)),
                  pl.BlockSpec((1, 2, 1, LANES), lambda b, hd: (hd, 0, 0, 0)),
                  pl.BlockSpec((1, LANES), lambda b, hd: (0, hd))],
        out_specs=pl.BlockSpec((1, seq, LANES), lambda b, hd: (b, 0, hd)),
        out_shape=jax.ShapeDtypeStruct((bsz, seq, h * GLA_DV), BF16),
        scratch_shapes=[pltpu.VMEM((2, LANES, LANES), F32), pltpu.VMEM((seq, LANES), F32)],
        compiler_params=_cparams(("arbitrary", "arbitrary")),
        name="gla",
    )(zg_x, zg_x, zg_x, zg_x, zs_x, zg_c, zg_c, zs_c, wup, bup, norm_g)


def _grid_conv_silu(src_ref, dst_ref, w_ref, b_ref, grid_w, scale):
    l = src_ref.shape[1]
    n_rows = l // grid_w
    col = lax.broadcasted_iota(jnp.int32, (grid_w, LANES), 0)

    def body(r, carry):
        acc = jnp.zeros((grid_w, LANES), F32) + b_ref[...]
        for dy in (-1, 0, 1):
            if n_rows == 1 and dy != 0:
                continue
            rr = jnp.clip(r + dy, 0, n_rows - 1)
            blk = src_ref[0, pl.ds(pl.multiple_of(rr * grid_w, grid_w), grid_w), :]
            valid = jnp.logical_and(r + dy >= 0, r + dy < n_rows)
            blk = jnp.where(valid, blk, 0.0)
            for dx in (-1, 0, 1):
                if dx == 0:
                    sh = blk
                else:
                    sh = pltpu.roll(blk, shift=(-dx) % grid_w, axis=0)
                    sh = jnp.where((col + dx >= 0) & (col + dx < grid_w), sh, 0.0)
                tap = (dy + 1) * 3 + (dx + 1)
                acc = acc + sh * w_ref[tap:tap + 1, :]
        dst_ref[pl.ds(pl.multiple_of(r * grid_w, grid_w), grid_w), :] = _silu(acc) * scale
        return carry

    lax.fori_loop(0, n_rows, body, 0)


def _ml_gates(gcol, grow, direction):
    c = gcol.shape[0]
    a_col = gcol[:, direction:direction + 1]
    i_col = gcol[:, 2 + direction:3 + direction]
    a_row = grow[direction:direction + 1, :]
    i_row = grow[2 + direction:3 + direction, :]
    a_end = a_row[:, c - 1:c] if direction == 0 else a_row[:, 0:1]
    return a_col, i_col, a_row, i_row, a_end


def _ml_chunk(q, k, v, gcol, grow, s_ref, m_ref, direction, want_out):
    c = k.shape[0]
    a_col, i_col, a_row, i_row, a_end = _ml_gates(gcol, grow, direction)
    lane = lax.broadcasted_iota(jnp.int32, (c, LANES), 1)
    v_aug = jnp.concatenate([v, jnp.where(lane == 0, 1.0, 0.0)], axis=1).astype(BF16)
    s = s_ref[...]
    m = m_ref[:, 0:1]
    out = None
    if want_out:
        r = lax.broadcasted_iota(jnp.int32, (c, c), 0)
        cc = lax.broadcasted_iota(jnp.int32, (c, c), 1)
        causal = (cc <= r) if direction == 0 else (cc >= r)
        dmat = jnp.where(causal, a_col - a_row + i_row, -jnp.inf)
        inter = a_col + m
        m_t = jnp.maximum(inter, jnp.max(dmat, axis=-1, keepdims=True))
        w_inter = jnp.exp(inter - m_t)
        qb = q.astype(BF16)
        qk = lax.dot_general(qb, k.astype(BF16), (((1,), (1,)), ((), ())), preferred_element_type=F32)
        p = (qk * jnp.exp(dmat - m_t)).astype(BF16)
        both = (jnp.dot(p, v_aug, preferred_element_type=F32)
                + w_inter * jnp.dot(qb, s.astype(BF16), preferred_element_type=F32))
        num = both[:, :ML_DH]
        den = both[:, ML_DH:ML_DH + 1]
        out = num / jnp.maximum(jnp.abs(den), jnp.exp(-m_t))
    g = a_end - a_col + i_col
    m_new = jnp.maximum(a_end + m, jnp.max(g, axis=0, keepdims=True))
    decay = jnp.exp(a_end + m - m_new)
    kw = (k * jnp.exp(g - m_new)).astype(BF16)
    upd = lax.dot_general(kw, v_aug, (((0,), (0,)), ((), ())), preferred_element_type=F32)
    s_ref[...] = decay * s + upd
    m_ref[...] = jnp.broadcast_to(m_new, m_ref.shape)
    return out


def _mlstm_kernel(q_ref, k_ref, v_ref, mo_ref, gcol_ref, grow_ref,
                  kc_ref, vc_ref, gcolc_ref, growc_ref,
                  wq_ref, wk_ref, bq_ref, bk_ref, ng_ref, o_ref,
                  cq_ref, ck_ref, ckc_ref, s_ref, m_ref, acc_ref):
    seq = q_ref.shape[1]
    ctx = kc_ref.shape[1]
    n = seq // ML_C
    nc = ctx // ML_C
    _grid_conv_silu(q_ref, cq_ref, wq_ref, bq_ref, GRID_W, 1.0)
    _grid_conv_silu(k_ref, ck_ref, wk_ref, bk_ref, GRID_W, ML_DH ** -0.5)
    _grid_conv_silu(kc_ref, ckc_ref, wk_ref, bk_ref, ctx, ML_DH ** -0.5)
    s_ref[...] = jnp.zeros_like(s_ref)
    m_ref[...] = jnp.zeros_like(m_ref)

    def rows(i):
        return pl.ds(pl.multiple_of(i * ML_C, ML_C), ML_C)

    def ctx_step(j, carry):
        for d in range(2):
            i = j if d == 0 else nc - 1 - j
            _ml_chunk(None, ckc_ref[rows(i), :], vc_ref[0, rows(i), :], gcolc_ref[0, rows(i), :],
                      growc_ref[0, 0, :, rows(i)], s_ref.at[d], m_ref.at[d], d, False)
        return carry

    lax.fori_loop(0, nc, ctx_step, 0)

    def finish(i, total):
        y = total * lax.rsqrt(jnp.mean(total * total, axis=-1, keepdims=True) + EPS) * ng_ref[...]
        o_ref[0, rows(i), :] = (_sigmoid(mo_ref[0, rows(i), :]) * y).astype(o_ref.dtype)

    def lat_step(j, carry, second):
        for d in range(2):
            i = j if d == 0 else n - 1 - j
            out = _ml_chunk(cq_ref[rows(i), :], ck_ref[rows(i), :], v_ref[0, rows(i), :],
                            gcol_ref[0, rows(i), :], grow_ref[0, 0, :, rows(i)],
                            s_ref.at[d], m_ref.at[d], d, True)
            if second:
                finish(i, acc_ref[rows(i), :] + out)
            else:
                acc_ref[rows(i), :] = out
        return carry

    lax.fori_loop(0, n // 2, functools.partial(lat_step, second=False), 0)
    lax.fori_loop(n // 2, n, functools.partial(lat_step, second=True), 0)


def _mlstm(zm_x, gcol_x, grow_x, zm_c, gcol_c, grow_c, conv_w, conv_b, norm_g):
    bsz, seq, _ = zm_x.shape
    ctx = zm_c.shape[1]
    assert seq % (2 * ML_C) == 0 and ctx % ML_C == 0 and seq % GRID_W == 0
    h = ML_HEADS

    def col(l, off):
        return pl.BlockSpec((1, l, LANES), lambda b, hd: (b, 0, off + hd))

    def gates(l):
        return [pl.BlockSpec((1, l, LANES), lambda b, hd: (b, 0, hd)),
                pl.BlockSpec((1, 1, 8, l), lambda b, hd: (b, hd, 0, 0))]

    return pl.pallas_call(
        _mlstm_kernel,
        grid=(bsz, h),
        in_specs=[col(seq, 0), col(seq, h), col(seq, 2 * h), col(seq, 3 * h)] + gates(seq)
                 + [col(ctx, h), col(ctx, 2 * h)] + gates(ctx)
                 + [pl.BlockSpec((9, LANES), lambda b, hd: (0, hd)),
                    pl.BlockSpec((9, LANES), lambda b, hd: (0, h + hd)),
                    pl.BlockSpec((1, LANES), lambda b, hd: (0, hd)),
                    pl.BlockSpec((1, LANES), lambda b, hd: (0, h + hd)),
                    pl.BlockSpec((1, LANES), lambda b, hd: (0, hd))],
        out_specs=pl.BlockSpec((1, seq, LANES), lambda b, hd: (b, 0, hd)),
        out_shape=jax.ShapeDtypeStruct((bsz, seq, h * ML_DH), BF16),
        scratch_shapes=[pltpu.VMEM((seq, LANES), F32), pltpu.VMEM((seq, LANES), F32),
                        pltpu.VMEM((ctx, LANES), F32),
                        pltpu.VMEM((2, LANES, 2 * LANES), F32), pltpu.VMEM((2, 1, LANES), F32),
                        pltpu.VMEM((seq, LANES), F32)],
        compiler_params=_cparams(("arbitrary", "arbitrary")),
        name="mlstm",
    )(zm_x, zm_x, zm_x, zm_x, gcol_x, grow_x, zm_c, zm_c, gcol_c, grow_c,
      conv_w, conv_w, conv_b, conv_b, norm_g)


_G0 = 0
_E0 = N_GROUPS


def _outproj_kernel(x_ref, ga_ref, ml_ref, mod_ref, wa_ref, wb_ref, g2_ref, wrh_ref, wrl_ref, br_ref,
                    x1_ref, h2_ref, ri_ref, rw_ref, cnt_ref, base_ref):
    tm = x_ref.shape[1]

    @pl.when((pl.program_id(0) == 0) & (pl.program_id(1) == 0))
    def _():
        base_ref[...] = jnp.zeros_like(base_ref)

    mix = (jnp.dot(ga_ref[0], wa_ref[...], preferred_element_type=F32)
           + jnp.dot(ml_ref[0], wb_ref[...], preferred_element_type=F32))
    x1 = x_ref[0] + mod_ref[0, 2:3, :] * mix
    x1_ref[0] = x1
    y = x1 * lax.rsqrt(jnp.mean(x1 * x1, axis=-1, keepdims=True) + EPS) * g2_ref[...]
    h2 = y * (1.0 + mod_ref[0, 4:5, :]) + mod_ref[0, 3:4, :]
    h2_ref[0] = h2

    h_hi = h2.astype(BF16)
    h_lo = (h2 - h_hi.astype(F32)).astype(BF16)
    logits = (jnp.dot(h_hi, wrh_ref[...], preferred_element_type=F32)
              + jnp.dot(h_lo, wrh_ref[...], preferred_element_type=F32)
              + jnp.dot(h_hi, wrl_ref[...], preferred_element_type=F32)) + br_ref[...]

    lane = lax.broadcasted_iota(jnp.int32, (tm, LANES), 1).astype(F32)
    neg = -jnp.inf
    big = float(LANES)
    is_g = lane < float(_E0)
    lg = jnp.where(is_g, logits, neg)
    gmax = jnp.max(lg, axis=-1, keepdims=True)
    gidx = jnp.min(jnp.where(lg == gmax, lane, big), axis=-1, keepdims=True)
    gw = 1.0 / jnp.sum(jnp.where(is_g, jnp.exp(logits - gmax), 0.0), axis=-1, keepdims=True)
    lo = float(_E0) + float(EXP_PER_GROUP) * gidx
    le = jnp.where((lane >= lo) & (lane < lo + float(EXP_PER_GROUP)), logits, neg)
    v1 = jnp.max(le, axis=-1, keepdims=True)
    i1 = jnp.min(jnp.where(le == v1, lane, big), axis=-1, keepdims=True)
    le2 = jnp.where(lane == i1, neg, le)
    v2 = jnp.max(le2, axis=-1, keepdims=True)
    i2 = jnp.min(jnp.where(le2 == v2, lane, big), axis=-1, keepdims=True)
    t = jnp.exp(v2 - v1)
    w1 = gw / (1.0 + t)
    w2 = gw * t / (1.0 + t)
    e1 = i1 - float(_E0)
    e2 = i2 - float(_E0)

    oh1 = lane == e1
    oh2 = lane == e2
    oh = jnp.where(oh1 | oh2, 1.0, 0.0)
    r = lax.broadcasted_iota(jnp.int32, (tm, tm), 0)
    c = lax.broadcasted_iota(jnp.int32, (tm, tm), 1)
    strict = jnp.where(c < r, 1.0, 0.0).astype(BF16)
    before = jnp.dot(strict, oh.astype(BF16), preferred_element_type=F32) + base_ref[...]
    rank1 = jnp.sum(jnp.where(oh1, before, 0.0), axis=-1, keepdims=True)
    rank2 = jnp.sum(jnp.where(oh2, before, 0.0), axis=-1, keepdims=True)
    total = base_ref[...] + jnp.sum(oh, axis=0, keepdims=True)
    base_ref[...] = total
    cnt_ref[...] = total

    ids = jnp.where(lane == 0.0, e1, jnp.where(lane == 1.0, e2,
                    jnp.where(lane == 2.0, rank1, jnp.where(lane == 3.0, rank2, 0.0))))
    ri_ref[0] = ids.astype(jnp.int32)
    rw_ref[0] = jnp.where(lane == 0.0, w1, jnp.where(lane == 1.0, w2, 0.0))


def _outproj(x, gla_o, ml_o, mods, wa, wb, g2, wrh, wrl, br, tm):
    bsz, seq, d = x.shape
    const = lambda shape: pl.BlockSpec(shape, lambda b, i: (0,) * len(shape))
    tile = lambda w: pl.BlockSpec((1, tm, w), lambda b, i: (b, i, 0))
    return pl.pallas_call(
        _outproj_kernel,
        grid=(bsz, seq // tm),
        in_specs=[tile(d), tile(gla_o.shape[2]), tile(ml_o.shape[2]),
                  pl.BlockSpec((1, N_MOD, d), lambda b, i: (b, 0, 0)),
                  const(wa.shape), const(wb.shape), const((1, d)),
                  const(wrh.shape), const(wrl.shape), const((1, LANES))],
        out_specs=[tile(d), tile(d), tile(LANES), tile(LANES), const((1, LANES))],
        out_shape=[jax.ShapeDtypeStruct((bsz, seq, d), F32),
                   jax.ShapeDtypeStruct((bsz, seq, d), F32),
                   jax.ShapeDtypeStruct((bsz, seq, LANES), jnp.int32),
                   jax.ShapeDtypeStruct((bsz, seq, LANES), F32),
                   jax.ShapeDtypeStruct((1, LANES), F32)],
        scratch_shapes=[pltpu.VMEM((1, LANES), F32)],
        compiler_params=_cparams(("arbitrary", "arbitrary")),
        name="outproj",
    )(x, gla_o, ml_o, mods, wa, wb, g2, wrh, wrl, br)


DISPATCH_BATCH = 256


def _dispatch_kernel(dest_ref, h_hbm, buf_hbm, sem):
    n_tok = h_hbm.shape[0]
    nb = n_tok // DISPATCH_BATCH

    def row_copy(tok, dst):
        return pltpu.make_async_copy(h_hbm.at[pl.ds(tok, 1), :], buf_hbm.at[pl.ds(dst, 1), :], sem)

    def wait_batch():
        def w(t, carry):
            row_copy(0, 0).wait()
            row_copy(0, 0).wait()
            return carry
        lax.fori_loop(0, DISPATCH_BATCH, w, 0)

    def batch(j, carry):
        def issue(t, c):
            tok = j * DISPATCH_BATCH + t
            row_copy(tok, dest_ref[2 * tok]).start()
            row_copy(tok, dest_ref[2 * tok + 1]).start()
            return c
        lax.fori_loop(0, DISPATCH_BATCH, issue, 0)

        @pl.when(j > 0)
        def _():
            wait_batch()
        return carry

    lax.fori_loop(0, nb, batch, 0)
    wait_batch()


def _dispatch(dest, h2, n_rows):
    n_tok, d = h2.shape
    assert n_tok % DISPATCH_BATCH == 0
    return pl.pallas_call(
        _dispatch_kernel,
        grid_spec=pltpu.PrefetchScalarGridSpec(
            num_scalar_prefetch=1, grid=(1,),
            in_specs=[pl.BlockSpec(memory_space=pl.ANY)],
            out_specs=pl.BlockSpec(memory_space=pl.ANY),
            scratch_shapes=[pltpu.SemaphoreType.DMA(())]),
        out_shape=jax.ShapeDtypeStruct((n_rows, d), h2.dtype),
        compiler_params=_cparams(("arbitrary",)),
        name="dispatch",
    )(dest, h2)


def _experts_kernel(be_ref, nv_ref, bi_ref, x_ref, w1_ref, w2_ref, y_ref, w1c_ref, w2c_ref):
    i = pl.program_id(0)
    prev = be_ref[jnp.maximum(i - 1, 0)]

    @pl.when((i == 0) | (be_ref[i] != prev))
    def _():
        w1c_ref[...] = w1_ref[0].astype(BF16)
        w2c_ref[...] = w2_ref[0].astype(BF16)

    nv = nv_ref[i]

    @pl.when(nv > 0)
    def _():
        row = lax.broadcasted_iota(jnp.int32, (x_ref.shape[0], 1), 0)
        x = jnp.where(row < nv, x_ref[...], 0.0).astype(BF16)
        h = jnp.dot(x, w1c_ref[...], preferred_element_type=F32)
        a = (_silu(h[:, :D_EXPERT]) * h[:, D_EXPERT:]).astype(BF16)
        y_ref[...] = jnp.dot(a, w2c_ref[...], preferred_element_type=F32)


def _experts(block_e, block_nv, block_idx, buf, w_in, w_out):
    n_rows, d = buf.shape
    nb = n_rows // MOE_BLK
    de2 = w_in.shape[2]
    return pl.pallas_call(
        _experts_kernel,
        grid_spec=pltpu.PrefetchScalarGridSpec(
            num_scalar_prefetch=3, grid=(nb,),
            in_specs=[pl.BlockSpec((MOE_BLK, d), lambda i, be, nv, bi: (bi[i], 0)),
                      pl.BlockSpec((1, d, de2), lambda i, be, nv, bi: (be[i], 0, 0)),
                      pl.BlockSpec((1, de2 // 2, d), lambda i, be, nv, bi: (be[i], 0, 0))],
            out_specs=pl.BlockSpec((MOE_BLK, d), lambda i, be, nv, bi: (bi[i], 0)),
            scratch_shapes=[pltpu.VMEM((d, de2), BF16), pltpu.VMEM((de2 // 2, d), BF16)]),
        out_shape=jax.ShapeDtypeStruct((n_rows, d), F32),
        compiler_params=_cparams(("arbitrary",)),
        name="experts",
    )(block_e, block_nv, block_idx, buf, w_in, w_out)


def _combine_kernel(dest_ref, x1_ref, rw_ref, mod_ref, fg_ref, yb_hbm, o_ref, ybuf_ref, sem):
    tc = x1_ref.shape[0]
    base = pl.program_id(0) * tc

    def row_copy(src, k, t):
        return pltpu.make_async_copy(yb_hbm.at[pl.ds(src, 1), :], ybuf_ref.at[k, pl.ds(t, 1), :], sem)

    def issue(t, c):
        row_copy(dest_ref[2 * (base + t)], 0, t).start()
        row_copy(dest_ref[2 * (base + t) + 1], 1, t).start()
        return c

    lax.fori_loop(0, tc, issue, 0)

    def wait(t, c):
        row_copy(0, 0, 0).wait()
        row_copy(0, 1, 0).wait()
        return c

    lax.fori_loop(0, tc, wait, 0)
    moe = rw_ref[:, 0:1] * ybuf_ref[0] + rw_ref[:, 1:2] * ybuf_ref[1]
    x2 = x1_ref[...] + mod_ref[0, 5:6, :] * moe
    o_ref[...] = x2 * lax.rsqrt(jnp.mean(x2 * x2, axis=-1, keepdims=True) + EPS) * fg_ref[...]


def _combine(dest, x1, rw, mods, fg, yb, tokens_per_batch, tc):
    n_tok, d = x1.shape
    tiles_per_batch = tokens_per_batch // tc
    return pl.pallas_call(
        _combine_kernel,
        grid_spec=pltpu.PrefetchScalarGridSpec(
            num_scalar_prefetch=1, grid=(n_tok // tc,),
            in_specs=[pl.BlockSpec((tc, d), lambda i, dst: (i, 0)),
                      pl.BlockSpec((tc, LANES), lambda i, dst: (i, 0)),
                      pl.BlockSpec((1, N_MOD, d), lambda i, dst: (i // tiles_per_batch, 0, 0)),
                      pl.BlockSpec((1, d), lambda i, dst: (0, 0)),
                      pl.BlockSpec(memory_space=pl.ANY)],
            out_specs=pl.BlockSpec((tc, d), lambda i, dst: (i, 0)),
            scratch_shapes=[pltpu.VMEM((2, tc, d), F32), pltpu.SemaphoreType.DMA(())]),
        out_shape=jax.ShapeDtypeStruct((n_tok, d), F32),
        compiler_params=_cparams(("arbitrary",)),
        name="combine",
    )(dest, x1, rw, mods, fg, yb)


def _prep_inproj_weights(w_in, gla_up_w, gla_up_b, ml_i_b, ml_f_b):
    d = w_in.shape[0]
    o_gq, o_gk, o_gv, o_gg = 0, GLA_QK_W, 2 * GLA_QK_W, 2 * GLA_QK_W + GLA_V_W
    o_lr = o_gg + GLA_V_W
    o_mqk = o_lr + 2 * GLA_LR
    o_mi = o_mqk + 4 * ML_W
    o_mf = o_mi + 2 * ML_HEADS

    def pad_heads(off):
        w = w_in[:, off:off + GLA_QK_W].reshape(d, GLA_HEADS, GLA_DK)
        return jnp.pad(w, ((0, 0), (0, 0), (0, LANES - GLA_DK))).reshape(d, GLA_HEADS * LANES)

    wg = jnp.concatenate([pad_heads(o_gq), pad_heads(o_gk), w_in[:, o_gv:o_gg], w_in[:, o_gg:o_lr]], axis=1)
    wm = w_in[:, o_mqk:o_mi]
    ws = jnp.concatenate([w_in[:, o_lr:o_mqk], w_in[:, o_mi:o_mf + 2 * ML_HEADS],
                          jnp.zeros((d, LANES - 2 * GLA_LR - 4 * ML_HEADS), w_in.dtype)], axis=1)
    bias = jnp.zeros((LANES,), F32)
    bias = bias.at[_MI0:_MI0 + 2 * ML_HEADS].set(ml_i_b.reshape(-1))
    bias = bias.at[_MF0:_MF0 + 2 * ML_HEADS].set(ml_f_b.reshape(-1))
    up = gla_up_w.reshape(2, GLA_LR, GLA_HEADS, GLA_DK).transpose(2, 0, 1, 3)
    wup = jnp.zeros((GLA_HEADS, 2, LANES, LANES), F32)
    for dr in range(2):
        wup = wup.at[:, dr, dr * GLA_LR:(dr + 1) * GLA_LR, :GLA_DK].set(up[:, dr])
    bup = jnp.pad(gla_up_b.reshape(2, GLA_HEADS, GLA_DK).transpose(1, 0, 2),
                  ((0, 0), (0, 0), (0, LANES - GLA_DK))).reshape(GLA_HEADS, 2, 1, LANES)
    return (wg.astype(BF16), wm.astype(BF16), ws.astype(BF16), ws.T.astype(BF16),
            bias.reshape(1, LANES), bias.reshape(LANES, 1), wup.astype(BF16), bup)


def _layer(x, ctx, mods, norm1_g, w_in, gla_up_w, gla_up_b, gla_norm_g, ml_conv_w, ml_conv_b,
           ml_i_b, ml_f_b, ml_norm_g, w_out, norm2_g, rg_w, rg_b, re_w, re_b, e_w_in, e_w_out, final_g):
    bsz, seq, d = x.shape
    n_tok = bsz * seq
    wg, wm, ws, wst, bcol, brow, wup, bup = _prep_inproj_weights(w_in, gla_up_w, gla_up_b, ml_i_b, ml_f_b)
    g1 = norm1_g.reshape(1, d)
    zg_x, zm_x, zs_x, gcol_x, grow_x = _inproj(x, mods, lambda b: b, g1, wg, wm, ws, wst, bcol, brow, 256)
    zg_c, zm_c, zs_c, gcol_c, grow_c = _inproj(ctx, mods, lambda b: bsz, g1, wg, wm, ws, wst, bcol, brow,
                                               min(256, ctx.shape[1]))
    gla_o = _gla(zg_x, zs_x, zg_c, zs_c, wup, bup, gla_norm_g.reshape(1, -1))
    ml_o = _mlstm(zm_x, gcol_x, grow_x, zm_c, gcol_c, grow_c,
                  ml_conv_w.reshape(9, -1), ml_conv_b.reshape(1, -1), ml_norm_g.reshape(1, -1))

    wr = jnp.zeros((d, LANES), F32).at[:, _G0:_E0].set(rg_w).at[:, _E0:_E0 + N_EXPERTS].set(re_w)
    br = jnp.zeros((1, LANES), F32).at[0, _G0:_E0].set(rg_b).at[0, _E0:_E0 + N_EXPERTS].set(re_b)
    wrh = wr.astype(BF16)
    wrl = (wr - wrh.astype(F32)).astype(BF16)
    x1, h2, ri, rw, cnt = _outproj(x, gla_o, ml_o, mods, w_out[:GLA_V_W].astype(BF16),
                                   w_out[GLA_V_W:].astype(BF16), norm2_g.reshape(1, d), wrh, wrl, br, 256)

    counts = cnt[0, :N_EXPERTS].astype(jnp.int32)
    nblk = (counts + MOE_BLK - 1) // MOE_BLK
    blk_end = jnp.cumsum(nblk)
    blk_start = blk_end - nblk
    n_used = blk_end[-1]
    nb_max = (2 * n_tok) // MOE_BLK + N_EXPERTS
    blk = jnp.arange(nb_max, dtype=jnp.int32)
    blk_c = jnp.minimum(blk, n_used - 1)
    block_e = jnp.minimum(jnp.searchsorted(blk_end, blk_c, side='right'), N_EXPERTS - 1).astype(jnp.int32)
    block_nv = jnp.clip(counts[block_e] - (blk_c - blk_start[block_e]) * MOE_BLK, 0, MOE_BLK)
    block_nv = jnp.where(blk < n_used, block_nv, 0).astype(jnp.int32)
    ri = ri.reshape(n_tok, LANES)
    pad_start = blk_start * MOE_BLK
    dest = (pad_start[ri[:, 0:2]] + ri[:, 2:4]).reshape(-1).astype(jnp.int32)

    buf = _dispatch(dest, h2.reshape(n_tok, d), nb_max * MOE_BLK)
    yb = _experts(block_e, block_nv, blk_c.astype(jnp.int32), buf, e_w_in, e_w_out)
    out = _combine(dest, x1.reshape(n_tok, d), rw.reshape(n_tok, LANES), mods, final_g.reshape(1, d), yb, seq, 256)
    return out.reshape(bsz, seq, d)


def kernel(x, c, ctx, c_ctx, ada_w, ada_b, norm1_g, w_in, gla_up_w, gla_up_b, gla_norm_g, ml_conv_w, ml_conv_b,
           ml_i_b, ml_f_b, ml_norm_g, w_out, norm2_g, router_group_w, router_group_b, router_expert_w,
           router_expert_b, expert_w_in, expert_w_out, final_norm_g):
    assert ada_w.shape[0] == 1, "single-layer stack"
    bsz, d = c.shape
    cc = jnp.concatenate([c, c_ctx[None, :], jnp.zeros((8 - bsz - 1, d), F32)], axis=0)
    mods = _modulation(cc, ada_w[0], ada_b[0]).reshape(8, N_MOD, d)
    return _layer(x, ctx, mods, norm1_g[0], w_in[0], gla_up_w[0], gla_up_b[0], gla_norm_g[0],
                  ml_conv_w[0], ml_conv_b[0], ml_i_b[0], ml_f_b[0], ml_norm_g[0], w_out[0], norm2_g[0],
                  router_group_w[0], router_group_b[0], router_expert_w[0], router_expert_b[0],
                  expert_w_in[0], expert_w_out[0], final_norm_g)
```

```python
import functools

import jax
import jax.numpy as jnp
from jax import lax
from jax.experimental import pallas as pl
from jax.experimental.pallas import tpu as pltpu

F32 = jnp.float32
BF16 = jnp.bfloat16

D_MODEL = 1024
GRID_W = 64
N_MOD = 6
EPS = 1e-6

GLA_HEADS = 4
GLA_DK = 64
GLA_DV = 128
GLA_LR = 16
GLA_TAU = 16.0
GLA_C = 64

ML_HEADS = 4
ML_DH = 128
ML_C = 128

N_GROUPS = 4
EXP_PER_GROUP = 8
N_EXPERTS = N_GROUPS * EXP_PER_GROUP
D_EXPERT = 512
MOE_BLK = 256

GLA_QK_W = GLA_HEADS * GLA_DK
GLA_V_W = GLA_HEADS * GLA_DV
ML_W = ML_HEADS * ML_DH
LANES = 128
VMEM_LIMIT = 56 * 1024 * 1024

_LR0 = 0
_MI0 = 2 * GLA_LR
_MF0 = _MI0 + 2 * ML_HEADS


def _cparams(sem):
    return pltpu.CompilerParams(dimension_semantics=sem, vmem_limit_bytes=VMEM_LIMIT)


def _sigmoid(x):
    return 1.0 / (1.0 + jnp.exp(-x))


def _silu(x):
    return x * _sigmoid(x)


def _log_sigmoid(x):
    return jnp.minimum(x, 0.0) - jnp.log1p(jnp.exp(-jnp.abs(x)))


def _split_dot(a_bf16_exact, x, dims=None):
    x_hi = x.astype(BF16)
    x_lo = (x - x_hi.astype(F32)).astype(BF16)
    if dims is None:
        f = lambda u: jnp.dot(a_bf16_exact, u, preferred_element_type=F32)
    else:
        f = lambda u: lax.dot_general(u, a_bf16_exact, dims, preferred_element_type=F32)
    return f(x_hi) + f(x_lo)


def _mod_kernel(c_ref, w_ref, b_ref, o_ref):
    c = c_ref[...]
    s = _silu(c).astype(BF16)
    o_ref[...] = jnp.dot(s, w_ref[...].astype(BF16), preferred_element_type=F32) + b_ref[...]


def _modulation(cc, ada_w, ada_b):
    rows, d = cc.shape
    n = ada_w.shape[1]
    tn = 1536
    return pl.pallas_call(
        _mod_kernel,
        grid=(n // tn,),
        in_specs=[pl.BlockSpec((rows, d), lambda j: (0, 0)),
                  pl.BlockSpec((d, tn), lambda j: (0, j)),
                  pl.BlockSpec((1, tn), lambda j: (0, j))],
        out_specs=pl.BlockSpec((rows, tn), lambda j: (0, j)),
        out_shape=jax.ShapeDtypeStruct((rows, n), F32),
        compiler_params=_cparams(("arbitrary",)),
        name="mod",
    )(cc, ada_w, ada_b.reshape(1, n))


def _inproj_kernel(x_ref, mod_ref, g_ref, wg_ref, wm_ref, ws_ref, wst_ref, bcol_ref, brow_ref,
                   zg_ref, zm_ref, zs_ref, gcol_ref, grow_ref):
    tm = x_ref.shape[1]
    x = x_ref[0]
    y = x * lax.rsqrt(jnp.mean(x * x, axis=-1, keepdims=True) + EPS) * g_ref[...]
    h = (y * (1.0 + mod_ref[0, 1:2, :]) + mod_ref[0, 0:1, :]).astype(BF16)
    zg_ref[0] = jnp.dot(h, wg_ref[...], preferred_element_type=F32)
    zm_ref[0] = jnp.dot(h, wm_ref[...], preferred_element_type=F32)
    zs = jnp.dot(h, ws_ref[...], preferred_element_type=F32) + bcol_ref[...]
    zst = lax.dot_general(wst_ref[...], h, (((1,), (1,)), ((), ())),
                          preferred_element_type=F32) + brow_ref[...]
    zs_ref[0] = zs

    r = lax.broadcasted_iota(jnp.int32, (tm, tm), 0)
    c = lax.broadcasted_iota(jnp.int32, (tm, tm), 1)
    shift = ML_C.bit_length() - 1
    same = jnp.right_shift(r, shift) == jnp.right_shift(c, shift)
    lower = jnp.where(same & (c <= r), 1.0, 0.0).astype(BF16)
    upper = jnp.where(same & (c >= r), 1.0, 0.0).astype(BF16)
    lsf = _log_sigmoid(zs)
    a_pre = _split_dot(lower, lsf)
    a_suf = _split_dot(upper, lsf)
    lsft = _log_sigmoid(zst)
    dims = (((1,), (0,)), ((), ()))
    a_pre_t = _split_dot(upper, lsft, dims)
    a_suf_t = _split_dot(lower, lsft, dims)

    lane = lax.broadcasted_iota(jnp.int32, (tm, LANES), 1)
    for hd in range(ML_HEADS):
        cols = (a_pre[:, _MF0 + hd:_MF0 + hd + 1],
                a_suf[:, _MF0 + ML_HEADS + hd:_MF0 + ML_HEADS + hd + 1],
                zs[:, _MI0 + hd:_MI0 + hd + 1],
                zs[:, _MI0 + ML_HEADS + hd:_MI0 + ML_HEADS + hd + 1])
        slab = jnp.zeros((tm, LANES), F32)
        for j, col in enumerate(cols):
            slab = jnp.where(lane == j, col, slab)
        gcol_ref[0, :, hd * LANES:(hd + 1) * LANES] = slab
        rows = (a_pre_t[_MF0 + hd:_MF0 + hd + 1, :],
                a_suf_t[_MF0 + ML_HEADS + hd:_MF0 + ML_HEADS + hd + 1, :],
                zst[_MI0 + hd:_MI0 + hd + 1, :],
                zst[_MI0 + ML_HEADS + hd:_MI0 + ML_HEADS + hd + 1, :])
        for j, row in enumerate(rows):
            grow_ref[0, hd, j:j + 1, :] = row
        grow_ref[0, hd, 4:8, :] = jnp.zeros((4, tm), F32)


def _inproj(x, mods, mod_row_of_batch, norm_g, wg, wm, ws, wst, bcol, brow, tm):
    bsz, l, d = x.shape
    assert l % tm == 0 and tm % ML_C == 0
    const = lambda shape: pl.BlockSpec(shape, lambda b, i: (0,) * len(shape))
    return pl.pallas_call(
        _inproj_kernel,
        grid=(bsz, l // tm),
        in_specs=[pl.BlockSpec((1, tm, d), lambda b, i: (b, i, 0)),
                  pl.BlockSpec((1, N_MOD, d), lambda b, i: (mod_row_of_batch(b), 0, 0)),
                  const((1, d)), const(wg.shape), const(wm.shape), const(ws.shape), const(wst.shape),
                  const((1, LANES)), const((LANES, 1))],
        out_specs=[pl.BlockSpec((1, tm, wg.shape[1]), lambda b, i: (b, i, 0)),
                   pl.BlockSpec((1, tm, wm.shape[1]), lambda b, i: (b, i, 0)),
                   pl.BlockSpec((1, tm, LANES), lambda b, i: (b, i, 0)),
                   pl.BlockSpec((1, tm, ML_HEADS * LANES), lambda b, i: (b, i, 0)),
                   pl.BlockSpec((1, ML_HEADS, 8, tm), lambda b, i: (b, 0, 0, i))],
        out_shape=[jax.ShapeDtypeStruct((bsz, l, wg.shape[1]), F32),
                   jax.ShapeDtypeStruct((bsz, l, wm.shape[1]), F32),
                   jax.ShapeDtypeStruct((bsz, l, LANES), F32),
                   jax.ShapeDtypeStruct((bsz, l, ML_HEADS * LANES), F32),
                   jax.ShapeDtypeStruct((bsz, ML_HEADS, 8, l), F32)],
        compiler_params=_cparams(("arbitrary", "arbitrary")),
        name="inproj",
    )(x, mods, norm_g, wg, wm, ws, wst, bcol, brow)


def _gla_chunk(q, k, v, zs, wup, bup, s_ref, direction, want_out):
    c = k.shape[0]
    logits = jnp.dot(zs.astype(BF16), wup, preferred_element_type=F32) + bup
    g = _log_sigmoid(logits) * (1.0 / GLA_TAU)
    r = lax.broadcasted_iota(jnp.int32, (c, c), 0)
    cc = lax.broadcasted_iota(jnp.int32, (c, c), 1)
    causal = (cc <= r) if direction == 0 else (cc >= r)
    b = _split_dot(jnp.where(causal, 1.0, 0.0).astype(BF16), g)
    b_end = b[c - 1:c, :] if direction == 0 else b[0:1, :]
    s = s_ref[...]
    out = None
    if want_out:
        b_mid = b[c // 2:c // 2 + 1, :]
        q_in = (q * jnp.exp(b - b_mid)).astype(BF16)
        k_in = (k * jnp.exp(b_mid - b)).astype(BF16)
        att = lax.dot_general(q_in, k_in, (((1,), (1,)), ((), ())), preferred_element_type=F32)
        att = jnp.where(causal, att, 0.0)
        out = jnp.dot(att.astype(BF16), v.astype(BF16), preferred_element_type=F32)
        out = out + lax.dot_general((q * jnp.exp(b)).astype(BF16), s.astype(BF16),
                                    (((1,), (1,)), ((), ())), preferred_element_type=F32)
    kd = (k * jnp.exp(b_end - b)).astype(BF16)
    upd = lax.dot_general(v.astype(BF16), kd, (((0,), (0,)), ((), ())), preferred_element_type=F32)
    s_ref[...] = jnp.exp(b_end) * s + upd
    return out


def _gla_kernel(q_ref, k_ref, v_ref, gg_ref, zs_ref, kc_ref, vc_ref, zsc_ref,
                wup_ref, bup_ref, ng_ref, o_ref, s_ref, acc_ref):
    seq = q_ref.shape[1]
    ctx = kc_ref.shape[1]
    n = seq // GLA_C
    nc = ctx // GLA_C
    s_ref[...] = jnp.zeros_like(s_ref)

    def rows(i):
        return pl.ds(pl.multiple_of(i * GLA_C, GLA_C), GLA_C)

    def ctx_step(j, carry):
        for d in range(2):
            i = j if d == 0 else nc - 1 - j
            _gla_chunk(None, kc_ref[0, rows(i), :], vc_ref[0, rows(i), :],
                       zsc_ref[0, rows(i), :], wup_ref[0, d], bup_ref[0, d], s_ref.at[d], d, False)
        return carry

    lax.fori_loop(0, nc, ctx_step, 0)

    def finish(i, total):
        y = total * lax.rsqrt(jnp.mean(total * total, axis=-1, keepdims=True) + EPS) * ng_ref[...]
        o_ref[0, rows(i), :] = (y * _silu(gg_ref[0, rows(i), :])).astype(o_ref.dtype)

    def lat_step(j, carry, second):
        for d in range(2):
            i = j if d == 0 else n - 1 - j
            out = _gla_chunk(q_ref[0, rows(i), :] * (GLA_DK ** -0.5), k_ref[0, rows(i), :],
                             v_ref[0, rows(i), :], zs_ref[0, rows(i), :], wup_ref[0, d], bup_ref[0, d],
                             s_ref.at[d], d, True)
            if second:
                finish(i, acc_ref[rows(i), :] + out)
            else:
                acc_ref[rows(i), :] = out
        return carry

    lax.fori_loop(0, n // 2, functools.partial(lat_step, second=False), 0)
    lax.fori_loop(n // 2, n, functools.partial(lat_step, second=True), 0)


def _gla(zg_x, zs_x, zg_c, zs_c, wup, bup, norm_g):
    bsz, seq, _ = zg_x.shape
    ctx = zg_c.shape[1]
    assert seq % (2 * GLA_C) == 0 and ctx % GLA_C == 0
    h = GLA_HEADS

    def col(l, off):
        return pl.BlockSpec((1, l, LANES), lambda b, hd: (b, 0, off + hd))

    return pl.pallas_call(
        _gla_kernel,
        grid=(bsz, h),
        in_specs=[col(seq, 0), col(seq, h), col(seq, 2 * h), col(seq, 3 * h),
                  pl.BlockSpec((1, seq, LANES), lambda b, hd: (b, 0, 0)),
                  col(ctx, h), col(ctx, 2 * h),
                  pl.BlockSpec((1, ctx, LANES), lambda b, hd: (b, 0, 0)),
                  pl.BlockSpec((1, 2, LANES, LANES), lambda b, hd: (hd, 0, 0, 0)),
                  pl.BlockSpec((1, 2, 1, LANES), lambda b, hd: (hd, 0, 0, 0)),
                  pl.BlockSpec((1, LANES), lambda b, hd: (0, hd))],
        out_specs=pl.BlockSpec((1, seq, LANES), lambda b, hd: (b, 0, hd)),
        out_shape=jax.ShapeDtypeStruct((bsz, seq, h * GLA_DV), BF16),
        scratch_shapes=[pltpu.VMEM((2, LANES, LANES), F32), pltpu.VMEM((seq, LANES), F32)],
        compiler_params=_cparams(("arbitrary", "arbitrary")),
        name="gla",
    )(zg_x, zg_x, zg_x, zg_x, zs_x, zg_c, zg_c, zs_c, wup, bup, norm_g)


def _grid_conv_silu(src_ref, dst_ref, w_ref, b_ref, grid_w, scale):
    l = src_ref.shape[1]
    n_rows = l // grid_w
    col = lax.broadcasted_iota(jnp.int32, (grid_w, LANES), 0)

    def body(r, carry):
        acc = jnp.zeros((grid_w, LANES), F32) + b_ref[...]
        for dy in (-1, 0, 1):
            if n_rows == 1 and dy != 0:
                continue
            rr = jnp.clip(r + dy, 0, n_rows - 1)
            blk = src_ref[0, pl.ds(pl.multiple_of(rr * grid_w, grid_w), grid_w), :]
            valid = jnp.logical_and(r + dy >= 0, r + dy < n_rows)
            blk = jnp.where(valid, blk, 0.0)
            for dx in (-1, 0, 1):
                if dx == 0:
                    sh = blk
                else:
                    sh = pltpu.roll(blk, shift=(-dx) % grid_w, axis=0)
                    sh = jnp.where((col + dx >= 0) & (col + dx < grid_w), sh, 0.0)
                tap = (dy + 1) * 3 + (dx + 1)
                acc = acc + sh * w_ref[tap:tap + 1, :]
        dst_ref[pl.ds(pl.multiple_of(r * grid_w, grid_w), grid_w), :] = _silu(acc) * scale
        return carry

    lax.fori_loop(0, n_rows, body, 0)


def _ml_gates(gcol, grow, direction):
    c = gcol.shape[0]
    a_col = gcol[:, direction:direction + 1]
    i_col = gcol[:, 2 + direction:3 + direction]
    a_row = grow[direction:direction + 1, :]
    i_row = grow[2 + direction:3 + direction, :]
    a_end = a_row[:, c - 1:c] if direction == 0 else a_row[:, 0:1]
    return a_col, i_col, a_row, i_row, a_end


def _ml_chunk(q, k, v, gcol, grow, s_ref, m_ref, direction, want_out):
    c = k.shape[0]
    a_col, i_col, a_row, i_row, a_end = _ml_gates(gcol, grow, direction)
    lane = lax.broadcasted_iota(jnp.int32, (c, LANES), 1)
    v_aug = jnp.concatenate([v, jnp.where(lane == 0, 1.0, 0.0)], axis=1).astype(BF16)
    s = s_ref[...]
    m = m_ref[:, 0:1]
    out = None
    if want_out:
        r = lax.broadcasted_iota(jnp.int32, (c, c), 0)
        cc = lax.broadcasted_iota(jnp.int32, (c, c), 1)
        causal = (cc <= r) if direction == 0 else (cc >= r)
        dmat = jnp.where(causal, a_col - a_row + i_row, -jnp.inf)
        inter = a_col + m
        m_t = jnp.maximum(inter, jnp.max(dmat, axis=-1, keepdims=True))
        w_inter = jnp.exp(inter - m_t)
        qb = q.astype(BF16)
        qk = lax.dot_general(qb, k.astype(BF16), (((1,), (1,)), ((), ())), preferred_element_type=F32)
        p = (qk * jnp.exp(dmat - m_t)).astype(BF16)
        both = (jnp.dot(p, v_aug, preferred_element_type=F32)
                + w_inter * jnp.dot(qb, s.astype(BF16), preferred_element_type=F32))
        num = both[:, :ML_DH]
        den = both[:, ML_DH:ML_DH + 1]
        out = num / jnp.maximum(jnp.abs(den), jnp.exp(-m_t))
    g = a_end - a_col + i_col
    m_new = jnp.maximum(a_end + m, jnp.max(g, axis=0, keepdims=True))
    decay = jnp.exp(a_end + m - m_new)
    kw = (k * jnp.exp(g - m_new)).astype(BF16)
    upd = lax.dot_general(kw, v_aug, (((0,), (0,)), ((), ())), preferred_element_type=F32)
    s_ref[...] = decay * s + upd
    m_ref[...] = jnp.broadcast_to(m_new, m_ref.shape)
    return out


def _mlstm_kernel(q_ref, k_ref, v_ref, mo_ref, gcol_ref, grow_ref,
                  kc_ref, vc_ref, gcolc_ref, growc_ref,
                  wq_ref, wk_ref, bq_ref, bk_ref, ng_ref, o_ref,
                  cq_ref, ck_ref, ckc_ref, s_ref, m_ref, acc_ref):
    seq = q_ref.shape[1]
    ctx = kc_ref.shape[1]
    n = seq // ML_C
    nc = ctx // ML_C
    _grid_conv_silu(q_ref, cq_ref, wq_ref, bq_ref, GRID_W, 1.0)
    _grid_conv_silu(k_ref, ck_ref, wk_ref, bk_ref, GRID_W, ML_DH ** -0.5)
    _grid_conv_silu(kc_ref, ckc_ref, wk_ref, bk_ref, ctx, ML_DH ** -0.5)
    s_ref[...] = jnp.zeros_like(s_ref)
    m_ref[...] = jnp.zeros_like(m_ref)

    def rows(i):
        return pl.ds(pl.multiple_of(i * ML_C, ML_C), ML_C)

    def ctx_step(j, carry):
        for d in range(2):
            i = j if d == 0 else nc - 1 - j
            _ml_chunk(None, ckc_ref[rows(i), :], vc_ref[0, rows(i), :], gcolc_ref[0, rows(i), :],
                      growc_ref[0, 0, :, rows(i)], s_ref.at[d], m_ref.at[d], d, False)
        return carry

    lax.fori_loop(0, nc, ctx_step, 0)

    def finish(i, total):
        y = total * lax.rsqrt(jnp.mean(total * total, axis=-1, keepdims=True) + EPS) * ng_ref[...]
        o_ref[0, rows(i), :] = (_sigmoid(mo_ref[0, rows(i), :]) * y).astype(o_ref.dtype)

    def lat_step(j, carry, second):
        for d in range(2):
            i = j if d == 0 else n - 1 - j
            out = _ml_chunk(cq_ref[rows(i), :], ck_ref[rows(i), :], v_ref[0, rows(i), :],
                            gcol_ref[0, rows(i), :], grow_ref[0, 0, :, rows(i)],
                            s_ref.at[d], m_ref.at[d], d, True)
            if second:
                finish(i, acc_ref[rows(i), :] + out)
            else:
                acc_ref[rows(i), :] = out
        return carry

    lax.fori_loop(0, n // 2, functools.partial(lat_step, second=False), 0)
    lax.fori_loop(n // 2, n, functools.partial(lat_step, second=True), 0)


def _mlstm(zm_x, gcol_x, grow_x, zm_c, gcol_c, grow_c, conv_w, conv_b, norm_g):
    bsz, seq, _ = zm_x.shape
    ctx = zm_c.shape[1]
    assert seq % (2 * ML_C) == 0 and ctx % ML_C == 0 and seq % GRID_W == 0
    h = ML_HEADS

    def col(l, off):
        return pl.BlockSpec((1, l, LANES), lambda b, hd: (b, 0, off + hd))

    def gates(l):
        return [pl.BlockSpec((1, l, LANES), lambda b, hd: (b, 0, hd)),
                pl.BlockSpec((1, 1, 8, l), lambda b, hd: (b, hd, 0, 0))]

    return pl.pallas_call(
        _mlstm_kernel,
        grid=(bsz, h),
        in_specs=[col(seq, 0), col(seq, h), col(seq, 2 * h), col(seq, 3 * h)] + gates(seq)
                 + [col(ctx, h), col(ctx, 2 * h)] + gates(ctx)
                 + [pl.BlockSpec((9, LANES), lambda b, hd: (0, hd)),
                    pl.BlockSpec((9, LANES), lambda b, hd: (0, h + hd)),
                    pl.BlockSpec((1, LANES), lambda b, hd: (0, hd)),
                    pl.BlockSpec((1, LANES), lambda b, hd: (0, h + hd)),
                    pl.BlockSpec((1, LANES), lambda b, hd: (0, hd))],
        out_specs=pl.BlockSpec((1, seq, LANES), lambda b, hd: (b, 0, hd)),
        out_shape=jax.ShapeDtypeStruct((bsz, seq, h * ML_DH), BF16),
        scratch_shapes=[pltpu.VMEM((seq, LANES), F32), pltpu.VMEM((seq, LANES), F32),
                        pltpu.VMEM((ctx, LANES), F32),
                        pltpu.VMEM((2, LANES, 2 * LANES), F32), pltpu.VMEM((2, 1, LANES), F32),
                        pltpu.VMEM((seq, LANES), F32)],
        compiler_params=_cparams(("arbitrary", "arbitrary")),
        name="mlstm",
    )(zm_x, zm_x, zm_x, zm_x, gcol_x, grow_x, zm_c, zm_c, gcol_c, grow_c,
      conv_w, conv_w, conv_b, conv_b, norm_g)


_G0 = 0
_E0 = N_GROUPS
RANK_BITS = 16
RANK_SPAN = 1 << RANK_BITS
ROW_UNROLL = 8
WAIT_UNROLL = 32


SUBLANES = 8


def _store_token_tiles(ref2d, val):
    n, w = val.shape
    k = w // LANES
    for c in range(k):
        ref2d[pl.ds(c, n, stride=k), :] = val[:, c * LANES:(c + 1) * LANES]


def _load_token_tiles(ref2d, first, n, k, step):
    return jnp.concatenate([ref2d[pl.ds(first + c, n, stride=step), :] for c in range(k)], axis=1)


def _outproj_kernel(x_ref, ga_ref, ml_ref, mod_ref, wa_ref, wb_ref, g2_ref, wrh_ref, wrl_ref, br_ref,
                    x1_ref, h2_ref, ri_ref, rw_ref, cnt_ref, base_ref):
    tm = x_ref.shape[1]

    @pl.when((pl.program_id(0) == 0) & (pl.program_id(1) == 0))
    def _():
        base_ref[...] = jnp.zeros_like(base_ref)

    mix = (jnp.dot(ga_ref[0], wa_ref[...], preferred_element_type=F32)
           + jnp.dot(ml_ref[0], wb_ref[...], preferred_element_type=F32))
    x1 = x_ref[0] + mod_ref[0, 2:3, :] * mix
    x1_ref[0] = x1
    y = x1 * lax.rsqrt(jnp.mean(x1 * x1, axis=-1, keepdims=True) + EPS) * g2_ref[...]
    h2 = y * (1.0 + mod_ref[0, 4:5, :]) + mod_ref[0, 3:4, :]
    _store_token_tiles(h2_ref, h2)

    h_hi = h2.astype(BF16)
    h_lo = (h2 - h_hi.astype(F32)).astype(BF16)
    logits = (jnp.dot(h_hi, wrh_ref[...], preferred_element_type=F32)
              + jnp.dot(h_lo, wrh_ref[...], preferred_element_type=F32)
              + jnp.dot(h_hi, wrl_ref[...], preferred_element_type=F32)) + br_ref[...]

    lane = lax.broadcasted_iota(jnp.int32, (tm, LANES), 1).astype(F32)
    neg = -jnp.inf
    big = float(LANES)
    is_g = lane < float(_E0)
    lg = jnp.where(is_g, logits, neg)
    gmax = jnp.max(lg, axis=-1, keepdims=True)
    gidx = jnp.min(jnp.where(lg == gmax, lane, big), axis=-1, keepdims=True)
    gw = 1.0 / jnp.sum(jnp.where(is_g, jnp.exp(logits - gmax), 0.0), axis=-1, keepdims=True)
    lo = float(_E0) + float(EXP_PER_GROUP) * gidx
    le = jnp.where((lane >= lo) & (lane < lo + float(EXP_PER_GROUP)), logits, neg)
    v1 = jnp.max(le, axis=-1, keepdims=True)
    i1 = jnp.min(jnp.where(le == v1, lane, big), axis=-1, keepdims=True)
    le2 = jnp.where(lane == i1, neg, le)
    v2 = jnp.max(le2, axis=-1, keepdims=True)
    i2 = jnp.min(jnp.where(le2 == v2, lane, big), axis=-1, keepdims=True)
    t = jnp.exp(v2 - v1)
    w1 = gw / (1.0 + t)
    w2 = gw * t / (1.0 + t)
    e1 = i1 - float(_E0)
    e2 = i2 - float(_E0)

    oh1 = lane == e1
    oh2 = lane == e2
    oh = jnp.where(oh1 | oh2, 1.0, 0.0)
    r = lax.broadcasted_iota(jnp.int32, (tm, tm), 0)
    c = lax.broadcasted_iota(jnp.int32, (tm, tm), 1)
    strict = jnp.where(c < r, 1.0, 0.0).astype(BF16)
    before = jnp.dot(strict, oh.astype(BF16), preferred_element_type=F32) + base_ref[...]
    rank1 = jnp.sum(jnp.where(oh1, before, 0.0), axis=-1, keepdims=True)
    rank2 = jnp.sum(jnp.where(oh2, before, 0.0), axis=-1, keepdims=True)
    total = base_ref[...] + jnp.sum(oh, axis=0, keepdims=True)
    base_ref[...] = total
    cnt_ref[...] = total

    ids = jnp.where(lane == 0.0, e1 * float(RANK_SPAN) + rank1,
                    jnp.where(lane == 1.0, e2 * float(RANK_SPAN) + rank2, 0.0))
    ri_ref[0] = ids.astype(jnp.int32)
    rw_ref[0] = jnp.where(lane == 0.0, w1, jnp.where(lane == 1.0, w2, 0.0))


def _outproj(x, gla_o, ml_o, mods, wa, wb, g2, wrh, wrl, br, tm):
    bsz, seq, d = x.shape
    const = lambda shape: pl.BlockSpec(shape, lambda b, i: (0,) * len(shape))
    tile = lambda w: pl.BlockSpec((1, tm, w), lambda b, i: (b, i, 0))
    return pl.pallas_call(
        _outproj_kernel,
        grid=(bsz, seq // tm),
        in_specs=[tile(d), tile(gla_o.shape[2]), tile(ml_o.shape[2]),
                  pl.BlockSpec((1, N_MOD, d), lambda b, i: (b, 0, 0)),
                  const(wa.shape), const(wb.shape), const((1, d)),
                  const(wrh.shape), const(wrl.shape), const((1, LANES))],
        out_specs=[tile(d),
                   pl.BlockSpec((tm * d // LANES, LANES), lambda b, i: (b * (seq // tm) + i, 0)),
                   tile(LANES), tile(LANES), const((1, LANES))],
        out_shape=[jax.ShapeDtypeStruct((bsz, seq, d), F32),
                   jax.ShapeDtypeStruct((bsz * seq * d // LANES, LANES), F32),
                   jax.ShapeDtypeStruct((bsz, seq, LANES), jnp.int32),
                   jax.ShapeDtypeStruct((bsz, seq, LANES), F32),
                   jax.ShapeDtypeStruct((1, LANES), F32)],
        scratch_shapes=[pltpu.VMEM((1, LANES), F32)],
        compiler_params=_cparams(("arbitrary", "arbitrary")),
        name="outproj",
    )(x, gla_o, ml_o, mods, wa, wb, g2, wrh, wrl, br)


def _experts_kernel(pk_ref, ps_ref, cnt_ref, be_ref, nv_ref, meta_ref, h_hbm, w1_ref, w2_ref, ytok_hbm,
                    src_ref, xbuf, ybuf, w1c_ref, w2c_ref, gsem, ssem):
    i = pl.program_id(0)
    n_steps = pl.num_programs(0)
    n_used = meta_ref[0]
    tr = SUBLANES
    n_tok = h_hbm.shape[0] // tr
    n_rows = src_ref.shape[0]
    blk = xbuf.shape[1] // tr
    slot = lax.rem(i, 2)

    def slab(j):
        return pl.ds(pl.multiple_of(j * tr, tr), tr)

    def gather_copy(tok, s, r):
        return pltpu.make_async_copy(h_hbm.at[slab(tok), :], xbuf.at[s, slab(r), :], gsem.at[s])

    def scatter_copy(a, s, r):
        return pltpu.make_async_copy(ybuf.at[s, slab(r), :], ytok_hbm.at[slab(a), :], ssem.at[s])

    def rows_loop(body):
        def step(r, c):
            body(r)
            return c
        lax.fori_loop(0, blk, step, 0, unroll=ROW_UNROLL)

    def issue_gather(b, s):
        def one(r):
            tok = lax.shift_right_logical(src_ref[b * blk + r], 1)
            gather_copy(jnp.minimum(tok, n_tok - 1), s, r).start()
        rows_loop(one)

    def wait_rows(copy):
        def step(r, c):
            copy.wait()
            return c
        lax.fori_loop(0, blk, step, 0, unroll=WAIT_UNROLL)

    def wait_gather(s):
        wait_rows(gather_copy(0, s, 0))

    def wait_scatter(s):
        wait_rows(scatter_copy(0, s, 0))

    @pl.when(i == 0)
    def _():
        ybuf[...] = jnp.zeros_like(ybuf)
        for s in range(2):
            tail = ytok_hbm.at[pl.ds((2 * n_tok + s * blk) * tr, blk * tr), :]
            cp = pltpu.make_async_copy(ybuf.at[s], tail, ssem.at[s])
            cp.start()
            cp.wait()

        def put(a, c):
            p = pk_ref[a]
            src_ref[ps_ref[lax.shift_right_logical(p, RANK_BITS)] + (p & (RANK_SPAN - 1))] = a
            return c
        lax.fori_loop(0, 2 * n_tok, put, 0, unroll=16)

        def pad(j, c):
            src_ref[j] = 2 * n_tok + (j & (2 * blk - 1))
            return c

        def pad_expert(e, c):
            lax.fori_loop(ps_ref[e] + cnt_ref[e], ps_ref[e + 1], pad, 0)
            return c
        lax.fori_loop(0, cnt_ref.shape[0], pad_expert, 0)
        used_end = ps_ref[cnt_ref.shape[0]]
        lax.fori_loop(used_end, jnp.minimum(used_end + blk, n_rows), pad, 0)
        issue_gather(0, 0)

    @pl.when(i < n_used)
    def _():
        wait_gather(slot)
        issue_gather(jnp.minimum(i + 1, n_steps - 1), 1 - slot)

        @pl.when((i == 0) | (be_ref[i] != be_ref[jnp.maximum(i - 1, 0)]))
        def _():
            w1c_ref[...] = w1_ref[0].astype(BF16)
            w2c_ref[...] = w2_ref[0].astype(BF16)

        @pl.when(i >= 2)
        def _():
            wait_scatter(slot)

        row = lax.broadcasted_iota(jnp.int32, (blk, 1), 0)
        x = _load_token_tiles(xbuf.at[slot], 0, blk, tr, tr)
        x = jnp.where(row < nv_ref[i], x, 0.0).astype(BF16)
        h = jnp.dot(x, w1c_ref[...], preferred_element_type=F32)
        a = (_silu(h[:, :D_EXPERT]) * h[:, D_EXPERT:]).astype(BF16)
        _store_token_tiles(ybuf.at[slot], jnp.dot(a, w2c_ref[...], preferred_element_type=F32))
        rows_loop(lambda r: scatter_copy(src_ref[i * blk + r], slot, r).start())

    @pl.when(i == n_steps - 1)
    def _():
        wait_gather(lax.rem(n_used, 2))
        wait_scatter(0)
        wait_scatter(1)


def _experts(packed, pad_start, counts, block_e, block_nv, meta, h2, w_in, w_out, nb):
    d = w_in.shape[1]
    tr = d // LANES
    assert tr == SUBLANES, "a token row must fill exactly one (8, 128) tile"
    n_tok = h2.shape[0] // tr
    de2 = w_in.shape[2]
    n_rows = nb * MOE_BLK
    assert 2 * n_tok >= 2 * MOE_BLK
    assert MOE_BLK & (MOE_BLK - 1) == 0
    wmap = lambda i, pk, ps, cnt, be, nv, meta: (be[i], 0, 0)
    return pl.pallas_call(
        _experts_kernel,
        grid_spec=pltpu.PrefetchScalarGridSpec(
            num_scalar_prefetch=6, grid=(nb,),
            in_specs=[pl.BlockSpec(memory_space=pl.ANY),
                      pl.BlockSpec((1, d, de2), wmap),
                      pl.BlockSpec((1, de2 // 2, d), wmap)],
            out_specs=pl.BlockSpec(memory_space=pl.ANY),
            scratch_shapes=[pltpu.SMEM((n_rows,), jnp.int32),
                            pltpu.VMEM((2, MOE_BLK * tr, LANES), F32), pltpu.VMEM((2, MOE_BLK * tr, LANES), F32),
                            pltpu.VMEM((d, de2), BF16), pltpu.VMEM((de2 // 2, d), BF16),
                            pltpu.SemaphoreType.DMA((2,)), pltpu.SemaphoreType.DMA((2,))]),
        out_shape=jax.ShapeDtypeStruct(((2 * n_tok + 2 * MOE_BLK) * tr, LANES), F32),
        compiler_params=_cparams(("arbitrary",)),
        name="experts",
    )(packed, pad_start, counts, block_e, block_nv, meta, h2, w_in, w_out)


def _combine_kernel(x1_ref, y_ref, rw_ref, mod_ref, fg_ref, o_ref):
    tc, d = x1_ref.shape
    tr = d // LANES
    y1 = _load_token_tiles(y_ref, 0, tc, tr, 2 * tr)
    y2 = _load_token_tiles(y_ref, tr, tc, tr, 2 * tr)
    moe = rw_ref[:, 0:1] * y1 + rw_ref[:, 1:2] * y2
    x2 = x1_ref[...] + mod_ref[0, 5:6, :] * moe
    o_ref[...] = x2 * lax.rsqrt(jnp.mean(x2 * x2, axis=-1, keepdims=True) + EPS) * fg_ref[...]


def _combine(x1, ytok, rw, mods, fg, tokens_per_batch, tc):
    n_tok, d = x1.shape
    tiles_per_batch = tokens_per_batch // tc
    return pl.pallas_call(
        _combine_kernel,
        grid=(n_tok // tc,),
        in_specs=[pl.BlockSpec((tc, d), lambda i: (i, 0)),
                  pl.BlockSpec((2 * tc * d // LANES, LANES), lambda i: (i, 0)),
                  pl.BlockSpec((tc, LANES), lambda i: (i, 0)),
                  pl.BlockSpec((1, N_MOD, d), lambda i: (i // tiles_per_batch, 0, 0)),
                  pl.BlockSpec((1, d), lambda i: (0, 0))],
        out_specs=pl.BlockSpec((tc, d), lambda i: (i, 0)),
        out_shape=jax.ShapeDtypeStruct((n_tok, d), F32),
        compiler_params=_cparams(("arbitrary",)),
        name="combine",
    )(x1, ytok, rw, mods, fg)


def _prep_inproj_weights(w_in, gla_up_w, gla_up_b, ml_i_b, ml_f_b):
    d = w_in.shape[0]
    o_gq, o_gk, o_gv, o_gg = 0, GLA_QK_W, 2 * GLA_QK_W, 2 * GLA_QK_W + GLA_V_W
    o_lr = o_gg + GLA_V_W
    o_mqk = o_lr + 2 * GLA_LR
    o_mi = o_mqk + 4 * ML_W
    o_mf = o_mi + 2 * ML_HEADS

    def pad_heads(off):
        w = w_in[:, off:off + GLA_QK_W].reshape(d, GLA_HEADS, GLA_DK)
        return jnp.pad(w, ((0, 0), (0, 0), (0, LANES - GLA_DK))).reshape(d, GLA_HEADS * LANES)

    wg = jnp.concatenate([pad_heads(o_gq), pad_heads(o_gk), w_in[:, o_gv:o_gg], w_in[:, o_gg:o_lr]], axis=1)
    wm = w_in[:, o_mqk:o_mi]
    ws = jnp.concatenate([w_in[:, o_lr:o_mqk], w_in[:, o_mi:o_mf + 2 * ML_HEADS],
                          jnp.zeros((d, LANES - 2 * GLA_LR - 4 * ML_HEADS), w_in.dtype)], axis=1)
    bias = jnp.zeros((LANES,), F32)
    bias = bias.at[_MI0:_MI0 + 2 * ML_HEADS].set(ml_i_b.reshape(-1))
    bias = bias.at[_MF0:_MF0 + 2 * ML_HEADS].set(ml_f_b.reshape(-1))
    up = gla_up_w.reshape(2, GLA_LR, GLA_HEADS, GLA_DK).transpose(2, 0, 1, 3)
    wup = jnp.zeros((GLA_HEADS, 2, LANES, LANES), F32)
    for dr in range(2):
        wup = wup.at[:, dr, dr * GLA_LR:(dr + 1) * GLA_LR, :GLA_DK].set(up[:, dr])
    bup = jnp.pad(gla_up_b.reshape(2, GLA_HEADS, GLA_DK).transpose(1, 0, 2),
                  ((0, 0), (0, 0), (0, LANES - GLA_DK))).reshape(GLA_HEADS, 2, 1, LANES)
    return (wg.astype(BF16), wm.astype(BF16), ws.astype(BF16), ws.T.astype(BF16),
            bias.reshape(1, LANES), bias.reshape(LANES, 1), wup.astype(BF16), bup)


def _layer(x, ctx, mods, norm1_g, w_in, gla_up_w, gla_up_b, gla_norm_g, ml_conv_w, ml_conv_b,
           ml_i_b, ml_f_b, ml_norm_g, w_out, norm2_g, rg_w, rg_b, re_w, re_b, e_w_in, e_w_out, final_g):
    bsz, seq, d = x.shape
    n_tok = bsz * seq
    wg, wm, ws, wst, bcol, brow, wup, bup = _prep_inproj_weights(w_in, gla_up_w, gla_up_b, ml_i_b, ml_f_b)
    g1 = norm1_g.reshape(1, d)
    zg_x, zm_x, zs_x, gcol_x, grow_x = _inproj(x, mods, lambda b: b, g1, wg, wm, ws, wst, bcol, brow, 256)
    zg_c, zm_c, zs_c, gcol_c, grow_c = _inproj(ctx, mods, lambda b: bsz, g1, wg, wm, ws, wst, bcol, brow,
                                               min(256, ctx.shape[1]))
    gla_o = _gla(zg_x, zs_x, zg_c, zs_c, wup, bup, gla_norm_g.reshape(1, -1))
    ml_o = _mlstm(zm_x, gcol_x, grow_x, zm_c, gcol_c, grow_c,
                  ml_conv_w.reshape(9, -1), ml_conv_b.reshape(1, -1), ml_norm_g.reshape(1, -1))

    wr = jnp.zeros((d, LANES), F32).at[:, _G0:_E0].set(rg_w).at[:, _E0:_E0 + N_EXPERTS].set(re_w)
    br = jnp.zeros((1, LANES), F32).at[0, _G0:_E0].set(rg_b).at[0, _E0:_E0 + N_EXPERTS].set(re_b)
    wrh = wr.astype(BF16)
    wrl = (wr - wrh.astype(F32)).astype(BF16)
    x1, h2, ri, rw, cnt = _outproj(x, gla_o, ml_o, mods, w_out[:GLA_V_W].astype(BF16),
                                   w_out[GLA_V_W:].astype(BF16), norm2_g.reshape(1, d), wrh, wrl, br, 256)

    counts = cnt[0, :N_EXPERTS].astype(jnp.int32)
    nblk = (counts + MOE_BLK - 1) // MOE_BLK
    blk_end = jnp.cumsum(nblk)
    blk_start = blk_end - nblk
    n_used = blk_end[-1]
    nb_max = (2 * n_tok) // MOE_BLK + N_EXPERTS
    blk = jnp.arange(nb_max, dtype=jnp.int32)
    blk_c = jnp.minimum(blk, n_used - 1)
    onehot = (blk_c[:, None] >= blk_start[None, :]) & (blk_c[:, None] < blk_end[None, :])
    pick = lambda v: jnp.sum(jnp.where(onehot, v[None, :], 0), axis=1)
    block_e = pick(jnp.arange(N_EXPERTS, dtype=jnp.int32)).astype(jnp.int32)
    block_nv = jnp.clip(pick(counts) - (blk_c - pick(blk_start)) * MOE_BLK, 0, MOE_BLK)
    block_nv = jnp.where(blk < n_used, block_nv, 0).astype(jnp.int32)
    pad_start = (jnp.concatenate([blk_start, blk_end[-1:]]) * MOE_BLK).astype(jnp.int32)
    packed = ri.reshape(n_tok, LANES)[:, 0:2].reshape(-1)
    meta = jnp.stack([n_used, n_used]).astype(jnp.int32)

    ytok = _experts(packed, pad_start, counts, block_e, block_nv, meta, h2, e_w_in, e_w_out, nb_max)
    out = _combine(x1.reshape(n_tok, d), ytok, rw.reshape(n_tok, LANES), mods, final_g.reshape(1, d), seq, 256)
    return out.reshape(bsz, seq, d)


def kernel(x, c, ctx, c_ctx, ada_w, ada_b, norm1_g, w_in, gla_up_w, gla_up_b, gla_norm_g, ml_conv_w, ml_conv_b,
           ml_i_b, ml_f_b, ml_norm_g, w_out, norm2_g, router_group_w, router_group_b, router_expert_w,
           router_expert_b, expert_w_in, expert_w_out, final_norm_g):
    assert ada_w.shape[0] == 1, "single-layer stack"
    bsz, d = c.shape
    cc = jnp.concatenate([c, c_ctx[None, :], jnp.zeros((8 - bsz - 1, d), F32)], axis=0)
    mods = _modulation(cc, ada_w[0], ada_b[0]).reshape(8, N_MOD, d)
    return _layer(x, ctx, mods, norm1_g[0], w_in[0], gla_up_w[0], gla_up_b[0], gla_norm_g[0],
                  ml_conv_w[0], ml_conv_b[0], ml_i_b[0], ml_f_b[0], ml_norm_g[0], w_out[0], norm2_g[0],
                  router_group_w[0], router_group_b[0], router_expert_w[0], router_expert_b[0],
                  expert_w_in[0], expert_w_out[0], final_norm_g)
```

```python
import functools

import jax
import jax.numpy as jnp
from jax import lax
from jax.experimental import pallas as pl
from jax.experimental.pallas import tpu as pltpu

F32 = jnp.float32
BF16 = jnp.bfloat16

D_MODEL = 1024
GRID_W = 64
N_MOD = 6
EPS = 1e-6

GLA_HEADS = 4
GLA_DK = 64
GLA_DV = 128
GLA_LR = 16
GLA_TAU = 16.0
GLA_C = 128
SCAN_UNROLL = 2

ML_HEADS = 4
ML_DH = 128
ML_C = 128

N_GROUPS = 4
EXP_PER_GROUP = 8
N_EXPERTS = N_GROUPS * EXP_PER_GROUP
D_EXPERT = 512
MOE_BLK = 256

GLA_QK_W = GLA_HEADS * GLA_DK
GLA_V_W = GLA_HEADS * GLA_DV
ML_W = ML_HEADS * ML_DH
LANES = 128
VMEM_LIMIT = 56 * 1024 * 1024

_LR0 = 0
_MI0 = 2 * GLA_LR
_MF0 = _MI0 + 2 * ML_HEADS


def _cparams(sem):
    return pltpu.CompilerParams(dimension_semantics=sem, vmem_limit_bytes=VMEM_LIMIT)


def _sigmoid(x):
    return 1.0 / (1.0 + jnp.exp(-x))


def _silu(x):
    return x * _sigmoid(x)


def _log_sigmoid(x):
    return jnp.minimum(x, 0.0) - jnp.log1p(jnp.exp(-jnp.abs(x)))


def _split_dot(a_bf16_exact, x, dims=None):
    x_hi = x.astype(BF16)
    x_lo = (x - x_hi.astype(F32)).astype(BF16)
    if dims is None:
        f = lambda u: jnp.dot(a_bf16_exact, u, preferred_element_type=F32)
    else:
        f = lambda u: lax.dot_general(u, a_bf16_exact, dims, preferred_element_type=F32)
    return f(x_hi) + f(x_lo)


def _mod_kernel(c_ref, w_ref, b_ref, o_ref):
    c = c_ref[...]
    s = _silu(c).astype(BF16)
    o_ref[...] = jnp.dot(s, w_ref[...].astype(BF16), preferred_element_type=F32) + b_ref[...]


def _modulation(cc, ada_w, ada_b):
    rows, d = cc.shape
    n = ada_w.shape[1]
    tn = 1536
    return pl.pallas_call(
        _mod_kernel,
        grid=(n // tn,),
        in_specs=[pl.BlockSpec((rows, d), lambda j: (0, 0)),
                  pl.BlockSpec((d, tn), lambda j: (0, j)),
                  pl.BlockSpec((1, tn), lambda j: (0, j))],
        out_specs=pl.BlockSpec((rows, tn), lambda j: (0, j)),
        out_shape=jax.ShapeDtypeStruct((rows, n), F32),
        compiler_params=_cparams(("arbitrary",)),
        name="mod",
    )(cc, ada_w, ada_b.reshape(1, n))


def _inproj_kernel(x_ref, mod_ref, g_ref, wg_ref, wm_ref, ws_ref, wst_ref, bcol_ref, brow_ref,
                   zg_ref, zm_ref, zs_ref, gcol_ref, grow_ref):
    tm = x_ref.shape[1]
    x = x_ref[0]
    y = x * lax.rsqrt(jnp.mean(x * x, axis=-1, keepdims=True) + EPS) * g_ref[...]
    h = (y * (1.0 + mod_ref[0, 1:2, :]) + mod_ref[0, 0:1, :]).astype(BF16)
    zg_ref[0] = jnp.dot(h, wg_ref[...], preferred_element_type=F32)
    zm_ref[0] = jnp.dot(h, wm_ref[...], preferred_element_type=F32)
    zs = jnp.dot(h, ws_ref[...], preferred_element_type=F32) + bcol_ref[...]
    zst = lax.dot_general(wst_ref[...], h, (((1,), (1,)), ((), ())),
                          preferred_element_type=F32) + brow_ref[...]
    zs_ref[0] = zs

    r = lax.broadcasted_iota(jnp.int32, (tm, tm), 0)
    c = lax.broadcasted_iota(jnp.int32, (tm, tm), 1)
    shift = ML_C.bit_length() - 1
    same = jnp.right_shift(r, shift) == jnp.right_shift(c, shift)
    lower = jnp.where(same & (c <= r), 1.0, 0.0).astype(BF16)
    upper = jnp.where(same & (c >= r), 1.0, 0.0).astype(BF16)
    lsf = _log_sigmoid(zs)
    a_pre = _split_dot(lower, lsf)
    a_suf = _split_dot(upper, lsf)
    lsft = _log_sigmoid(zst)
    dims = (((1,), (0,)), ((), ()))
    a_pre_t = _split_dot(upper, lsft, dims)
    a_suf_t = _split_dot(lower, lsft, dims)

    lane = lax.broadcasted_iota(jnp.int32, (tm, LANES), 1)
    for hd in range(ML_HEADS):
        cols = (a_pre[:, _MF0 + hd:_MF0 + hd + 1],
                a_suf[:, _MF0 + ML_HEADS + hd:_MF0 + ML_HEADS + hd + 1],
                zs[:, _MI0 + hd:_MI0 + hd + 1],
                zs[:, _MI0 + ML_HEADS + hd:_MI0 + ML_HEADS + hd + 1])
        slab = jnp.zeros((tm, LANES), F32)
        for j, col in enumerate(cols):
            slab = jnp.where(lane == j, col, slab)
        gcol_ref[0, :, hd * LANES:(hd + 1) * LANES] = slab
        rows = (a_pre_t[_MF0 + hd:_MF0 + hd + 1, :],
                a_suf_t[_MF0 + ML_HEADS + hd:_MF0 + ML_HEADS + hd + 1, :],
                zst[_MI0 + hd:_MI0 + hd + 1, :],
                zst[_MI0 + ML_HEADS + hd:_MI0 + ML_HEADS + hd + 1, :])
        for j, row in enumerate(rows):
            grow_ref[0, hd, j:j + 1, :] = row
        grow_ref[0, hd, 4:8, :] = jnp.zeros((4, tm), F32)


def _inproj(x, mods, mod_row_of_batch, norm_g, wg, wm, ws, wst, bcol, brow, tm):
    bsz, l, d = x.shape
    assert l % tm == 0 and tm % ML_C == 0
    const = lambda shape: pl.BlockSpec(shape, lambda b, i: (0,) * len(shape))
    return pl.pallas_call(
        _inproj_kernel,
        grid=(bsz, l // tm),
        in_specs=[pl.BlockSpec((1, tm, d), lambda b, i: (b, i, 0)),
                  pl.BlockSpec((1, N_MOD, d), lambda b, i: (mod_row_of_batch(b), 0, 0)),
                  const((1, d)), const(wg.shape), const(wm.shape), const(ws.shape), const(wst.shape),
                  const((1, LANES)), const((LANES, 1))],
        out_specs=[pl.BlockSpec((1, tm, wg.shape[1]), lambda b, i: (b, i, 0)),
                   pl.BlockSpec((1, tm, wm.shape[1]), lambda b, i: (b, i, 0)),
                   pl.BlockSpec((1, tm, LANES), lambda b, i: (b, i, 0)),
                   pl.BlockSpec((1, tm, ML_HEADS * LANES), lambda b, i: (b, i, 0)),
                   pl.BlockSpec((1, ML_HEADS, 8, tm), lambda b, i: (b, 0, 0, i))],
        out_shape=[jax.ShapeDtypeStruct((bsz, l, wg.shape[1]), F32),
                   jax.ShapeDtypeStruct((bsz, l, wm.shape[1]), F32),
                   jax.ShapeDtypeStruct((bsz, l, LANES), F32),
                   jax.ShapeDtypeStruct((bsz, l, ML_HEADS * LANES), F32),
                   jax.ShapeDtypeStruct((bsz, ML_HEADS, 8, l), F32)],
        compiler_params=_cparams(("arbitrary", "arbitrary")),
        name="inproj",
    )(x, mods, norm_g, wg, wm, ws, wst, bcol, brow)


def _round_robin(chains):
    results = [None] * len(chains)
    live = list(enumerate(chains))
    while live:
        still = []
        for idx, chain in live:
            try:
                next(chain)
                still.append((idx, chain))
            except StopIteration as done:
                results[idx] = done.value
        live = still
    return results


def _gla_chunk(q, k, v, zs, wup, bup, state, direction, want_out):
    c = k.shape[0]
    logits = jnp.dot(zs.astype(BF16), wup, preferred_element_type=F32) + bup
    yield
    g = _log_sigmoid(logits) * (1.0 / GLA_TAU)
    r = lax.broadcasted_iota(jnp.int32, (c, c), 0)
    cc = lax.broadcasted_iota(jnp.int32, (c, c), 1)
    causal = (cc <= r) if direction == 0 else (cc >= r)
    b = _split_dot(jnp.where(causal, 1.0, 0.0).astype(BF16), g)
    yield
    b_end = b[c - 1:c, :] if direction == 0 else b[0:1, :]
    kd = (k * jnp.exp(b_end - b)).astype(BF16)
    upd = lax.dot_general(v.astype(BF16), kd, (((0,), (0,)), ((), ())), preferred_element_type=F32)
    s = state[direction]
    state[direction] = jnp.exp(b_end) * s + upd
    if not want_out:
        return None
    b_mid = b[c // 2:c // 2 + 1, :]
    q_in = (q * jnp.exp(b - b_mid)).astype(BF16)
    k_in = (k * jnp.exp(b_mid - b)).astype(BF16)
    att = lax.dot_general(q_in, k_in, (((1,), (1,)), ((), ())), preferred_element_type=F32)
    inter = lax.dot_general((q * jnp.exp(b)).astype(BF16), s.astype(BF16),
                            (((1,), (1,)), ((), ())), preferred_element_type=F32)
    yield
    att = jnp.where(causal, att, 0.0)
    return jnp.dot(att.astype(BF16), v.astype(BF16), preferred_element_type=F32) + inter


def _scan_order(j, n, unroll):
    return [(d, j * unroll + u if d == 0 else n - 1 - (j * unroll + u)) for u in range(unroll) for d in range(2)]


def _gla_kernel(q_ref, k_ref, v_ref, gg_ref, zs_ref, kc_ref, vc_ref, zsc_ref,
                wup_ref, bup_ref, ng_ref, o_ref, s_ref, acc_ref):
    seq = q_ref.shape[1]
    ctx = kc_ref.shape[1]
    n = seq // GLA_C
    nc = ctx // GLA_C
    s_ref[...] = jnp.zeros_like(s_ref)

    def rows(i):
        return pl.ds(pl.multiple_of(i * GLA_C, GLA_C), GLA_C)

    def ctx_step(j, carry):
        order = _scan_order(j, nc, 1)
        ins = [(kc_ref[0, rows(i), :], vc_ref[0, rows(i), :], zsc_ref[0, rows(i), :]) for _, i in order]
        s = [s_ref[0], s_ref[1]]
        _round_robin([_gla_chunk(None, k, v, zs, wup_ref[0, d], bup_ref[0, d], s, d, False)
                      for (d, _), (k, v, zs) in zip(order, ins)])
        s_ref[0] = s[0]
        s_ref[1] = s[1]
        return carry

    lax.fori_loop(0, nc, ctx_step, 0)

    def lat_step(j, carry, second):
        order = _scan_order(j, n, SCAN_UNROLL)
        ins = [(q_ref[0, rows(i), :], k_ref[0, rows(i), :], v_ref[0, rows(i), :], zs_ref[0, rows(i), :])
               for _, i in order]
        prev = [(acc_ref[rows(i), :], gg_ref[0, rows(i), :]) for _, i in order] if second else None
        s = [s_ref[0], s_ref[1]]
        outs = _round_robin([_gla_chunk(q * (GLA_DK ** -0.5), k, v, zs, wup_ref[0, d], bup_ref[0, d], s, d, True)
                             for (d, _), (q, k, v, zs) in zip(order, ins)])
        s_ref[0] = s[0]
        s_ref[1] = s[1]
        for idx, (_, i) in enumerate(order):
            if second:
                total = prev[idx][0] + outs[idx]
                y = total * lax.rsqrt(jnp.mean(total * total, axis=-1, keepdims=True) + EPS) * ng_ref[...]
                o_ref[0, rows(i), :] = (y * _silu(prev[idx][1])).astype(o_ref.dtype)
            else:
                acc_ref[rows(i), :] = outs[idx]
        return carry

    half = n // (2 * SCAN_UNROLL)
    lax.fori_loop(0, half, functools.partial(lat_step, second=False), 0)
    lax.fori_loop(half, 2 * half, functools.partial(lat_step, second=True), 0)


def _gla(zg_x, zs_x, zg_c, zs_c, wup, bup, norm_g):
    bsz, seq, _ = zg_x.shape
    ctx = zg_c.shape[1]
    assert seq % (2 * SCAN_UNROLL * GLA_C) == 0 and ctx % GLA_C == 0
    h = GLA_HEADS

    def col(l, off):
        return pl.BlockSpec((1, l, LANES), lambda b, hd: (b, 0, off + hd))

    return pl.pallas_call(
        _gla_kernel,
        grid=(bsz, h),
        in_specs=[col(seq, 0), col(seq, h), col(seq, 2 * h), col(seq, 3 * h),
                  pl.BlockSpec((1, seq, LANES), lambda b, hd: (b, 0, 0)),
                  col(ctx, h), col(ctx, 2 * h),
                  pl.BlockSpec((1, ctx, LANES), lambda b, hd: (b, 0, 0)),
                  pl.BlockSpec((1, 2, LANES, LANES), lambda b, hd: (hd, 0, 0, 0)),
                  pl.BlockSpec((1, 2, 1, LANES), lambda b, hd: (hd, 0, 0, 0)),
                  pl.BlockSpec((1, LANES), lambda b, hd: (0, hd))],
        out_specs=pl.BlockSpec((1, seq, LANES), lambda b, hd: (b, 0, hd)),
        out_shape=jax.ShapeDtypeStruct((bsz, seq, h * GLA_DV), BF16),
        scratch_shapes=[pltpu.VMEM((2, LANES, LANES), F32), pltpu.VMEM((seq, LANES), F32)],
        compiler_params=_cparams(("arbitrary", "arbitrary")),
        name="gla",
    )(zg_x, zg_x, zg_x, zg_x, zs_x, zg_c, zg_c, zs_c, wup, bup, norm_g)


def _grid_conv_silu(src_ref, dst_ref, w_ref, b_ref, grid_w, scale):
    l = src_ref.shape[1]
    n_rows = l // grid_w
    col = lax.broadcasted_iota(jnp.int32, (grid_w, LANES), 0)

    def body(r, carry):
        acc = jnp.zeros((grid_w, LANES), F32) + b_ref[...]
        for dy in (-1, 0, 1):
            if n_rows == 1 and dy != 0:
                continue
            rr = jnp.clip(r + dy, 0, n_rows - 1)
            blk = src_ref[0, pl.ds(pl.multiple_of(rr * grid_w, grid_w), grid_w), :]
            valid = jnp.logical_and(r + dy >= 0, r + dy < n_rows)
            blk = jnp.where(valid, blk, 0.0)
            for dx in (-1, 0, 1):
                if dx == 0:
                    sh = blk
                else:
                    sh = pltpu.roll(blk, shift=(-dx) % grid_w, axis=0)
                    sh = jnp.where((col + dx >= 0) & (col + dx < grid_w), sh, 0.0)
                tap = (dy + 1) * 3 + (dx + 1)
                acc = acc + sh * w_ref[tap:tap + 1, :]
        dst_ref[pl.ds(pl.multiple_of(r * grid_w, grid_w), grid_w), :] = _silu(acc) * scale
        return carry

    lax.fori_loop(0, n_rows, body, 0)


def _ml_gates(gcol, grow, direction):
    c = gcol.shape[0]
    a_col = gcol[:, direction:direction + 1]
    i_col = gcol[:, 2 + direction:3 + direction]
    a_row = grow[direction:direction + 1, :]
    i_row = grow[2 + direction:3 + direction, :]
    a_end = a_row[:, c - 1:c] if direction == 0 else a_row[:, 0:1]
    return a_col, i_col, a_row, i_row, a_end


def _ml_chunk(q, k, v, gcol, grow, state, mstate, direction, want_out):
    c = k.shape[0]
    a_col, i_col, a_row, i_row, a_end = _ml_gates(gcol, grow, direction)
    lane = lax.broadcasted_iota(jnp.int32, (c, LANES), 1)
    v_aug = jnp.concatenate([v, jnp.where(lane == 0, 1.0, 0.0)], axis=1).astype(BF16)
    g = a_end - a_col + i_col
    g_max = jnp.max(g, axis=0, keepdims=True)
    if want_out:
        r = lax.broadcasted_iota(jnp.int32, (c, c), 0)
        cc = lax.broadcasted_iota(jnp.int32, (c, c), 1)
        causal = (cc <= r) if direction == 0 else (cc >= r)
        dmat = jnp.where(causal, a_col - a_row + i_row, -jnp.inf)
        d_max = jnp.max(dmat, axis=-1, keepdims=True)
        qb = q.astype(BF16)
        qk = lax.dot_general(qb, k.astype(BF16), (((1,), (1,)), ((), ())), preferred_element_type=F32)
    yield
    s, m = state[direction], mstate[direction]
    m_new = jnp.maximum(a_end + m, g_max)
    decay = jnp.exp(a_end + m - m_new)
    kw = (k * jnp.exp(g - m_new)).astype(BF16)
    upd = lax.dot_general(kw, v_aug, (((0,), (0,)), ((), ())), preferred_element_type=F32)
    state[direction] = decay * s + upd
    mstate[direction] = m_new
    if not want_out:
        return None
    inter = a_col + m
    m_t = jnp.maximum(inter, d_max)
    w_inter = jnp.exp(inter - m_t)
    p = (qk * jnp.exp(dmat - m_t)).astype(BF16)
    pv = jnp.dot(p, v_aug, preferred_element_type=F32)
    qs = jnp.dot(qb, s.astype(BF16), preferred_element_type=F32)
    yield
    both = pv + w_inter * qs
    num = both[:, :ML_DH]
    den = both[:, ML_DH:ML_DH + 1]
    return num / jnp.maximum(jnp.abs(den), jnp.exp(-m_t))


def _mlstm_kernel(q_ref, k_ref, v_ref, mo_ref, gcol_ref, grow_ref,
                  kc_ref, vc_ref, gcolc_ref, growc_ref,
                  wq_ref, wk_ref, bq_ref, bk_ref, ng_ref, o_ref,
                  cq_ref, ck_ref, ckc_ref, s_ref, m_ref, acc_ref):
    seq = q_ref.shape[1]
    ctx = kc_ref.shape[1]
    n = seq // ML_C
    nc = ctx // ML_C
    _grid_conv_silu(q_ref, cq_ref, wq_ref, bq_ref, GRID_W, 1.0)
    _grid_conv_silu(k_ref, ck_ref, wk_ref, bk_ref, GRID_W, ML_DH ** -0.5)
    _grid_conv_silu(kc_ref, ckc_ref, wk_ref, bk_ref, ctx, ML_DH ** -0.5)
    s_ref[...] = jnp.zeros_like(s_ref)
    m_ref[...] = jnp.zeros_like(m_ref)

    def rows(i):
        return pl.ds(pl.multiple_of(i * ML_C, ML_C), ML_C)

    def load_state():
        return [s_ref[0], s_ref[1]], [m_ref[0, :, 0:1], m_ref[1, :, 0:1]]

    def store_state(s, m):
        for d in range(2):
            s_ref[d] = s[d]
            m_ref[d] = jnp.broadcast_to(m[d], m_ref.shape[1:])

    def ctx_step(j, carry):
        order = _scan_order(j, nc, 1)
        ins = [(ckc_ref[rows(i), :], vc_ref[0, rows(i), :], gcolc_ref[0, rows(i), :], growc_ref[0, 0, :, rows(i)])
               for _, i in order]
        s, m = load_state()
        _round_robin([_ml_chunk(None, k, v, gcol, grow, s, m, d, False)
                      for (d, _), (k, v, gcol, grow) in zip(order, ins)])
        store_state(s, m)
        return carry

    lax.fori_loop(0, nc, ctx_step, 0)

    def lat_step(j, carry, second):
        order = _scan_order(j, n, SCAN_UNROLL)
        ins = [(cq_ref[rows(i), :], ck_ref[rows(i), :], v_ref[0, rows(i), :], gcol_ref[0, rows(i), :],
                grow_ref[0, 0, :, rows(i)]) for _, i in order]
        prev = [(acc_ref[rows(i), :], mo_ref[0, rows(i), :]) for _, i in order] if second else None
        s, m = load_state()
        outs = _round_robin([_ml_chunk(q, k, v, gcol, grow, s, m, d, True)
                             for (d, _), (q, k, v, gcol, grow) in zip(order, ins)])
        store_state(s, m)
        for idx, (_, i) in enumerate(order):
            if second:
                total = prev[idx][0] + outs[idx]
                y = total * lax.rsqrt(jnp.mean(total * total, axis=-1, keepdims=True) + EPS) * ng_ref[...]
                o_ref[0, rows(i), :] = (_sigmoid(prev[idx][1]) * y).astype(o_ref.dtype)
            else:
                acc_ref[rows(i), :] = outs[idx]
        return carry

    half = n // (2 * SCAN_UNROLL)
    lax.fori_loop(0, half, functools.partial(lat_step, second=False), 0)
    lax.fori_loop(half, 2 * half, functools.partial(lat_step, second=True), 0)


def _mlstm(zm_x, gcol_x, grow_x, zm_c, gcol_c, grow_c, conv_w, conv_b, norm_g):
    bsz, seq, _ = zm_x.shape
    ctx = zm_c.shape[1]
    assert seq % (2 * SCAN_UNROLL * ML_C) == 0 and ctx % ML_C == 0 and seq % GRID_W == 0
    h = ML_HEADS

    def col(l, off):
        return pl.BlockSpec((1, l, LANES), lambda b, hd: (b, 0, off + hd))

    def gates(l):
        return [pl.BlockSpec((1, l, LANES), lambda b, hd: (b, 0, hd)),
                pl.BlockSpec((1, 1, 8, l), lambda b, hd: (b, hd, 0, 0))]

    return pl.pallas_call(
        _mlstm_kernel,
        grid=(bsz, h),
        in_specs=[col(seq, 0), col(seq, h), col(seq, 2 * h), col(seq, 3 * h)] + gates(seq)
                 + [col(ctx, h), col(ctx, 2 * h)] + gates(ctx)
                 + [pl.BlockSpec((9, LANES), lambda b, hd: (0, hd)),
                    pl.BlockSpec((9, LANES), lambda b, hd: (0, h + hd)),
                    pl.BlockSpec((1, LANES), lambda b, hd: (0, hd)),
                    pl.BlockSpec((1, LANES), lambda b, hd: (0, h + hd)),
                    pl.BlockSpec((1, LANES), lambda b, hd: (0, hd))],
        out_specs=pl.BlockSpec((1, seq, LANES), lambda b, hd: (b, 0, hd)),
        out_shape=jax.ShapeDtypeStruct((bsz, seq, h * ML_DH), BF16),
        scratch_shapes=[pltpu.VMEM((seq, LANES), F32), pltpu.VMEM((seq, LANES), F32),
                        pltpu.VMEM((ctx, LANES), F32),
                        pltpu.VMEM((2, LANES, 2 * LANES), F32), pltpu.VMEM((2, 1, LANES), F32),
                        pltpu.VMEM((seq, LANES), F32)],
        compiler_params=_cparams(("arbitrary", "arbitrary")),
        name="mlstm",
    )(zm_x, zm_x, zm_x, zm_x, gcol_x, grow_x, zm_c, zm_c, gcol_c, grow_c,
      conv_w, conv_w, conv_b, conv_b, norm_g)


_G0 = 0
_E0 = N_GROUPS
RANK_BITS = 16
RANK_SPAN = 1 << RANK_BITS
ROW_UNROLL = 8
WAIT_UNROLL = 32


SUBLANES = 8


def _store_token_tiles(ref2d, val):
    n, w = val.shape
    k = w // LANES
    for c in range(k):
        ref2d[pl.ds(c, n, stride=k), :] = val[:, c * LANES:(c + 1) * LANES]


def _load_token_tiles(ref2d, first, n, k, step):
    return jnp.concatenate([ref2d[pl.ds(first + c, n, stride=step), :] for c in range(k)], axis=1)


def _outproj_kernel(x_ref, ga_ref, ml_ref, mod_ref, wa_ref, wb_ref, g2_ref, wrh_ref, wrl_ref, br_ref,
                    x1_ref, h2_ref, ri_ref, rw_ref, cnt_ref, base_ref):
    tm = x_ref.shape[1]

    @pl.when((pl.program_id(0) == 0) & (pl.program_id(1) == 0))
    def _():
        base_ref[...] = jnp.zeros_like(base_ref)

    mix = (jnp.dot(ga_ref[0], wa_ref[...], preferred_element_type=F32)
           + jnp.dot(ml_ref[0], wb_ref[...], preferred_element_type=F32))
    x1 = x_ref[0] + mod_ref[0, 2:3, :] * mix
    x1_ref[0] = x1
    y = x1 * lax.rsqrt(jnp.mean(x1 * x1, axis=-1, keepdims=True) + EPS) * g2_ref[...]
    h2 = y * (1.0 + mod_ref[0, 4:5, :]) + mod_ref[0, 3:4, :]
    _store_token_tiles(h2_ref, h2)

    h_hi = h2.astype(BF16)
    h_lo = (h2 - h_hi.astype(F32)).astype(BF16)
    logits = (jnp.dot(h_hi, wrh_ref[...], preferred_element_type=F32)
              + jnp.dot(h_lo, wrh_ref[...], preferred_element_type=F32)
              + jnp.dot(h_hi, wrl_ref[...], preferred_element_type=F32)) + br_ref[...]

    lane = lax.broadcasted_iota(jnp.int32, (tm, LANES), 1).astype(F32)
    neg = -jnp.inf
    big = float(LANES)
    is_g = lane < float(_E0)
    lg = jnp.where(is_g, logits, neg)
    gmax = jnp.max(lg, axis=-1, keepdims=True)
    gidx = jnp.min(jnp.where(lg == gmax, lane, big), axis=-1, keepdims=True)
    gw = 1.0 / jnp.sum(jnp.where(is_g, jnp.exp(logits - gmax), 0.0), axis=-1, keepdims=True)
    lo = float(_E0) + float(EXP_PER_GROUP) * gidx
    le = jnp.where((lane >= lo) & (lane < lo + float(EXP_PER_GROUP)), logits, neg)
    v1 = jnp.max(le, axis=-1, keepdims=True)
    i1 = jnp.min(jnp.where(le == v1, lane, big), axis=-1, keepdims=True)
    le2 = jnp.where(lane == i1, neg, le)
    v2 = jnp.max(le2, axis=-1, keepdims=True)
    i2 = jnp.min(jnp.where(le2 == v2, lane, big), axis=-1, keepdims=True)
    t = jnp.exp(v2 - v1)
    w1 = gw / (1.0 + t)
    w2 = gw * t / (1.0 + t)
    e1 = i1 - float(_E0)
    e2 = i2 - float(_E0)

    oh1 = lane == e1
    oh2 = lane == e2
    oh = jnp.where(oh1 | oh2, 1.0, 0.0)
    r = lax.broadcasted_iota(jnp.int32, (tm, tm), 0)
    c = lax.broadcasted_iota(jnp.int32, (tm, tm), 1)
    strict = jnp.where(c < r, 1.0, 0.0).astype(BF16)
    before = jnp.dot(strict, oh.astype(BF16), preferred_element_type=F32) + base_ref[...]
    rank1 = jnp.sum(jnp.where(oh1, before, 0.0), axis=-1, keepdims=True)
    rank2 = jnp.sum(jnp.where(oh2, before, 0.0), axis=-1, keepdims=True)
    total = base_ref[...] + jnp.sum(oh, axis=0, keepdims=True)
    base_ref[...] = total
    cnt_ref[...] = total

    ids = jnp.where(lane == 0.0, e1 * float(RANK_SPAN) + rank1,
                    jnp.where(lane == 1.0, e2 * float(RANK_SPAN) + rank2, 0.0))
    ri_ref[0] = ids.astype(jnp.int32)
    rw_ref[0] = jnp.where(lane == 0.0, w1, jnp.where(lane == 1.0, w2, 0.0))


def _outproj(x, gla_o, ml_o, mods, wa, wb, g2, wrh, wrl, br, tm):
    bsz, seq, d = x.shape
    const = lambda shape: pl.BlockSpec(shape, lambda b, i: (0,) * len(shape))
    tile = lambda w: pl.BlockSpec((1, tm, w), lambda b, i: (b, i, 0))
    return pl.pallas_call(
        _outproj_kernel,
        grid=(bsz, seq // tm),
        in_specs=[tile(d), tile(gla_o.shape[2]), tile(ml_o.shape[2]),
                  pl.BlockSpec((1, N_MOD, d), lambda b, i: (b, 0, 0)),
                  const(wa.shape), const(wb.shape), const((1, d)),
                  const(wrh.shape), const(wrl.shape), const((1, LANES))],
        out_specs=[tile(d),
                   pl.BlockSpec((tm * d // LANES, LANES), lambda b, i: (b * (seq // tm) + i, 0)),
                   tile(LANES), tile(LANES), const((1, LANES))],
        out_shape=[jax.ShapeDtypeStruct((bsz, seq, d), F32),
                   jax.ShapeDtypeStruct((bsz * seq * d // LANES, LANES), F32),
                   jax.ShapeDtypeStruct((bsz, seq, LANES), jnp.int32),
                   jax.ShapeDtypeStruct((bsz, seq, LANES), F32),
                   jax.ShapeDtypeStruct((1, LANES), F32)],
        scratch_shapes=[pltpu.VMEM((1, LANES), F32)],
        compiler_params=_cparams(("arbitrary", "arbitrary")),
        name="outproj",
    )(x, gla_o, ml_o, mods, wa, wb, g2, wrh, wrl, br)


def _experts_kernel(pk_ref, ps_ref, cnt_ref, be_ref, nv_ref, meta_ref, h_hbm, w1_ref, w2_ref, ytok_hbm,
                    src_ref, xbuf, ybuf, w1c_ref, w2c_ref, gsem, ssem):
    i = pl.program_id(0)
    n_steps = pl.num_programs(0)
    n_used = meta_ref[0]
    tr = SUBLANES
    n_tok = h_hbm.shape[0] // tr
    n_rows = src_ref.shape[0]
    blk = xbuf.shape[1] // tr
    slot = lax.rem(i, 2)

    def slab(j):
        return pl.ds(pl.multiple_of(j * tr, tr), tr)

    def gather_copy(tok, s, r):
        return pltpu.make_async_copy(h_hbm.at[slab(tok), :], xbuf.at[s, slab(r), :], gsem.at[s])

    def scatter_copy(a, s, r):
        return pltpu.make_async_copy(ybuf.at[s, slab(r), :], ytok_hbm.at[slab(a), :], ssem.at[s])

    def rows_loop(body):
        def step(r, c):
            body(r)
            return c
        lax.fori_loop(0, blk, step, 0, unroll=ROW_UNROLL)

    def issue_gather(b, s):
        def one(r):
            tok = lax.shift_right_logical(src_ref[b * blk + r], 1)
            gather_copy(jnp.minimum(tok, n_tok - 1), s, r).start()
        rows_loop(one)

    def wait_rows(copy):
        def step(r, c):
            copy.wait()
            return c
        lax.fori_loop(0, blk, step, 0, unroll=WAIT_UNROLL)

    def wait_gather(s):
        wait_rows(gather_copy(0, s, 0))

    def wait_scatter(s):
        wait_rows(scatter_copy(0, s, 0))

    @pl.when(i == 0)
    def _():
        ybuf[...] = jnp.zeros_like(ybuf)
        for s in range(2):
            tail = ytok_hbm.at[pl.ds((2 * n_tok + s * blk) * tr, blk * tr), :]
            cp = pltpu.make_async_copy(ybuf.at[s], tail, ssem.at[s])
            cp.start()
            cp.wait()

        def put(a, c):
            p = pk_ref[a]
            src_ref[ps_ref[lax.shift_right_logical(p, RANK_BITS)] + (p & (RANK_SPAN - 1))] = a
            return c
        lax.fori_loop(0, 2 * n_tok, put, 0, unroll=16)

        def pad(j, c):
            src_ref[j] = 2 * n_tok + (j & (2 * blk - 1))
            return c

        def pad_expert(e, c):
            lax.fori_loop(ps_ref[e] + cnt_ref[e], ps_ref[e + 1], pad, 0)
            return c
        lax.fori_loop(0, cnt_ref.shape[0], pad_expert, 0)
        used_end = ps_ref[cnt_ref.shape[0]]
        lax.fori_loop(used_end, jnp.minimum(used_end + blk, n_rows), pad, 0)
        issue_gather(0, 0)

    @pl.when(i < n_used)
    def _():
        wait_gather(slot)
        issue_gather(jnp.minimum(i + 1, n_steps - 1), 1 - slot)

        @pl.when((i == 0) | (be_ref[i] != be_ref[jnp.maximum(i - 1, 0)]))
        def _():
            w1c_ref[...] = w1_ref[0].astype(BF16)
            w2c_ref[...] = w2_ref[0].astype(BF16)

        @pl.when(i >= 2)
        def _():
            wait_scatter(slot)

        row = lax.broadcasted_iota(jnp.int32, (blk, 1), 0)
        x = _load_token_tiles(xbuf.at[slot], 0, blk, tr, tr)
        x = jnp.where(row < nv_ref[i], x, 0.0).astype(BF16)
        h = jnp.dot(x, w1c_ref[...], preferred_element_type=F32)
        a = (_silu(h[:, :D_EXPERT]) * h[:, D_EXPERT:]).astype(BF16)
        _store_token_tiles(ybuf.at[slot], jnp.dot(a, w2c_ref[...], preferred_element_type=F32))
        rows_loop(lambda r: scatter_copy(src_ref[i * blk + r], slot, r).start())

    @pl.when(i == n_steps - 1)
    def _():
        wait_gather(lax.rem(n_used, 2))
        wait_scatter(0)
        wait_scatter(1)


def _experts(packed, pad_start, counts, block_e, block_nv, meta, h2, w_in, w_out, nb):
    d = w_in.shape[1]
    tr = d // LANES
    assert tr == SUBLANES, "a token row must fill exactly one (8, 128) tile"
    n_tok = h2.shape[0] // tr
    de2 = w_in.shape[2]
    n_rows = nb * MOE_BLK
    assert 2 * n_tok >= 2 * MOE_BLK
    assert MOE_BLK & (MOE_BLK - 1) == 0
    wmap = lambda i, pk, ps, cnt, be, nv, meta: (be[i], 0, 0)
    return pl.pallas_call(
        _experts_kernel,
        grid_spec=pltpu.PrefetchScalarGridSpec(
            num_scalar_prefetch=6, grid=(nb,),
            in_specs=[pl.BlockSpec(memory_space=pl.ANY),
                      pl.BlockSpec((1, d, de2), wmap),
                      pl.BlockSpec((1, de2 // 2, d), wmap)],
            out_specs=pl.BlockSpec(memory_space=pl.ANY),
            scratch_shapes=[pltpu.SMEM((n_rows,), jnp.int32),
                            pltpu.VMEM((2, MOE_BLK * tr, LANES), F32), pltpu.VMEM((2, MOE_BLK * tr, LANES), F32),
                            pltpu.VMEM((d, de2), BF16), pltpu.VMEM((de2 // 2, d), BF16),
                            pltpu.SemaphoreType.DMA((2,)), pltpu.SemaphoreType.DMA((2,))]),
        out_shape=jax.ShapeDtypeStruct(((2 * n_tok + 2 * MOE_BLK) * tr, LANES), F32),
        compiler_params=_cparams(("arbitrary",)),
        name="experts",
    )(packed, pad_start, counts, block_e, block_nv, meta, h2, w_in, w_out)


def _combine_kernel(x1_ref, y_ref, rw_ref, mod_ref, fg_ref, o_ref):
    tc, d = x1_ref.shape
    tr = d // LANES
    y1 = _load_token_tiles(y_ref, 0, tc, tr, 2 * tr)
    y2 = _load_token_tiles(y_ref, tr, tc, tr, 2 * tr)
    moe = rw_ref[:, 0:1] * y1 + rw_ref[:, 1:2] * y2
    x2 = x1_ref[...] + mod_ref[0, 5:6, :] * moe
    o_ref[...] = x2 * lax.rsqrt(jnp.mean(x2 * x2, axis=-1, keepdims=True) + EPS) * fg_ref[...]


def _combine(x1, ytok, rw, mods, fg, tokens_per_batch, tc):
    n_tok, d = x1.shape
    tiles_per_batch = tokens_per_batch // tc
    return pl.pallas_call(
        _combine_kernel,
        grid=(n_tok // tc,),
        in_specs=[pl.BlockSpec((tc, d), lambda i: (i, 0)),
                  pl.BlockSpec((2 * tc * d // LANES, LANES), lambda i: (i, 0)),
                  pl.BlockSpec((tc, LANES), lambda i: (i, 0)),
                  pl.BlockSpec((1, N_MOD, d), lambda i: (i // tiles_per_batch, 0, 0)),
                  pl.BlockSpec((1, d), lambda i: (0, 0))],
        out_specs=pl.BlockSpec((tc, d), lambda i: (i, 0)),
        out_shape=jax.ShapeDtypeStruct((n_tok, d), F32),
        compiler_params=_cparams(("arbitrary",)),
        name="combine",
    )(x1, ytok, rw, mods, fg)


def _prep_inproj_weights(w_in, gla_up_w, gla_up_b, ml_i_b, ml_f_b):
    d = w_in.shape[0]
    o_gq, o_gk, o_gv, o_gg = 0, GLA_QK_W, 2 * GLA_QK_W, 2 * GLA_QK_W + GLA_V_W
    o_lr = o_gg + GLA_V_W
    o_mqk = o_lr + 2 * GLA_LR
    o_mi = o_mqk + 4 * ML_W
    o_mf = o_mi + 2 * ML_HEADS

    def pad_heads(off):
        w = w_in[:, off:off + GLA_QK_W].reshape(d, GLA_HEADS, GLA_DK)
        return jnp.pad(w, ((0, 0), (0, 0), (0, LANES - GLA_DK))).reshape(d, GLA_HEADS * LANES)

    wg = jnp.concatenate([pad_heads(o_gq), pad_heads(o_gk), w_in[:, o_gv:o_gg], w_in[:, o_gg:o_lr]], axis=1)
    wm = w_in[:, o_mqk:o_mi]
    ws = jnp.concatenate([w_in[:, o_lr:o_mqk], w_in[:, o_mi:o_mf + 2 * ML_HEADS],
                          jnp.zeros((d, LANES - 2 * GLA_LR - 4 * ML_HEADS), w_in.dtype)], axis=1)
    bias = jnp.zeros((LANES,), F32)
    bias = bias.at[_MI0:_MI0 + 2 * ML_HEADS].set(ml_i_b.reshape(-1))
    bias = bias.at[_MF0:_MF0 + 2 * ML_HEADS].set(ml_f_b.reshape(-1))
    up = gla_up_w.reshape(2, GLA_LR, GLA_HEADS, GLA_DK).transpose(2, 0, 1, 3)
    wup = jnp.zeros((GLA_HEADS, 2, LANES, LANES), F32)
    for dr in range(2):
        wup = wup.at[:, dr, dr * GLA_LR:(dr + 1) * GLA_LR, :GLA_DK].set(up[:, dr])
    bup = jnp.pad(gla_up_b.reshape(2, GLA_HEADS, GLA_DK).transpose(1, 0, 2),
                  ((0, 0), (0, 0), (0, LANES - GLA_DK))).reshape(GLA_HEADS, 2, 1, LANES)
    return (wg.astype(BF16), wm.astype(BF16), ws.astype(BF16), ws.T.astype(BF16),
            bias.reshape(1, LANES), bias.reshape(LANES, 1), wup.astype(BF16), bup)


def _layer(x, ctx, mods, norm1_g, w_in, gla_up_w, gla_up_b, gla_norm_g, ml_conv_w, ml_conv_b,
           ml_i_b, ml_f_b, ml_norm_g, w_out, norm2_g, rg_w, rg_b, re_w, re_b, e_w_in, e_w_out, final_g):
    bsz, seq, d = x.shape
    n_tok = bsz * seq
    wg, wm, ws, wst, bcol, brow, wup, bup = _prep_inproj_weights(w_in, gla_up_w, gla_up_b, ml_i_b, ml_f_b)
    g1 = norm1_g.reshape(1, d)
    zg_x, zm_x, zs_x, gcol_x, grow_x = _inproj(x, mods, lambda b: b, g1, wg, wm, ws, wst, bcol, brow, 256)
    zg_c, zm_c, zs_c, gcol_c, grow_c = _inproj(ctx, mods, lambda b: bsz, g1, wg, wm, ws, wst, bcol, brow,
                                               min(256, ctx.shape[1]))
    gla_o = _gla(zg_x, zs_x, zg_c, zs_c, wup, bup, gla_norm_g.reshape(1, -1))
    ml_o = _mlstm(zm_x, gcol_x, grow_x, zm_c, gcol_c, grow_c,
                  ml_conv_w.reshape(9, -1), ml_conv_b.reshape(1, -1), ml_norm_g.reshape(1, -1))

    wr = jnp.zeros((d, LANES), F32).at[:, _G0:_E0].set(rg_w).at[:, _E0:_E0 + N_EXPERTS].set(re_w)
    br = jnp.zeros((1, LANES), F32).at[0, _G0:_E0].set(rg_b).at[0, _E0:_E0 + N_EXPERTS].set(re_b)
    wrh = wr.astype(BF16)
    wrl = (wr - wrh.astype(F32)).astype(BF16)
    x1, h2, ri, rw, cnt = _outproj(x, gla_o, ml_o, mods, w_out[:GLA_V_W].astype(BF16),
                                   w_out[GLA_V_W:].astype(BF16), norm2_g.reshape(1, d), wrh, wrl, br, 256)

    counts = cnt[0, :N_EXPERTS].astype(jnp.int32)
    nblk = (counts + MOE_BLK - 1) // MOE_BLK
    blk_end = jnp.cumsum(nblk)
    blk_start = blk_end - nblk
    n_used = blk_end[-1]
    nb_max = (2 * n_tok) // MOE_BLK + N_EXPERTS
    blk = jnp.arange(nb_max, dtype=jnp.int32)
    blk_c = jnp.minimum(blk, n_used - 1)
    onehot = (blk_c[:, None] >= blk_start[None, :]) & (blk_c[:, None] < blk_end[None, :])
    pick = lambda v: jnp.sum(jnp.where(onehot, v[None, :], 0), axis=1)
    block_e = pick(jnp.arange(N_EXPERTS, dtype=jnp.int32)).astype(jnp.int32)
    block_nv = jnp.clip(pick(counts) - (blk_c - pick(blk_start)) * MOE_BLK, 0, MOE_BLK)
    block_nv = jnp.where(blk < n_used, block_nv, 0).astype(jnp.int32)
    pad_start = (jnp.concatenate([blk_start, blk_end[-1:]]) * MOE_BLK).astype(jnp.int32)
    packed = ri.reshape(n_tok, LANES)[:, 0:2].reshape(-1)
    meta = jnp.stack([n_used, n_used]).astype(jnp.int32)

    ytok = _experts(packed, pad_start, counts, block_e, block_nv, meta, h2, e_w_in, e_w_out, nb_max)
    out = _combine(x1.reshape(n_tok, d), ytok, rw.reshape(n_tok, LANES), mods, final_g.reshape(1, d), seq, 256)
    return out.reshape(bsz, seq, d)


def kernel(x, c, ctx, c_ctx, ada_w, ada_b, norm1_g, w_in, gla_up_w, gla_up_b, gla_norm_g, ml_conv_w, ml_conv_b,
           ml_i_b, ml_f_b, ml_norm_g, w_out, norm2_g, router_group_w, router_group_b, router_expert_w,
           router_expert_b, expert_w_in, expert_w_out, final_norm_g):
    assert ada_w.shape[0] == 1, "single-layer stack"
    bsz, d = c.shape
    cc = jnp.concatenate([c, c_ctx[None, :], jnp.zeros((8 - bsz - 1, d), F32)], axis=0)
    mods = _modulation(cc, ada_w[0], ada_b[0]).reshape(8, N_MOD, d)
    return _layer(x, ctx, mods, norm1_g[0], w_in[0], gla_up_w[0], gla_up_b[0], gla_norm_g[0],
                  ml_conv_w[0], ml_conv_b[0], ml_i_b[0], ml_f_b[0], ml_norm_g[0], w_out[0], norm2_g[0],
                  router_group_w[0], router_group_b[0], router_expert_w[0], router_expert_b[0],
                  expert_w_in[0], expert_w_out[0], final_norm_g)
```

```python
import functools

import jax
import jax.numpy as jnp
from jax import lax
from jax.experimental import pallas as pl
from jax.experimental.pallas import tpu as pltpu

F32 = jnp.float32
BF16 = jnp.bfloat16

D_MODEL = 1024
GRID_W = 64
N_MOD = 6
EPS = 1e-6

GLA_HEADS = 4
GLA_DK = 64
GLA_DV = 128
GLA_LR = 16
GLA_TAU = 16.0
GLA_C = 128
SCAN_UNROLL = 2

ML_HEADS = 4
ML_DH = 128
ML_C = 128

N_GROUPS = 4
EXP_PER_GROUP = 8
N_EXPERTS = N_GROUPS * EXP_PER_GROUP
D_EXPERT = 512
MOE_BLK = 256

GLA_QK_W = GLA_HEADS * GLA_DK
GLA_V_W = GLA_HEADS * GLA_DV
ML_W = ML_HEADS * ML_DH
LANES = 128
VMEM_LIMIT = 56 * 1024 * 1024

_LR0 = 0
_MI0 = 2 * GLA_LR
_MF0 = _MI0 + 2 * ML_HEADS


def _cparams(sem):
    return pltpu.CompilerParams(dimension_semantics=sem, vmem_limit_bytes=VMEM_LIMIT)


def _sigmoid(x):
    return 1.0 / (1.0 + jnp.exp(-x))


def _silu(x):
    return x * _sigmoid(x)


def _log_sigmoid(x):
    return jnp.minimum(x, 0.0) - jnp.log1p(jnp.exp(-jnp.abs(x)))


def _split_dot(a_bf16_exact, x, dims=None):
    x_hi = x.astype(BF16)
    x_lo = (x - x_hi.astype(F32)).astype(BF16)
    if dims is None:
        f = lambda u: jnp.dot(a_bf16_exact, u, preferred_element_type=F32)
    else:
        f = lambda u: lax.dot_general(u, a_bf16_exact, dims, preferred_element_type=F32)
    return f(x_hi) + f(x_lo)


def _mod_kernel(c_ref, w_ref, b_ref, o_ref):
    c = c_ref[...]
    s = _silu(c).astype(BF16)
    o_ref[...] = jnp.dot(s, w_ref[...].astype(BF16), preferred_element_type=F32) + b_ref[...]


def _modulation(cc, ada_w, ada_b):
    rows, d = cc.shape
    n = ada_w.shape[1]
    tn = 1536
    return pl.pallas_call(
        _mod_kernel,
        grid=(n // tn,),
        in_specs=[pl.BlockSpec((rows, d), lambda j: (0, 0)),
                  pl.BlockSpec((d, tn), lambda j: (0, j)),
                  pl.BlockSpec((1, tn), lambda j: (0, j))],
        out_specs=pl.BlockSpec((rows, tn), lambda j: (0, j)),
        out_shape=jax.ShapeDtypeStruct((rows, n), F32),
        compiler_params=_cparams(("arbitrary",)),
        name="mod",
    )(cc, ada_w, ada_b.reshape(1, n))


def _inproj_kernel(x_ref, mod_ref, g_ref, wg_ref, wm_ref, ws_ref, wst_ref, bcol_ref, brow_ref,
                   zg_ref, zm_ref, zs_ref, gcol_ref, grow_ref):
    tm = x_ref.shape[1]
    x = x_ref[0]
    y = x * lax.rsqrt(jnp.mean(x * x, axis=-1, keepdims=True) + EPS) * g_ref[...]
    h = (y * (1.0 + mod_ref[0, 1:2, :]) + mod_ref[0, 0:1, :]).astype(BF16)
    zg_ref[0] = jnp.dot(h, wg_ref[...], preferred_element_type=F32)
    zm_ref[0] = jnp.dot(h, wm_ref[...], preferred_element_type=F32)
    zs = jnp.dot(h, ws_ref[...], preferred_element_type=F32) + bcol_ref[...]
    zst = lax.dot_general(wst_ref[...], h, (((1,), (1,)), ((), ())),
                          preferred_element_type=F32) + brow_ref[...]
    zs_ref[0] = zs

    r = lax.broadcasted_iota(jnp.int32, (tm, tm), 0)
    c = lax.broadcasted_iota(jnp.int32, (tm, tm), 1)
    shift = ML_C.bit_length() - 1
    same = jnp.right_shift(r, shift) == jnp.right_shift(c, shift)
    lower = jnp.where(same & (c <= r), 1.0, 0.0).astype(BF16)
    upper = jnp.where(same & (c >= r), 1.0, 0.0).astype(BF16)
    lsf = _log_sigmoid(zs)
    a_pre = _split_dot(lower, lsf)
    a_suf = _split_dot(upper, lsf)
    lsft = _log_sigmoid(zst)
    dims = (((1,), (0,)), ((), ()))
    a_pre_t = _split_dot(upper, lsft, dims)
    a_suf_t = _split_dot(lower, lsft, dims)

    lane = lax.broadcasted_iota(jnp.int32, (tm, LANES), 1)
    for hd in range(ML_HEADS):
        cols = (a_pre[:, _MF0 + hd:_MF0 + hd + 1],
                a_suf[:, _MF0 + ML_HEADS + hd:_MF0 + ML_HEADS + hd + 1],
                zs[:, _MI0 + hd:_MI0 + hd + 1],
                zs[:, _MI0 + ML_HEADS + hd:_MI0 + ML_HEADS + hd + 1])
        slab = jnp.zeros((tm, LANES), F32)
        for j, col in enumerate(cols):
            slab = jnp.where(lane == j, col, slab)
        gcol_ref[0, :, hd * LANES:(hd + 1) * LANES] = slab
        rows = (a_pre_t[_MF0 + hd:_MF0 + hd + 1, :],
                a_suf_t[_MF0 + ML_HEADS + hd:_MF0 + ML_HEADS + hd + 1, :],
                zst[_MI0 + hd:_MI0 + hd + 1, :],
                zst[_MI0 + ML_HEADS + hd:_MI0 + ML_HEADS + hd + 1, :])
        for j, row in enumerate(rows):
            grow_ref[0, hd, j:j + 1, :] = row
        grow_ref[0, hd, 4:8, :] = jnp.zeros((4, tm), F32)


def _inproj(x, mods, mod_row_of_batch, norm_g, wg, wm, ws, wst, bcol, brow, tm):
    bsz, l, d = x.shape
    assert l % tm == 0 and tm % ML_C == 0
    const = lambda shape: pl.BlockSpec(shape, lambda b, i: (0,) * len(shape))
    return pl.pallas_call(
        _inproj_kernel,
        grid=(bsz, l // tm),
        in_specs=[pl.BlockSpec((1, tm, d), lambda b, i: (b, i, 0)),
                  pl.BlockSpec((1, N_MOD, d), lambda b, i: (mod_row_of_batch(b), 0, 0)),
                  const((1, d)), const(wg.shape), const(wm.shape), const(ws.shape), const(wst.shape),
                  const((1, LANES)), const((LANES, 1))],
        out_specs=[pl.BlockSpec((1, tm, wg.shape[1]), lambda b, i: (b, i, 0)),
                   pl.BlockSpec((1, tm, wm.shape[1]), lambda b, i: (b, i, 0)),
                   pl.BlockSpec((1, tm, LANES), lambda b, i: (b, i, 0)),
                   pl.BlockSpec((1, tm, ML_HEADS * LANES), lambda b, i: (b, i, 0)),
                   pl.BlockSpec((1, ML_HEADS, 8, tm), lambda b, i: (b, 0, 0, i))],
        out_shape=[jax.ShapeDtypeStruct((bsz, l, wg.shape[1]), F32),
                   jax.ShapeDtypeStruct((bsz, l, wm.shape[1]), F32),
                   jax.ShapeDtypeStruct((bsz, l, LANES), F32),
                   jax.ShapeDtypeStruct((bsz, l, ML_HEADS * LANES), F32),
                   jax.ShapeDtypeStruct((bsz, ML_HEADS, 8, l), F32)],
        compiler_params=_cparams(("arbitrary", "arbitrary")),
        name="inproj",
    )(x, mods, norm_g, wg, wm, ws, wst, bcol, brow)


def _round_robin(chains):
    results = [None] * len(chains)
    live = list(enumerate(chains))
    while live:
        still = []
        for idx, chain in live:
            try:
                next(chain)
                still.append((idx, chain))
            except StopIteration as done:
                results[idx] = done.value
        live = still
    return results


def _gla_chunk(q, k, v, zs, wup, bup, state, direction, want_out):
    c = k.shape[0]
    logits = jnp.dot(zs.astype(BF16), wup, preferred_element_type=F32) + bup
    yield
    g = _log_sigmoid(logits) * (1.0 / GLA_TAU)
    r = lax.broadcasted_iota(jnp.int32, (c, c), 0)
    cc = lax.broadcasted_iota(jnp.int32, (c, c), 1)
    causal = (cc <= r) if direction == 0 else (cc >= r)
    b = _split_dot(jnp.where(causal, 1.0, 0.0).astype(BF16), g)
    yield
    b_end = b[c - 1:c, :] if direction == 0 else b[0:1, :]
    kd = (k * jnp.exp(b_end - b)).astype(BF16)
    upd = lax.dot_general(v.astype(BF16), kd, (((0,), (0,)), ((), ())), preferred_element_type=F32)
    s = state[direction]
    state[direction] = jnp.exp(b_end) * s + upd
    if not want_out:
        return None
    b_mid = b[c // 2:c // 2 + 1, :]
    q_in = (q * jnp.exp(b - b_mid)).astype(BF16)
    k_in = (k * jnp.exp(b_mid - b)).astype(BF16)
    att = lax.dot_general(q_in, k_in, (((1,), (1,)), ((), ())), preferred_element_type=F32)
    inter = lax.dot_general((q * jnp.exp(b)).astype(BF16), s.astype(BF16),
                            (((1,), (1,)), ((), ())), preferred_element_type=F32)
    yield
    att = jnp.where(causal, att, 0.0)
    return jnp.dot(att.astype(BF16), v.astype(BF16), preferred_element_type=F32) + inter


def _scan_order(j, n, unroll):
    return [(d, j * unroll + u if d == 0 else n - 1 - (j * unroll + u)) for u in range(unroll) for d in range(2)]


def _gla_kernel(q_ref, k_ref, v_ref, gg_ref, zs_ref, kc_ref, vc_ref, zsc_ref,
                wup_ref, bup_ref, ng_ref, o_ref, s_ref, acc_ref):
    seq = q_ref.shape[1]
    ctx = kc_ref.shape[1]
    n = seq // GLA_C
    nc = ctx // GLA_C
    s_ref[...] = jnp.zeros_like(s_ref)

    def rows(i):
        return pl.ds(pl.multiple_of(i * GLA_C, GLA_C), GLA_C)

    def ctx_step(j, carry):
        order = _scan_order(j, nc, 1)
        ins = [(kc_ref[0, rows(i), :], vc_ref[0, rows(i), :], zsc_ref[0, rows(i), :]) for _, i in order]
        s = [s_ref[0], s_ref[1]]
        _round_robin([_gla_chunk(None, k, v, zs, wup_ref[0, d], bup_ref[0, d], s, d, False)
                      for (d, _), (k, v, zs) in zip(order, ins)])
        s_ref[0] = s[0]
        s_ref[1] = s[1]
        return carry

    lax.fori_loop(0, nc, ctx_step, 0)

    def lat_step(j, carry, second):
        order = _scan_order(j, n, SCAN_UNROLL)
        ins = [(q_ref[0, rows(i), :], k_ref[0, rows(i), :], v_ref[0, rows(i), :], zs_ref[0, rows(i), :])
               for _, i in order]
        prev = [(acc_ref[rows(i), :], gg_ref[0, rows(i), :]) for _, i in order] if second else None
        s = [s_ref[0], s_ref[1]]
        outs = _round_robin([_gla_chunk(q * (GLA_DK ** -0.5), k, v, zs, wup_ref[0, d], bup_ref[0, d], s, d, True)
                             for (d, _), (q, k, v, zs) in zip(order, ins)])
        s_ref[0] = s[0]
        s_ref[1] = s[1]
        for idx, (_, i) in enumerate(order):
            if second:
                total = prev[idx][0] + outs[idx]
                y = total * lax.rsqrt(jnp.mean(total * total, axis=-1, keepdims=True) + EPS) * ng_ref[...]
                o_ref[0, rows(i), :] = (y * _silu(prev[idx][1])).astype(o_ref.dtype)
            else:
                acc_ref[rows(i), :] = outs[idx]
        return carry

    half = n // (2 * SCAN_UNROLL)
    lax.fori_loop(0, half, functools.partial(lat_step, second=False), 0)
    lax.fori_loop(half, 2 * half, functools.partial(lat_step, second=True), 0)


def _gla(zg_x, zs_x, zg_c, zs_c, wup, bup, norm_g):
    bsz, seq, _ = zg_x.shape
    ctx = zg_c.shape[1]
    assert seq % (2 * SCAN_UNROLL * GLA_C) == 0 and ctx % GLA_C == 0
    h = GLA_HEADS

    def col(l, off):
        return pl.BlockSpec((1, l, LANES), lambda b, hd: (b, 0, off + hd))

    return pl.pallas_call(
        _gla_kernel,
        grid=(bsz, h),
        in_specs=[col(seq, 0), col(seq, h), col(seq, 2 * h), col(seq, 3 * h),
                  pl.BlockSpec((1, seq, LANES), lambda b, hd: (b, 0, 0)),
                  col(ctx, h), col(ctx, 2 * h),
                  pl.BlockSpec((1, ctx, LANES), lambda b, hd: (b, 0, 0)),
                  pl.BlockSpec((1, 2, LANES, LANES), lambda b, hd: (hd, 0, 0, 0)),
                  pl.BlockSpec((1, 2, 1, LANES), lambda b, hd: (hd, 0, 0, 0)),
                  pl.BlockSpec((1, LANES), lambda b, hd: (0, hd))],
        out_specs=pl.BlockSpec((1, seq, LANES), lambda b, hd: (b, 0, hd)),
        out_shape=jax.ShapeDtypeStruct((bsz, seq, h * GLA_DV), BF16),
        scratch_shapes=[pltpu.VMEM((2, LANES, LANES), F32), pltpu.VMEM((seq, LANES), F32)],
        compiler_params=_cparams(("arbitrary", "arbitrary")),
        name="gla",
    )(zg_x, zg_x, zg_x, zg_x, zs_x, zg_c, zg_c, zs_c, wup, bup, norm_g)


def _grid_conv_silu(src_ref, dst_ref, w_ref, b_ref, grid_w, scale):
    l = src_ref.shape[1]
    n_rows = l // grid_w
    col = lax.broadcasted_iota(jnp.int32, (grid_w, LANES), 0)

    def body(r, carry):
        acc = jnp.zeros((grid_w, LANES), F32) + b_ref[...]
        for dy in (-1, 0, 1):
            if n_rows == 1 and dy != 0:
                continue
            rr = jnp.clip(r + dy, 0, n_rows - 1)
            blk = src_ref[0, pl.ds(pl.multiple_of(rr * grid_w, grid_w), grid_w), :]
            valid = jnp.logical_and(r + dy >= 0, r + dy < n_rows)
            blk = jnp.where(valid, blk, 0.0)
            for dx in (-1, 0, 1):
                if dx == 0:
                    sh = blk
                else:
                    sh = pltpu.roll(blk, shift=(-dx) % grid_w, axis=0)
                    sh = jnp.where((col + dx >= 0) & (col + dx < grid_w), sh, 0.0)
                tap = (dy + 1) * 3 + (dx + 1)
                acc = acc + sh * w_ref[tap:tap + 1, :]
        dst_ref[pl.ds(pl.multiple_of(r * grid_w, grid_w), grid_w), :] = _silu(acc) * scale
        return carry

    lax.fori_loop(0, n_rows, body, 0)


def _ml_gates(gcol, grow, direction):
    c = gcol.shape[0]
    a_col = gcol[:, direction:direction + 1]
    i_col = gcol[:, 2 + direction:3 + direction]
    a_row = grow[direction:direction + 1, :]
    i_row = grow[2 + direction:3 + direction, :]
    a_end = a_row[:, c - 1:c] if direction == 0 else a_row[:, 0:1]
    return a_col, i_col, a_row, i_row, a_end


def _ml_chunk(q, k, v, gcol, grow, state, mstate, direction, want_out):
    c = k.shape[0]
    a_col, i_col, a_row, i_row, a_end = _ml_gates(gcol, grow, direction)
    lane = lax.broadcasted_iota(jnp.int32, (c, LANES), 1)
    v_aug = jnp.concatenate([v, jnp.where(lane == 0, 1.0, 0.0)], axis=1).astype(BF16)
    g = a_end - a_col + i_col
    g_max = jnp.max(g, axis=0, keepdims=True)
    if want_out:
        r = lax.broadcasted_iota(jnp.int32, (c, c), 0)
        cc = lax.broadcasted_iota(jnp.int32, (c, c), 1)
        causal = (cc <= r) if direction == 0 else (cc >= r)
        dmat = jnp.where(causal, a_col - a_row + i_row, -jnp.inf)
        d_max = jnp.max(dmat, axis=-1, keepdims=True)
        qb = q.astype(BF16)
        qk = lax.dot_general(qb, k.astype(BF16), (((1,), (1,)), ((), ())), preferred_element_type=F32)
    yield
    s, m = state[direction], mstate[direction]
    m_new = jnp.maximum(a_end + m, g_max)
    decay = jnp.exp(a_end + m - m_new)
    kw = (k * jnp.exp(g - m_new)).astype(BF16)
    upd = lax.dot_general(kw, v_aug, (((0,), (0,)), ((), ())), preferred_element_type=F32)
    state[direction] = decay * s + upd
    mstate[direction] = m_new
    if not want_out:
        return None
    inter = a_col + m
    m_t = jnp.maximum(inter, d_max)
    w_inter = jnp.exp(inter - m_t)
    p = (qk * jnp.exp(dmat - m_t)).astype(BF16)
    pv = jnp.dot(p, v_aug, preferred_element_type=F32)
    qs = jnp.dot(qb, s.astype(BF16), preferred_element_type=F32)
    yield
    both = pv + w_inter * qs
    num = both[:, :ML_DH]
    den = both[:, ML_DH:ML_DH + 1]
    return num / jnp.maximum(jnp.abs(den), jnp.exp(-m_t))


def _mlstm_kernel(q_ref, k_ref, v_ref, mo_ref, gcol_ref, grow_ref,
                  kc_ref, vc_ref, gcolc_ref, growc_ref,
                  wq_ref, wk_ref, bq_ref, bk_ref, ng_ref, o_ref,
                  cq_ref, ck_ref, ckc_ref, s_ref, m_ref, acc_ref):
    seq = q_ref.shape[1]
    ctx = kc_ref.shape[1]
    n = seq // ML_C
    nc = ctx // ML_C
    _grid_conv_silu(q_ref, cq_ref, wq_ref, bq_ref, GRID_W, 1.0)
    _grid_conv_silu(k_ref, ck_ref, wk_ref, bk_ref, GRID_W, ML_DH ** -0.5)
    _grid_conv_silu(kc_ref, ckc_ref, wk_ref, bk_ref, ctx, ML_DH ** -0.5)
    s_ref[...] = jnp.zeros_like(s_ref)
    m_ref[...] = jnp.zeros_like(m_ref)

    def rows(i):
        return pl.ds(pl.multiple_of(i * ML_C, ML_C), ML_C)

    def load_state():
        return [s_ref[0], s_ref[1]], [m_ref[0, :, 0:1], m_ref[1, :, 0:1]]

    def store_state(s, m):
        for d in range(2):
            s_ref[d] = s[d]
            m_ref[d] = jnp.broadcast_to(m[d], m_ref.shape[1:])

    def ctx_step(j, carry):
        order = _scan_order(j, nc, 1)
        ins = [(ckc_ref[rows(i), :], vc_ref[0, rows(i), :], gcolc_ref[0, rows(i), :], growc_ref[0, 0, :, rows(i)])
               for _, i in order]
        s, m = load_state()
        _round_robin([_ml_chunk(None, k, v, gcol, grow, s, m, d, False)
                      for (d, _), (k, v, gcol, grow) in zip(order, ins)])
        store_state(s, m)
        return carry

    lax.fori_loop(0, nc, ctx_step, 0)

    def lat_step(j, carry, second):
        order = _scan_order(j, n, SCAN_UNROLL)
        ins = [(cq_ref[rows(i), :], ck_ref[rows(i), :], v_ref[0, rows(i), :], gcol_ref[0, rows(i), :],
                grow_ref[0, 0, :, rows(i)]) for _, i in order]
        prev = [(acc_ref[rows(i), :], mo_ref[0, rows(i), :]) for _, i in order] if second else None
        s, m = load_state()
        outs = _round_robin([_ml_chunk(q, k, v, gcol, grow, s, m, d, True)
                             for (d, _), (q, k, v, gcol, grow) in zip(order, ins)])
        store_state(s, m)
        for idx, (_, i) in enumerate(order):
            if second:
                total = prev[idx][0] + outs[idx]
                y = total * lax.rsqrt(jnp.mean(total * total, axis=-1, keepdims=True) + EPS) * ng_ref[...]
                o_ref[0, rows(i), :] = (_sigmoid(prev[idx][1]) * y).astype(o_ref.dtype)
            else:
                acc_ref[rows(i), :] = outs[idx]
        return carry

    half = n // (2 * SCAN_UNROLL)
    lax.fori_loop(0, half, functools.partial(lat_step, second=False), 0)
    lax.fori_loop(half, 2 * half, functools.partial(lat_step, second=True), 0)


def _mlstm(zm_x, gcol_x, grow_x, zm_c, gcol_c, grow_c, conv_w, conv_b, norm_g):
    bsz, seq, _ = zm_x.shape
    ctx = zm_c.shape[1]
    assert seq % (2 * SCAN_UNROLL * ML_C) == 0 and ctx % ML_C == 0 and seq % GRID_W == 0
    h = ML_HEADS

    def col(l, off):
        return pl.BlockSpec((1, l, LANES), lambda b, hd: (b, 0, off + hd))

    def gates(l):
        return [pl.BlockSpec((1, l, LANES), lambda b, hd: (b, 0, hd)),
                pl.BlockSpec((1, 1, 8, l), lambda b, hd: (b, hd, 0, 0))]

    return pl.pallas_call(
        _mlstm_kernel,
        grid=(bsz, h),
        in_specs=[col(seq, 0), col(seq, h), col(seq, 2 * h), col(seq, 3 * h)] + gates(seq)
                 + [col(ctx, h), col(ctx, 2 * h)] + gates(ctx)
                 + [pl.BlockSpec((9, LANES), lambda b, hd: (0, hd)),
                    pl.BlockSpec((9, LANES), lambda b, hd: (0, h + hd)),
                    pl.BlockSpec((1, LANES), lambda b, hd: (0, hd)),
                    pl.BlockSpec((1, LANES), lambda b, hd: (0, h + hd)),
                    pl.BlockSpec((1, LANES), lambda b, hd: (0, hd))],
        out_specs=pl.BlockSpec((1, seq, LANES), lambda b, hd: (b, 0, hd)),
        out_shape=jax.ShapeDtypeStruct((bsz, seq, h * ML_DH), BF16),
        scratch_shapes=[pltpu.VMEM((seq, LANES), F32), pltpu.VMEM((seq, LANES), F32),
                        pltpu.VMEM((ctx, LANES), F32),
                        pltpu.VMEM((2, LANES, 2 * LANES), F32), pltpu.VMEM((2, 1, LANES), F32),
                        pltpu.VMEM((seq, LANES), F32)],
        compiler_params=_cparams(("arbitrary", "arbitrary")),
        name="mlstm",
    )(zm_x, zm_x, zm_x, zm_x, gcol_x, grow_x, zm_c, zm_c, gcol_c, grow_c,
      conv_w, conv_w, conv_b, conv_b, norm_g)


_G0 = 0
_E0 = N_GROUPS
RANK_BITS = 16
RANK_SPAN = 1 << RANK_BITS
ROW_UNROLL = 8
WAIT_UNROLL = 32


SUBLANES = 8


def _store_token_tiles(ref2d, val):
    n, w = val.shape
    k = w // LANES
    for c in range(k):
        ref2d[pl.ds(c, n, stride=k), :] = val[:, c * LANES:(c + 1) * LANES]


def _load_token_tiles(ref2d, first, n, k, step):
    return jnp.concatenate([ref2d[pl.ds(first + c, n, stride=step), :] for c in range(k)], axis=1)


def _outproj_kernel(x_ref, ga_ref, ml_ref, mod_ref, wa_ref, wb_ref, g2_ref, wrh_ref, wrl_ref, br_ref,
                    x1_ref, h2_ref, ri_ref, rw_ref, cnt_ref, base_ref):
    tm = x_ref.shape[1]

    @pl.when((pl.program_id(0) == 0) & (pl.program_id(1) == 0))
    def _():
        base_ref[...] = jnp.zeros_like(base_ref)

    mix = (jnp.dot(ga_ref[0], wa_ref[...], preferred_element_type=F32)
           + jnp.dot(ml_ref[0], wb_ref[...], preferred_element_type=F32))
    x1 = x_ref[0] + mod_ref[0, 2:3, :] * mix
    x1_ref[0] = x1
    y = x1 * lax.rsqrt(jnp.mean(x1 * x1, axis=-1, keepdims=True) + EPS) * g2_ref[...]
    h2 = y * (1.0 + mod_ref[0, 4:5, :]) + mod_ref[0, 3:4, :]
    _store_token_tiles(h2_ref, h2)

    h_hi = h2.astype(BF16)
    h_lo = (h2 - h_hi.astype(F32)).astype(BF16)
    logits = (jnp.dot(h_hi, wrh_ref[...], preferred_element_type=F32)
              + jnp.dot(h_lo, wrh_ref[...], preferred_element_type=F32)
              + jnp.dot(h_hi, wrl_ref[...], preferred_element_type=F32)) + br_ref[...]

    lane = lax.broadcasted_iota(jnp.int32, (tm, LANES), 1).astype(F32)
    neg = -jnp.inf
    big = float(LANES)
    is_g = lane < float(_E0)
    lg = jnp.where(is_g, logits, neg)
    gmax = jnp.max(lg, axis=-1, keepdims=True)
    gidx = jnp.min(jnp.where(lg == gmax, lane, big), axis=-1, keepdims=True)
    gw = 1.0 / jnp.sum(jnp.where(is_g, jnp.exp(logits - gmax), 0.0), axis=-1, keepdims=True)
    lo = float(_E0) + float(EXP_PER_GROUP) * gidx
    le = jnp.where((lane >= lo) & (lane < lo + float(EXP_PER_GROUP)), logits, neg)
    v1 = jnp.max(le, axis=-1, keepdims=True)
    i1 = jnp.min(jnp.where(le == v1, lane, big), axis=-1, keepdims=True)
    le2 = jnp.where(lane == i1, neg, le)
    v2 = jnp.max(le2, axis=-1, keepdims=True)
    i2 = jnp.min(jnp.where(le2 == v2, lane, big), axis=-1, keepdims=True)
    t = jnp.exp(v2 - v1)
    w1 = gw / (1.0 + t)
    w2 = gw * t / (1.0 + t)
    e1 = i1 - float(_E0)
    e2 = i2 - float(_E0)

    oh1 = lane == e1
    oh2 = lane == e2
    oh = jnp.where(oh1 | oh2, 1.0, 0.0)
    r = lax.broadcasted_iota(jnp.int32, (tm, tm), 0)
    c = lax.broadcasted_iota(jnp.int32, (tm, tm), 1)
    strict = jnp.where(c < r, 1.0, 0.0).astype(BF16)
    before = jnp.dot(strict, oh.astype(BF16), preferred_element_type=F32) + base_ref[...]
    rank1 = jnp.sum(jnp.where(oh1, before, 0.0), axis=-1, keepdims=True)
    rank2 = jnp.sum(jnp.where(oh2, before, 0.0), axis=-1, keepdims=True)
    total = base_ref[...] + jnp.sum(oh, axis=0, keepdims=True)
    base_ref[...] = total
    cnt_ref[...] = total

    ids = jnp.where(lane == 0.0, e1 * float(RANK_SPAN) + rank1,
                    jnp.where(lane == 1.0, e2 * float(RANK_SPAN) + rank2, 0.0))
    ri_ref[0] = ids.astype(jnp.int32)
    rw_ref[0] = jnp.where(lane == 0.0, w1, jnp.where(lane == 1.0, w2, 0.0))


def _outproj(x, gla_o, ml_o, mods, wa, wb, g2, wrh, wrl, br, tm):
    bsz, seq, d = x.shape
    const = lambda shape: pl.BlockSpec(shape, lambda b, i: (0,) * len(shape))
    tile = lambda w: pl.BlockSpec((1, tm, w), lambda b, i: (b, i, 0))
    return pl.pallas_call(
        _outproj_kernel,
        grid=(bsz, seq // tm),
        in_specs=[tile(d), tile(gla_o.shape[2]), tile(ml_o.shape[2]),
                  pl.BlockSpec((1, N_MOD, d), lambda b, i: (b, 0, 0)),
                  const(wa.shape), const(wb.shape), const((1, d)),
                  const(wrh.shape), const(wrl.shape), const((1, LANES))],
        out_specs=[tile(d),
                   pl.BlockSpec((tm * d // LANES, LANES), lambda b, i: (b * (seq // tm) + i, 0)),
                   tile(LANES), tile(LANES), const((1, LANES))],
        out_shape=[jax.ShapeDtypeStruct((bsz, seq, d), F32),
                   jax.ShapeDtypeStruct((bsz * seq * d // LANES, LANES), F32),
                   jax.ShapeDtypeStruct((bsz, seq, LANES), jnp.int32),
                   jax.ShapeDtypeStruct((bsz, seq, LANES), F32),
                   jax.ShapeDtypeStruct((1, LANES), F32)],
        scratch_shapes=[pltpu.VMEM((1, LANES), F32)],
        compiler_params=_cparams(("arbitrary", "arbitrary")),
        name="outproj",
    )(x, gla_o, ml_o, mods, wa, wb, g2, wrh, wrl, br)


def _experts_kernel(dest_ref, ps_ref, cnt_ref, be_ref, nv_ref, meta_ref, h_hbm, w1_ref, w2_ref, ytok_hbm,
                    src_ref, xbuf, ybuf, w1c_ref, w2c_ref, gsem, ssem):
    i = pl.program_id(0)
    n_steps = pl.num_programs(0)
    n_used = meta_ref[0]
    tr = SUBLANES
    n_tok = h_hbm.shape[0] // tr
    n_rows = src_ref.shape[0]
    blk = xbuf.shape[1] // tr
    slot = lax.rem(i, 2)

    def slab(j):
        return pl.ds(pl.multiple_of(j * tr, tr), tr)

    def gather_copy(tok, s, r):
        return pltpu.make_async_copy(h_hbm.at[slab(tok), :], xbuf.at[s, slab(r), :], gsem.at[s])

    def scatter_copy(a, s, r):
        return pltpu.make_async_copy(ybuf.at[s, slab(r), :], ytok_hbm.at[slab(a), :], ssem.at[s])

    def rows_loop(body):
        def step(r2, c):
            body(2 * r2, 0)
            body(2 * r2 + 1, 1)
            return c
        lax.fori_loop(0, blk // 2, step, 0, unroll=ROW_UNROLL // 2)

    def issue_gather(b, s):
        def one(r, prio):
            tok = lax.shift_right_logical(src_ref[b * blk + r], 1)
            gather_copy(jnp.minimum(tok, n_tok - 1), s, r).start(priority=prio)
        rows_loop(one)

    def wait_rows(copy):
        def step(r, c):
            copy.wait()
            return c
        lax.fori_loop(0, blk, step, 0, unroll=WAIT_UNROLL)

    def wait_gather(s):
        wait_rows(gather_copy(0, s, 0))

    def wait_scatter(s):
        wait_rows(scatter_copy(0, s, 0))

    @pl.when(i == 0)
    def _():
        ybuf[...] = jnp.zeros_like(ybuf)
        for s in range(2):
            tail = ytok_hbm.at[pl.ds((2 * n_tok + s * blk) * tr, blk * tr), :]
            cp = pltpu.make_async_copy(ybuf.at[s], tail, ssem.at[s])
            cp.start()
            cp.wait()

        def put(a, c):
            src_ref[dest_ref[a]] = a
            return c
        lax.fori_loop(0, 2 * n_tok, put, 0, unroll=16)

        def pad(j, c):
            src_ref[j] = 2 * n_tok + (j & (2 * blk - 1))
            return c

        def pad_expert(e, c):
            lax.fori_loop(ps_ref[e] + cnt_ref[e], ps_ref[e + 1], pad, 0)
            return c
        lax.fori_loop(0, cnt_ref.shape[0], pad_expert, 0)
        used_end = ps_ref[cnt_ref.shape[0]]
        lax.fori_loop(used_end, jnp.minimum(used_end + blk, n_rows), pad, 0)
        issue_gather(0, 0)

    @pl.when(i < n_used)
    def _():
        wait_gather(slot)
        issue_gather(jnp.minimum(i + 1, n_steps - 1), 1 - slot)

        @pl.when((i == 0) | (be_ref[i] != be_ref[jnp.maximum(i - 1, 0)]))
        def _():
            w1c_ref[...] = w1_ref[0].astype(BF16)
            w2c_ref[...] = w2_ref[0].astype(BF16)

        @pl.when(i >= 2)
        def _():
            wait_scatter(slot)

        row = lax.broadcasted_iota(jnp.int32, (blk, 1), 0)
        x = _load_token_tiles(xbuf.at[slot], 0, blk, tr, tr)
        x = jnp.where(row < nv_ref[i], x, 0.0).astype(BF16)
        h = jnp.dot(x, w1c_ref[...], preferred_element_type=F32)
        a = (_silu(h[:, :D_EXPERT]) * h[:, D_EXPERT:]).astype(BF16)
        _store_token_tiles(ybuf.at[slot], jnp.dot(a, w2c_ref[...], preferred_element_type=F32))
        rows_loop(lambda r, prio: scatter_copy(src_ref[i * blk + r], slot, r).start(priority=prio))

    @pl.when(i == n_steps - 1)
    def _():
        wait_gather(lax.rem(n_used, 2))
        wait_scatter(0)
        wait_scatter(1)


def _experts(dest, pad_start, counts, block_e, block_nv, meta, h2, w_in, w_out, nb):
    d = w_in.shape[1]
    tr = d // LANES
    assert tr == SUBLANES, "a token row must fill exactly one (8, 128) tile"
    n_tok = h2.shape[0] // tr
    de2 = w_in.shape[2]
    n_rows = nb * MOE_BLK
    assert 2 * n_tok >= 2 * MOE_BLK
    assert MOE_BLK & (MOE_BLK - 1) == 0
    wmap = lambda i, pk, ps, cnt, be, nv, meta: (be[i], 0, 0)
    return pl.pallas_call(
        _experts_kernel,
        grid_spec=pltpu.PrefetchScalarGridSpec(
            num_scalar_prefetch=6, grid=(nb,),
            in_specs=[pl.BlockSpec(memory_space=pl.ANY),
                      pl.BlockSpec((1, d, de2), wmap),
                      pl.BlockSpec((1, de2 // 2, d), wmap)],
            out_specs=pl.BlockSpec(memory_space=pl.ANY),
            scratch_shapes=[pltpu.SMEM((n_rows,), jnp.int32),
                            pltpu.VMEM((2, MOE_BLK * tr, LANES), F32), pltpu.VMEM((2, MOE_BLK * tr, LANES), F32),
                            pltpu.VMEM((d, de2), BF16), pltpu.VMEM((de2 // 2, d), BF16),
                            pltpu.SemaphoreType.DMA((2,)), pltpu.SemaphoreType.DMA((2,))]),
        out_shape=jax.ShapeDtypeStruct(((2 * n_tok + 2 * MOE_BLK) * tr, LANES), F32),
        compiler_params=_cparams(("arbitrary",)),
        name="experts",
    )(dest, pad_start, counts, block_e, block_nv, meta, h2, w_in, w_out)


def _combine_kernel(x1_ref, y_ref, rw_ref, mod_ref, fg_ref, o_ref):
    tc, d = x1_ref.shape
    tr = d // LANES
    y1 = _load_token_tiles(y_ref, 0, tc, tr, 2 * tr)
    y2 = _load_token_tiles(y_ref, tr, tc, tr, 2 * tr)
    moe = rw_ref[:, 0:1] * y1 + rw_ref[:, 1:2] * y2
    x2 = x1_ref[...] + mod_ref[0, 5:6, :] * moe
    o_ref[...] = x2 * lax.rsqrt(jnp.mean(x2 * x2, axis=-1, keepdims=True) + EPS) * fg_ref[...]


def _combine(x1, ytok, rw, mods, fg, tokens_per_batch, tc):
    n_tok, d = x1.shape
    tiles_per_batch = tokens_per_batch // tc
    return pl.pallas_call(
        _combine_kernel,
        grid=(n_tok // tc,),
        in_specs=[pl.BlockSpec((tc, d), lambda i: (i, 0)),
                  pl.BlockSpec((2 * tc * d // LANES, LANES), lambda i: (i, 0)),
                  pl.BlockSpec((tc, LANES), lambda i: (i, 0)),
                  pl.BlockSpec((1, N_MOD, d), lambda i: (i // tiles_per_batch, 0, 0)),
                  pl.BlockSpec((1, d), lambda i: (0, 0))],
        out_specs=pl.BlockSpec((tc, d), lambda i: (i, 0)),
        out_shape=jax.ShapeDtypeStruct((n_tok, d), F32),
        compiler_params=_cparams(("arbitrary",)),
        name="combine",
    )(x1, ytok, rw, mods, fg)


def _prep_inproj_weights(w_in, gla_up_w, gla_up_b, ml_i_b, ml_f_b):
    d = w_in.shape[0]
    o_gq, o_gk, o_gv, o_gg = 0, GLA_QK_W, 2 * GLA_QK_W, 2 * GLA_QK_W + GLA_V_W
    o_lr = o_gg + GLA_V_W
    o_mqk = o_lr + 2 * GLA_LR
    o_mi = o_mqk + 4 * ML_W
    o_mf = o_mi + 2 * ML_HEADS

    def pad_heads(off):
        w = w_in[:, off:off + GLA_QK_W].reshape(d, GLA_HEADS, GLA_DK)
        return jnp.pad(w, ((0, 0), (0, 0), (0, LANES - GLA_DK))).reshape(d, GLA_HEADS * LANES)

    wg = jnp.concatenate([pad_heads(o_gq), pad_heads(o_gk), w_in[:, o_gv:o_gg], w_in[:, o_gg:o_lr]], axis=1)
    wm = w_in[:, o_mqk:o_mi]
    ws = jnp.concatenate([w_in[:, o_lr:o_mqk], w_in[:, o_mi:o_mf + 2 * ML_HEADS],
                          jnp.zeros((d, LANES - 2 * GLA_LR - 4 * ML_HEADS), w_in.dtype)], axis=1)
    bias = jnp.zeros((LANES,), F32)
    bias = bias.at[_MI0:_MI0 + 2 * ML_HEADS].set(ml_i_b.reshape(-1))
    bias = bias.at[_MF0:_MF0 + 2 * ML_HEADS].set(ml_f_b.reshape(-1))
    up = gla_up_w.reshape(2, GLA_LR, GLA_HEADS, GLA_DK).transpose(2, 0, 1, 3)
    wup = jnp.zeros((GLA_HEADS, 2, LANES, LANES), F32)
    for dr in range(2):
        wup = wup.at[:, dr, dr * GLA_LR:(dr + 1) * GLA_LR, :GLA_DK].set(up[:, dr])
    bup = jnp.pad(gla_up_b.reshape(2, GLA_HEADS, GLA_DK).transpose(1, 0, 2),
                  ((0, 0), (0, 0), (0, LANES - GLA_DK))).reshape(GLA_HEADS, 2, 1, LANES)
    return (wg.astype(BF16), wm.astype(BF16), ws.astype(BF16), ws.T.astype(BF16),
            bias.reshape(1, LANES), bias.reshape(LANES, 1), wup.astype(BF16), bup)


def _layer(x, ctx, mods, norm1_g, w_in, gla_up_w, gla_up_b, gla_norm_g, ml_conv_w, ml_conv_b,
           ml_i_b, ml_f_b, ml_norm_g, w_out, norm2_g, rg_w, rg_b, re_w, re_b, e_w_in, e_w_out, final_g):
    bsz, seq, d = x.shape
    n_tok = bsz * seq
    wg, wm, ws, wst, bcol, brow, wup, bup = _prep_inproj_weights(w_in, gla_up_w, gla_up_b, ml_i_b, ml_f_b)
    g1 = norm1_g.reshape(1, d)
    zg_x, zm_x, zs_x, gcol_x, grow_x = _inproj(x, mods, lambda b: b, g1, wg, wm, ws, wst, bcol, brow, 256)
    zg_c, zm_c, zs_c, gcol_c, grow_c = _inproj(ctx, mods, lambda b: bsz, g1, wg, wm, ws, wst, bcol, brow,
                                               min(256, ctx.shape[1]))
    gla_o = _gla(zg_x, zs_x, zg_c, zs_c, wup, bup, gla_norm_g.reshape(1, -1))
    ml_o = _mlstm(zm_x, gcol_x, grow_x, zm_c, gcol_c, grow_c,
                  ml_conv_w.reshape(9, -1), ml_conv_b.reshape(1, -1), ml_norm_g.reshape(1, -1))

    wr = jnp.zeros((d, LANES), F32).at[:, _G0:_E0].set(rg_w).at[:, _E0:_E0 + N_EXPERTS].set(re_w)
    br = jnp.zeros((1, LANES), F32).at[0, _G0:_E0].set(rg_b).at[0, _E0:_E0 + N_EXPERTS].set(re_b)
    wrh = wr.astype(BF16)
    wrl = (wr - wrh.astype(F32)).astype(BF16)
    x1, h2, ri, rw, cnt = _outproj(x, gla_o, ml_o, mods, w_out[:GLA_V_W].astype(BF16),
                                   w_out[GLA_V_W:].astype(BF16), norm2_g.reshape(1, d), wrh, wrl, br, 256)

    counts = cnt[0, :N_EXPERTS].astype(jnp.int32)
    nblk = (counts + MOE_BLK - 1) // MOE_BLK
    blk_end = jnp.cumsum(nblk)
    blk_start = blk_end - nblk
    n_used = blk_end[-1]
    nb_max = (2 * n_tok) // MOE_BLK + N_EXPERTS
    blk = jnp.arange(nb_max, dtype=jnp.int32)
    blk_c = jnp.minimum(blk, n_used - 1)
    onehot = (blk_c[:, None] >= blk_start[None, :]) & (blk_c[:, None] < blk_end[None, :])
    pick = lambda v: jnp.sum(jnp.where(onehot, v[None, :], 0), axis=1)
    block_e = pick(jnp.arange(N_EXPERTS, dtype=jnp.int32)).astype(jnp.int32)
    block_nv = jnp.clip(pick(counts) - (blk_c - pick(blk_start)) * MOE_BLK, 0, MOE_BLK)
    block_nv = jnp.where(blk < n_used, block_nv, 0).astype(jnp.int32)
    pad_start = (jnp.concatenate([blk_start, blk_end[-1:]]) * MOE_BLK).astype(jnp.int32)
    packed = ri.reshape(n_tok, LANES)[:, 0:2].reshape(-1)
    e_of = lax.shift_right_logical(packed, RANK_BITS)
    start_of = jnp.sum(jnp.where(e_of[:, None] == jnp.arange(N_EXPERTS, dtype=jnp.int32)[None, :],
                                 pad_start[None, :N_EXPERTS], 0), axis=1)
    dest = (start_of + (packed & (RANK_SPAN - 1))).astype(jnp.int32)
    meta = jnp.stack([n_used, n_used]).astype(jnp.int32)

    ytok = _experts(dest, pad_start, counts, block_e, block_nv, meta, h2, e_w_in, e_w_out, nb_max)
    out = _combine(x1.reshape(n_tok, d), ytok, rw.reshape(n_tok, LANES), mods, final_g.reshape(1, d), seq, 256)
    return out.reshape(bsz, seq, d)


def kernel(x, c, ctx, c_ctx, ada_w, ada_b, norm1_g, w_in, gla_up_w, gla_up_b, gla_norm_g, ml_conv_w, ml_conv_b,
           ml_i_b, ml_f_b, ml_norm_g, w_out, norm2_g, router_group_w, router_group_b, router_expert_w,
           router_expert_b, expert_w_in, expert_w_out, final_norm_g):
    assert ada_w.shape[0] == 1, "single-layer stack"
    bsz, d = c.shape
    cc = jnp.concatenate([c, c_ctx[None, :], jnp.zeros((8 - bsz - 1, d), F32)], axis=0)
    mods = _modulation(cc, ada_w[0], ada_b[0]).reshape(8, N_MOD, d)
    return _layer(x, ctx, mods, norm1_g[0], w_in[0], gla_up_w[0], gla_up_b[0], gla_norm_g[0],
                  ml_conv_w[0], ml_conv_b[0], ml_i_b[0], ml_f_b[0], ml_norm_g[0], w_out[0], norm2_g[0],
                  router_group_w[0], router_group_b[0], router_expert_w[0], router_expert_b[0],
                  expert_w_in[0], expert_w_out[0], final_norm_g)
```

```python
import functools

import jax
import jax.numpy as jnp
from jax import lax
from jax.experimental import pallas as pl
from jax.experimental.pallas import tpu as pltpu

F32 = jnp.float32
BF16 = jnp.bfloat16

D_MODEL = 1024
GRID_W = 64
N_MOD = 6
EPS = 1e-6

GLA_HEADS = 4
GLA_DK = 64
GLA_DV = 128
GLA_LR = 16
GLA_TAU = 16.0
GLA_C = 128
SCAN_UNROLL = 2

ML_HEADS = 4
ML_DH = 128
ML_C = 128

N_GROUPS = 4
EXP_PER_GROUP = 8
N_EXPERTS = N_GROUPS * EXP_PER_GROUP
D_EXPERT = 512
MOE_BLK = 256

GLA_QK_W = GLA_HEADS * GLA_DK
GLA_V_W = GLA_HEADS * GLA_DV
ML_W = ML_HEADS * ML_DH
LANES = 128
VMEM_LIMIT = 56 * 1024 * 1024

_LR0 = 0
_MI0 = 2 * GLA_LR
_MF0 = _MI0 + 2 * ML_HEADS


def _cparams(sem):
    return pltpu.CompilerParams(dimension_semantics=sem, vmem_limit_bytes=VMEM_LIMIT)


def _sigmoid(x):
    return 1.0 / (1.0 + jnp.exp(-x))


def _silu(x):
    return x * _sigmoid(x)


def _log_sigmoid(x):
    return jnp.minimum(x, 0.0) - jnp.log1p(jnp.exp(-jnp.abs(x)))


def _split_dot(a_bf16_exact, x, dims=None):
    x_hi = x.astype(BF16)
    x_lo = (x - x_hi.astype(F32)).astype(BF16)
    if dims is None:
        f = lambda u: jnp.dot(a_bf16_exact, u, preferred_element_type=F32)
    else:
        f = lambda u: lax.dot_general(u, a_bf16_exact, dims, preferred_element_type=F32)
    return f(x_hi) + f(x_lo)


def _mod_kernel(c_ref, w_ref, b_ref, o_ref):
    c = c_ref[...]
    s = _silu(c).astype(BF16)
    o_ref[...] = jnp.dot(s, w_ref[...].astype(BF16), preferred_element_type=F32) + b_ref[...]


def _modulation(cc, ada_w, ada_b):
    rows, d = cc.shape
    n = ada_w.shape[1]
    tn = 1536
    return pl.pallas_call(
        _mod_kernel,
        grid=(n // tn,),
        in_specs=[pl.BlockSpec((rows, d), lambda j: (0, 0)),
                  pl.BlockSpec((d, tn), lambda j: (0, j)),
                  pl.BlockSpec((1, tn), lambda j: (0, j))],
        out_specs=pl.BlockSpec((rows, tn), lambda j: (0, j)),
        out_shape=jax.ShapeDtypeStruct((rows, n), F32),
        compiler_params=_cparams(("arbitrary",)),
        name="mod",
    )(cc, ada_w, ada_b.reshape(1, n))


def _inproj_kernel(x_ref, mod_ref, g_ref, wg_ref, wm_ref, ws_ref, wst_ref, bcol_ref, brow_ref,
                   zg_ref, zm_ref, zs_ref, gcol_ref, grow_ref):
    tm = x_ref.shape[1]
    x = x_ref[0]
    y = x * lax.rsqrt(jnp.mean(x * x, axis=-1, keepdims=True) + EPS) * g_ref[...]
    h = (y * (1.0 + mod_ref[0, 1:2, :]) + mod_ref[0, 0:1, :]).astype(BF16)
    zg_ref[0] = jnp.dot(h, wg_ref[...], preferred_element_type=F32)
    zm_ref[0] = jnp.dot(h, wm_ref[...], preferred_element_type=F32)
    zs = jnp.dot(h, ws_ref[...], preferred_element_type=F32) + bcol_ref[...]
    zst = lax.dot_general(wst_ref[...], h, (((1,), (1,)), ((), ())),
                          preferred_element_type=F32) + brow_ref[...]
    zs_ref[0] = zs

    r = lax.broadcasted_iota(jnp.int32, (tm, tm), 0)
    c = lax.broadcasted_iota(jnp.int32, (tm, tm), 1)
    shift = ML_C.bit_length() - 1
    same = jnp.right_shift(r, shift) == jnp.right_shift(c, shift)
    lower = jnp.where(same & (c <= r), 1.0, 0.0).astype(BF16)
    upper = jnp.where(same & (c >= r), 1.0, 0.0).astype(BF16)
    lsf = _log_sigmoid(zs)
    a_pre = _split_dot(lower, lsf)
    a_suf = _split_dot(upper, lsf)
    lsft = _log_sigmoid(zst)
    dims = (((1,), (0,)), ((), ()))
    a_pre_t = _split_dot(upper, lsft, dims)
    a_suf_t = _split_dot(lower, lsft, dims)

    lane = lax.broadcasted_iota(jnp.int32, (tm, LANES), 1)
    for hd in range(ML_HEADS):
        cols = (a_pre[:, _MF0 + hd:_MF0 + hd + 1],
                a_suf[:, _MF0 + ML_HEADS + hd:_MF0 + ML_HEADS + hd + 1],
                zs[:, _MI0 + hd:_MI0 + hd + 1],
                zs[:, _MI0 + ML_HEADS + hd:_MI0 + ML_HEADS + hd + 1])
        slab = jnp.zeros((tm, LANES), F32)
        for j, col in enumerate(cols):
            slab = jnp.where(lane == j, col, slab)
        gcol_ref[0, :, hd * LANES:(hd + 1) * LANES] = slab
        rows = (a_pre_t[_MF0 + hd:_MF0 + hd + 1, :],
                a_suf_t[_MF0 + ML_HEADS + hd:_MF0 + ML_HEADS + hd + 1, :],
                zst[_MI0 + hd:_MI0 + hd + 1, :],
                zst[_MI0 + ML_HEADS + hd:_MI0 + ML_HEADS + hd + 1, :])
        for j, row in enumerate(rows):
            grow_ref[0, hd, j:j + 1, :] = row
        grow_ref[0, hd, 4:8, :] = jnp.zeros((4, tm), F32)


def _inproj(x, mods, mod_row_of_batch, norm_g, wg, wm, ws, wst, bcol, brow, tm):
    bsz, l, d = x.shape
    assert l % tm == 0 and tm % ML_C == 0
    const = lambda shape: pl.BlockSpec(shape, lambda b, i: (0,) * len(shape))
    return pl.pallas_call(
        _inproj_kernel,
        grid=(bsz, l // tm),
        in_specs=[pl.BlockSpec((1, tm, d), lambda b, i: (b, i, 0)),
                  pl.BlockSpec((1, N_MOD, d), lambda b, i: (mod_row_of_batch(b), 0, 0)),
                  const((1, d)), const(wg.shape), const(wm.shape), const(ws.shape), const(wst.shape),
                  const((1, LANES)), const((LANES, 1))],
        out_specs=[pl.BlockSpec((1, tm, wg.shape[1]), lambda b, i: (b, i, 0)),
                   pl.BlockSpec((1, tm, wm.shape[1]), lambda b, i: (b, i, 0)),
                   pl.BlockSpec((1, tm, LANES), lambda b, i: (b, i, 0)),
                   pl.BlockSpec((1, tm, ML_HEADS * LANES), lambda b, i: (b, i, 0)),
                   pl.BlockSpec((1, ML_HEADS, 8, tm), lambda b, i: (b, 0, 0, i))],
        out_shape=[jax.ShapeDtypeStruct((bsz, l, wg.shape[1]), F32),
                   jax.ShapeDtypeStruct((bsz, l, wm.shape[1]), F32),
                   jax.ShapeDtypeStruct((bsz, l, LANES), F32),
                   jax.ShapeDtypeStruct((bsz, l, ML_HEADS * LANES), F32),
                   jax.ShapeDtypeStruct((bsz, ML_HEADS, 8, l), F32)],
        compiler_params=_cparams(("arbitrary", "arbitrary")),
        name="inproj",
    )(x, mods, norm_g, wg, wm, ws, wst, bcol, brow)


def _round_robin(chains):
    results = [None] * len(chains)
    live = list(enumerate(chains))
    while live:
        still = []
        for idx, chain in live:
            try:
                next(chain)
                still.append((idx, chain))
            except StopIteration as done:
                results[idx] = done.value
        live = still
    return results


def _gla_chunk(q, k, v, zs, wup, bup, state, direction, want_out):
    c = k.shape[0]
    logits = jnp.dot(zs.astype(BF16), wup, preferred_element_type=F32) + bup
    yield
    g = _log_sigmoid(logits) * (1.0 / GLA_TAU)
    r = lax.broadcasted_iota(jnp.int32, (c, c), 0)
    cc = lax.broadcasted_iota(jnp.int32, (c, c), 1)
    causal = (cc <= r) if direction == 0 else (cc >= r)
    b = _split_dot(jnp.where(causal, 1.0, 0.0).astype(BF16), g)
    yield
    b_end = b[c - 1:c, :] if direction == 0 else b[0:1, :]
    kd = (k * jnp.exp(b_end - b)).astype(BF16)
    upd = lax.dot_general(v.astype(BF16), kd, (((0,), (0,)), ((), ())), preferred_element_type=F32)
    s = state[direction]
    state[direction] = jnp.exp(b_end) * s + upd
    if not want_out:
        return None
    b_mid = b[c // 2:c // 2 + 1, :]
    q_in = (q * jnp.exp(b - b_mid)).astype(BF16)
    k_in = (k * jnp.exp(b_mid - b)).astype(BF16)
    att = lax.dot_general(q_in, k_in, (((1,), (1,)), ((), ())), preferred_element_type=F32)
    inter = lax.dot_general((q * jnp.exp(b)).astype(BF16), s.astype(BF16),
                            (((1,), (1,)), ((), ())), preferred_element_type=F32)
    yield
    att = jnp.where(causal, att, 0.0)
    return jnp.dot(att.astype(BF16), v.astype(BF16), preferred_element_type=F32) + inter


def _scan_order(j, n, unroll):
    return [(d, j * unroll + u if d == 0 else n - 1 - (j * unroll + u)) for u in range(unroll) for d in range(2)]


def _gla_kernel(q_ref, k_ref, v_ref, gg_ref, zs_ref, kc_ref, vc_ref, zsc_ref,
                wup_ref, bup_ref, ng_ref, o_ref, s_ref, acc_ref):
    seq = q_ref.shape[1]
    ctx = kc_ref.shape[1]
    n = seq // GLA_C
    nc = ctx // GLA_C
    s_ref[...] = jnp.zeros_like(s_ref)

    def rows(i):
        return pl.ds(pl.multiple_of(i * GLA_C, GLA_C), GLA_C)

    def ctx_step(j, carry):
        order = _scan_order(j, nc, 1)
        ins = [(kc_ref[0, rows(i), :], vc_ref[0, rows(i), :], zsc_ref[0, rows(i), :]) for _, i in order]
        s = [s_ref[0], s_ref[1]]
        _round_robin([_gla_chunk(None, k, v, zs, wup_ref[0, d], bup_ref[0, d], s, d, False)
                      for (d, _), (k, v, zs) in zip(order, ins)])
        s_ref[0] = s[0]
        s_ref[1] = s[1]
        return carry

    lax.fori_loop(0, nc, ctx_step, 0)

    def lat_step(j, carry, second):
        order = _scan_order(j, n, SCAN_UNROLL)
        ins = [(q_ref[0, rows(i), :], k_ref[0, rows(i), :], v_ref[0, rows(i), :], zs_ref[0, rows(i), :])
               for _, i in order]
        prev = [(acc_ref[rows(i), :], gg_ref[0, rows(i), :]) for _, i in order] if second else None
        s = [s_ref[0], s_ref[1]]
        outs = _round_robin([_gla_chunk(q * (GLA_DK ** -0.5), k, v, zs, wup_ref[0, d], bup_ref[0, d], s, d, True)
                             for (d, _), (q, k, v, zs) in zip(order, ins)])
        s_ref[0] = s[0]
        s_ref[1] = s[1]
        for idx, (_, i) in enumerate(order):
            if second:
                total = prev[idx][0] + outs[idx]
                y = total * lax.rsqrt(jnp.mean(total * total, axis=-1, keepdims=True) + EPS) * ng_ref[...]
                o_ref[0, rows(i), :] = (y * _silu(prev[idx][1])).astype(o_ref.dtype)
            else:
                acc_ref[rows(i), :] = outs[idx]
        return carry

    half = n // (2 * SCAN_UNROLL)
    lax.fori_loop(0, half, functools.partial(lat_step, second=False), 0)
    lax.fori_loop(half, 2 * half, functools.partial(lat_step, second=True), 0)


def _gla(zg_x, zs_x, zg_c, zs_c, wup, bup, norm_g):
    bsz, seq, _ = zg_x.shape
    ctx = zg_c.shape[1]
    assert seq % (2 * SCAN_UNROLL * GLA_C) == 0 and ctx % GLA_C == 0
    h = GLA_HEADS

    def col(l, off):
        return pl.BlockSpec((1, l, LANES), lambda b, hd: (b, 0, off + hd))

    return pl.pallas_call(
        _gla_kernel,
        grid=(bsz, h),
        in_specs=[col(seq, 0), col(seq, h), col(seq, 2 * h), col(seq, 3 * h),
                  pl.BlockSpec((1, seq, LANES), lambda b, hd: (b, 0, 0)),
                  col(ctx, h), col(ctx, 2 * h),
                  pl.BlockSpec((1, ctx, LANES), lambda b, hd: (b, 0, 0)),
                  pl.BlockSpec((1, 2, LANES, LANES), lambda b, hd: (hd, 0, 0, 0)),
                  pl.BlockSpec((1, 2, 1, LANES), lambda b, hd: (hd, 0, 0, 0)),
                  pl.BlockSpec((1, LANES), lambda b, hd: (0, hd))],
        out_specs=pl.BlockSpec((1, seq, LANES), lambda b, hd: (b, 0, hd)),
        out_shape=jax.ShapeDtypeStruct((bsz, seq, h * GLA_DV), BF16),
        scratch_shapes=[pltpu.VMEM((2, LANES, LANES), F32), pltpu.VMEM((seq, LANES), F32)],
        compiler_params=_cparams(("arbitrary", "arbitrary")),
        name="gla",
    )(zg_x, zg_x, zg_x, zg_x, zs_x, zg_c, zg_c, zs_c, wup, bup, norm_g)


def _grid_conv_silu(src_ref, dst_ref, w_ref, b_ref, grid_w, scale):
    l = src_ref.shape[1]
    n_rows = l // grid_w
    col = lax.broadcasted_iota(jnp.int32, (grid_w, LANES), 0)

    def body(r, carry):
        acc = jnp.zeros((grid_w, LANES), F32) + b_ref[...]
        for dy in (-1, 0, 1):
            if n_rows == 1 and dy != 0:
                continue
            rr = jnp.clip(r + dy, 0, n_rows - 1)
            blk = src_ref[0, pl.ds(pl.multiple_of(rr * grid_w, grid_w), grid_w), :]
            valid = jnp.logical_and(r + dy >= 0, r + dy < n_rows)
            blk = jnp.where(valid, blk, 0.0)
            for dx in (-1, 0, 1):
                if dx == 0:
                    sh = blk
                else:
                    sh = pltpu.roll(blk, shift=(-dx) % grid_w, axis=0)
                    sh = jnp.where((col + dx >= 0) & (col + dx < grid_w), sh, 0.0)
                tap = (dy + 1) * 3 + (dx + 1)
                acc = acc + sh * w_ref[tap:tap + 1, :]
        dst_ref[pl.ds(pl.multiple_of(r * grid_w, grid_w), grid_w), :] = _silu(acc) * scale
        return carry

    lax.fori_loop(0, n_rows, body, 0)


def _ml_gates(gcol, grow, direction):
    c = gcol.shape[0]
    a_col = gcol[:, direction:direction + 1]
    i_col = gcol[:, 2 + direction:3 + direction]
    a_row = grow[direction:direction + 1, :]
    i_row = grow[2 + direction:3 + direction, :]
    a_end = a_row[:, c - 1:c] if direction == 0 else a_row[:, 0:1]
    return a_col, i_col, a_row, i_row, a_end


def _ml_chunk(q, k, v, gcol, grow, state, mstate, direction, want_out):
    c = k.shape[0]
    a_col, i_col, a_row, i_row, a_end = _ml_gates(gcol, grow, direction)
    lane = lax.broadcasted_iota(jnp.int32, (c, LANES), 1)
    v_aug = jnp.concatenate([v, jnp.where(lane == 0, 1.0, 0.0)], axis=1).astype(BF16)
    g = a_end - a_col + i_col
    g_max = jnp.max(g, axis=0, keepdims=True)
    if want_out:
        r = lax.broadcasted_iota(jnp.int32, (c, c), 0)
        cc = lax.broadcasted_iota(jnp.int32, (c, c), 1)
        causal = (cc <= r) if direction == 0 else (cc >= r)
        dmat = jnp.where(causal, a_col - a_row + i_row, -jnp.inf)
        d_max = jnp.max(dmat, axis=-1, keepdims=True)
        qb = q.astype(BF16)
        qk = lax.dot_general(qb, k.astype(BF16), (((1,), (1,)), ((), ())), preferred_element_type=F32)
    yield
    s, m = state[direction], mstate[direction]
    m_new = jnp.maximum(a_end + m, g_max)
    decay = jnp.exp(a_end + m - m_new)
    kw = (k * jnp.exp(g - m_new)).astype(BF16)
    upd = lax.dot_general(kw, v_aug, (((0,), (0,)), ((), ())), preferred_element_type=F32)
    state[direction] = decay * s + upd
    mstate[direction] = m_new
    if not want_out:
        return None
    inter = a_col + m
    m_t = jnp.maximum(inter, d_max)
    w_inter = jnp.exp(inter - m_t)
    p = (qk * jnp.exp(dmat - m_t)).astype(BF16)
    pv = jnp.dot(p, v_aug, preferred_element_type=F32)
    qs = jnp.dot(qb, s.astype(BF16), preferred_element_type=F32)
    yield
    both = pv + w_inter * qs
    num = both[:, :ML_DH]
    den = both[:, ML_DH:ML_DH + 1]
    return num / jnp.maximum(jnp.abs(den), jnp.exp(-m_t))


def _mlstm_kernel(q_ref, k_ref, v_ref, mo_ref, gcol_ref, grow_ref,
                  kc_ref, vc_ref, gcolc_ref, growc_ref,
                  wq_ref, wk_ref, bq_ref, bk_ref, ng_ref, o_ref,
                  cq_ref, ck_ref, ckc_ref, s_ref, m_ref, acc_ref):
    seq = q_ref.shape[1]
    ctx = kc_ref.shape[1]
    n = seq // ML_C
    nc = ctx // ML_C
    _grid_conv_silu(q_ref, cq_ref, wq_ref, bq_ref, GRID_W, 1.0)
    _grid_conv_silu(k_ref, ck_ref, wk_ref, bk_ref, GRID_W, ML_DH ** -0.5)
    _grid_conv_silu(kc_ref, ckc_ref, wk_ref, bk_ref, ctx, ML_DH ** -0.5)
    s_ref[...] = jnp.zeros_like(s_ref)
    m_ref[...] = jnp.zeros_like(m_ref)

    def rows(i):
        return pl.ds(pl.multiple_of(i * ML_C, ML_C), ML_C)

    def load_state():
        return [s_ref[0], s_ref[1]], [m_ref[0, :, 0:1], m_ref[1, :, 0:1]]

    def store_state(s, m):
        for d in range(2):
            s_ref[d] = s[d]
            m_ref[d] = jnp.broadcast_to(m[d], m_ref.shape[1:])

    def ctx_step(j, carry):
        order = _scan_order(j, nc, 1)
        ins = [(ckc_ref[rows(i), :], vc_ref[0, rows(i), :], gcolc_ref[0, rows(i), :], growc_ref[0, 0, :, rows(i)])
               for _, i in order]
        s, m = load_state()
        _round_robin([_ml_chunk(None, k, v, gcol, grow, s, m, d, False)
                      for (d, _), (k, v, gcol, grow) in zip(order, ins)])
        store_state(s, m)
        return carry

    lax.fori_loop(0, nc, ctx_step, 0)

    def lat_step(j, carry, second):
        order = _scan_order(j, n, SCAN_UNROLL)
        ins = [(cq_ref[rows(i), :], ck_ref[rows(i), :], v_ref[0, rows(i), :], gcol_ref[0, rows(i), :],
                grow_ref[0, 0, :, rows(i)]) for _, i in order]
        prev = [(acc_ref[rows(i), :], mo_ref[0, rows(i), :]) for _, i in order] if second else None
        s, m = load_state()
        outs = _round_robin([_ml_chunk(q, k, v, gcol, grow, s, m, d, True)
                             for (d, _), (q, k, v, gcol, grow) in zip(order, ins)])
        store_state(s, m)
        for idx, (_, i) in enumerate(order):
            if second:
                total = prev[idx][0] + outs[idx]
                y = total * lax.rsqrt(jnp.mean(total * total, axis=-1, keepdims=True) + EPS) * ng_ref[...]
                o_ref[0, rows(i), :] = (_sigmoid(prev[idx][1]) * y).astype(o_ref.dtype)
            else:
                acc_ref[rows(i), :] = outs[idx]
        return carry

    half = n // (2 * SCAN_UNROLL)
    lax.fori_loop(0, half, functools.partial(lat_step, second=False), 0)
    lax.fori_loop(half, 2 * half, functools.partial(lat_step, second=True), 0)


def _mlstm(zm_x, gcol_x, grow_x, zm_c, gcol_c, grow_c, conv_w, conv_b, norm_g):
    bsz, seq, _ = zm_x.shape
    ctx = zm_c.shape[1]
    assert seq % (2 * SCAN_UNROLL * ML_C) == 0 and ctx % ML_C == 0 and seq % GRID_W == 0
    h = ML_HEADS

    def col(l, off):
        return pl.BlockSpec((1, l, LANES), lambda b, hd: (b, 0, off + hd))

    def gates(l):
        return [pl.BlockSpec((1, l, LANES), lambda b, hd: (b, 0, hd)),
                pl.BlockSpec((1, 1, 8, l), lambda b, hd: (b, hd, 0, 0))]

    return pl.pallas_call(
        _mlstm_kernel,
        grid=(bsz, h),
        in_specs=[col(seq, 0), col(seq, h), col(seq, 2 * h), col(seq, 3 * h)] + gates(seq)
                 + [col(ctx, h), col(ctx, 2 * h)] + gates(ctx)
                 + [pl.BlockSpec((9, LANES), lambda b, hd: (0, hd)),
                    pl.BlockSpec((9, LANES), lambda b, hd: (0, h + hd)),
                    pl.BlockSpec((1, LANES), lambda b, hd: (0, hd)),
                    pl.BlockSpec((1, LANES), lambda b, hd: (0, h + hd)),
                    pl.BlockSpec((1, LANES), lambda b, hd: (0, hd))],
        out_specs=pl.BlockSpec((1, seq, LANES), lambda b, hd: (b, 0, hd)),
        out_shape=jax.ShapeDtypeStruct((bsz, seq, h * ML_DH), BF16),
        scratch_shapes=[pltpu.VMEM((seq, LANES), F32), pltpu.VMEM((seq, LANES), F32),
                        pltpu.VMEM((ctx, LANES), F32),
                        pltpu.VMEM((2, LANES, 2 * LANES), F32), pltpu.VMEM((2, 1, LANES), F32),
                        pltpu.VMEM((seq, LANES), F32)],
        compiler_params=_cparams(("arbitrary", "arbitrary")),
        name="mlstm",
    )(zm_x, zm_x, zm_x, zm_x, gcol_x, grow_x, zm_c, zm_c, gcol_c, grow_c,
      conv_w, conv_w, conv_b, conv_b, norm_g)


_G0 = 0
_E0 = N_GROUPS
RANK_BITS = 16
RANK_SPAN = 1 << RANK_BITS
ROW_UNROLL = 8
WAIT_UNROLL = 32
GATHER_RING = 3


SUBLANES = 8


def _store_token_tiles(ref2d, val):
    n, w = val.shape
    k = w // LANES
    for c in range(k):
        ref2d[pl.ds(c, n, stride=k), :] = val[:, c * LANES:(c + 1) * LANES]


def _load_token_tiles(ref2d, first, n, k, step):
    return jnp.concatenate([ref2d[pl.ds(first + c, n, stride=step), :] for c in range(k)], axis=1)


def _outproj_kernel(x_ref, ga_ref, ml_ref, mod_ref, wa_ref, wb_ref, g2_ref, wrh_ref, wrl_ref, br_ref,
                    x1_ref, h2_ref, ri_ref, rw_ref, cnt_ref, base_ref):
    tm = x_ref.shape[1]

    @pl.when((pl.program_id(0) == 0) & (pl.program_id(1) == 0))
    def _():
        base_ref[...] = jnp.zeros_like(base_ref)

    mix = (jnp.dot(ga_ref[0], wa_ref[...], preferred_element_type=F32)
           + jnp.dot(ml_ref[0], wb_ref[...], preferred_element_type=F32))
    x1 = x_ref[0] + mod_ref[0, 2:3, :] * mix
    x1_ref[0] = x1
    y = x1 * lax.rsqrt(jnp.mean(x1 * x1, axis=-1, keepdims=True) + EPS) * g2_ref[...]
    h2 = y * (1.0 + mod_ref[0, 4:5, :]) + mod_ref[0, 3:4, :]
    _store_token_tiles(h2_ref, h2)

    h_hi = h2.astype(BF16)
    h_lo = (h2 - h_hi.astype(F32)).astype(BF16)
    logits = (jnp.dot(h_hi, wrh_ref[...], preferred_element_type=F32)
              + jnp.dot(h_lo, wrh_ref[...], preferred_element_type=F32)
              + jnp.dot(h_hi, wrl_ref[...], preferred_element_type=F32)) + br_ref[...]

    lane = lax.broadcasted_iota(jnp.int32, (tm, LANES), 1).astype(F32)
    neg = -jnp.inf
    big = float(LANES)
    is_g = lane < float(_E0)
    lg = jnp.where(is_g, logits, neg)
    gmax = jnp.max(lg, axis=-1, keepdims=True)
    gidx = jnp.min(jnp.where(lg == gmax, lane, big), axis=-1, keepdims=True)
    gw = 1.0 / jnp.sum(jnp.where(is_g, jnp.exp(logits - gmax), 0.0), axis=-1, keepdims=True)
    lo = float(_E0) + float(EXP_PER_GROUP) * gidx
    le = jnp.where((lane >= lo) & (lane < lo + float(EXP_PER_GROUP)), logits, neg)
    v1 = jnp.max(le, axis=-1, keepdims=True)
    i1 = jnp.min(jnp.where(le == v1, lane, big), axis=-1, keepdims=True)
    le2 = jnp.where(lane == i1, neg, le)
    v2 = jnp.max(le2, axis=-1, keepdims=True)
    i2 = jnp.min(jnp.where(le2 == v2, lane, big), axis=-1, keepdims=True)
    t = jnp.exp(v2 - v1)
    w1 = gw / (1.0 + t)
    w2 = gw * t / (1.0 + t)
    e1 = i1 - float(_E0)
    e2 = i2 - float(_E0)

    oh1 = lane == e1
    oh2 = lane == e2
    oh = jnp.where(oh1 | oh2, 1.0, 0.0)
    r = lax.broadcasted_iota(jnp.int32, (tm, tm), 0)
    c = lax.broadcasted_iota(jnp.int32, (tm, tm), 1)
    strict = jnp.where(c < r, 1.0, 0.0).astype(BF16)
    before = jnp.dot(strict, oh.astype(BF16), preferred_element_type=F32) + base_ref[...]
    rank1 = jnp.sum(jnp.where(oh1, before, 0.0), axis=-1, keepdims=True)
    rank2 = jnp.sum(jnp.where(oh2, before, 0.0), axis=-1, keepdims=True)
    total = base_ref[...] + jnp.sum(oh, axis=0, keepdims=True)
    base_ref[...] = total
    cnt_ref[...] = total

    ids = jnp.where(lane == 0.0, e1 * float(RANK_SPAN) + rank1,
                    jnp.where(lane == 1.0, e2 * float(RANK_SPAN) + rank2, 0.0))
    ri_ref[0] = ids.astype(jnp.int32)
    rw_ref[0] = jnp.where(lane == 0.0, w1, jnp.where(lane == 1.0, w2, 0.0))


def _outproj(x, gla_o, ml_o, mods, wa, wb, g2, wrh, wrl, br, tm):
    bsz, seq, d = x.shape
    const = lambda shape: pl.BlockSpec(shape, lambda b, i: (0,) * len(shape))
    tile = lambda w: pl.BlockSpec((1, tm, w), lambda b, i: (b, i, 0))
    return pl.pallas_call(
        _outproj_kernel,
        grid=(bsz, seq // tm),
        in_specs=[tile(d), tile(gla_o.shape[2]), tile(ml_o.shape[2]),
                  pl.BlockSpec((1, N_MOD, d), lambda b, i: (b, 0, 0)),
                  const(wa.shape), const(wb.shape), const((1, d)),
                  const(wrh.shape), const(wrl.shape), const((1, LANES))],
        out_specs=[tile(d),
                   pl.BlockSpec((tm * d // LANES, LANES), lambda b, i: (b * (seq // tm) + i, 0)),
                   tile(LANES), tile(LANES), const((1, LANES))],
        out_shape=[jax.ShapeDtypeStruct((bsz, seq, d), F32),
                   jax.ShapeDtypeStruct((bsz * seq * d // LANES, LANES), F32),
                   jax.ShapeDtypeStruct((bsz, seq, LANES), jnp.int32),
                   jax.ShapeDtypeStruct((bsz, seq, LANES), F32),
                   jax.ShapeDtypeStruct((1, LANES), F32)],
        scratch_shapes=[pltpu.VMEM((1, LANES), F32)],
        compiler_params=_cparams(("arbitrary", "arbitrary")),
        name="outproj",
    )(x, gla_o, ml_o, mods, wa, wb, g2, wrh, wrl, br)


def _experts_kernel(dest_ref, ps_ref, cnt_ref, be_ref, nv_ref, meta_ref, h_hbm, w1_ref, w2_ref, ytok_hbm,
                    src_ref, xbuf, ybuf, w1c_ref, w2c_ref, gsem, ssem):
    i = pl.program_id(0)
    n_steps = pl.num_programs(0)
    n_used = meta_ref[0]
    tr = SUBLANES
    n_tok = h_hbm.shape[0] // tr
    n_rows = src_ref.shape[0]
    blk = xbuf.shape[1] // tr
    n_x = xbuf.shape[0]
    slot = lax.rem(i, 2)
    xslot = lax.rem(i, n_x)

    def slab(j):
        return pl.ds(pl.multiple_of(j * tr, tr), tr)

    def gather_copy(tok, s, r):
        return pltpu.make_async_copy(h_hbm.at[slab(tok), :], xbuf.at[s, slab(r), :], gsem.at[s])

    def scatter_copy(a, s, r):
        return pltpu.make_async_copy(ybuf.at[s, slab(r), :], ytok_hbm.at[slab(a), :], ssem.at[s])

    def rows_loop(body):
        def step(r2, c):
            body(2 * r2, 0)
            body(2 * r2 + 1, 1)
            return c
        lax.fori_loop(0, blk // 2, step, 0, unroll=ROW_UNROLL // 2)

    def gather_row(b, s, r, prio):
        tok = lax.shift_right_logical(src_ref[b * blk + r], 1)
        gather_copy(jnp.minimum(tok, n_tok - 1), s, r).start(priority=prio)

    def issue_gather(b, s):
        rows_loop(lambda r, prio: gather_row(b, s, r, prio))

    def wait_rows(copy):
        def step(r, c):
            copy.wait()
            return c
        lax.fori_loop(0, blk, step, 0, unroll=WAIT_UNROLL)

    def wait_gather(s):
        wait_rows(gather_copy(0, s, 0))

    def wait_scatter(s):
        wait_rows(scatter_copy(0, s, 0))

    @pl.when(i == 0)
    def _():
        ybuf[...] = jnp.zeros_like(ybuf)
        for s in range(2):
            tail = ytok_hbm.at[pl.ds((2 * n_tok + s * blk) * tr, blk * tr), :]
            cp = pltpu.make_async_copy(ybuf.at[s], tail, ssem.at[s])
            cp.start()
            cp.wait()

        def put(a, c):
            src_ref[dest_ref[a]] = a
            return c
        lax.fori_loop(0, 2 * n_tok, put, 0, unroll=16)

        def pad(j, c):
            src_ref[j] = 2 * n_tok + (j & (2 * blk - 1))
            return c

        def pad_expert(e, c):
            lax.fori_loop(ps_ref[e] + cnt_ref[e], ps_ref[e + 1], pad, 0)
            return c
        lax.fori_loop(0, cnt_ref.shape[0], pad_expert, 0)
        used_end = ps_ref[cnt_ref.shape[0]]
        lax.fori_loop(used_end, jnp.minimum(used_end + (n_x - 1) * blk, n_rows), pad, 0)
        for b in range(n_x - 1):
            issue_gather(min(b, n_rows // blk - 1), b)

    @pl.when(i < n_used)
    def _():
        wait_gather(xslot)

        @pl.when((i == 0) | (be_ref[i] != be_ref[jnp.maximum(i - 1, 0)]))
        def _():
            w1c_ref[...] = w1_ref[0].astype(BF16)
            w2c_ref[...] = w2_ref[0].astype(BF16)

        @pl.when(i >= 2)
        def _():
            wait_scatter(slot)

        row = lax.broadcasted_iota(jnp.int32, (blk, 1), 0)
        x = _load_token_tiles(xbuf.at[xslot], 0, blk, tr, tr)
        x = jnp.where(row < nv_ref[i], x, 0.0).astype(BF16)
        h = jnp.dot(x, w1c_ref[...], preferred_element_type=F32)
        a = (_silu(h[:, :D_EXPERT]) * h[:, D_EXPERT:]).astype(BF16)
        _store_token_tiles(ybuf.at[slot], jnp.dot(a, w2c_ref[...], preferred_element_type=F32))
        ahead = jnp.minimum(i + n_x - 1, n_steps - 1)
        aslot = lax.rem(i + n_x - 1, n_x)

        def move_row(r, prio):
            scatter_copy(src_ref[i * blk + r], slot, r).start(priority=prio)
            gather_row(ahead, aslot, r, prio)
        rows_loop(move_row)

    @pl.when(i == n_steps - 1)
    def _():
        for b in range(n_x - 1):
            wait_gather(lax.rem(n_used + b, n_x))
        wait_scatter(0)
        wait_scatter(1)


def _experts(dest, pad_start, counts, block_e, block_nv, meta, h2, w_in, w_out, nb):
    d = w_in.shape[1]
    tr = d // LANES
    assert tr == SUBLANES, "a token row must fill exactly one (8, 128) tile"
    n_tok = h2.shape[0] // tr
    de2 = w_in.shape[2]
    n_rows = nb * MOE_BLK
    assert 2 * n_tok >= 2 * MOE_BLK
    assert MOE_BLK & (MOE_BLK - 1) == 0
    wmap = lambda i, pk, ps, cnt, be, nv, meta: (be[i], 0, 0)
    return pl.pallas_call(
        _experts_kernel,
        grid_spec=pltpu.PrefetchScalarGridSpec(
            num_scalar_prefetch=6, grid=(nb,),
            in_specs=[pl.BlockSpec(memory_space=pl.ANY),
                      pl.BlockSpec((1, d, de2), wmap),
                      pl.BlockSpec((1, de2 // 2, d), wmap)],
            out_specs=pl.BlockSpec(memory_space=pl.ANY),
            scratch_shapes=[pltpu.SMEM((n_rows,), jnp.int32),
                            pltpu.VMEM((GATHER_RING, MOE_BLK * tr, LANES), F32),
                            pltpu.VMEM((2, MOE_BLK * tr, LANES), F32),
                            pltpu.VMEM((d, de2), BF16), pltpu.VMEM((de2 // 2, d), BF16),
                            pltpu.SemaphoreType.DMA((GATHER_RING,)), pltpu.SemaphoreType.DMA((2,))]),
        out_shape=jax.ShapeDtypeStruct(((2 * n_tok + 2 * MOE_BLK) * tr, LANES), F32),
        compiler_params=_cparams(("arbitrary",)),
        name="experts",
    )(dest, pad_start, counts, block_e, block_nv, meta, h2, w_in, w_out)


def _combine_kernel(x1_ref, y_ref, rw_ref, mod_ref, fg_ref, o_ref):
    tc, d = x1_ref.shape
    tr = d // LANES
    y1 = _load_token_tiles(y_ref, 0, tc, tr, 2 * tr)
    y2 = _load_token_tiles(y_ref, tr, tc, tr, 2 * tr)
    moe = rw_ref[:, 0:1] * y1 + rw_ref[:, 1:2] * y2
    x2 = x1_ref[...] + mod_ref[0, 5:6, :] * moe
    o_ref[...] = x2 * lax.rsqrt(jnp.mean(x2 * x2, axis=-1, keepdims=True) + EPS) * fg_ref[...]


def _combine(x1, ytok, rw, mods, fg, tokens_per_batch, tc):
    n_tok, d = x1.shape
    tiles_per_batch = tokens_per_batch // tc
    return pl.pallas_call(
        _combine_kernel,
        grid=(n_tok // tc,),
        in_specs=[pl.BlockSpec((tc, d), lambda i: (i, 0)),
                  pl.BlockSpec((2 * tc * d // LANES, LANES), lambda i: (i, 0)),
                  pl.BlockSpec((tc, LANES), lambda i: (i, 0)),
                  pl.BlockSpec((1, N_MOD, d), lambda i: (i // tiles_per_batch, 0, 0)),
                  pl.BlockSpec((1, d), lambda i: (0, 0))],
        out_specs=pl.BlockSpec((tc, d), lambda i: (i, 0)),
        out_shape=jax.ShapeDtypeStruct((n_tok, d), F32),
        compiler_params=_cparams(("arbitrary",)),
        name="combine",
    )(x1, ytok, rw, mods, fg)


def _prep_inproj_weights(w_in, gla_up_w, gla_up_b, ml_i_b, ml_f_b):
    d = w_in.shape[0]
    o_gq, o_gk, o_gv, o_gg = 0, GLA_QK_W, 2 * GLA_QK_W, 2 * GLA_QK_W + GLA_V_W
    o_lr = o_gg + GLA_V_W
    o_mqk = o_lr + 2 * GLA_LR
    o_mi = o_mqk + 4 * ML_W
    o_mf = o_mi + 2 * ML_HEADS

    def pad_heads(off):
        w = w_in[:, off:off + GLA_QK_W].reshape(d, GLA_HEADS, GLA_DK)
        return jnp.pad(w, ((0, 0), (0, 0), (0, LANES - GLA_DK))).reshape(d, GLA_HEADS * LANES)

    wg = jnp.concatenate([pad_heads(o_gq), pad_heads(o_gk), w_in[:, o_gv:o_gg], w_in[:, o_gg:o_lr]], axis=1)
    wm = w_in[:, o_mqk:o_mi]
    ws = jnp.concatenate([w_in[:, o_lr:o_mqk], w_in[:, o_mi:o_mf + 2 * ML_HEADS],
                          jnp.zeros((d, LANES - 2 * GLA_LR - 4 * ML_HEADS), w_in.dtype)], axis=1)
    bias = jnp.zeros((LANES,), F32)
    bias = bias.at[_MI0:_MI0 + 2 * ML_HEADS].set(ml_i_b.reshape(-1))
    bias = bias.at[_MF0:_MF0 + 2 * ML_HEADS].set(ml_f_b.reshape(-1))
    up = gla_up_w.reshape(2, GLA_LR, GLA_HEADS, GLA_DK).transpose(2, 0, 1, 3)
    wup = jnp.zeros((GLA_HEADS, 2, LANES, LANES), F32)
    for dr in range(2):
        wup = wup.at[:, dr, dr * GLA_LR:(dr + 1) * GLA_LR, :GLA_DK].set(up[:, dr])
    bup = jnp.pad(gla_up_b.reshape(2, GLA_HEADS, GLA_DK).transpose(1, 0, 2),
                  ((0, 0), (0, 0), (0, LANES - GLA_DK))).reshape(GLA_HEADS, 2, 1, LANES)
    return (wg.astype(BF16), wm.astype(BF16), ws.astype(BF16), ws.T.astype(BF16),
            bias.reshape(1, LANES), bias.reshape(LANES, 1), wup.astype(BF16), bup)


def _layer(x, ctx, mods, norm1_g, w_in, gla_up_w, gla_up_b, gla_norm_g, ml_conv_w, ml_conv_b,
           ml_i_b, ml_f_b, ml_norm_g, w_out, norm2_g, rg_w, rg_b, re_w, re_b, e_w_in, e_w_out, final_g):
    bsz, seq, d = x.shape
    n_tok = bsz * seq
    wg, wm, ws, wst, bcol, brow, wup, bup = _prep_inproj_weights(w_in, gla_up_w, gla_up_b, ml_i_b, ml_f_b)
    g1 = norm1_g.reshape(1, d)
    zg_x, zm_x, zs_x, gcol_x, grow_x = _inproj(x, mods, lambda b: b, g1, wg, wm, ws, wst, bcol, brow, 256)
    zg_c, zm_c, zs_c, gcol_c, grow_c = _inproj(ctx, mods, lambda b: bsz, g1, wg, wm, ws, wst, bcol, brow,
                                               min(256, ctx.shape[1]))
    gla_o = _gla(zg_x, zs_x, zg_c, zs_c, wup, bup, gla_norm_g.reshape(1, -1))
    ml_o = _mlstm(zm_x, gcol_x, grow_x, zm_c, gcol_c, grow_c,
                  ml_conv_w.reshape(9, -1), ml_conv_b.reshape(1, -1), ml_norm_g.reshape(1, -1))

    wr = jnp.zeros((d, LANES), F32).at[:, _G0:_E0].set(rg_w).at[:, _E0:_E0 + N_EXPERTS].set(re_w)
    br = jnp.zeros((1, LANES), F32).at[0, _G0:_E0].set(rg_b).at[0, _E0:_E0 + N_EXPERTS].set(re_b)
    wrh = wr.astype(BF16)
    wrl = (wr - wrh.astype(F32)).astype(BF16)
    x1, h2, ri, rw, cnt = _outproj(x, gla_o, ml_o, mods, w_out[:GLA_V_W].astype(BF16),
                                   w_out[GLA_V_W:].astype(BF16), norm2_g.reshape(1, d), wrh, wrl, br, 256)

    counts = cnt[0, :N_EXPERTS].astype(jnp.int32)
    nblk = (counts + MOE_BLK - 1) // MOE_BLK
    blk_end = jnp.cumsum(nblk)
    blk_start = blk_end - nblk
    n_used = blk_end[-1]
    nb_max = (2 * n_tok) // MOE_BLK + N_EXPERTS
    blk = jnp.arange(nb_max, dtype=jnp.int32)
    blk_c = jnp.minimum(blk, n_used - 1)
    onehot = (blk_c[:, None] >= blk_start[None, :]) & (blk_c[:, None] < blk_end[None, :])
    pick = lambda v: jnp.sum(jnp.where(onehot, v[None, :], 0), axis=1)
    block_e = pick(jnp.arange(N_EXPERTS, dtype=jnp.int32)).astype(jnp.int32)
    block_nv = jnp.clip(pick(counts) - (blk_c - pick(blk_start)) * MOE_BLK, 0, MOE_BLK)
    block_nv = jnp.where(blk < n_used, block_nv, 0).astype(jnp.int32)
    pad_start = (jnp.concatenate([blk_start, blk_end[-1:]]) * MOE_BLK).astype(jnp.int32)
    packed = ri.reshape(n_tok, LANES)[:, 0:2].reshape(-1)
    e_of = lax.shift_right_logical(packed, RANK_BITS)
    start_of = jnp.sum(jnp.where(e_of[:, None] == jnp.arange(N_EXPERTS, dtype=jnp.int32)[None, :],
                                 pad_start[None, :N_EXPERTS], 0), axis=1)
    dest = (start_of + (packed & (RANK_SPAN - 1))).astype(jnp.int32)
    meta = jnp.stack([n_used, n_used]).astype(jnp.int32)

    ytok = _experts(dest, pad_start, counts, block_e, block_nv, meta, h2, e_w_in, e_w_out, nb_max)
    out = _combine(x1.reshape(n_tok, d), ytok, rw.reshape(n_tok, LANES), mods, final_g.reshape(1, d), seq, 256)
    return out.reshape(bsz, seq, d)


def kernel(x, c, ctx, c_ctx, ada_w, ada_b, norm1_g, w_in, gla_up_w, gla_up_b, gla_norm_g, ml_conv_w, ml_conv_b,
           ml_i_b, ml_f_b, ml_norm_g, w_out, norm2_g, router_group_w, router_group_b, router_expert_w,
           router_expert_b, expert_w_in, expert_w_out, final_norm_g):
    assert ada_w.shape[0] == 1, "single-layer stack"
    bsz, d = c.shape
    cc = jnp.concatenate([c, c_ctx[None, :], jnp.zeros((8 - bsz - 1, d), F32)], axis=0)
    mods = _modulation(cc, ada_w[0], ada_b[0]).reshape(8, N_MOD, d)
    return _layer(x, ctx, mods, norm1_g[0], w_in[0], gla_up_w[0], gla_up_b[0], gla_norm_g[0],
                  ml_conv_w[0], ml_conv_b[0], ml_i_b[0], ml_f_b[0], ml_norm_g[0], w_out[0], norm2_g[0],
                  router_group_w[0], router_group_b[0], router_expert_w[0], router_expert_b[0],
                  expert_w_in[0], expert_w_out[0], final_norm_g)
```

```python
import functools

import jax
import jax.numpy as jnp
from jax import lax
from jax.experimental import pallas as pl
from jax.experimental.pallas import tpu as pltpu

F32 = jnp.float32
BF16 = jnp.bfloat16

D_MODEL = 1024
GRID_W = 64
N_MOD = 6
EPS = 1e-6

GLA_HEADS = 4
GLA_DK = 64
GLA_DV = 128
GLA_LR = 16
GLA_TAU = 16.0
GLA_C = 128
SCAN_UNROLL = 4

ML_HEADS = 4
ML_DH = 128
ML_C = 128

N_GROUPS = 4
EXP_PER_GROUP = 8
N_EXPERTS = N_GROUPS * EXP_PER_GROUP
D_EXPERT = 512
MOE_BLK = 256

GLA_QK_W = GLA_HEADS * GLA_DK
GLA_V_W = GLA_HEADS * GLA_DV
ML_W = ML_HEADS * ML_DH
LANES = 128
VMEM_LIMIT = 56 * 1024 * 1024

_LR0 = 0
_MI0 = 2 * GLA_LR
_MF0 = _MI0 + 2 * ML_HEADS


def _cparams(sem):
    return pltpu.CompilerParams(dimension_semantics=sem, vmem_limit_bytes=VMEM_LIMIT)


def _sigmoid(x):
    return 1.0 / (1.0 + jnp.exp(-x))


def _silu(x):
    return x * _sigmoid(x)


def _log_sigmoid(x):
    return jnp.minimum(x, 0.0) - jnp.log1p(jnp.exp(-jnp.abs(x)))


def _split_dot(a_bf16_exact, x, dims=None):
    x_hi = x.astype(BF16)
    x_lo = (x - x_hi.astype(F32)).astype(BF16)
    if dims is None:
        f = lambda u: jnp.dot(a_bf16_exact, u, preferred_element_type=F32)
    else:
        f = lambda u: lax.dot_general(u, a_bf16_exact, dims, preferred_element_type=F32)
    return f(x_hi) + f(x_lo)


def _mod_kernel(c_ref, w_ref, b_ref, o_ref):
    c = c_ref[...]
    s = _silu(c).astype(BF16)
    o_ref[...] = jnp.dot(s, w_ref[...].astype(BF16), preferred_element_type=F32) + b_ref[...]


def _modulation(cc, ada_w, ada_b):
    rows, d = cc.shape
    n = ada_w.shape[1]
    tn = 1536
    return pl.pallas_call(
        _mod_kernel,
        grid=(n // tn,),
        in_specs=[pl.BlockSpec((rows, d), lambda j: (0, 0)),
                  pl.BlockSpec((d, tn), lambda j: (0, j)),
                  pl.BlockSpec((1, tn), lambda j: (0, j))],
        out_specs=pl.BlockSpec((rows, tn), lambda j: (0, j)),
        out_shape=jax.ShapeDtypeStruct((rows, n), F32),
        compiler_params=_cparams(("arbitrary",)),
        name="mod",
    )(cc, ada_w, ada_b.reshape(1, n))


def _inproj_kernel(x_ref, mod_ref, g_ref, wg_ref, wm_ref, ws_ref, wst_ref, bcol_ref, brow_ref,
                   zg_ref, zm_ref, zs_ref, gcol_ref, grow_ref):
    tm = x_ref.shape[1]
    x = x_ref[0]
    y = x * lax.rsqrt(jnp.mean(x * x, axis=-1, keepdims=True) + EPS) * g_ref[...]
    h = (y * (1.0 + mod_ref[0, 1:2, :]) + mod_ref[0, 0:1, :]).astype(BF16)
    zg_ref[0] = jnp.dot(h, wg_ref[...], preferred_element_type=F32)
    zm_ref[0] = jnp.dot(h, wm_ref[...], preferred_element_type=F32)
    zs = jnp.dot(h, ws_ref[...], preferred_element_type=F32) + bcol_ref[...]
    zst = lax.dot_general(wst_ref[...], h, (((1,), (1,)), ((), ())),
                          preferred_element_type=F32) + brow_ref[...]
    zs_ref[0] = zs

    r = lax.broadcasted_iota(jnp.int32, (tm, tm), 0)
    c = lax.broadcasted_iota(jnp.int32, (tm, tm), 1)
    shift = ML_C.bit_length() - 1
    same = jnp.right_shift(r, shift) == jnp.right_shift(c, shift)
    lower = jnp.where(same & (c <= r), 1.0, 0.0).astype(BF16)
    upper = jnp.where(same & (c >= r), 1.0, 0.0).astype(BF16)
    lsf = _log_sigmoid(zs)
    a_pre = _split_dot(lower, lsf)
    a_suf = _split_dot(upper, lsf)
    lsft = _log_sigmoid(zst)
    dims = (((1,), (0,)), ((), ()))
    a_pre_t = _split_dot(upper, lsft, dims)
    a_suf_t = _split_dot(lower, lsft, dims)

    lane = lax.broadcasted_iota(jnp.int32, (tm, LANES), 1)
    for hd in range(ML_HEADS):
        cols = (a_pre[:, _MF0 + hd:_MF0 + hd + 1],
                a_suf[:, _MF0 + ML_HEADS + hd:_MF0 + ML_HEADS + hd + 1],
                zs[:, _MI0 + hd:_MI0 + hd + 1],
                zs[:, _MI0 + ML_HEADS + hd:_MI0 + ML_HEADS + hd + 1])
        slab = jnp.zeros((tm, LANES), F32)
        for j, col in enumerate(cols):
            slab = jnp.where(lane == j, col, slab)
        gcol_ref[0, :, hd * LANES:(hd + 1) * LANES] = slab
        rows = (a_pre_t[_MF0 + hd:_MF0 + hd + 1, :],
                a_suf_t[_MF0 + ML_HEADS + hd:_MF0 + ML_HEADS + hd + 1, :],
                zst[_MI0 + hd:_MI0 + hd + 1, :],
                zst[_MI0 + ML_HEADS + hd:_MI0 + ML_HEADS + hd + 1, :])
        for j, row in enumerate(rows):
            grow_ref[0, hd, j:j + 1, :] = row
        grow_ref[0, hd, 4:8, :] = jnp.zeros((4, tm), F32)


def _inproj(x, mods, mod_row_of_batch, norm_g, wg, wm, ws, wst, bcol, brow, tm):
    bsz, l, d = x.shape
    assert l % tm == 0 and tm % ML_C == 0
    const = lambda shape: pl.BlockSpec(shape, lambda b, i: (0,) * len(shape))
    return pl.pallas_call(
        _inproj_kernel,
        grid=(bsz, l // tm),
        in_specs=[pl.BlockSpec((1, tm, d), lambda b, i: (b, i, 0)),
                  pl.BlockSpec((1, N_MOD, d), lambda b, i: (mod_row_of_batch(b), 0, 0)),
                  const((1, d)), const(wg.shape), const(wm.shape), const(ws.shape), const(wst.shape),
                  const((1, LANES)), const((LANES, 1))],
        out_specs=[pl.BlockSpec((1, tm, wg.shape[1]), lambda b, i: (b, i, 0)),
                   pl.BlockSpec((1, tm, wm.shape[1]), lambda b, i: (b, i, 0)),
                   pl.BlockSpec((1, tm, LANES), lambda b, i: (b, i, 0)),
                   pl.BlockSpec((1, tm, ML_HEADS * LANES), lambda b, i: (b, i, 0)),
                   pl.BlockSpec((1, ML_HEADS, 8, tm), lambda b, i: (b, 0, 0, i))],
        out_shape=[jax.ShapeDtypeStruct((bsz, l, wg.shape[1]), F32),
                   jax.ShapeDtypeStruct((bsz, l, wm.shape[1]), F32),
                   jax.ShapeDtypeStruct((bsz, l, LANES), F32),
                   jax.ShapeDtypeStruct((bsz, l, ML_HEADS * LANES), F32),
                   jax.ShapeDtypeStruct((bsz, ML_HEADS, 8, l), F32)],
        compiler_params=_cparams(("arbitrary", "arbitrary")),
        name="inproj",
    )(x, mods, norm_g, wg, wm, ws, wst, bcol, brow)


def _round_robin(chains):
    results = [None] * len(chains)
    live = list(enumerate(chains))
    while live:
        still = []
        for idx, chain in live:
            try:
                next(chain)
                still.append((idx, chain))
            except StopIteration as done:
                results[idx] = done.value
        live = still
    return results


def _gla_chunk(q, k, v, zs, wup, bup, state, direction, want_out):
    c = k.shape[0]
    logits = jnp.dot(zs.astype(BF16), wup, preferred_element_type=F32) + bup
    yield
    g = _log_sigmoid(logits) * (1.0 / GLA_TAU)
    r = lax.broadcasted_iota(jnp.int32, (c, c), 0)
    cc = lax.broadcasted_iota(jnp.int32, (c, c), 1)
    causal = (cc <= r) if direction == 0 else (cc >= r)
    b = _split_dot(jnp.where(causal, 1.0, 0.0).astype(BF16), g)
    yield
    b_end = b[c - 1:c, :] if direction == 0 else b[0:1, :]
    kd = (k * jnp.exp(b_end - b)).astype(BF16)
    upd = lax.dot_general(v.astype(BF16), kd, (((0,), (0,)), ((), ())), preferred_element_type=F32)
    s = state[direction]
    state[direction] = jnp.exp(b_end) * s + upd
    if not want_out:
        return None
    b_mid = b[c // 2:c // 2 + 1, :]
    q_in = (q * jnp.exp(b - b_mid)).astype(BF16)
    k_in = (k * jnp.exp(b_mid - b)).astype(BF16)
    att = lax.dot_general(q_in, k_in, (((1,), (1,)), ((), ())), preferred_element_type=F32)
    inter = lax.dot_general((q * jnp.exp(b)).astype(BF16), s.astype(BF16),
                            (((1,), (1,)), ((), ())), preferred_element_type=F32)
    yield
    att = jnp.where(causal, att, 0.0)
    return jnp.dot(att.astype(BF16), v.astype(BF16), preferred_element_type=F32) + inter


def _scan_order(j, n, unroll):
    return [(d, j * unroll + u if d == 0 else n - 1 - (j * unroll + u)) for u in range(unroll) for d in range(2)]


def _gla_kernel(q_ref, k_ref, v_ref, gg_ref, zs_ref, kc_ref, vc_ref, zsc_ref,
                wup_ref, bup_ref, ng_ref, o_ref, s_ref, acc_ref):
    seq = q_ref.shape[1]
    ctx = kc_ref.shape[1]
    n = seq // GLA_C
    nc = ctx // GLA_C
    s_ref[...] = jnp.zeros_like(s_ref)

    def rows(i):
        return pl.ds(pl.multiple_of(i * GLA_C, GLA_C), GLA_C)

    def ctx_step(j, carry):
        order = _scan_order(j, nc, 1)
        ins = [(kc_ref[0, rows(i), :], vc_ref[0, rows(i), :], zsc_ref[0, rows(i), :]) for _, i in order]
        s = [s_ref[0], s_ref[1]]
        _round_robin([_gla_chunk(None, k, v, zs, wup_ref[0, d], bup_ref[0, d], s, d, False)
                      for (d, _), (k, v, zs) in zip(order, ins)])
        s_ref[0] = s[0]
        s_ref[1] = s[1]
        return carry

    lax.fori_loop(0, nc, ctx_step, 0)

    def lat_step(j, carry, second):
        order = _scan_order(j, n, SCAN_UNROLL)
        ins = [(q_ref[0, rows(i), :], k_ref[0, rows(i), :], v_ref[0, rows(i), :], zs_ref[0, rows(i), :])
               for _, i in order]
        prev = [(acc_ref[rows(i), :], gg_ref[0, rows(i), :]) for _, i in order] if second else None
        s = [s_ref[0], s_ref[1]]
        outs = _round_robin([_gla_chunk(q * (GLA_DK ** -0.5), k, v, zs, wup_ref[0, d], bup_ref[0, d], s, d, True)
                             for (d, _), (q, k, v, zs) in zip(order, ins)])
        s_ref[0] = s[0]
        s_ref[1] = s[1]
        for idx, (_, i) in enumerate(order):
            if second:
                total = prev[idx][0] + outs[idx]
                y = total * lax.rsqrt(jnp.mean(total * total, axis=-1, keepdims=True) + EPS) * ng_ref[...]
                o_ref[0, rows(i), :] = (y * _silu(prev[idx][1])).astype(o_ref.dtype)
            else:
                acc_ref[rows(i), :] = outs[idx]
        return carry

    half = n // (2 * SCAN_UNROLL)
    lax.fori_loop(0, half, functools.partial(lat_step, second=False), 0)
    lax.fori_loop(half, 2 * half, functools.partial(lat_step, second=True), 0)


def _gla(zg_x, zs_x, zg_c, zs_c, wup, bup, norm_g):
    bsz, seq, _ = zg_x.shape
    ctx = zg_c.shape[1]
    assert seq % (2 * SCAN_UNROLL * GLA_C) == 0 and ctx % GLA_C == 0
    h = GLA_HEADS

    def col(l, off):
        return pl.BlockSpec((1, l, LANES), lambda b, hd: (b, 0, off + hd))

    return pl.pallas_call(
        _gla_kernel,
        grid=(bsz, h),
        in_specs=[col(seq, 0), col(seq, h), col(seq, 2 * h), col(seq, 3 * h),
                  pl.BlockSpec((1, seq, LANES), lambda b, hd: (b, 0, 0)),
                  col(ctx, h), col(ctx, 2 * h),
                  pl.BlockSpec((1, ctx, LANES), lambda b, hd: (b, 0, 0)),
                  pl.BlockSpec((1, 2, LANES, LANES), lambda b, hd: (hd, 0, 0, 0)),
                  pl.BlockSpec((1, 2, 1, LANES), lambda b, hd: (hd, 0, 0, 0)),
                  pl.BlockSpec((1, LANES), lambda b, hd: (0, hd))],
        out_specs=pl.BlockSpec((1, seq, LANES), lambda b, hd: (b, 0, hd)),
        out_shape=jax.ShapeDtypeStruct((bsz, seq, h * GLA_DV), BF16),
        scratch_shapes=[pltpu.VMEM((2, LANES, LANES), F32), pltpu.VMEM((seq, LANES), F32)],
        compiler_params=_cparams(("arbitrary", "arbitrary")),
        name="gla",
    )(zg_x, zg_x, zg_x, zg_x, zs_x, zg_c, zg_c, zs_c, wup, bup, norm_g)


def _grid_conv_silu(src_ref, dst_ref, w_ref, b_ref, grid_w, scale):
    l = src_ref.shape[1]
    n_rows = l // grid_w
    col = lax.broadcasted_iota(jnp.int32, (grid_w, LANES), 0)

    def body(r, carry):
        acc = jnp.zeros((grid_w, LANES), F32) + b_ref[...]
        for dy in (-1, 0, 1):
            if n_rows == 1 and dy != 0:
                continue
            rr = jnp.clip(r + dy, 0, n_rows - 1)
            blk = src_ref[0, pl.ds(pl.multiple_of(rr * grid_w, grid_w), grid_w), :]
            valid = jnp.logical_and(r + dy >= 0, r + dy < n_rows)
            blk = jnp.where(valid, blk, 0.0)
            for dx in (-1, 0, 1):
                if dx == 0:
                    sh = blk
                else:
                    sh = pltpu.roll(blk, shift=(-dx) % grid_w, axis=0)
                    sh = jnp.where((col + dx >= 0) & (col + dx < grid_w), sh, 0.0)
                tap = (dy + 1) * 3 + (dx + 1)
                acc = acc + sh * w_ref[tap:tap + 1, :]
        dst_ref[pl.ds(pl.multiple_of(r * grid_w, grid_w), grid_w), :] = _silu(acc) * scale
        return carry

    lax.fori_loop(0, n_rows, body, 0)


def _ml_chunk(qt, k, vt, gcol, grow, state, mstate, direction, want_out):
    c = k.shape[0]
    a_row = grow[direction:direction + 1, :]
    i_row = grow[2 + direction:3 + direction, :]
    a_end = a_row[:, c - 1:c] if direction == 0 else a_row[:, 0:1]
    g = a_end - a_row + i_row
    g_max = jnp.max(g, axis=-1, keepdims=True)
    sub = lax.broadcasted_iota(jnp.int32, (LANES, c), 0)
    vt_aug = jnp.concatenate([vt, jnp.where(sub == 0, 1.0, 0.0)], axis=0)
    kb = k.astype(BF16)
    if want_out:
        c_col = gcol[:, direction:direction + 1] - gcol[:, 2 + direction:3 + direction]
        s_i = lax.broadcasted_iota(jnp.int32, (c, c), 0)
        t_i = lax.broadcasted_iota(jnp.int32, (c, c), 1)
        visible = (s_i <= t_i) if direction == 0 else (s_i >= t_i)
        dmat = jnp.where(visible, a_row - c_col, -jnp.inf)
        d_max = jnp.max(dmat, axis=0, keepdims=True)
        qb = qt.astype(BF16)
        kq = jnp.dot(kb, qb, preferred_element_type=F32)
    yield
    s, m = state[direction], mstate[direction]
    m_new = jnp.maximum(a_end + m, g_max)
    decay = jnp.exp(a_end + m - m_new)
    vw = (vt_aug * jnp.exp(g - m_new)).astype(BF16)
    state[direction] = decay * s + jnp.dot(vw, kb, preferred_element_type=F32)
    mstate[direction] = m_new
    if not want_out:
        return None
    inter = a_row + m
    m_t = jnp.maximum(inter, d_max)
    w_inter = jnp.exp(inter - m_t)
    p = (kq * jnp.exp(dmat - m_t)).astype(BF16)
    pv = jnp.dot(vt_aug.astype(BF16), p, preferred_element_type=F32)
    sq = jnp.dot(s.astype(BF16), qb, preferred_element_type=F32)
    yield
    both = pv + w_inter * sq
    den = both[ML_DH:ML_DH + 1, :]
    return both[:ML_DH, :] / jnp.maximum(jnp.abs(den), jnp.exp(-m_t))


def _mlstm_kernel(q_ref, k_ref, v_ref, mo_ref, gcol_ref, grow_ref,
                  kc_ref, vc_ref, growc_ref,
                  wq_ref, wk_ref, bq_ref, bk_ref, ng_ref, o_ref,
                  cq_ref, ck_ref, ckc_ref, qt_ref, vt_ref, vct_ref, s_ref, m_ref, acc_ref):
    seq = q_ref.shape[1]
    ctx = kc_ref.shape[1]
    n = seq // ML_C
    nc = ctx // ML_C
    _grid_conv_silu(q_ref, cq_ref, wq_ref, bq_ref, GRID_W, 1.0)
    _grid_conv_silu(k_ref, ck_ref, wk_ref, bk_ref, GRID_W, ML_DH ** -0.5)
    _grid_conv_silu(kc_ref, ckc_ref, wk_ref, bk_ref, ctx, ML_DH ** -0.5)
    s_ref[...] = jnp.zeros_like(s_ref)
    m_ref[...] = jnp.zeros_like(m_ref)

    def rows(i):
        return pl.ds(pl.multiple_of(i * ML_C, ML_C), ML_C)

    def transpose_chunks(i, carry):
        qt_ref[:, rows(i)] = cq_ref[rows(i), :].T
        vt_ref[:, rows(i)] = v_ref[0, rows(i), :].T
        return carry

    lax.fori_loop(0, n, transpose_chunks, 0, unroll=2)
    for i in range(nc):
        vct_ref[:, i * ML_C:(i + 1) * ML_C] = vc_ref[0, i * ML_C:(i + 1) * ML_C, :].T

    def load_state():
        return [s_ref[0], s_ref[1]], [m_ref[0, :, 0:1], m_ref[1, :, 0:1]]

    def store_state(s, m):
        for d in range(2):
            s_ref[d] = s[d]
            m_ref[d] = jnp.broadcast_to(m[d], m_ref.shape[1:])

    def ctx_step(j, carry):
        order = _scan_order(j, nc, 1)
        ins = [(ckc_ref[rows(i), :], vct_ref[:, rows(i)], growc_ref[0, 0, :, rows(i)]) for _, i in order]
        s, m = load_state()
        _round_robin([_ml_chunk(None, k, vt, None, grow, s, m, d, False)
                      for (d, _), (k, vt, grow) in zip(order, ins)])
        store_state(s, m)
        return carry

    lax.fori_loop(0, nc, ctx_step, 0)

    def lat_step(j, carry, second):
        order = _scan_order(j, n, SCAN_UNROLL)
        ins = [(qt_ref[:, rows(i)], ck_ref[rows(i), :], vt_ref[:, rows(i)], gcol_ref[0, rows(i), :],
                grow_ref[0, 0, :, rows(i)]) for _, i in order]
        prev = [(acc_ref[:, rows(i)], mo_ref[0, rows(i), :]) for _, i in order] if second else None
        s, m = load_state()
        outs = _round_robin([_ml_chunk(qt, k, vt, gcol, grow, s, m, d, True)
                             for (d, _), (qt, k, vt, gcol, grow) in zip(order, ins)])
        store_state(s, m)
        for idx, (_, i) in enumerate(order):
            if second:
                total = prev[idx][0] + outs[idx]
                y = total * lax.rsqrt(jnp.mean(total * total, axis=0, keepdims=True) + EPS) * ng_ref[...]
                o_ref[0, rows(i), :] = (_sigmoid(prev[idx][1]) * y.T).astype(o_ref.dtype)
            else:
                acc_ref[:, rows(i)] = outs[idx]
        return carry

    half = n // (2 * SCAN_UNROLL)
    lax.fori_loop(0, half, functools.partial(lat_step, second=False), 0)
    lax.fori_loop(half, 2 * half, functools.partial(lat_step, second=True), 0)


def _mlstm(zm_x, gcol_x, grow_x, zm_c, grow_c, conv_w, conv_b, norm_g):
    bsz, seq, _ = zm_x.shape
    ctx = zm_c.shape[1]
    assert seq % (2 * SCAN_UNROLL * ML_C) == 0 and ctx % ML_C == 0 and seq % GRID_W == 0
    h = ML_HEADS

    def col(l, off):
        return pl.BlockSpec((1, l, LANES), lambda b, hd: (b, 0, off + hd))

    def gates(l):
        return [pl.BlockSpec((1, l, LANES), lambda b, hd: (b, 0, hd)),
                pl.BlockSpec((1, 1, 8, l), lambda b, hd: (b, hd, 0, 0))]

    return pl.pallas_call(
        _mlstm_kernel,
        grid=(bsz, h),
        in_specs=[col(seq, 0), col(seq, h), col(seq, 2 * h), col(seq, 3 * h)] + gates(seq)
                 + [col(ctx, h), col(ctx, 2 * h), gates(ctx)[1]]
                 + [pl.BlockSpec((9, LANES), lambda b, hd: (0, hd)),
                    pl.BlockSpec((9, LANES), lambda b, hd: (0, h + hd)),
                    pl.BlockSpec((1, LANES), lambda b, hd: (0, hd)),
                    pl.BlockSpec((1, LANES), lambda b, hd: (0, h + hd)),
                    pl.BlockSpec((LANES, 1), lambda b, hd: (hd, 0))],
        out_specs=pl.BlockSpec((1, seq, LANES), lambda b, hd: (b, 0, hd)),
        out_shape=jax.ShapeDtypeStruct((bsz, seq, h * ML_DH), BF16),
        scratch_shapes=[pltpu.VMEM((seq, LANES), F32), pltpu.VMEM((seq, LANES), F32),
                        pltpu.VMEM((ctx, LANES), F32),
                        pltpu.VMEM((LANES, seq), F32), pltpu.VMEM((LANES, seq), F32),
                        pltpu.VMEM((LANES, ctx), F32),
                        pltpu.VMEM((2, 2 * LANES, LANES), F32), pltpu.VMEM((2, 1, LANES), F32),
                        pltpu.VMEM((LANES, seq), F32)],
        compiler_params=_cparams(("arbitrary", "arbitrary")),
        name="mlstm",
    )(zm_x, zm_x, zm_x, zm_x, gcol_x, grow_x, zm_c, zm_c, grow_c,
      conv_w, conv_w, conv_b, conv_b, norm_g.reshape(-1, 1))


_G0 = 0
_E0 = N_GROUPS
RANK_BITS = 16
RANK_SPAN = 1 << RANK_BITS
ROW_UNROLL = 8
WAIT_UNROLL = 32
GATHER_RING = 3


SUBLANES = 8


def _store_token_tiles(ref2d, val):
    n, w = val.shape
    k = w // LANES
    for c in range(k):
        ref2d[pl.ds(c, n, stride=k), :] = val[:, c * LANES:(c + 1) * LANES]


def _load_token_tiles(ref2d, first, n, k, step):
    return jnp.concatenate([ref2d[pl.ds(first + c, n, stride=step), :] for c in range(k)], axis=1)


def _outproj_kernel(x_ref, ga_ref, ml_ref, mod_ref, wa_ref, wb_ref, g2_ref, wrh_ref, wrl_ref, br_ref,
                    x1_ref, h2_ref, ri_ref, rw_ref, cnt_ref, base_ref):
    tm = x_ref.shape[1]

    @pl.when((pl.program_id(0) == 0) & (pl.program_id(1) == 0))
    def _():
        base_ref[...] = jnp.zeros_like(base_ref)

    mix = (jnp.dot(ga_ref[0], wa_ref[...], preferred_element_type=F32)
           + jnp.dot(ml_ref[0], wb_ref[...], preferred_element_type=F32))
    x1 = x_ref[0] + mod_ref[0, 2:3, :] * mix
    x1_ref[0] = x1
    y = x1 * lax.rsqrt(jnp.mean(x1 * x1, axis=-1, keepdims=True) + EPS) * g2_ref[...]
    h2 = y * (1.0 + mod_ref[0, 4:5, :]) + mod_ref[0, 3:4, :]
    _store_token_tiles(h2_ref, h2)

    h_hi = h2.astype(BF16)
    h_lo = (h2 - h_hi.astype(F32)).astype(BF16)
    logits = (jnp.dot(h_hi, wrh_ref[...], preferred_element_type=F32)
              + jnp.dot(h_lo, wrh_ref[...], preferred_element_type=F32)
              + jnp.dot(h_hi, wrl_ref[...], preferred_element_type=F32)) + br_ref[...]

    lane = lax.broadcasted_iota(jnp.int32, (tm, LANES), 1).astype(F32)
    neg = -jnp.inf
    big = float(LANES)
    is_g = lane < float(_E0)
    lg = jnp.where(is_g, logits, neg)
    gmax = jnp.max(lg, axis=-1, keepdims=True)
    gidx = jnp.min(jnp.where(lg == gmax, lane, big), axis=-1, keepdims=True)
    gw = 1.0 / jnp.sum(jnp.where(is_g, jnp.exp(logits - gmax), 0.0), axis=-1, keepdims=True)
    lo = float(_E0) + float(EXP_PER_GROUP) * gidx
    le = jnp.where((lane >= lo) & (lane < lo + float(EXP_PER_GROUP)), logits, neg)
    v1 = jnp.max(le, axis=-1, keepdims=True)
    i1 = jnp.min(jnp.where(le == v1, lane, big), axis=-1, keepdims=True)
    le2 = jnp.where(lane == i1, neg, le)
    v2 = jnp.max(le2, axis=-1, keepdims=True)
    i2 = jnp.min(jnp.where(le2 == v2, lane, big), axis=-1, keepdims=True)
    t = jnp.exp(v2 - v1)
    w1 = gw / (1.0 + t)
    w2 = gw * t / (1.0 + t)
    e1 = i1 - float(_E0)
    e2 = i2 - float(_E0)

    oh1 = lane == e1
    oh2 = lane == e2
    oh = jnp.where(oh1 | oh2, 1.0, 0.0)
    r = lax.broadcasted_iota(jnp.int32, (tm, tm), 0)
    c = lax.broadcasted_iota(jnp.int32, (tm, tm), 1)
    strict = jnp.where(c < r, 1.0, 0.0).astype(BF16)
    before = jnp.dot(strict, oh.astype(BF16), preferred_element_type=F32) + base_ref[...]
    rank1 = jnp.sum(jnp.where(oh1, before, 0.0), axis=-1, keepdims=True)
    rank2 = jnp.sum(jnp.where(oh2, before, 0.0), axis=-1, keepdims=True)
    total = base_ref[...] + jnp.sum(oh, axis=0, keepdims=True)
    base_ref[...] = total
    cnt_ref[...] = total

    ids = jnp.where(lane == 0.0, e1 * float(RANK_SPAN) + rank1,
                    jnp.where(lane == 1.0, e2 * float(RANK_SPAN) + rank2, 0.0))
    ri_ref[0] = ids.astype(jnp.int32)
    rw_ref[0] = jnp.where(lane == 0.0, w1, jnp.where(lane == 1.0, w2, 0.0))


def _outproj(x, gla_o, ml_o, mods, wa, wb, g2, wrh, wrl, br, tm):
    bsz, seq, d = x.shape
    const = lambda shape: pl.BlockSpec(shape, lambda b, i: (0,) * len(shape))
    tile = lambda w: pl.BlockSpec((1, tm, w), lambda b, i: (b, i, 0))
    return pl.pallas_call(
        _outproj_kernel,
        grid=(bsz, seq // tm),
        in_specs=[tile(d), tile(gla_o.shape[2]), tile(ml_o.shape[2]),
                  pl.BlockSpec((1, N_MOD, d), lambda b, i: (b, 0, 0)),
                  const(wa.shape), const(wb.shape), const((1, d)),
                  const(wrh.shape), const(wrl.shape), const((1, LANES))],
        out_specs=[tile(d),
                   pl.BlockSpec((tm * d // LANES, LANES), lambda b, i: (b * (seq // tm) + i, 0)),
                   tile(LANES), tile(LANES), const((1, LANES))],
        out_shape=[jax.ShapeDtypeStruct((bsz, seq, d), F32),
                   jax.ShapeDtypeStruct((bsz * seq * d // LANES, LANES), F32),
                   jax.ShapeDtypeStruct((bsz, seq, LANES), jnp.int32),
                   jax.ShapeDtypeStruct((bsz, seq, LANES), F32),
                   jax.ShapeDtypeStruct((1, LANES), F32)],
        scratch_shapes=[pltpu.VMEM((1, LANES), F32)],
        compiler_params=_cparams(("arbitrary", "arbitrary")),
        name="outproj",
    )(x, gla_o, ml_o, mods, wa, wb, g2, wrh, wrl, br)


def _experts_kernel(dest_ref, ps_ref, cnt_ref, be_ref, nv_ref, meta_ref, h_hbm, w1_ref, w2_ref, ytok_hbm,
                    src_ref, xbuf, ybuf, w1c_ref, w2c_ref, gsem, ssem):
    i = pl.program_id(0)
    n_steps = pl.num_programs(0)
    n_used = meta_ref[0]
    tr = SUBLANES
    n_tok = h_hbm.shape[0] // tr
    n_rows = src_ref.shape[0]
    blk = xbuf.shape[1] // tr
    n_x = xbuf.shape[0]
    slot = lax.rem(i, 2)
    xslot = lax.rem(i, n_x)

    def slab(j):
        return pl.ds(pl.multiple_of(j * tr, tr), tr)

    def gather_copy(tok, s, r):
        return pltpu.make_async_copy(h_hbm.at[slab(tok), :], xbuf.at[s, slab(r), :], gsem.at[s])

    def scatter_copy(a, s, r):
        return pltpu.make_async_copy(ybuf.at[s, slab(r), :], ytok_hbm.at[slab(a), :], ssem.at[s])

    def rows_loop(body):
        def step(r2, c):
            body(2 * r2, 0)
            body(2 * r2 + 1, 1)
            return c
        lax.fori_loop(0, blk // 2, step, 0, unroll=ROW_UNROLL // 2)

    def gather_row(b, s, r, prio):
        tok = lax.shift_right_logical(src_ref[b * blk + r], 1)
        gather_copy(jnp.minimum(tok, n_tok - 1), s, r).start(priority=prio)

    def issue_gather(b, s):
        rows_loop(lambda r, prio: gather_row(b, s, r, prio))

    def wait_rows(copy):
        def step(r, c):
            copy.wait()
            return c
        lax.fori_loop(0, blk, step, 0, unroll=WAIT_UNROLL)

    def wait_gather(s):
        wait_rows(gather_copy(0, s, 0))

    def wait_scatter(s):
        wait_rows(scatter_copy(0, s, 0))

    @pl.when(i == 0)
    def _():
        ybuf[...] = jnp.zeros_like(ybuf)
        for s in range(2):
            tail = ytok_hbm.at[pl.ds((2 * n_tok + s * blk) * tr, blk * tr), :]
            cp = pltpu.make_async_copy(ybuf.at[s], tail, ssem.at[s])
            cp.start()
            cp.wait()

        def put(a, c):
            src_ref[dest_ref[a]] = a
            return c
        lax.fori_loop(0, 2 * n_tok, put, 0, unroll=16)

        def pad(j, c):
            src_ref[j] = 2 * n_tok + (j & (2 * blk - 1))
            return c

        def pad_expert(e, c):
            lax.fori_loop(ps_ref[e] + cnt_ref[e], ps_ref[e + 1], pad, 0)
            return c
        lax.fori_loop(0, cnt_ref.shape[0], pad_expert, 0)
        used_end = ps_ref[cnt_ref.shape[0]]
        lax.fori_loop(used_end, jnp.minimum(used_end + (n_x - 1) * blk, n_rows), pad, 0)
        for b in range(n_x - 1):
            issue_gather(min(b, n_rows // blk - 1), b)

    @pl.when(i < n_used)
    def _():
        wait_gather(xslot)

        @pl.when((i == 0) | (be_ref[i] != be_ref[jnp.maximum(i - 1, 0)]))
        def _():
            w1c_ref[...] = w1_ref[0].astype(BF16)
            w2c_ref[...] = w2_ref[0].astype(BF16)

        @pl.when(i >= 2)
        def _():
            wait_scatter(slot)

        row = lax.broadcasted_iota(jnp.int32, (blk, 1), 0)
        x = _load_token_tiles(xbuf.at[xslot], 0, blk, tr, tr)
        x = jnp.where(row < nv_ref[i], x, 0.0).astype(BF16)
        h = jnp.dot(x, w1c_ref[...], preferred_element_type=F32)
        a = (_silu(h[:, :D_EXPERT]) * h[:, D_EXPERT:]).astype(BF16)
        _store_token_tiles(ybuf.at[slot], jnp.dot(a, w2c_ref[...], preferred_element_type=F32))
        ahead = jnp.minimum(i + n_x - 1, n_steps - 1)
        aslot = lax.rem(i + n_x - 1, n_x)

        def move_row(r, prio):
            scatter_copy(src_ref[i * blk + r], slot, r).start(priority=prio)
            gather_row(ahead, aslot, r, prio)
        rows_loop(move_row)

    @pl.when(i == n_steps - 1)
    def _():
        for b in range(n_x - 1):
            wait_gather(lax.rem(n_used + b, n_x))
        wait_scatter(0)
        wait_scatter(1)


def _experts(dest, pad_start, counts, block_e, block_nv, meta, h2, w_in, w_out, nb):
    d = w_in.shape[1]
    tr = d // LANES
    assert tr == SUBLANES, "a token row must fill exactly one (8, 128) tile"
    n_tok = h2.shape[0] // tr
    de2 = w_in.shape[2]
    n_rows = nb * MOE_BLK
    assert 2 * n_tok >= 2 * MOE_BLK
    assert MOE_BLK & (MOE_BLK - 1) == 0
    wmap = lambda i, pk, ps, cnt, be, nv, meta: (be[i], 0, 0)
    return pl.pallas_call(
        _experts_kernel,
        grid_spec=pltpu.PrefetchScalarGridSpec(
            num_scalar_prefetch=6, grid=(nb,),
            in_specs=[pl.BlockSpec(memory_space=pl.ANY),
                      pl.BlockSpec((1, d, de2), wmap),
                      pl.BlockSpec((1, de2 // 2, d), wmap)],
            out_specs=pl.BlockSpec(memory_space=pl.ANY),
            scratch_shapes=[pltpu.SMEM((n_rows,), jnp.int32),
                            pltpu.VMEM((GATHER_RING, MOE_BLK * tr, LANES), F32),
                            pltpu.VMEM((2, MOE_BLK * tr, LANES), F32),
                            pltpu.VMEM((d, de2), BF16), pltpu.VMEM((de2 // 2, d), BF16),
                            pltpu.SemaphoreType.DMA((GATHER_RING,)), pltpu.SemaphoreType.DMA((2,))]),
        out_shape=jax.ShapeDtypeStruct(((2 * n_tok + 2 * MOE_BLK) * tr, LANES), F32),
        compiler_params=_cparams(("arbitrary",)),
        name="experts",
    )(dest, pad_start, counts, block_e, block_nv, meta, h2, w_in, w_out)


def _combine_kernel(x1_ref, y_ref, rw_ref, mod_ref, fg_ref, o_ref):
    tc, d = x1_ref.shape
    tr = d // LANES
    y1 = _load_token_tiles(y_ref, 0, tc, tr, 2 * tr)
    y2 = _load_token_tiles(y_ref, tr, tc, tr, 2 * tr)
    moe = rw_ref[:, 0:1] * y1 + rw_ref[:, 1:2] * y2
    x2 = x1_ref[...] + mod_ref[0, 5:6, :] * moe
    o_ref[...] = x2 * lax.rsqrt(jnp.mean(x2 * x2, axis=-1, keepdims=True) + EPS) * fg_ref[...]


def _combine(x1, ytok, rw, mods, fg, tokens_per_batch, tc):
    n_tok, d = x1.shape
    tiles_per_batch = tokens_per_batch // tc
    return pl.pallas_call(
        _combine_kernel,
        grid=(n_tok // tc,),
        in_specs=[pl.BlockSpec((tc, d), lambda i: (i, 0)),
                  pl.BlockSpec((2 * tc * d // LANES, LANES), lambda i: (i, 0)),
                  pl.BlockSpec((tc, LANES), lambda i: (i, 0)),
                  pl.BlockSpec((1, N_MOD, d), lambda i: (i // tiles_per_batch, 0, 0)),
                  pl.BlockSpec((1, d), lambda i: (0, 0))],
        out_specs=pl.BlockSpec((tc, d), lambda i: (i, 0)),
        out_shape=jax.ShapeDtypeStruct((n_tok, d), F32),
        compiler_params=_cparams(("arbitrary",)),
        name="combine",
    )(x1, ytok, rw, mods, fg)


def _prep_inproj_weights(w_in, gla_up_w, gla_up_b, ml_i_b, ml_f_b):
    d = w_in.shape[0]
    o_gq, o_gk, o_gv, o_gg = 0, GLA_QK_W, 2 * GLA_QK_W, 2 * GLA_QK_W + GLA_V_W
    o_lr = o_gg + GLA_V_W
    o_mqk = o_lr + 2 * GLA_LR
    o_mi = o_mqk + 4 * ML_W
    o_mf = o_mi + 2 * ML_HEADS

    def pad_heads(off):
        w = w_in[:, off:off + GLA_QK_W].reshape(d, GLA_HEADS, GLA_DK)
        return jnp.pad(w, ((0, 0), (0, 0), (0, LANES - GLA_DK))).reshape(d, GLA_HEADS * LANES)

    wg = jnp.concatenate([pad_heads(o_gq), pad_heads(o_gk), w_in[:, o_gv:o_gg], w_in[:, o_gg:o_lr]], axis=1)
    wm = w_in[:, o_mqk:o_mi]
    ws = jnp.concatenate([w_in[:, o_lr:o_mqk], w_in[:, o_mi:o_mf + 2 * ML_HEADS],
                          jnp.zeros((d, LANES - 2 * GLA_LR - 4 * ML_HEADS), w_in.dtype)], axis=1)
    bias = jnp.zeros((LANES,), F32)
    bias = bias.at[_MI0:_MI0 + 2 * ML_HEADS].set(ml_i_b.reshape(-1))
    bias = bias.at[_MF0:_MF0 + 2 * ML_HEADS].set(ml_f_b.reshape(-1))
    up = gla_up_w.reshape(2, GLA_LR, GLA_HEADS, GLA_DK).transpose(2, 0, 1, 3)
    wup = jnp.zeros((GLA_HEADS, 2, LANES, LANES), F32)
    for dr in range(2):
        wup = wup.at[:, dr, dr * GLA_LR:(dr + 1) * GLA_LR, :GLA_DK].set(up[:, dr])
    bup = jnp.pad(gla_up_b.reshape(2, GLA_HEADS, GLA_DK).transpose(1, 0, 2),
                  ((0, 0), (0, 0), (0, LANES - GLA_DK))).reshape(GLA_HEADS, 2, 1, LANES)
    return (wg.astype(BF16), wm.astype(BF16), ws.astype(BF16), ws.T.astype(BF16),
            bias.reshape(1, LANES), bias.reshape(LANES, 1), wup.astype(BF16), bup)


def _layer(x, ctx, mods, norm1_g, w_in, gla_up_w, gla_up_b, gla_norm_g, ml_conv_w, ml_conv_b,
           ml_i_b, ml_f_b, ml_norm_g, w_out, norm2_g, rg_w, rg_b, re_w, re_b, e_w_in, e_w_out, final_g):
    bsz, seq, d = x.shape
    n_tok = bsz * seq
    wg, wm, ws, wst, bcol, brow, wup, bup = _prep_inproj_weights(w_in, gla_up_w, gla_up_b, ml_i_b, ml_f_b)
    g1 = norm1_g.reshape(1, d)
    zg_x, zm_x, zs_x, gcol_x, grow_x = _inproj(x, mods, lambda b: b, g1, wg, wm, ws, wst, bcol, brow, 256)
    zg_c, zm_c, zs_c, gcol_c, grow_c = _inproj(ctx, mods, lambda b: bsz, g1, wg, wm, ws, wst, bcol, brow,
                                               min(256, ctx.shape[1]))
    gla_o = _gla(zg_x, zs_x, zg_c, zs_c, wup, bup, gla_norm_g.reshape(1, -1))
    ml_o = _mlstm(zm_x, gcol_x, grow_x, zm_c, grow_c,
                  ml_conv_w.reshape(9, -1), ml_conv_b.reshape(1, -1), ml_norm_g.reshape(1, -1))

    wr = jnp.zeros((d, LANES), F32).at[:, _G0:_E0].set(rg_w).at[:, _E0:_E0 + N_EXPERTS].set(re_w)
    br = jnp.zeros((1, LANES), F32).at[0, _G0:_E0].set(rg_b).at[0, _E0:_E0 + N_EXPERTS].set(re_b)
    wrh = wr.astype(BF16)
    wrl = (wr - wrh.astype(F32)).astype(BF16)
    x1, h2, ri, rw, cnt = _outproj(x, gla_o, ml_o, mods, w_out[:GLA_V_W].astype(BF16),
                                   w_out[GLA_V_W:].astype(BF16), norm2_g.reshape(1, d), wrh, wrl, br, 256)

    counts = cnt[0, :N_EXPERTS].astype(jnp.int32)
    nblk = (counts + MOE_BLK - 1) // MOE_BLK
    blk_end = jnp.cumsum(nblk)
    blk_start = blk_end - nblk
    n_used = blk_end[-1]
    nb_max = (2 * n_tok) // MOE_BLK + N_EXPERTS
    blk = jnp.arange(nb_max, dtype=jnp.int32)
    blk_c = jnp.minimum(blk, n_used - 1)
    onehot = (blk_c[:, None] >= blk_start[None, :]) & (blk_c[:, None] < blk_end[None, :])
    pick = lambda v: jnp.sum(jnp.where(onehot, v[None, :], 0), axis=1)
    block_e = pick(jnp.arange(N_EXPERTS, dtype=jnp.int32)).astype(jnp.int32)
    block_nv = jnp.clip(pick(counts) - (blk_c - pick(blk_start)) * MOE_BLK, 0, MOE_BLK)
    block_nv = jnp.where(blk < n_used, block_nv, 0).astype(jnp.int32)
    pad_start = (jnp.concatenate([blk_start, blk_end[-1:]]) * MOE_BLK).astype(jnp.int32)
    packed = ri.reshape(n_tok, LANES)[:, 0:2].reshape(-1)
    e_of = lax.shift_right_logical(packed, RANK_BITS)
    start_of = jnp.sum(jnp.where(e_of[:, None] == jnp.arange(N_EXPERTS, dtype=jnp.int32)[None, :],
                                 pad_start[None, :N_EXPERTS], 0), axis=1)
    dest = (start_of + (packed & (RANK_SPAN - 1))).astype(jnp.int32)
    meta = jnp.stack([n_used, n_used]).astype(jnp.int32)

    ytok = _experts(dest, pad_start, counts, block_e, block_nv, meta, h2, e_w_in, e_w_out, nb_max)
    out = _combine(x1.reshape(n_tok, d), ytok, rw.reshape(n_tok, LANES), mods, final_g.reshape(1, d), seq, 256)
    return out.reshape(bsz, seq, d)


def kernel(x, c, ctx, c_ctx, ada_w, ada_b, norm1_g, w_in, gla_up_w, gla_up_b, gla_norm_g, ml_conv_w, ml_conv_b,
           ml_i_b, ml_f_b, ml_norm_g, w_out, norm2_g, router_group_w, router_group_b, router_expert_w,
           router_expert_b, expert_w_in, expert_w_out, final_norm_g):
    assert ada_w.shape[0] == 1, "single-layer stack"
    bsz, d = c.shape
    cc = jnp.concatenate([c, c_ctx[None, :], jnp.zeros((8 - bsz - 1, d), F32)], axis=0)
    mods = _modulation(cc, ada_w[0], ada_b[0]).reshape(8, N_MOD, d)
    return _layer(x, ctx, mods, norm1_g[0], w_in[0], gla_up_w[0], gla_up_b[0], gla_norm_g[0],
                  ml_conv_w[0], ml_conv_b[0], ml_i_b[0], ml_f_b[0], ml_norm_g[0], w_out[0], norm2_g[0],
                  router_group_w[0], router_group_b[0], router_expert_w[0], router_expert_b[0],
                  expert_w_in[0], expert_w_out[0], final_norm_g)
```

```python
import functools

import jax
import jax.numpy as jnp
from jax import lax
from jax.experimental import pallas as pl
from jax.experimental.pallas import tpu as pltpu

F32 = jnp.float32
BF16 = jnp.bfloat16

D_MODEL = 1024
GRID_W = 64
N_MOD = 6
EPS = 1e-6

GLA_HEADS = 4
GLA_DK = 64
GLA_DV = 128
GLA_LR = 16
GLA_TAU = 16.0
GLA_C = 128
SCAN_UNROLL = 4

ML_HEADS = 4
ML_DH = 128
ML_C = 128

N_GROUPS = 4
EXP_PER_GROUP = 8
N_EXPERTS = N_GROUPS * EXP_PER_GROUP
D_EXPERT = 512
MOE_BLK = 256

GLA_QK_W = GLA_HEADS * GLA_DK
GLA_V_W = GLA_HEADS * GLA_DV
ML_W = ML_HEADS * ML_DH
LANES = 128
VMEM_LIMIT = 56 * 1024 * 1024

_LR0 = 0
_MI0 = 2 * GLA_LR
_MF0 = _MI0 + 2 * ML_HEADS


def _cparams(sem):
    return pltpu.CompilerParams(dimension_semantics=sem, vmem_limit_bytes=VMEM_LIMIT)


def _sigmoid(x):
    return 1.0 / (1.0 + jnp.exp(-x))


def _silu(x):
    return x * _sigmoid(x)


def _log_sigmoid(x):
    return jnp.minimum(x, 0.0) - jnp.log1p(jnp.exp(-jnp.abs(x)))


def _split_dot(a_bf16_exact, x, dims=None):
    x_hi = x.astype(BF16)
    x_lo = (x - x_hi.astype(F32)).astype(BF16)
    if dims is None:
        f = lambda u: jnp.dot(a_bf16_exact, u, preferred_element_type=F32)
    else:
        f = lambda u: lax.dot_general(u, a_bf16_exact, dims, preferred_element_type=F32)
    return f(x_hi) + f(x_lo)


def _mod_kernel(c_ref, w_ref, b_ref, o_ref):
    c = c_ref[...]
    s = _silu(c).astype(BF16)
    o_ref[...] = jnp.dot(s, w_ref[...].astype(BF16), preferred_element_type=F32) + b_ref[...]


def _modulation(cc, ada_w, ada_b):
    rows, d = cc.shape
    n = ada_w.shape[1]
    tn = 1536
    return pl.pallas_call(
        _mod_kernel,
        grid=(n // tn,),
        in_specs=[pl.BlockSpec((rows, d), lambda j: (0, 0)),
                  pl.BlockSpec((d, tn), lambda j: (0, j)),
                  pl.BlockSpec((1, tn), lambda j: (0, j))],
        out_specs=pl.BlockSpec((rows, tn), lambda j: (0, j)),
        out_shape=jax.ShapeDtypeStruct((rows, n), F32),
        compiler_params=_cparams(("arbitrary",)),
        name="mod",
    )(cc, ada_w, ada_b.reshape(1, n))


def _inproj_kernel(x_ref, mod_ref, g_ref, wg_ref, wm_ref, ws_ref, wst_ref, bcol_ref, brow_ref,
                   zg_ref, zm_ref, zs_ref, gcol_ref, grow_ref):
    tm = x_ref.shape[1]
    x = x_ref[0]
    y = x * lax.rsqrt(jnp.mean(x * x, axis=-1, keepdims=True) + EPS) * g_ref[...]
    h = (y * (1.0 + mod_ref[0, 1:2, :]) + mod_ref[0, 0:1, :]).astype(BF16)
    zg_ref[0] = jnp.dot(h, wg_ref[...], preferred_element_type=F32)
    zm_ref[0] = jnp.dot(h, wm_ref[...], preferred_element_type=F32)
    zs = jnp.dot(h, ws_ref[...], preferred_element_type=F32) + bcol_ref[...]
    zst = lax.dot_general(wst_ref[...], h, (((1,), (1,)), ((), ())),
                          preferred_element_type=F32) + brow_ref[...]
    zs_ref[0] = zs

    r = lax.broadcasted_iota(jnp.int32, (tm, tm), 0)
    c = lax.broadcasted_iota(jnp.int32, (tm, tm), 1)
    shift = ML_C.bit_length() - 1
    same = jnp.right_shift(r, shift) == jnp.right_shift(c, shift)
    lower = jnp.where(same & (c <= r), 1.0, 0.0).astype(BF16)
    upper = jnp.where(same & (c >= r), 1.0, 0.0).astype(BF16)
    chunks = range(0, tm, ML_C)
    lsf = _log_sigmoid(zs)
    a_pre = _split_dot(lower, lsf)
    tot = jnp.concatenate([jnp.broadcast_to(a_pre[o + ML_C - 1:o + ML_C, :], (ML_C, LANES)) for o in chunks], axis=0)
    a_suf = tot - a_pre + lsf
    lsft = _log_sigmoid(zst)
    a_pre_t = _split_dot(upper, lsft, (((1,), (0,)), ((), ())))
    tot_t = jnp.concatenate([jnp.broadcast_to(a_pre_t[:, o + ML_C - 1:o + ML_C], (LANES, ML_C)) for o in chunks], axis=1)
    a_suf_t = tot_t - a_pre_t + lsft

    lane = lax.broadcasted_iota(jnp.int32, (tm, LANES), 1)
    for hd in range(ML_HEADS):
        cols = (a_pre[:, _MF0 + hd:_MF0 + hd + 1],
                a_suf[:, _MF0 + ML_HEADS + hd:_MF0 + ML_HEADS + hd + 1],
                zs[:, _MI0 + hd:_MI0 + hd + 1],
                zs[:, _MI0 + ML_HEADS + hd:_MI0 + ML_HEADS + hd + 1])
        slab = jnp.zeros((tm, LANES), F32)
        for j, col in enumerate(cols):
            slab = jnp.where(lane == j, col, slab)
        gcol_ref[0, :, hd * LANES:(hd + 1) * LANES] = slab
        rows = (a_pre_t[_MF0 + hd:_MF0 + hd + 1, :],
                a_suf_t[_MF0 + ML_HEADS + hd:_MF0 + ML_HEADS + hd + 1, :],
                zst[_MI0 + hd:_MI0 + hd + 1, :],
                zst[_MI0 + ML_HEADS + hd:_MI0 + ML_HEADS + hd + 1, :])
        for j, row in enumerate(rows):
            grow_ref[0, hd, j:j + 1, :] = row
        grow_ref[0, hd, 4:8, :] = jnp.zeros((4, tm), F32)


def _inproj(x, mods, mod_row_of_batch, norm_g, wg, wm, ws, wst, bcol, brow, tm):
    bsz, l, d = x.shape
    assert l % tm == 0 and tm % ML_C == 0
    const = lambda shape: pl.BlockSpec(shape, lambda b, i: (0,) * len(shape))
    return pl.pallas_call(
        _inproj_kernel,
        grid=(bsz, l // tm),
        in_specs=[pl.BlockSpec((1, tm, d), lambda b, i: (b, i, 0)),
                  pl.BlockSpec((1, N_MOD, d), lambda b, i: (mod_row_of_batch(b), 0, 0)),
                  const((1, d)), const(wg.shape), const(wm.shape), const(ws.shape), const(wst.shape),
                  const((1, LANES)), const((LANES, 1))],
        out_specs=[pl.BlockSpec((1, tm, wg.shape[1]), lambda b, i: (b, i, 0)),
                   pl.BlockSpec((1, tm, wm.shape[1]), lambda b, i: (b, i, 0)),
                   pl.BlockSpec((1, tm, LANES), lambda b, i: (b, i, 0)),
                   pl.BlockSpec((1, tm, ML_HEADS * LANES), lambda b, i: (b, i, 0)),
                   pl.BlockSpec((1, ML_HEADS, 8, tm), lambda b, i: (b, 0, 0, i))],
        out_shape=[jax.ShapeDtypeStruct((bsz, l, wg.shape[1]), F32),
                   jax.ShapeDtypeStruct((bsz, l, wm.shape[1]), F32),
                   jax.ShapeDtypeStruct((bsz, l, LANES), F32),
                   jax.ShapeDtypeStruct((bsz, l, ML_HEADS * LANES), F32),
                   jax.ShapeDtypeStruct((bsz, ML_HEADS, 8, l), F32)],
        compiler_params=_cparams(("arbitrary", "arbitrary")),
        name="inproj",
    )(x, mods, norm_g, wg, wm, ws, wst, bcol, brow)


def _round_robin(chains):
    results = [None] * len(chains)
    live = list(enumerate(chains))
    while live:
        still = []
        for idx, chain in live:
            try:
                next(chain)
                still.append((idx, chain))
            except StopIteration as done:
                results[idx] = done.value
        live = still
    return results


def _gla_chunk(q, k, v, zs, wup, bup, state, direction, want_out):
    c = k.shape[0]
    logits = jnp.dot(zs.astype(BF16), wup, preferred_element_type=F32) + bup
    yield
    g = _log_sigmoid(logits) * (1.0 / GLA_TAU)
    r = lax.broadcasted_iota(jnp.int32, (c, c), 0)
    cc = lax.broadcasted_iota(jnp.int32, (c, c), 1)
    causal = (cc <= r) if direction == 0 else (cc >= r)
    b = _split_dot(jnp.where(causal, 1.0, 0.0).astype(BF16), g)
    yield
    b_end = b[c - 1:c, :] if direction == 0 else b[0:1, :]
    kd = (k * jnp.exp(b_end - b)).astype(BF16)
    upd = lax.dot_general(v.astype(BF16), kd, (((0,), (0,)), ((), ())), preferred_element_type=F32)
    s = state[direction]
    state[direction] = jnp.exp(b_end) * s + upd
    if not want_out:
        return None
    b_mid = b[c // 2:c // 2 + 1, :]
    q_in = (q * jnp.exp(b - b_mid)).astype(BF16)
    k_in = (k * jnp.exp(b_mid - b)).astype(BF16)
    att = lax.dot_general(q_in, k_in, (((1,), (1,)), ((), ())), preferred_element_type=F32)
    inter = lax.dot_general((q * jnp.exp(b)).astype(BF16), s.astype(BF16),
                            (((1,), (1,)), ((), ())), preferred_element_type=F32)
    yield
    att = jnp.where(causal, att, 0.0)
    return jnp.dot(att.astype(BF16), v.astype(BF16), preferred_element_type=F32) + inter


def _scan_order(j, n, unroll):
    return [(d, j * unroll + u if d == 0 else n - 1 - (j * unroll + u)) for u in range(unroll) for d in range(2)]


def _gla_kernel(q_ref, k_ref, v_ref, gg_ref, zs_ref, kc_ref, vc_ref, zsc_ref,
                wup_ref, bup_ref, ng_ref, o_ref, s_ref, acc_ref):
    seq = q_ref.shape[1]
    ctx = kc_ref.shape[1]
    n = seq // GLA_C
    nc = ctx // GLA_C
    s_ref[...] = jnp.zeros_like(s_ref)

    def rows(i):
        return pl.ds(pl.multiple_of(i * GLA_C, GLA_C), GLA_C)

    heads_per_slab = LANES // GLA_DK
    lo = lax.rem(pl.program_id(1), heads_per_slab) * GLA_DK
    lane = lax.broadcasted_iota(jnp.int32, (GLA_C, LANES), 1)
    mine = (lane >= lo) & (lane < lo + GLA_DK)

    def own(t):
        return jnp.where(mine, t, 0.0)

    def ctx_step(j, carry):
        order = _scan_order(j, nc, 1)
        ins = [(own(kc_ref[0, rows(i), :]), vc_ref[0, rows(i), :], zsc_ref[0, rows(i), :]) for _, i in order]
        s = [s_ref[0], s_ref[1]]
        _round_robin([_gla_chunk(None, k, v, zs, wup_ref[0, d], bup_ref[0, d], s, d, False)
                      for (d, _), (k, v, zs) in zip(order, ins)])
        s_ref[0] = s[0]
        s_ref[1] = s[1]
        return carry

    lax.fori_loop(0, nc, ctx_step, 0)

    def lat_step(j, carry, second):
        order = _scan_order(j, n, SCAN_UNROLL)
        ins = [(own(q_ref[0, rows(i), :]), own(k_ref[0, rows(i), :]), v_ref[0, rows(i), :], zs_ref[0, rows(i), :])
               for _, i in order]
        prev = [(acc_ref[rows(i), :], gg_ref[0, rows(i), :]) for _, i in order] if second else None
        s = [s_ref[0], s_ref[1]]
        outs = _round_robin([_gla_chunk(q * (GLA_DK ** -0.5), k, v, zs, wup_ref[0, d], bup_ref[0, d], s, d, True)
                             for (d, _), (q, k, v, zs) in zip(order, ins)])
        s_ref[0] = s[0]
        s_ref[1] = s[1]
        for idx, (_, i) in enumerate(order):
            if second:
                total = prev[idx][0] + outs[idx]
                y = total * lax.rsqrt(jnp.mean(total * total, axis=-1, keepdims=True) + EPS) * ng_ref[...]
                o_ref[0, rows(i), :] = (y * _silu(prev[idx][1])).astype(o_ref.dtype)
            else:
                acc_ref[rows(i), :] = outs[idx]
        return carry

    half = n // (2 * SCAN_UNROLL)
    lax.fori_loop(0, half, functools.partial(lat_step, second=False), 0)
    lax.fori_loop(half, 2 * half, functools.partial(lat_step, second=True), 0)


def _gla(zg_x, zs_x, zg_c, zs_c, wup, bup, norm_g):
    bsz, seq, _ = zg_x.shape
    ctx = zg_c.shape[1]
    assert seq % (2 * SCAN_UNROLL * GLA_C) == 0 and ctx % GLA_C == 0
    h = GLA_HEADS

    hps = LANES // GLA_DK
    qk_slabs = h // hps

    def qk(l, off):
        return pl.BlockSpec((1, l, LANES), lambda b, hd: (b, 0, off + hd // hps))

    def col(l, off):
        return pl.BlockSpec((1, l, LANES), lambda b, hd: (b, 0, off + hd))

    return pl.pallas_call(
        _gla_kernel,
        grid=(bsz, h),
        in_specs=[qk(seq, 0), qk(seq, qk_slabs), col(seq, 2 * qk_slabs), col(seq, 2 * qk_slabs + h),
                  pl.BlockSpec((1, seq, LANES), lambda b, hd: (b, 0, 0)),
                  qk(ctx, qk_slabs), col(ctx, 2 * qk_slabs),
                  pl.BlockSpec((1, ctx, LANES), lambda b, hd: (b, 0, 0)),
                  pl.BlockSpec((1, 2, LANES, LANES), lambda b, hd: (hd, 0, 0, 0)),
                  pl.BlockSpec((1, 2, 1, LANES), lambda b, hd: (hd, 0, 0, 0)),
                  pl.BlockSpec((1, LANES), lambda b, hd: (0, hd))],
        out_specs=pl.BlockSpec((1, seq, LANES), lambda b, hd: (b, 0, hd)),
        out_shape=jax.ShapeDtypeStruct((bsz, seq, h * GLA_DV), BF16),
        scratch_shapes=[pltpu.VMEM((2, LANES, LANES), F32), pltpu.VMEM((seq, LANES), F32)],
        compiler_params=_cparams(("arbitrary", "arbitrary")),
        name="gla",
    )(zg_x, zg_x, zg_x, zg_x, zs_x, zg_c, zg_c, zs_c, wup, bup, norm_g)


def _grid_conv_silu(src_ref, dst_ref, w_ref, b_ref, grid_w, scale):
    l = src_ref.shape[1]
    n_rows = l // grid_w
    col = lax.broadcasted_iota(jnp.int32, (grid_w, LANES), 0)

    def body(r, carry):
        acc = jnp.zeros((grid_w, LANES), F32) + b_ref[...]
        for dy in (-1, 0, 1):
            if n_rows == 1 and dy != 0:
                continue
            rr = jnp.clip(r + dy, 0, n_rows - 1)
            blk = src_ref[0, pl.ds(pl.multiple_of(rr * grid_w, grid_w), grid_w), :]
            valid = jnp.logical_and(r + dy >= 0, r + dy < n_rows)
            blk = jnp.where(valid, blk, 0.0)
            for dx in (-1, 0, 1):
                if dx == 0:
                    sh = blk
                else:
                    sh = pltpu.roll(blk, shift=(-dx) % grid_w, axis=0)
                    sh = jnp.where((col + dx >= 0) & (col + dx < grid_w), sh, 0.0)
                tap = (dy + 1) * 3 + (dx + 1)
                acc = acc + sh * w_ref[tap:tap + 1, :]
        dst_ref[pl.ds(pl.multiple_of(r * grid_w, grid_w), grid_w), :] = _silu(acc) * scale
        return carry

    lax.fori_loop(0, n_rows, body, 0)


def _ml_chunk(qt, k, vt, gcol, grow, state, mstate, direction, want_out):
    c = k.shape[0]
    a_row = grow[direction:direction + 1, :]
    i_row = grow[2 + direction:3 + direction, :]
    a_end = a_row[:, c - 1:c] if direction == 0 else a_row[:, 0:1]
    g = a_end - a_row + i_row
    g_max = jnp.max(g, axis=-1, keepdims=True)
    sub = lax.broadcasted_iota(jnp.int32, (LANES, c), 0)
    vt_aug = jnp.concatenate([vt, jnp.where(sub == 0, 1.0, 0.0)], axis=0)
    kb = k.astype(BF16)
    if want_out:
        c_col = gcol[:, direction:direction + 1] - gcol[:, 2 + direction:3 + direction]
        s_i = lax.broadcasted_iota(jnp.int32, (c, c), 0)
        t_i = lax.broadcasted_iota(jnp.int32, (c, c), 1)
        visible = (s_i <= t_i) if direction == 0 else (s_i >= t_i)
        dmat = jnp.where(visible, a_row - c_col, -jnp.inf)
        d_max = jnp.max(dmat, axis=0, keepdims=True)
        qb = qt.astype(BF16)
        kq = jnp.dot(kb, qb, preferred_element_type=F32)
    yield
    s, m = state[direction], mstate[direction]
    m_new = jnp.maximum(a_end + m, g_max)
    decay = jnp.exp(a_end + m - m_new)
    vw = (vt_aug * jnp.exp(g - m_new)).astype(BF16)
    state[direction] = decay * s + jnp.dot(vw, kb, preferred_element_type=F32)
    mstate[direction] = m_new
    if not want_out:
        return None
    inter = a_row + m
    m_t = jnp.maximum(inter, d_max)
    w_inter = jnp.exp(inter - m_t)
    p = (kq * jnp.exp(dmat - m_t)).astype(BF16)
    pv = jnp.dot(vt_aug.astype(BF16), p, preferred_element_type=F32)
    sq = jnp.dot(s.astype(BF16), qb, preferred_element_type=F32)
    yield
    both = pv + w_inter * sq
    den = both[ML_DH:ML_DH + 1, :]
    return both[:ML_DH, :] / jnp.maximum(jnp.abs(den), jnp.exp(-m_t))


def _mlstm_kernel(q_ref, k_ref, v_ref, mo_ref, gcol_ref, grow_ref,
                  kc_ref, vc_ref, growc_ref,
                  wq_ref, wk_ref, bq_ref, bk_ref, ng_ref, o_ref,
                  cq_ref, ck_ref, ckc_ref, qt_ref, vt_ref, vct_ref, s_ref, m_ref, acc_ref):
    seq = q_ref.shape[1]
    ctx = kc_ref.shape[1]
    n = seq // ML_C
    nc = ctx // ML_C
    _grid_conv_silu(q_ref, cq_ref, wq_ref, bq_ref, GRID_W, 1.0)
    _grid_conv_silu(k_ref, ck_ref, wk_ref, bk_ref, GRID_W, ML_DH ** -0.5)
    _grid_conv_silu(kc_ref, ckc_ref, wk_ref, bk_ref, ctx, ML_DH ** -0.5)
    s_ref[...] = jnp.zeros_like(s_ref)
    m_ref[...] = jnp.zeros_like(m_ref)

    def rows(i):
        return pl.ds(pl.multiple_of(i * ML_C, ML_C), ML_C)

    def transpose_chunks(i, carry):
        qt_ref[:, rows(i)] = cq_ref[rows(i), :].T
        vt_ref[:, rows(i)] = v_ref[0, rows(i), :].T
        return carry

    lax.fori_loop(0, n, transpose_chunks, 0, unroll=2)
    for i in range(nc):
        vct_ref[:, i * ML_C:(i + 1) * ML_C] = vc_ref[0, i * ML_C:(i + 1) * ML_C, :].T

    def load_state():
        return [s_ref[0], s_ref[1]], [m_ref[0, :, 0:1], m_ref[1, :, 0:1]]

    def store_state(s, m):
        for d in range(2):
            s_ref[d] = s[d]
            m_ref[d] = jnp.broadcast_to(m[d], m_ref.shape[1:])

    def ctx_step(j, carry):
        order = _scan_order(j, nc, 1)
        ins = [(ckc_ref[rows(i), :], vct_ref[:, rows(i)], growc_ref[0, 0, :, rows(i)]) for _, i in order]
        s, m = load_state()
        _round_robin([_ml_chunk(None, k, vt, None, grow, s, m, d, False)
                      for (d, _), (k, vt, grow) in zip(order, ins)])
        store_state(s, m)
        return carry

    lax.fori_loop(0, nc, ctx_step, 0)

    def lat_step(j, carry, second):
        order = _scan_order(j, n, SCAN_UNROLL)
        ins = [(qt_ref[:, rows(i)], ck_ref[rows(i), :], vt_ref[:, rows(i)], gcol_ref[0, rows(i), :],
                grow_ref[0, 0, :, rows(i)]) for _, i in order]
        prev = [(acc_ref[:, rows(i)], mo_ref[0, rows(i), :]) for _, i in order] if second else None
        s, m = load_state()
        outs = _round_robin([_ml_chunk(qt, k, vt, gcol, grow, s, m, d, True)
                             for (d, _), (qt, k, vt, gcol, grow) in zip(order, ins)])
        store_state(s, m)
        for idx, (_, i) in enumerate(order):
            if second:
                total = prev[idx][0] + outs[idx]
                y = total * lax.rsqrt(jnp.mean(total * total, axis=0, keepdims=True) + EPS) * ng_ref[...]
                o_ref[0, rows(i), :] = (_sigmoid(prev[idx][1]) * y.T).astype(o_ref.dtype)
            else:
                acc_ref[:, rows(i)] = outs[idx]
        return carry

    half = n // (2 * SCAN_UNROLL)
    lax.fori_loop(0, half, functools.partial(lat_step, second=False), 0)
    lax.fori_loop(half, 2 * half, functools.partial(lat_step, second=True), 0)


def _mlstm(zm_x, gcol_x, grow_x, zm_c, grow_c, conv_w, conv_b, norm_g):
    bsz, seq, _ = zm_x.shape
    ctx = zm_c.shape[1]
    assert seq % (2 * SCAN_UNROLL * ML_C) == 0 and ctx % ML_C == 0 and seq % GRID_W == 0
    h = ML_HEADS

    def col(l, off):
        return pl.BlockSpec((1, l, LANES), lambda b, hd: (b, 0, off + hd))

    def gates(l):
        return [pl.BlockSpec((1, l, LANES), lambda b, hd: (b, 0, hd)),
                pl.BlockSpec((1, 1, 8, l), lambda b, hd: (b, hd, 0, 0))]

    return pl.pallas_call(
        _mlstm_kernel,
        grid=(bsz, h),
        in_specs=[col(seq, 0), col(seq, h), col(seq, 2 * h), col(seq, 3 * h)] + gates(seq)
                 + [col(ctx, h), col(ctx, 2 * h), gates(ctx)[1]]
                 + [pl.BlockSpec((9, LANES), lambda b, hd: (0, hd)),
                    pl.BlockSpec((9, LANES), lambda b, hd: (0, h + hd)),
                    pl.BlockSpec((1, LANES), lambda b, hd: (0, hd)),
                    pl.BlockSpec((1, LANES), lambda b, hd: (0, h + hd)),
                    pl.BlockSpec((LANES, 1), lambda b, hd: (hd, 0))],
        out_specs=pl.BlockSpec((1, seq, LANES), lambda b, hd: (b, 0, hd)),
        out_shape=jax.ShapeDtypeStruct((bsz, seq, h * ML_DH), BF16),
        scratch_shapes=[pltpu.VMEM((seq, LANES), F32), pltpu.VMEM((seq, LANES), F32),
                        pltpu.VMEM((ctx, LANES), F32),
                        pltpu.VMEM((LANES, seq), F32), pltpu.VMEM((LANES, seq), F32),
                        pltpu.VMEM((LANES, ctx), F32),
                        pltpu.VMEM((2, 2 * LANES, LANES), F32), pltpu.VMEM((2, 1, LANES), F32),
                        pltpu.VMEM((LANES, seq), F32)],
        compiler_params=_cparams(("arbitrary", "arbitrary")),
        name="mlstm",
    )(zm_x, zm_x, zm_x, zm_x, gcol_x, grow_x, zm_c, zm_c, grow_c,
      conv_w, conv_w, conv_b, conv_b, norm_g.reshape(-1, 1))


_G0 = 0
_E0 = N_GROUPS
RANK_BITS = 16
RANK_SPAN = 1 << RANK_BITS
ROW_UNROLL = 8
WAIT_UNROLL = 32
GATHER_RING = 3


SUBLANES = 8


def _store_token_tiles(ref2d, val):
    n, w = val.shape
    k = w // LANES
    for c in range(k):
        ref2d[pl.ds(c, n, stride=k), :] = val[:, c * LANES:(c + 1) * LANES]


def _load_token_tiles(ref2d, first, n, k, step):
    return jnp.concatenate([ref2d[pl.ds(first + c, n, stride=step), :] for c in range(k)], axis=1)


def _outproj_kernel(x_ref, ga_ref, ml_ref, mod_ref, wa_ref, wb_ref, g2_ref, wrh_ref, wrl_ref, br_ref,
                    x1_ref, h2_ref, ri_ref, rw_ref, cnt_ref, base_ref):
    tm = x_ref.shape[1]

    @pl.when((pl.program_id(0) == 0) & (pl.program_id(1) == 0))
    def _():
        base_ref[...] = jnp.zeros_like(base_ref)

    mix = (jnp.dot(ga_ref[0], wa_ref[...], preferred_element_type=F32)
           + jnp.dot(ml_ref[0], wb_ref[...], preferred_element_type=F32))
    x1 = x_ref[0] + mod_ref[0, 2:3, :] * mix
    x1_ref[0] = x1
    y = x1 * lax.rsqrt(jnp.mean(x1 * x1, axis=-1, keepdims=True) + EPS) * g2_ref[...]
    h2 = y * (1.0 + mod_ref[0, 4:5, :]) + mod_ref[0, 3:4, :]
    _store_token_tiles(h2_ref, h2)

    h_hi = h2.astype(BF16)
    h_lo = (h2 - h_hi.astype(F32)).astype(BF16)
    logits = (jnp.dot(h_hi, wrh_ref[...], preferred_element_type=F32)
              + jnp.dot(h_lo, wrh_ref[...], preferred_element_type=F32)
              + jnp.dot(h_hi, wrl_ref[...], preferred_element_type=F32)) + br_ref[...]

    lane = lax.broadcasted_iota(jnp.int32, (tm, LANES), 1).astype(F32)
    neg = -jnp.inf
    big = float(LANES)
    is_g = lane < float(_E0)
    lg = jnp.where(is_g, logits, neg)
    gmax = jnp.max(lg, axis=-1, keepdims=True)
    gidx = jnp.min(jnp.where(lg == gmax, lane, big), axis=-1, keepdims=True)
    gw = 1.0 / jnp.sum(jnp.where(is_g, jnp.exp(logits - gmax), 0.0), axis=-1, keepdims=True)
    lo = float(_E0) + float(EXP_PER_GROUP) * gidx
    le = jnp.where((lane >= lo) & (lane < lo + float(EXP_PER_GROUP)), logits, neg)
    v1 = jnp.max(le, axis=-1, keepdims=True)
    i1 = jnp.min(jnp.where(le == v1, lane, big), axis=-1, keepdims=True)
    le2 = jnp.where(lane == i1, neg, le)
    v2 = jnp.max(le2, axis=-1, keepdims=True)
    i2 = jnp.min(jnp.where(le2 == v2, lane, big), axis=-1, keepdims=True)
    t = jnp.exp(v2 - v1)
    w1 = gw / (1.0 + t)
    w2 = gw * t / (1.0 + t)
    e1 = i1 - float(_E0)
    e2 = i2 - float(_E0)

    oh1 = lane == e1
    oh2 = lane == e2
    oh = jnp.where(oh1 | oh2, 1.0, 0.0)
    r = lax.broadcasted_iota(jnp.int32, (tm, tm), 0)
    c = lax.broadcasted_iota(jnp.int32, (tm, tm), 1)
    strict = jnp.where(c < r, 1.0, 0.0).astype(BF16)
    before = jnp.dot(strict, oh.astype(BF16), preferred_element_type=F32) + base_ref[...]
    rank1 = jnp.sum(jnp.where(oh1, before, 0.0), axis=-1, keepdims=True)
    rank2 = jnp.sum(jnp.where(oh2, before, 0.0), axis=-1, keepdims=True)
    total = base_ref[...] + jnp.sum(oh, axis=0, keepdims=True)
    base_ref[...] = total
    cnt_ref[...] = total

    ids = jnp.where(lane == 0.0, e1 * float(RANK_SPAN) + rank1,
                    jnp.where(lane == 1.0, e2 * float(RANK_SPAN) + rank2, 0.0))
    ri_ref[...] = ids.astype(jnp.int32)
    rw_ref[...] = jnp.where(lane == 0.0, w1, jnp.where(lane == 1.0, w2, 0.0))


def _outproj(x, gla_o, ml_o, mods, wa, wb, g2, wrh, wrl, br, tm):
    bsz, seq, d = x.shape
    const = lambda shape: pl.BlockSpec(shape, lambda b, i: (0,) * len(shape))
    tile = lambda w: pl.BlockSpec((1, tm, w), lambda b, i: (b, i, 0))
    flat = lambda rows: pl.BlockSpec((rows, LANES), lambda b, i: (b * (seq // tm) + i, 0))
    return pl.pallas_call(
        _outproj_kernel,
        grid=(bsz, seq // tm),
        in_specs=[tile(d), tile(gla_o.shape[2]), tile(ml_o.shape[2]),
                  pl.BlockSpec((1, N_MOD, d), lambda b, i: (b, 0, 0)),
                  const(wa.shape), const(wb.shape), const((1, d)),
                  const(wrh.shape), const(wrl.shape), const((1, LANES))],
        out_specs=[tile(d),
                   pl.BlockSpec((tm * d // LANES, LANES), lambda b, i: (b * (seq // tm) + i, 0)),
                   flat(tm), flat(tm), const((1, LANES))],
        out_shape=[jax.ShapeDtypeStruct((bsz, seq, d), F32),
                   jax.ShapeDtypeStruct((bsz * seq * d // LANES, LANES), F32),
                   jax.ShapeDtypeStruct((bsz * seq, LANES), jnp.int32),
                   jax.ShapeDtypeStruct((bsz * seq, LANES), F32),
                   jax.ShapeDtypeStruct((1, LANES), F32)],
        scratch_shapes=[pltpu.VMEM((1, LANES), F32)],
        compiler_params=_cparams(("arbitrary", "arbitrary")),
        name="outproj",
    )(x, gla_o, ml_o, mods, wa, wb, g2, wrh, wrl, br)


def _experts_kernel(dest_ref, ps_ref, cnt_ref, be_ref, nv_ref, meta_ref, h_hbm, w1_ref, w2_ref, ytok_hbm,
                    src_ref, xbuf, ybuf, w1c_ref, w2c_ref, gsem, ssem):
    i = pl.program_id(0)
    n_steps = pl.num_programs(0)
    n_used = meta_ref[0]
    tr = SUBLANES
    n_tok = h_hbm.shape[0] // tr
    n_rows = src_ref.shape[0]
    blk = xbuf.shape[1] // tr
    n_x = xbuf.shape[0]
    slot = lax.rem(i, 2)
    xslot = lax.rem(i, n_x)

    def slab(j):
        return pl.ds(pl.multiple_of(j * tr, tr), tr)

    def gather_copy(tok, s, r):
        return pltpu.make_async_copy(h_hbm.at[slab(tok), :], xbuf.at[s, slab(r), :], gsem.at[s])

    def scatter_copy(a, s, r):
        return pltpu.make_async_copy(ybuf.at[s, slab(r), :], ytok_hbm.at[slab(a), :], ssem.at[s])

    def rows_loop(body):
        def step(r2, c):
            body(2 * r2, 0)
            body(2 * r2 + 1, 1)
            return c
        lax.fori_loop(0, blk // 2, step, 0, unroll=ROW_UNROLL // 2)

    def gather_row(b, s, r, prio):
        tok = lax.shift_right_logical(src_ref[b * blk + r], 1)
        gather_copy(jnp.minimum(tok, n_tok - 1), s, r).start(priority=prio)

    def issue_gather(b, s):
        rows_loop(lambda r, prio: gather_row(b, s, r, prio))

    def wait_rows(copy):
        def step(r, c):
            copy.wait()
            return c
        lax.fori_loop(0, blk, step, 0, unroll=WAIT_UNROLL)

    def wait_gather(s):
        wait_rows(gather_copy(0, s, 0))

    def wait_scatter(s):
        wait_rows(scatter_copy(0, s, 0))

    @pl.when(i == 0)
    def _():
        ybuf[...] = jnp.zeros_like(ybuf)
        for s in range(2):
            tail = ytok_hbm.at[pl.ds((2 * n_tok + s * blk) * tr, blk * tr), :]
            cp = pltpu.make_async_copy(ybuf.at[s], tail, ssem.at[s])
            cp.start()
            cp.wait()

        def put(a, c):
            src_ref[dest_ref[a]] = a
            return c
        lax.fori_loop(0, 2 * n_tok, put, 0, unroll=16)

        def pad(j, c):
            src_ref[j] = 2 * n_tok + (j & (2 * blk - 1))
            return c

        def pad_expert(e, c):
            lax.fori_loop(ps_ref[e] + cnt_ref[e], ps_ref[e + 1], pad, 0)
            return c
        lax.fori_loop(0, cnt_ref.shape[0], pad_expert, 0)
        used_end = ps_ref[cnt_ref.shape[0]]
        lax.fori_loop(used_end, jnp.minimum(used_end + (n_x - 1) * blk, n_rows), pad, 0)
        for b in range(n_x - 1):
            issue_gather(min(b, n_rows // blk - 1), b)

    @pl.when(i < n_used)
    def _():
        wait_gather(xslot)

        @pl.when((i == 0) | (be_ref[i] != be_ref[jnp.maximum(i - 1, 0)]))
        def _():
            w1c_ref[...] = w1_ref[0].astype(BF16)
            w2c_ref[...] = w2_ref[0].astype(BF16)

        @pl.when(i >= 2)
        def _():
            wait_scatter(slot)

        row = lax.broadcasted_iota(jnp.int32, (blk, 1), 0)
        x = _load_token_tiles(xbuf.at[xslot], 0, blk, tr, tr)
        x = jnp.where(row < nv_ref[i], x, 0.0).astype(BF16)
        h = jnp.dot(x, w1c_ref[...], preferred_element_type=F32)
        a = (_silu(h[:, :D_EXPERT]) * h[:, D_EXPERT:]).astype(BF16)
        _store_token_tiles(ybuf.at[slot], jnp.dot(a, w2c_ref[...], preferred_element_type=F32))
        ahead = jnp.minimum(i + n_x - 1, n_steps - 1)
        aslot = lax.rem(i + n_x - 1, n_x)

        def move_row(r, prio):
            scatter_copy(src_ref[i * blk + r], slot, r).start(priority=prio)
            gather_row(ahead, aslot, r, prio)
        rows_loop(move_row)

    @pl.when(i == n_steps - 1)
    def _():
        for b in range(n_x - 1):
            wait_gather(lax.rem(n_used + b, n_x))
        wait_scatter(0)
        wait_scatter(1)


def _experts(dest, pad_start, counts, block_e, block_nv, meta, h2, w_in, w_out, nb):
    d = w_in.shape[1]
    tr = d // LANES
    assert tr == SUBLANES, "a token row must fill exactly one (8, 128) tile"
    n_tok = h2.shape[0] // tr
    de2 = w_in.shape[2]
    n_rows = nb * MOE_BLK
    assert 2 * n_tok >= 2 * MOE_BLK
    assert MOE_BLK & (MOE_BLK - 1) == 0
    wmap = lambda i, pk, ps, cnt, be, nv, meta: (be[i], 0, 0)
    return pl.pallas_call(
        _experts_kernel,
        grid_spec=pltpu.PrefetchScalarGridSpec(
            num_scalar_prefetch=6, grid=(nb,),
            in_specs=[pl.BlockSpec(memory_space=pl.ANY),
                      pl.BlockSpec((1, d, de2), wmap),
                      pl.BlockSpec((1, de2 // 2, d), wmap)],
            out_specs=pl.BlockSpec(memory_space=pl.ANY),
            scratch_shapes=[pltpu.SMEM((n_rows,), jnp.int32),
                            pltpu.VMEM((GATHER_RING, MOE_BLK * tr, LANES), F32),
                            pltpu.VMEM((2, MOE_BLK * tr, LANES), F32),
                            pltpu.VMEM((d, de2), BF16), pltpu.VMEM((de2 // 2, d), BF16),
                            pltpu.SemaphoreType.DMA((GATHER_RING,)), pltpu.SemaphoreType.DMA((2,))]),
        out_shape=jax.ShapeDtypeStruct(((2 * n_tok + 2 * MOE_BLK) * tr, LANES), F32),
        compiler_params=_cparams(("arbitrary",)),
        name="experts",
    )(dest, pad_start, counts, block_e, block_nv, meta, h2, w_in, w_out)


def _combine_kernel(x1_ref, y_ref, rw_ref, mod_ref, fg_ref, o_ref):
    tc, d = x1_ref.shape
    tr = d // LANES
    y1 = _load_token_tiles(y_ref, 0, tc, tr, 2 * tr)
    y2 = _load_token_tiles(y_ref, tr, tc, tr, 2 * tr)
    moe = rw_ref[:, 0:1] * y1 + rw_ref[:, 1:2] * y2
    x2 = x1_ref[...] + mod_ref[0, 5:6, :] * moe
    o_ref[...] = x2 * lax.rsqrt(jnp.mean(x2 * x2, axis=-1, keepdims=True) + EPS) * fg_ref[...]


def _combine(x1, ytok, rw, mods, fg, tokens_per_batch, tc):
    n_tok, d = x1.shape
    tiles_per_batch = tokens_per_batch // tc
    return pl.pallas_call(
        _combine_kernel,
        grid=(n_tok // tc,),
        in_specs=[pl.BlockSpec((tc, d), lambda i: (i, 0)),
                  pl.BlockSpec((2 * tc * d // LANES, LANES), lambda i: (i, 0)),
                  pl.BlockSpec((tc, LANES), lambda i: (i, 0)),
                  pl.BlockSpec((1, N_MOD, d), lambda i: (i // tiles_per_batch, 0, 0)),
                  pl.BlockSpec((1, d), lambda i: (0, 0))],
        out_specs=pl.BlockSpec((tc, d), lambda i: (i, 0)),
        out_shape=jax.ShapeDtypeStruct((n_tok, d), F32),
        compiler_params=_cparams(("arbitrary",)),
        name="combine",
    )(x1, ytok, rw, mods, fg)


def _prep_inproj_weights(w_in, gla_up_w, gla_up_b, ml_i_b, ml_f_b):
    d = w_in.shape[0]
    o_gq, o_gk, o_gv, o_gg = 0, GLA_QK_W, 2 * GLA_QK_W, 2 * GLA_QK_W + GLA_V_W
    o_lr = o_gg + GLA_V_W
    o_mqk = o_lr + 2 * GLA_LR
    o_mi = o_mqk + 4 * ML_W
    o_mf = o_mi + 2 * ML_HEADS

    wg = w_in[:, o_gq:o_lr]
    wm = w_in[:, o_mqk:o_mi]
    ws = jnp.concatenate([w_in[:, o_lr:o_mqk], w_in[:, o_mi:o_mf + 2 * ML_HEADS],
                          jnp.zeros((d, LANES - 2 * GLA_LR - 4 * ML_HEADS), w_in.dtype)], axis=1)
    bias = jnp.zeros((LANES,), F32)
    bias = bias.at[_MI0:_MI0 + 2 * ML_HEADS].set(ml_i_b.reshape(-1))
    bias = bias.at[_MF0:_MF0 + 2 * ML_HEADS].set(ml_f_b.reshape(-1))
    up = gla_up_w.reshape(2, GLA_LR, GLA_HEADS, GLA_DK).transpose(2, 0, 1, 3)
    ub = gla_up_b.reshape(2, GLA_HEADS, GLA_DK).transpose(1, 0, 2)
    wup = jnp.zeros((GLA_HEADS, 2, LANES, LANES), F32)
    bup = jnp.zeros((GLA_HEADS, 2, 1, LANES), F32)
    for hd in range(GLA_HEADS):
        lo = (hd % (LANES // GLA_DK)) * GLA_DK
        for dr in range(2):
            wup = wup.at[hd, dr, dr * GLA_LR:(dr + 1) * GLA_LR, lo:lo + GLA_DK].set(up[hd, dr])
        bup = bup.at[hd, :, 0, lo:lo + GLA_DK].set(ub[hd])
    return (wg.astype(BF16), wm.astype(BF16), ws.astype(BF16), ws.T.astype(BF16),
            bias.reshape(1, LANES), bias.reshape(LANES, 1), wup.astype(BF16), bup)


def _layer(x, ctx, mods, norm1_g, w_in, gla_up_w, gla_up_b, gla_norm_g, ml_conv_w, ml_conv_b,
           ml_i_b, ml_f_b, ml_norm_g, w_out, norm2_g, rg_w, rg_b, re_w, re_b, e_w_in, e_w_out, final_g):
    bsz, seq, d = x.shape
    n_tok = bsz * seq
    wg, wm, ws, wst, bcol, brow, wup, bup = _prep_inproj_weights(w_in, gla_up_w, gla_up_b, ml_i_b, ml_f_b)
    g1 = norm1_g.reshape(1, d)
    zg_x, zm_x, zs_x, gcol_x, grow_x = _inproj(x, mods, lambda b: b, g1, wg, wm, ws, wst, bcol, brow, 256)
    zg_c, zm_c, zs_c, gcol_c, grow_c = _inproj(ctx, mods, lambda b: bsz, g1, wg, wm, ws, wst, bcol, brow,
                                               min(256, ctx.shape[1]))
    gla_o = _gla(zg_x, zs_x, zg_c, zs_c, wup, bup, gla_norm_g.reshape(1, -1))
    ml_o = _mlstm(zm_x, gcol_x, grow_x, zm_c, grow_c,
                  ml_conv_w.reshape(9, -1), ml_conv_b.reshape(1, -1), ml_norm_g.reshape(1, -1))

    wr = jnp.zeros((d, LANES), F32).at[:, _G0:_E0].set(rg_w).at[:, _E0:_E0 + N_EXPERTS].set(re_w)
    br = jnp.zeros((1, LANES), F32).at[0, _G0:_E0].set(rg_b).at[0, _E0:_E0 + N_EXPERTS].set(re_b)
    wrh = wr.astype(BF16)
    wrl = (wr - wrh.astype(F32)).astype(BF16)
    x1, h2, ri, rw, cnt = _outproj(x, gla_o, ml_o, mods, w_out[:GLA_V_W].astype(BF16),
                                   w_out[GLA_V_W:].astype(BF16), norm2_g.reshape(1, d), wrh, wrl, br, 256)

    counts = cnt[0, :N_EXPERTS].astype(jnp.int32)
    nblk = (counts + MOE_BLK - 1) // MOE_BLK
    blk_end = jnp.cumsum(nblk)
    blk_start = blk_end - nblk
    n_used = blk_end[-1]
    nb_max = (2 * n_tok) // MOE_BLK + N_EXPERTS
    blk = jnp.arange(nb_max, dtype=jnp.int32)
    blk_c = jnp.minimum(blk, n_used - 1)
    onehot = (blk_c[:, None] >= blk_start[None, :]) & (blk_c[:, None] < blk_end[None, :])
    pick = lambda v: jnp.sum(jnp.where(onehot, v[None, :], 0), axis=1)
    block_e = pick(jnp.arange(N_EXPERTS, dtype=jnp.int32)).astype(jnp.int32)
    block_nv = jnp.clip(pick(counts) - (blk_c - pick(blk_start)) * MOE_BLK, 0, MOE_BLK)
    block_nv = jnp.where(blk < n_used, block_nv, 0).astype(jnp.int32)
    pad_start = (jnp.concatenate([blk_start, blk_end[-1:]]) * MOE_BLK).astype(jnp.int32)
    packed = ri[:, 0:2].reshape(-1)
    e_of = lax.shift_right_logical(packed, RANK_BITS)
    start_of = jnp.sum(jnp.where(e_of[:, None] == jnp.arange(N_EXPERTS, dtype=jnp.int32)[None, :],
                                 pad_start[None, :N_EXPERTS], 0), axis=1)
    dest = (start_of + (packed & (RANK_SPAN - 1))).astype(jnp.int32)
    meta = jnp.stack([n_used, n_used]).astype(jnp.int32)

    ytok = _experts(dest, pad_start, counts, block_e, block_nv, meta, h2, e_w_in, e_w_out, nb_max)
    out = _combine(x1.reshape(n_tok, d), ytok, rw, mods, final_g.reshape(1, d), seq, 256)
    return out.reshape(bsz, seq, d)


def kernel(x, c, ctx, c_ctx, ada_w, ada_b, norm1_g, w_in, gla_up_w, gla_up_b, gla_norm_g, ml_conv_w, ml_conv_b,
           ml_i_b, ml_f_b, ml_norm_g, w_out, norm2_g, router_group_w, router_group_b, router_expert_w,
           router_expert_b, expert_w_in, expert_w_out, final_norm_g):
    assert ada_w.shape[0] == 1, "single-layer stack"
    bsz, d = c.shape
    cc = jnp.concatenate([c, c_ctx[None, :], jnp.zeros((8 - bsz - 1, d), F32)], axis=0)
    mods = _modulation(cc, ada_w[0], ada_b[0]).reshape(8, N_MOD, d)
    return _layer(x, ctx, mods, norm1_g[0], w_in[0], gla_up_w[0], gla_up_b[0], gla_norm_g[0],
                  ml_conv_w[0], ml_conv_b[0], ml_i_b[0], ml_f_b[0], ml_norm_g[0], w_out[0], norm2_g[0],
                  router_group_w[0], router_group_b[0], router_expert_w[0], router_expert_b[0],
                  expert_w_in[0], expert_w_out[0], final_norm_g)
```

```python
import functools

import jax
import jax.numpy as jnp
from jax import lax
from jax.experimental import pallas as pl
from jax.experimental.pallas import tpu as pltpu

F32 = jnp.float32
BF16 = jnp.bfloat16

D_MODEL = 1024
GRID_W = 64
N_MOD = 6
EPS = 1e-6

GLA_HEADS = 4
GLA_DK = 64
GLA_DV = 128
GLA_LR = 16
GLA_TAU = 16.0
GLA_C = 128
SCAN_UNROLL = 4

ML_HEADS = 4
ML_DH = 128
ML_C = 128

N_GROUPS = 4
EXP_PER_GROUP = 8
N_EXPERTS = N_GROUPS * EXP_PER_GROUP
D_EXPERT = 512
MOE_BLK = 256

GLA_QK_W = GLA_HEADS * GLA_DK
GLA_V_W = GLA_HEADS * GLA_DV
ML_W = ML_HEADS * ML_DH
LANES = 128
VMEM_LIMIT = 56 * 1024 * 1024

_LR0 = 0
_MI0 = 2 * GLA_LR
_MF0 = _MI0 + 2 * ML_HEADS


def _cparams(sem):
    return pltpu.CompilerParams(dimension_semantics=sem, vmem_limit_bytes=VMEM_LIMIT)


def _sigmoid(x):
    return 1.0 / (1.0 + jnp.exp(-x))


def _silu(x):
    return x * _sigmoid(x)


def _log_sigmoid(x):
    return jnp.minimum(x, 0.0) - jnp.log1p(jnp.exp(-jnp.abs(x)))


def _split_dot(a_bf16_exact, x, dims=None):
    x_hi = x.astype(BF16)
    x_lo = (x - x_hi.astype(F32)).astype(BF16)
    if dims is None:
        f = lambda u: jnp.dot(a_bf16_exact, u, preferred_element_type=F32)
    else:
        f = lambda u: lax.dot_general(u, a_bf16_exact, dims, preferred_element_type=F32)
    return f(x_hi) + f(x_lo)


def _mod_kernel(c_ref, w_ref, b_ref, o_ref):
    c = c_ref[...]
    s = _silu(c).astype(BF16)
    o_ref[...] = jnp.dot(s, w_ref[...].astype(BF16), preferred_element_type=F32) + b_ref[...]


def _modulation(cc, ada_w, ada_b):
    rows, d = cc.shape
    n = ada_w.shape[1]
    tn = 1536
    return pl.pallas_call(
        _mod_kernel,
        grid=(n // tn,),
        in_specs=[pl.BlockSpec((rows, d), lambda j: (0, 0)),
                  pl.BlockSpec((d, tn), lambda j: (0, j)),
                  pl.BlockSpec((1, tn), lambda j: (0, j))],
        out_specs=pl.BlockSpec((rows, tn), lambda j: (0, j)),
        out_shape=jax.ShapeDtypeStruct((rows, n), F32),
        compiler_params=_cparams(("arbitrary",)),
        name="mod",
    )(cc, ada_w, ada_b.reshape(1, n))


def _inproj_kernel(x_ref, mod_ref, g_ref, wg_ref, wm_ref, ws_ref, wst_ref, bcol_ref, brow_ref,
                   zg_ref, zm_ref, zs_ref, gcol_ref, grow_ref):
    tm = x_ref.shape[1]
    x = x_ref[0]
    y = x * lax.rsqrt(jnp.mean(x * x, axis=-1, keepdims=True) + EPS) * g_ref[...]
    h = (y * (1.0 + mod_ref[0, 1:2, :]) + mod_ref[0, 0:1, :]).astype(BF16)
    zg_ref[0] = jnp.dot(h, wg_ref[...], preferred_element_type=F32)
    zm_ref[0] = jnp.dot(h, wm_ref[...], preferred_element_type=F32)
    zs = jnp.dot(h, ws_ref[...], preferred_element_type=F32) + bcol_ref[...]
    zst = lax.dot_general(wst_ref[...], h, (((1,), (1,)), ((), ())),
                          preferred_element_type=F32) + brow_ref[...]
    zs_ref[0] = zs

    r = lax.broadcasted_iota(jnp.int32, (tm, tm), 0)
    c = lax.broadcasted_iota(jnp.int32, (tm, tm), 1)
    shift = ML_C.bit_length() - 1
    same = jnp.right_shift(r, shift) == jnp.right_shift(c, shift)
    lower = jnp.where(same & (c <= r), 1.0, 0.0).astype(BF16)
    upper = jnp.where(same & (c >= r), 1.0, 0.0).astype(BF16)
    chunks = range(0, tm, ML_C)
    lsf = _log_sigmoid(zs)
    a_pre = _split_dot(lower, lsf)
    tot = jnp.concatenate([jnp.broadcast_to(a_pre[o + ML_C - 1:o + ML_C, :], (ML_C, LANES)) for o in chunks], axis=0)
    a_suf = tot - a_pre + lsf
    lsft = _log_sigmoid(zst)
    a_pre_t = _split_dot(upper, lsft, (((1,), (0,)), ((), ())))
    tot_t = jnp.concatenate([jnp.broadcast_to(a_pre_t[:, o + ML_C - 1:o + ML_C], (LANES, ML_C)) for o in chunks], axis=1)
    a_suf_t = tot_t - a_pre_t + lsft

    lane = lax.broadcasted_iota(jnp.int32, (tm, LANES), 1)
    for hd in range(ML_HEADS):
        cols = (a_pre[:, _MF0 + hd:_MF0 + hd + 1],
                a_suf[:, _MF0 + ML_HEADS + hd:_MF0 + ML_HEADS + hd + 1],
                zs[:, _MI0 + hd:_MI0 + hd + 1],
                zs[:, _MI0 + ML_HEADS + hd:_MI0 + ML_HEADS + hd + 1])
        slab = jnp.zeros((tm, LANES), F32)
        for j, col in enumerate(cols):
            slab = jnp.where(lane == j, col, slab)
        gcol_ref[0, :, hd * LANES:(hd + 1) * LANES] = slab
        rows = (a_pre_t[_MF0 + hd:_MF0 + hd + 1, :],
                a_suf_t[_MF0 + ML_HEADS + hd:_MF0 + ML_HEADS + hd + 1, :],
                zst[_MI0 + hd:_MI0 + hd + 1, :],
                zst[_MI0 + ML_HEADS + hd:_MI0 + ML_HEADS + hd + 1, :])
        for j, row in enumerate(rows):
            grow_ref[0, hd, j:j + 1, :] = row
        grow_ref[0, hd, 4:8, :] = jnp.zeros((4, tm), F32)


def _inproj(x, mods, mod_row_of_batch, norm_g, wg, wm, ws, wst, bcol, brow, tm):
    bsz, l, d = x.shape
    assert l % tm == 0 and tm % ML_C == 0
    const = lambda shape: pl.BlockSpec(shape, lambda b, i: (0,) * len(shape))
    return pl.pallas_call(
        _inproj_kernel,
        grid=(bsz, l // tm),
        in_specs=[pl.BlockSpec((1, tm, d), lambda b, i: (b, i, 0)),
                  pl.BlockSpec((1, N_MOD, d), lambda b, i: (mod_row_of_batch(b), 0, 0)),
                  const((1, d)), const(wg.shape), const(wm.shape), const(ws.shape), const(wst.shape),
                  const((1, LANES)), const((LANES, 1))],
        out_specs=[pl.BlockSpec((1, tm, wg.shape[1]), lambda b, i: (b, i, 0)),
                   pl.BlockSpec((1, tm, wm.shape[1]), lambda b, i: (b, i, 0)),
                   pl.BlockSpec((1, tm, LANES), lambda b, i: (b, i, 0)),
                   pl.BlockSpec((1, tm, ML_HEADS * LANES), lambda b, i: (b, i, 0)),
                   pl.BlockSpec((1, ML_HEADS, 8, tm), lambda b, i: (b, 0, 0, i))],
        out_shape=[jax.ShapeDtypeStruct((bsz, l, wg.shape[1]), F32),
                   jax.ShapeDtypeStruct((bsz, l, wm.shape[1]), F32),
                   jax.ShapeDtypeStruct((bsz, l, LANES), F32),
                   jax.ShapeDtypeStruct((bsz, l, ML_HEADS * LANES), F32),
                   jax.ShapeDtypeStruct((bsz, ML_HEADS, 8, l), F32)],
        compiler_params=_cparams(("arbitrary", "arbitrary")),
        name="inproj",
    )(x, mods, norm_g, wg, wm, ws, wst, bcol, brow)


def _round_robin(chains):
    results = [None] * len(chains)
    live = list(enumerate(chains))
    while live:
        still = []
        for idx, chain in live:
            try:
                next(chain)
                still.append((idx, chain))
            except StopIteration as done:
                results[idx] = done.value
        live = still
    return results


def _visibility(c):
    r = lax.broadcasted_iota(jnp.int32, (c, c), 0)
    cc = lax.broadcasted_iota(jnp.int32, (c, c), 1)
    masks = [cc <= r, cc >= r]
    return masks, [jnp.where(m, 1.0, 0.0).astype(BF16) for m in masks]


def _gla_chunk(q, k, v, zs, wup, bup, state, direction, want_out, causal, tri):
    c = k.shape[0]
    logits = jnp.dot(zs.astype(BF16), wup, preferred_element_type=F32) + bup
    yield
    g = _log_sigmoid(logits) * (1.0 / GLA_TAU)
    b = _split_dot(tri, g)
    yield
    b_end = b[c - 1:c, :] if direction == 0 else b[0:1, :]
    kd = (k * jnp.exp(b_end - b)).astype(BF16)
    upd = lax.dot_general(v.astype(BF16), kd, (((0,), (0,)), ((), ())), preferred_element_type=F32)
    s = state[direction]
    state[direction] = jnp.exp(b_end) * s + upd
    if not want_out:
        return None
    b_mid = b[c // 2:c // 2 + 1, :]
    q_in = (q * jnp.exp(b - b_mid)).astype(BF16)
    k_in = (k * jnp.exp(b_mid - b)).astype(BF16)
    att = lax.dot_general(q_in, k_in, (((1,), (1,)), ((), ())), preferred_element_type=F32)
    inter = lax.dot_general((q * jnp.exp(b)).astype(BF16), s.astype(BF16),
                            (((1,), (1,)), ((), ())), preferred_element_type=F32)
    yield
    att = jnp.where(causal, att, 0.0)
    return jnp.dot(att.astype(BF16), v.astype(BF16), preferred_element_type=F32) + inter


def _scan_order(j, n, unroll):
    return [(d, j * unroll + u if d == 0 else n - 1 - (j * unroll + u)) for u in range(unroll) for d in range(2)]


def _gla_kernel(q_ref, k_ref, v_ref, gg_ref, zs_ref, kc_ref, vc_ref, zsc_ref,
                wup_ref, bup_ref, ng_ref, o_ref, s_ref, acc_ref):
    seq = q_ref.shape[1]
    ctx = kc_ref.shape[1]
    n = seq // GLA_C
    nc = ctx // GLA_C
    s_ref[...] = jnp.zeros_like(s_ref)

    def rows(i):
        return pl.ds(pl.multiple_of(i * GLA_C, GLA_C), GLA_C)

    masks, tris = _visibility(GLA_C)
    heads_per_slab = LANES // GLA_DK
    lo = lax.rem(pl.program_id(1), heads_per_slab) * GLA_DK
    lane = lax.broadcasted_iota(jnp.int32, (GLA_C, LANES), 1)
    mine = (lane >= lo) & (lane < lo + GLA_DK)

    def own(t):
        return jnp.where(mine, t, 0.0)

    def ctx_step(j, carry):
        order = _scan_order(j, nc, 1)
        ins = [(own(kc_ref[0, rows(i), :]), vc_ref[0, rows(i), :], zsc_ref[0, rows(i), :]) for _, i in order]
        s = [s_ref[0], s_ref[1]]
        _round_robin([_gla_chunk(None, k, v, zs, wup_ref[0, d], bup_ref[0, d], s, d, False, masks[d], tris[d])
                      for (d, _), (k, v, zs) in zip(order, ins)])
        s_ref[0] = s[0]
        s_ref[1] = s[1]
        return carry

    lax.fori_loop(0, nc, ctx_step, 0)

    def lat_step(j, carry, second):
        order = _scan_order(j, n, SCAN_UNROLL)
        ins = [(own(q_ref[0, rows(i), :]), own(k_ref[0, rows(i), :]), v_ref[0, rows(i), :], zs_ref[0, rows(i), :])
               for _, i in order]
        prev = [(acc_ref[rows(i), :], gg_ref[0, rows(i), :]) for _, i in order] if second else None
        s = [s_ref[0], s_ref[1]]
        outs = _round_robin([_gla_chunk(q * (GLA_DK ** -0.5), k, v, zs, wup_ref[0, d], bup_ref[0, d], s, d, True,
                                        masks[d], tris[d])
                             for (d, _), (q, k, v, zs) in zip(order, ins)])
        s_ref[0] = s[0]
        s_ref[1] = s[1]
        for idx, (_, i) in enumerate(order):
            if second:
                total = prev[idx][0] + outs[idx]
                y = total * lax.rsqrt(jnp.mean(total * total, axis=-1, keepdims=True) + EPS) * ng_ref[...]
                o_ref[0, rows(i), :] = (y * _silu(prev[idx][1])).astype(o_ref.dtype)
            else:
                acc_ref[rows(i), :] = outs[idx]
        return carry

    half = n // (2 * SCAN_UNROLL)
    lax.fori_loop(0, half, functools.partial(lat_step, second=False), 0)
    lax.fori_loop(half, 2 * half, functools.partial(lat_step, second=True), 0)


def _gla(zg_x, zs_x, zg_c, zs_c, wup, bup, norm_g):
    bsz, seq, _ = zg_x.shape
    ctx = zg_c.shape[1]
    assert seq % (2 * SCAN_UNROLL * GLA_C) == 0 and ctx % GLA_C == 0
    h = GLA_HEADS

    hps = LANES // GLA_DK
    qk_slabs = h // hps

    def qk(l, off):
        return pl.BlockSpec((1, l, LANES), lambda b, hd: (b, 0, off + hd // hps))

    def col(l, off):
        return pl.BlockSpec((1, l, LANES), lambda b, hd: (b, 0, off + hd))

    return pl.pallas_call(
        _gla_kernel,
        grid=(bsz, h),
        in_specs=[qk(seq, 0), qk(seq, qk_slabs), col(seq, 2 * qk_slabs), col(seq, 2 * qk_slabs + h),
                  pl.BlockSpec((1, seq, LANES), lambda b, hd: (b, 0, 0)),
                  qk(ctx, qk_slabs), col(ctx, 2 * qk_slabs),
                  pl.BlockSpec((1, ctx, LANES), lambda b, hd: (b, 0, 0)),
                  pl.BlockSpec((1, 2, LANES, LANES), lambda b, hd: (hd, 0, 0, 0)),
                  pl.BlockSpec((1, 2, 1, LANES), lambda b, hd: (hd, 0, 0, 0)),
                  pl.BlockSpec((1, LANES), lambda b, hd: (0, hd))],
        out_specs=pl.BlockSpec((1, seq, LANES), lambda b, hd: (b, 0, hd)),
        out_shape=jax.ShapeDtypeStruct((bsz, seq, h * GLA_DV), BF16),
        scratch_shapes=[pltpu.VMEM((2, LANES, LANES), F32), pltpu.VMEM((seq, LANES), F32)],
        compiler_params=_cparams(("arbitrary", "arbitrary")),
        name="gla",
    )(zg_x, zg_x, zg_x, zg_x, zs_x, zg_c, zg_c, zs_c, wup, bup, norm_g)


def _grid_conv_silu(src_ref, dst_ref, w_ref, b_ref, grid_w, scale):
    l = src_ref.shape[1]
    n_rows = l // grid_w
    col = lax.broadcasted_iota(jnp.int32, (grid_w, LANES), 0)

    def body(r, carry):
        acc = jnp.zeros((grid_w, LANES), F32) + b_ref[...]
        for dy in (-1, 0, 1):
            if n_rows == 1 and dy != 0:
                continue
            rr = jnp.clip(r + dy, 0, n_rows - 1)
            blk = src_ref[0, pl.ds(pl.multiple_of(rr * grid_w, grid_w), grid_w), :]
            valid = jnp.logical_and(r + dy >= 0, r + dy < n_rows)
            blk = jnp.where(valid, blk, 0.0)
            for dx in (-1, 0, 1):
                if dx == 0:
                    sh = blk
                else:
                    sh = pltpu.roll(blk, shift=(-dx) % grid_w, axis=0)
                    sh = jnp.where((col + dx >= 0) & (col + dx < grid_w), sh, 0.0)
                tap = (dy + 1) * 3 + (dx + 1)
                acc = acc + sh * w_ref[tap:tap + 1, :]
        dst_ref[pl.ds(pl.multiple_of(r * grid_w, grid_w), grid_w), :] = _silu(acc) * scale
        return carry

    lax.fori_loop(0, n_rows, body, 0)


def _ml_chunk(qb, k, vt, gcol, grow, state, mstate, direction, want_out, visible):
    c = k.shape[0]
    a_row = grow[direction:direction + 1, :]
    i_row = grow[2 + direction:3 + direction, :]
    a_end = a_row[:, c - 1:c] if direction == 0 else a_row[:, 0:1]
    g = a_end - a_row + i_row
    g_max = jnp.max(g, axis=-1, keepdims=True)
    head = 2 * SUBLANES
    pad_rows = jnp.zeros((LANES - head, c), BF16)
    first = lax.broadcasted_iota(jnp.int32, (head, c), 0) == 0
    kb = k.astype(BF16)
    if want_out:
        c_col = gcol[:, direction:direction + 1] - gcol[:, 2 + direction:3 + direction]
        dmat = jnp.where(visible, a_row - c_col, -jnp.inf)
        d_max = jnp.max(dmat, axis=0, keepdims=True)
        kq = jnp.dot(kb, qb, preferred_element_type=F32)
    yield
    s, m = state[direction], mstate[direction]
    m_new = jnp.maximum(a_end + m, g_max)
    decay = jnp.exp(a_end + m - m_new)
    w = jnp.exp(g - m_new)
    vw = jnp.concatenate([(vt * w).astype(BF16), jnp.where(first, w, 0.0).astype(BF16), pad_rows], axis=0)
    state[direction] = decay * s + jnp.dot(vw, kb, preferred_element_type=F32)
    mstate[direction] = m_new
    if not want_out:
        return None
    inter = a_row + m
    m_t = jnp.maximum(inter, d_max)
    w_inter = jnp.exp(inter - m_t)
    p = (kq * jnp.exp(dmat - m_t)).astype(BF16)
    vt_aug = jnp.concatenate([vt.astype(BF16), jnp.where(first, 1.0, 0.0).astype(BF16), pad_rows], axis=0)
    pv = jnp.dot(vt_aug, p, preferred_element_type=F32)
    sq = jnp.dot(s.astype(BF16), qb, preferred_element_type=F32)
    yield
    both = pv + w_inter * sq
    den = both[ML_DH:ML_DH + 1, :]
    return both[:ML_DH, :] / jnp.maximum(jnp.abs(den), jnp.exp(-m_t))


def _mlstm_kernel(q_ref, k_ref, v_ref, mo_ref, gcol_ref, grow_ref,
                  kc_ref, vc_ref, growc_ref,
                  wq_ref, wk_ref, bq_ref, bk_ref, ng_ref, o_ref,
                  cq_ref, ck_ref, ckc_ref, qt_ref, vt_ref, vct_ref, s_ref, m_ref, acc_ref):
    seq = q_ref.shape[1]
    ctx = kc_ref.shape[1]
    n = seq // ML_C
    nc = ctx // ML_C
    _grid_conv_silu(q_ref, cq_ref, wq_ref, bq_ref, GRID_W, 1.0)
    _grid_conv_silu(k_ref, ck_ref, wk_ref, bk_ref, GRID_W, ML_DH ** -0.5)
    _grid_conv_silu(kc_ref, ckc_ref, wk_ref, bk_ref, ctx, ML_DH ** -0.5)
    s_ref[...] = jnp.zeros_like(s_ref)
    m_ref[...] = jnp.zeros_like(m_ref)

    def rows(i):
        return pl.ds(pl.multiple_of(i * ML_C, ML_C), ML_C)

    def transpose_chunks(i, carry):
        qt_ref[:, rows(i)] = cq_ref[rows(i), :].T.astype(qt_ref.dtype)
        vt_ref[:, rows(i)] = v_ref[0, rows(i), :].T
        return carry

    lax.fori_loop(0, n, transpose_chunks, 0, unroll=2)
    for i in range(nc):
        vct_ref[:, i * ML_C:(i + 1) * ML_C] = vc_ref[0, i * ML_C:(i + 1) * ML_C, :].T

    def load_state():
        return [s_ref[0], s_ref[1]], [m_ref[0, :, 0:1], m_ref[1, :, 0:1]]

    def store_state(s, m):
        for d in range(2):
            s_ref[d] = s[d]
            m_ref[d] = jnp.broadcast_to(m[d], m_ref.shape[1:])

    masks, _ = _visibility(ML_C)
    visible = [masks[1], masks[0]]

    def ctx_step(j, carry):
        order = _scan_order(j, nc, 1)
        ins = [(ckc_ref[rows(i), :], vct_ref[:, rows(i)], growc_ref[0, 0, :, rows(i)]) for _, i in order]
        s, m = load_state()
        _round_robin([_ml_chunk(None, k, vt, None, grow, s, m, d, False, None)
                      for (d, _), (k, vt, grow) in zip(order, ins)])
        store_state(s, m)
        return carry

    lax.fori_loop(0, nc, ctx_step, 0)

    def lat_step(j, carry, second):
        order = _scan_order(j, n, SCAN_UNROLL)
        ins = [(qt_ref[:, rows(i)], ck_ref[rows(i), :], vt_ref[:, rows(i)], gcol_ref[0, rows(i), :],
                grow_ref[0, 0, :, rows(i)]) for _, i in order]
        prev = [(acc_ref[:, rows(i)], mo_ref[0, rows(i), :]) for _, i in order] if second else None
        s, m = load_state()
        outs = _round_robin([_ml_chunk(qb, k, vt, gcol, grow, s, m, d, True, visible[d])
                             for (d, _), (qb, k, vt, gcol, grow) in zip(order, ins)])
        store_state(s, m)
        for idx, (_, i) in enumerate(order):
            if second:
                total = prev[idx][0] + outs[idx]
                y = total * lax.rsqrt(jnp.mean(total * total, axis=0, keepdims=True) + EPS) * ng_ref[...]
                o_ref[0, rows(i), :] = (_sigmoid(prev[idx][1]) * y.T).astype(o_ref.dtype)
            else:
                acc_ref[:, rows(i)] = outs[idx]
        return carry

    half = n // (2 * SCAN_UNROLL)
    lax.fori_loop(0, half, functools.partial(lat_step, second=False), 0)
    lax.fori_loop(half, 2 * half, functools.partial(lat_step, second=True), 0)


def _mlstm(zm_x, gcol_x, grow_x, zm_c, grow_c, conv_w, conv_b, norm_g):
    bsz, seq, _ = zm_x.shape
    ctx = zm_c.shape[1]
    assert seq % (2 * SCAN_UNROLL * ML_C) == 0 and ctx % ML_C == 0 and seq % GRID_W == 0
    h = ML_HEADS

    def col(l, off):
        return pl.BlockSpec((1, l, LANES), lambda b, hd: (b, 0, off + hd))

    def gates(l):
        return [pl.BlockSpec((1, l, LANES), lambda b, hd: (b, 0, hd)),
                pl.BlockSpec((1, 1, 8, l), lambda b, hd: (b, hd, 0, 0))]

    return pl.pallas_call(
        _mlstm_kernel,
        grid=(bsz, h),
        in_specs=[col(seq, 0), col(seq, h), col(seq, 2 * h), col(seq, 3 * h)] + gates(seq)
                 + [col(ctx, h), col(ctx, 2 * h), gates(ctx)[1]]
                 + [pl.BlockSpec((9, LANES), lambda b, hd: (0, hd)),
                    pl.BlockSpec((9, LANES), lambda b, hd: (0, h + hd)),
                    pl.BlockSpec((1, LANES), lambda b, hd: (0, hd)),
                    pl.BlockSpec((1, LANES), lambda b, hd: (0, h + hd)),
                    pl.BlockSpec((LANES, 1), lambda b, hd: (hd, 0))],
        out_specs=pl.BlockSpec((1, seq, LANES), lambda b, hd: (b, 0, hd)),
        out_shape=jax.ShapeDtypeStruct((bsz, seq, h * ML_DH), BF16),
        scratch_shapes=[pltpu.VMEM((seq, LANES), F32), pltpu.VMEM((seq, LANES), F32),
                        pltpu.VMEM((ctx, LANES), F32),
                        pltpu.VMEM((LANES, seq), BF16), pltpu.VMEM((LANES, seq), F32),
                        pltpu.VMEM((LANES, ctx), F32),
                        pltpu.VMEM((2, 2 * LANES, LANES), F32), pltpu.VMEM((2, 1, LANES), F32),
                        pltpu.VMEM((LANES, seq), F32)],
        compiler_params=_cparams(("arbitrary", "arbitrary")),
        name="mlstm",
    )(zm_x, zm_x, zm_x, zm_x, gcol_x, grow_x, zm_c, zm_c, grow_c,
      conv_w, conv_w, conv_b, conv_b, norm_g.reshape(-1, 1))


_G0 = 0
_E0 = N_GROUPS
RANK_BITS = 16
RANK_SPAN = 1 << RANK_BITS
ROW_GROUP = 16
GATHER_RING = 3


SUBLANES = 8


def _store_token_tiles(ref2d, val):
    n, w = val.shape
    k = w // LANES
    for c in range(k):
        ref2d[pl.ds(c, n, stride=k), :] = val[:, c * LANES:(c + 1) * LANES]


def _load_token_tiles(ref2d, first, n, k, step):
    return jnp.concatenate([ref2d[pl.ds(first + c, n, stride=step), :] for c in range(k)], axis=1)


def _outproj_kernel(x_ref, ga_ref, ml_ref, mod_ref, wa_ref, wb_ref, g2_ref, wrh_ref, wrl_ref, br_ref,
                    x1_ref, h2_ref, ri_ref, rw_ref, cnt_ref, base_ref):
    tm = x_ref.shape[1]

    @pl.when((pl.program_id(0) == 0) & (pl.program_id(1) == 0))
    def _():
        base_ref[...] = jnp.zeros_like(base_ref)

    mix = (jnp.dot(ga_ref[0], wa_ref[...], preferred_element_type=F32)
           + jnp.dot(ml_ref[0], wb_ref[...], preferred_element_type=F32))
    x1 = x_ref[0] + mod_ref[0, 2:3, :] * mix
    x1_ref[0] = x1
    y = x1 * lax.rsqrt(jnp.mean(x1 * x1, axis=-1, keepdims=True) + EPS) * g2_ref[...]
    h2 = y * (1.0 + mod_ref[0, 4:5, :]) + mod_ref[0, 3:4, :]
    _store_token_tiles(h2_ref, h2)

    h_hi = h2.astype(BF16)
    h_lo = (h2 - h_hi.astype(F32)).astype(BF16)
    logits = (jnp.dot(h_hi, wrh_ref[...], preferred_element_type=F32)
              + jnp.dot(h_lo, wrh_ref[...], preferred_element_type=F32)
              + jnp.dot(h_hi, wrl_ref[...], preferred_element_type=F32)) + br_ref[...]

    lane = lax.broadcasted_iota(jnp.int32, (tm, LANES), 1).astype(F32)
    neg = -jnp.inf
    big = float(LANES)
    is_g = lane < float(_E0)
    lg = jnp.where(is_g, logits, neg)
    gmax = jnp.max(lg, axis=-1, keepdims=True)
    gidx = jnp.min(jnp.where(lg == gmax, lane, big), axis=-1, keepdims=True)
    gw = 1.0 / jnp.sum(jnp.where(is_g, jnp.exp(logits - gmax), 0.0), axis=-1, keepdims=True)
    lo = float(_E0) + float(EXP_PER_GROUP) * gidx
    le = jnp.where((lane >= lo) & (lane < lo + float(EXP_PER_GROUP)), logits, neg)
    v1 = jnp.max(le, axis=-1, keepdims=True)
    i1 = jnp.min(jnp.where(le == v1, lane, big), axis=-1, keepdims=True)
    le2 = jnp.where(lane == i1, neg, le)
    v2 = jnp.max(le2, axis=-1, keepdims=True)
    i2 = jnp.min(jnp.where(le2 == v2, lane, big), axis=-1, keepdims=True)
    t = jnp.exp(v2 - v1)
    w1 = gw / (1.0 + t)
    w2 = gw * t / (1.0 + t)
    e1 = i1 - float(_E0)
    e2 = i2 - float(_E0)

    oh1 = lane == e1
    oh2 = lane == e2
    oh = jnp.where(oh1 | oh2, 1.0, 0.0)
    r = lax.broadcasted_iota(jnp.int32, (tm, tm), 0)
    c = lax.broadcasted_iota(jnp.int32, (tm, tm), 1)
    strict = jnp.where(c < r, 1.0, 0.0).astype(BF16)
    before = jnp.dot(strict, oh.astype(BF16), preferred_element_type=F32) + base_ref[...]
    rank1 = jnp.sum(jnp.where(oh1, before, 0.0), axis=-1, keepdims=True)
    rank2 = jnp.sum(jnp.where(oh2, before, 0.0), axis=-1, keepdims=True)
    total = base_ref[...] + jnp.sum(oh, axis=0, keepdims=True)
    base_ref[...] = total
    cnt_ref[...] = total

    ids = jnp.where(lane == 0.0, e1 * float(RANK_SPAN) + rank1,
                    jnp.where(lane == 1.0, e2 * float(RANK_SPAN) + rank2, 0.0))
    ri_ref[...] = ids.astype(jnp.int32)
    rw_ref[...] = jnp.where(lane == 0.0, w1, jnp.where(lane == 1.0, w2, 0.0))


def _outproj(x, gla_o, ml_o, mods, wa, wb, g2, wrh, wrl, br, tm):
    bsz, seq, d = x.shape
    const = lambda shape: pl.BlockSpec(shape, lambda b, i: (0,) * len(shape))
    tile = lambda w: pl.BlockSpec((1, tm, w), lambda b, i: (b, i, 0))
    flat = lambda rows: pl.BlockSpec((rows, LANES), lambda b, i: (b * (seq // tm) + i, 0))
    return pl.pallas_call(
        _outproj_kernel,
        grid=(bsz, seq // tm),
        in_specs=[tile(d), tile(gla_o.shape[2]), tile(ml_o.shape[2]),
                  pl.BlockSpec((1, N_MOD, d), lambda b, i: (b, 0, 0)),
                  const(wa.shape), const(wb.shape), const((1, d)),
                  const(wrh.shape), const(wrl.shape), const((1, LANES))],
        out_specs=[tile(d),
                   pl.BlockSpec((tm * d // LANES, LANES), lambda b, i: (b * (seq // tm) + i, 0)),
                   flat(tm), flat(tm), const((1, LANES))],
        out_shape=[jax.ShapeDtypeStruct((bsz, seq, d), F32),
                   jax.ShapeDtypeStruct((bsz * seq * d // LANES, LANES), F32),
                   jax.ShapeDtypeStruct((bsz * seq, LANES), jnp.int32),
                   jax.ShapeDtypeStruct((bsz * seq, LANES), F32),
                   jax.ShapeDtypeStruct((1, LANES), F32)],
        scratch_shapes=[pltpu.VMEM((1, LANES), F32)],
        compiler_params=_cparams(("arbitrary", "arbitrary")),
        name="outproj",
    )(x, gla_o, ml_o, mods, wa, wb, g2, wrh, wrl, br)


def _experts_kernel(dest_ref, ps_ref, cnt_ref, be_ref, nv_ref, meta_ref, h_hbm, w1_ref, w2_ref, ytok_hbm,
                    src_ref, xbuf, ybuf, w1c_ref, w2c_ref, gsem, ssem):
    i = pl.program_id(0)
    n_steps = pl.num_programs(0)
    n_used = meta_ref[0]
    tr = SUBLANES
    n_tok = h_hbm.shape[0] // tr
    n_rows = src_ref.shape[0]
    blk = xbuf.shape[1] // tr
    n_x = xbuf.shape[0]
    slot = lax.rem(i, 2)
    xslot = lax.rem(i, n_x)

    def slab(j):
        return pl.ds(pl.multiple_of(j * tr, tr), tr)

    def gather_copy(tok, s, r):
        return pltpu.make_async_copy(h_hbm.at[slab(tok), :], xbuf.at[s, slab(r), :], gsem.at[s])

    def scatter_copy(a, s, r):
        return pltpu.make_async_copy(ybuf.at[s, slab(r), :], ytok_hbm.at[slab(a), :], ssem.at[s])

    def groups(b):
        nv = jnp.where(b < n_steps, nv_ref[jnp.minimum(b, n_steps - 1)], 0)
        return lax.shift_right_logical(nv + (ROW_GROUP - 1), ROW_GROUP.bit_length() - 1)

    def rows_loop(b, body):
        def step(g, c):
            for u in range(ROW_GROUP):
                body(g * ROW_GROUP + u, u % 2)
            return c
        lax.fori_loop(0, groups(b), step, 0)

    def issue_gather(b, s):
        def one(r, prio):
            tok = lax.shift_right_logical(src_ref[b * blk + r], 1)
            gather_copy(jnp.minimum(tok, n_tok - 1), s, r).start(priority=prio)
        rows_loop(b, one)

    def wait_gather(b, s):
        rows_loop(b, lambda r, prio: gather_copy(0, s, 0).wait())

    def wait_scatter(b, s):
        rows_loop(b, lambda r, prio: scatter_copy(0, s, 0).wait())

    @pl.when(i == 0)
    def _():
        xbuf[...] = jnp.zeros_like(xbuf)
        ybuf[...] = jnp.zeros_like(ybuf)
        for s in range(2):
            tail = ytok_hbm.at[pl.ds((2 * n_tok + s * blk) * tr, blk * tr), :]
            cp = pltpu.make_async_copy(ybuf.at[s], tail, ssem.at[s])
            cp.start()
            cp.wait()

        def put(a, c):
            src_ref[dest_ref[a]] = a
            return c
        lax.fori_loop(0, 2 * n_tok, put, 0, unroll=16)

        def pad(j, c):
            src_ref[j] = 2 * n_tok + (j & (2 * blk - 1))
            return c

        def pad_expert(e, c):
            lax.fori_loop(ps_ref[e] + cnt_ref[e], ps_ref[e + 1], pad, 0)
            return c
        lax.fori_loop(0, cnt_ref.shape[0], pad_expert, 0)
        for b in range(n_x - 1):
            issue_gather(min(b, n_rows // blk - 1), b)

    @pl.when(i < n_used)
    def _():
        wait_gather(i, xslot)

        @pl.when((i == 0) | (be_ref[i] != be_ref[jnp.maximum(i - 1, 0)]))
        def _():
            w1c_ref[...] = w1_ref[0].astype(BF16)
            w2c_ref[...] = w2_ref[0].astype(BF16)

        @pl.when(i >= 2)
        def _():
            wait_scatter(i - 2, slot)

        row = lax.broadcasted_iota(jnp.int32, (blk, 1), 0)
        x = _load_token_tiles(xbuf.at[xslot], 0, blk, tr, tr)
        x = jnp.where(row < nv_ref[i], x, 0.0).astype(BF16)
        h = jnp.dot(x, w1c_ref[...], preferred_element_type=F32)
        a = (_silu(h[:, :D_EXPERT]) * h[:, D_EXPERT:]).astype(BF16)
        _store_token_tiles(ybuf.at[slot], jnp.dot(a, w2c_ref[...], preferred_element_type=F32))
        rows_loop(i, lambda r, prio: scatter_copy(src_ref[i * blk + r], slot, r).start(priority=prio))
        issue_gather(i + n_x - 1, lax.rem(i + n_x - 1, n_x))

    @pl.when(i == n_steps - 1)
    def _():
        for back in (2, 1):
            wait_scatter(n_used - back, lax.rem(n_used - back, 2))


def _experts(dest, pad_start, counts, block_e, block_nv, meta, h2, w_in, w_out, nb):
    d = w_in.shape[1]
    tr = d // LANES
    assert tr == SUBLANES, "a token row must fill exactly one (8, 128) tile"
    n_tok = h2.shape[0] // tr
    de2 = w_in.shape[2]
    n_rows = nb * MOE_BLK
    assert 2 * n_tok >= 2 * MOE_BLK
    assert MOE_BLK & (MOE_BLK - 1) == 0
    wmap = lambda i, pk, ps, cnt, be, nv, meta: (be[i], 0, 0)
    return pl.pallas_call(
        _experts_kernel,
        grid_spec=pltpu.PrefetchScalarGridSpec(
            num_scalar_prefetch=6, grid=(nb,),
            in_specs=[pl.BlockSpec(memory_space=pl.ANY),
                      pl.BlockSpec((1, d, de2), wmap),
                      pl.BlockSpec((1, de2 // 2, d), wmap)],
            out_specs=pl.BlockSpec(memory_space=pl.ANY),
            scratch_shapes=[pltpu.SMEM((n_rows,), jnp.int32),
                            pltpu.VMEM((GATHER_RING, MOE_BLK * tr, LANES), F32),
                            pltpu.VMEM((2, MOE_BLK * tr, LANES), F32),
                            pltpu.VMEM((d, de2), BF16), pltpu.VMEM((de2 // 2, d), BF16),
                            pltpu.SemaphoreType.DMA((GATHER_RING,)), pltpu.SemaphoreType.DMA((2,))]),
        out_shape=jax.ShapeDtypeStruct(((2 * n_tok + 2 * MOE_BLK) * tr, LANES), F32),
        compiler_params=_cparams(("arbitrary",)),
        name="experts",
    )(dest, pad_start, counts, block_e, block_nv, meta, h2, w_in, w_out)


def _combine_kernel(x1_ref, y_ref, rw_ref, mod_ref, fg_ref, o_ref):
    tc, d = x1_ref.shape
    tr = d // LANES
    y1 = _load_token_tiles(y_ref, 0, tc, tr, 2 * tr)
    y2 = _load_token_tiles(y_ref, tr, tc, tr, 2 * tr)
    moe = rw_ref[:, 0:1] * y1 + rw_ref[:, 1:2] * y2
    x2 = x1_ref[...] + mod_ref[0, 5:6, :] * moe
    o_ref[...] = x2 * lax.rsqrt(jnp.mean(x2 * x2, axis=-1, keepdims=True) + EPS) * fg_ref[...]


def _combine(x1, ytok, rw, mods, fg, tokens_per_batch, tc):
    n_tok, d = x1.shape
    tiles_per_batch = tokens_per_batch // tc
    return pl.pallas_call(
        _combine_kernel,
        grid=(n_tok // tc,),
        in_specs=[pl.BlockSpec((tc, d), lambda i: (i, 0)),
                  pl.BlockSpec((2 * tc * d // LANES, LANES), lambda i: (i, 0)),
                  pl.BlockSpec((tc, LANES), lambda i: (i, 0)),
                  pl.BlockSpec((1, N_MOD, d), lambda i: (i // tiles_per_batch, 0, 0)),
                  pl.BlockSpec((1, d), lambda i: (0, 0))],
        out_specs=pl.BlockSpec((tc, d), lambda i: (i, 0)),
        out_shape=jax.ShapeDtypeStruct((n_tok, d), F32),
        compiler_params=_cparams(("arbitrary",)),
        name="combine",
    )(x1, ytok, rw, mods, fg)


def _prep_inproj_weights(w_in, gla_up_w, gla_up_b, ml_i_b, ml_f_b):
    d = w_in.shape[0]
    o_gq, o_gk, o_gv, o_gg = 0, GLA_QK_W, 2 * GLA_QK_W, 2 * GLA_QK_W + GLA_V_W
    o_lr = o_gg + GLA_V_W
    o_mqk = o_lr + 2 * GLA_LR
    o_mi = o_mqk + 4 * ML_W
    o_mf = o_mi + 2 * ML_HEADS

    wg = w_in[:, o_gq:o_lr]
    wm = w_in[:, o_mqk:o_mi]
    ws = jnp.concatenate([w_in[:, o_lr:o_mqk], w_in[:, o_mi:o_mf + 2 * ML_HEADS],
                          jnp.zeros((d, LANES - 2 * GLA_LR - 4 * ML_HEADS), w_in.dtype)], axis=1)
    bias = jnp.zeros((LANES,), F32)
    bias = bias.at[_MI0:_MI0 + 2 * ML_HEADS].set(ml_i_b.reshape(-1))
    bias = bias.at[_MF0:_MF0 + 2 * ML_HEADS].set(ml_f_b.reshape(-1))
    up = gla_up_w.reshape(2, GLA_LR, GLA_HEADS, GLA_DK).transpose(2, 0, 1, 3)
    ub = gla_up_b.reshape(2, GLA_HEADS, GLA_DK).transpose(1, 0, 2)
    wup = jnp.zeros((GLA_HEADS, 2, LANES, LANES), F32)
    bup = jnp.zeros((GLA_HEADS, 2, 1, LANES), F32)
    for hd in range(GLA_HEADS):
        lo = (hd % (LANES // GLA_DK)) * GLA_DK
        for dr in range(2):
            wup = wup.at[hd, dr, dr * GLA_LR:(dr + 1) * GLA_LR, lo:lo + GLA_DK].set(up[hd, dr])
        bup = bup.at[hd, :, 0, lo:lo + GLA_DK].set(ub[hd])
    return (wg.astype(BF16), wm.astype(BF16), ws.astype(BF16), ws.T.astype(BF16),
            bias.reshape(1, LANES), bias.reshape(LANES, 1), wup.astype(BF16), bup)


def _layer(x, ctx, mods, norm1_g, w_in, gla_up_w, gla_up_b, gla_norm_g, ml_conv_w, ml_conv_b,
           ml_i_b, ml_f_b, ml_norm_g, w_out, norm2_g, rg_w, rg_b, re_w, re_b, e_w_in, e_w_out, final_g):
    bsz, seq, d = x.shape
    n_tok = bsz * seq
    wg, wm, ws, wst, bcol, brow, wup, bup = _prep_inproj_weights(w_in, gla_up_w, gla_up_b, ml_i_b, ml_f_b)
    g1 = norm1_g.reshape(1, d)
    zg_x, zm_x, zs_x, gcol_x, grow_x = _inproj(x, mods, lambda b: b, g1, wg, wm, ws, wst, bcol, brow, 256)
    zg_c, zm_c, zs_c, gcol_c, grow_c = _inproj(ctx, mods, lambda b: bsz, g1, wg, wm, ws, wst, bcol, brow,
                                               min(256, ctx.shape[1]))
    gla_o = _gla(zg_x, zs_x, zg_c, zs_c, wup, bup, gla_norm_g.reshape(1, -1))
    ml_o = _mlstm(zm_x, gcol_x, grow_x, zm_c, grow_c,
                  ml_conv_w.reshape(9, -1), ml_conv_b.reshape(1, -1), ml_norm_g.reshape(1, -1))

    wr = jnp.zeros((d, LANES), F32).at[:, _G0:_E0].set(rg_w).at[:, _E0:_E0 + N_EXPERTS].set(re_w)
    br = jnp.zeros((1, LANES), F32).at[0, _G0:_E0].set(rg_b).at[0, _E0:_E0 + N_EXPERTS].set(re_b)
    wrh = wr.astype(BF16)
    wrl = (wr - wrh.astype(F32)).astype(BF16)
    x1, h2, ri, rw, cnt = _outproj(x, gla_o, ml_o, mods, w_out[:GLA_V_W].astype(BF16),
                                   w_out[GLA_V_W:].astype(BF16), norm2_g.reshape(1, d), wrh, wrl, br, 256)

    counts = cnt[0, :N_EXPERTS].astype(jnp.int32)
    nblk = (counts + MOE_BLK - 1) // MOE_BLK
    blk_end = jnp.cumsum(nblk)
    blk_start = blk_end - nblk
    n_used = blk_end[-1]
    nb_max = (2 * n_tok) // MOE_BLK + N_EXPERTS
    blk = jnp.arange(nb_max, dtype=jnp.int32)
    blk_c = jnp.minimum(blk, n_used - 1)
    onehot = (blk_c[:, None] >= blk_start[None, :]) & (blk_c[:, None] < blk_end[None, :])
    pick = lambda v: jnp.sum(jnp.where(onehot, v[None, :], 0), axis=1)
    block_e = pick(jnp.arange(N_EXPERTS, dtype=jnp.int32)).astype(jnp.int32)
    block_nv = jnp.clip(pick(counts) - (blk_c - pick(blk_start)) * MOE_BLK, 0, MOE_BLK)
    block_nv = jnp.where(blk < n_used, block_nv, 0).astype(jnp.int32)
    pad_start = (jnp.concatenate([blk_start, blk_end[-1:]]) * MOE_BLK).astype(jnp.int32)
    packed = ri[:, 0:2].reshape(-1)
    e_of = lax.shift_right_logical(packed, RANK_BITS)
    start_of = jnp.sum(jnp.where(e_of[:, None] == jnp.arange(N_EXPERTS, dtype=jnp.int32)[None, :],
                                 pad_start[None, :N_EXPERTS], 0), axis=1)
    dest = (start_of + (packed & (RANK_SPAN - 1))).astype(jnp.int32)
    meta = jnp.stack([n_used, n_used]).astype(jnp.int32)

    ytok = _experts(dest, pad_start, counts, block_e, block_nv, meta, h2, e_w_in, e_w_out, nb_max)
    out = _combine(x1.reshape(n_tok, d), ytok, rw, mods, final_g.reshape(1, d), seq, 256)
    return out.reshape(bsz, seq, d)


def kernel(x, c, ctx, c_ctx, ada_w, ada_b, norm1_g, w_in, gla_up_w, gla_up_b, gla_norm_g, ml_conv_w, ml_conv_b,
           ml_i_b, ml_f_b, ml_norm_g, w_out, norm2_g, router_group_w, router_group_b, router_expert_w,
           router_expert_b, expert_w_in, expert_w_out, final_norm_g):
    assert ada_w.shape[0] == 1, "single-layer stack"
    bsz, d = c.shape
    cc = jnp.concatenate([c, c_ctx[None, :], jnp.zeros((8 - bsz - 1, d), F32)], axis=0)
    mods = _modulation(cc, ada_w[0], ada_b[0]).reshape(8, N_MOD, d)
    return _layer(x, ctx, mods, norm1_g[0], w_in[0], gla_up_w[0], gla_up_b[0], gla_norm_g[0],
                  ml_conv_w[0], ml_conv_b[0], ml_i_b[0], ml_f_b[0], ml_norm_g[0], w_out[0], norm2_g[0],
                  router_group_w[0], router_group_b[0], router_expert_w[0], router_expert_b[0],
                  expert_w_in[0], expert_w_out[0], final_norm_g)
```

```python
import functools

import jax
import jax.numpy as jnp
from jax import lax
from jax.experimental import pallas as pl
from jax.experimental.pallas import tpu as pltpu

F32 = jnp.float32
BF16 = jnp.bfloat16

D_MODEL = 1024
GRID_W = 64
N_MOD = 6
EPS = 1e-6

GLA_HEADS = 4
GLA_DK = 64
GLA_DV = 128
GLA_LR = 16
GLA_TAU = 16.0
GLA_C = 128
SCAN_UNROLL = 4

ML_HEADS = 4
ML_DH = 128
ML_C = 128

N_GROUPS = 4
EXP_PER_GROUP = 8
N_EXPERTS = N_GROUPS * EXP_PER_GROUP
D_EXPERT = 512
MOE_BLK = 256

GLA_QK_W = GLA_HEADS * GLA_DK
GLA_V_W = GLA_HEADS * GLA_DV
ML_W = ML_HEADS * ML_DH
LANES = 128
VMEM_LIMIT = 56 * 1024 * 1024

_LR0 = 0
_MI0 = 2 * GLA_LR
_MF0 = _MI0 + 2 * ML_HEADS


def _cparams(sem):
    return pltpu.CompilerParams(dimension_semantics=sem, vmem_limit_bytes=VMEM_LIMIT)


def _sigmoid(x):
    return 1.0 / (1.0 + jnp.exp(-x))


def _silu(x):
    return x * _sigmoid(x)


def _log_sigmoid(x):
    return jnp.minimum(x, 0.0) - jnp.log1p(jnp.exp(-jnp.abs(x)))


def _split_dot(a_bf16_exact, x, dims=None):
    x_hi = x.astype(BF16)
    x_lo = (x - x_hi.astype(F32)).astype(BF16)
    if dims is None:
        f = lambda u: jnp.dot(a_bf16_exact, u, preferred_element_type=F32)
    else:
        f = lambda u: lax.dot_general(u, a_bf16_exact, dims, preferred_element_type=F32)
    return f(x_hi) + f(x_lo)


def _mod_kernel(c_ref, w_ref, b_ref, o_ref):
    c = c_ref[...]
    s = _silu(c).astype(BF16)
    o_ref[...] = jnp.dot(s, w_ref[...].astype(BF16), preferred_element_type=F32) + b_ref[...]


def _modulation(cc, ada_w, ada_b):
    rows, d = cc.shape
    n = ada_w.shape[1]
    tn = 1536
    return pl.pallas_call(
        _mod_kernel,
        grid=(n // tn,),
        in_specs=[pl.BlockSpec((rows, d), lambda j: (0, 0)),
                  pl.BlockSpec((d, tn), lambda j: (0, j)),
                  pl.BlockSpec((1, tn), lambda j: (0, j))],
        out_specs=pl.BlockSpec((rows, tn), lambda j: (0, j)),
        out_shape=jax.ShapeDtypeStruct((rows, n), F32),
        compiler_params=_cparams(("arbitrary",)),
        name="mod",
    )(cc, ada_w, ada_b.reshape(1, n))


def _inproj_kernel(x_ref, mod_ref, g_ref, wg_ref, wm_ref, ws_ref, wst_ref, bcol_ref, brow_ref,
                   zg_ref, zm_ref, zs_ref, gcol_ref, grow_ref):
    tm = x_ref.shape[1]
    x = x_ref[0]
    y = x * lax.rsqrt(jnp.mean(x * x, axis=-1, keepdims=True) + EPS) * g_ref[...]
    h = (y * (1.0 + mod_ref[0, 1:2, :]) + mod_ref[0, 0:1, :]).astype(BF16)
    zg_ref[0] = jnp.dot(h, wg_ref[...], preferred_element_type=F32)
    zm_ref[0] = jnp.dot(h, wm_ref[...], preferred_element_type=F32)
    zs = jnp.dot(h, ws_ref[...], preferred_element_type=F32) + bcol_ref[...]
    zst = lax.dot_general(wst_ref[...], h, (((1,), (1,)), ((), ())),
                          preferred_element_type=F32) + brow_ref[...]
    zs_ref[0] = zs

    r = lax.broadcasted_iota(jnp.int32, (tm, tm), 0)
    c = lax.broadcasted_iota(jnp.int32, (tm, tm), 1)
    shift = ML_C.bit_length() - 1
    same = jnp.right_shift(r, shift) == jnp.right_shift(c, shift)
    lower = jnp.where(same & (c <= r), 1.0, 0.0).astype(BF16)
    upper = jnp.where(same & (c >= r), 1.0, 0.0).astype(BF16)
    chunks = range(0, tm, ML_C)
    lsf = _log_sigmoid(zs)
    a_pre = _split_dot(lower, lsf)
    tot = jnp.concatenate([jnp.broadcast_to(a_pre[o + ML_C - 1:o + ML_C, :], (ML_C, LANES)) for o in chunks], axis=0)
    a_suf = tot - a_pre + lsf
    lsft = _log_sigmoid(zst)
    a_pre_t = _split_dot(upper, lsft, (((1,), (0,)), ((), ())))
    tot_t = jnp.concatenate([jnp.broadcast_to(a_pre_t[:, o + ML_C - 1:o + ML_C], (LANES, ML_C)) for o in chunks], axis=1)
    a_suf_t = tot_t - a_pre_t + lsft

    lane = lax.broadcasted_iota(jnp.int32, (tm, LANES), 1)
    for hd in range(ML_HEADS):
        cols = (a_pre[:, _MF0 + hd:_MF0 + hd + 1],
                a_suf[:, _MF0 + ML_HEADS + hd:_MF0 + ML_HEADS + hd + 1],
                zs[:, _MI0 + hd:_MI0 + hd + 1],
                zs[:, _MI0 + ML_HEADS + hd:_MI0 + ML_HEADS + hd + 1])
        slab = jnp.zeros((tm, LANES), F32)
        for j, col in enumerate(cols):
            slab = jnp.where(lane == j, col, slab)
        gcol_ref[0, :, hd * LANES:(hd + 1) * LANES] = slab
        rows = (a_pre_t[_MF0 + hd:_MF0 + hd + 1, :],
                a_suf_t[_MF0 + ML_HEADS + hd:_MF0 + ML_HEADS + hd + 1, :],
                zst[_MI0 + hd:_MI0 + hd + 1, :],
                zst[_MI0 + ML_HEADS + hd:_MI0 + ML_HEADS + hd + 1, :])
        for j, row in enumerate(rows):
            grow_ref[0, hd, j:j + 1, :] = row
        grow_ref[0, hd, 4:8, :] = jnp.zeros((4, tm), F32)


def _inproj(x, mods, mod_row_of_batch, norm_g, wg, wm, ws, wst, bcol, brow, tm):
    bsz, l, d = x.shape
    assert l % tm == 0 and tm % ML_C == 0
    const = lambda shape: pl.BlockSpec(shape, lambda b, i: (0,) * len(shape))
    return pl.pallas_call(
        _inproj_kernel,
        grid=(bsz, l // tm),
        in_specs=[pl.BlockSpec((1, tm, d), lambda b, i: (b, i, 0)),
                  pl.BlockSpec((1, N_MOD, d), lambda b, i: (mod_row_of_batch(b), 0, 0)),
                  const((1, d)), const(wg.shape), const(wm.shape), const(ws.shape), const(wst.shape),
                  const((1, LANES)), const((LANES, 1))],
        out_specs=[pl.BlockSpec((1, tm, wg.shape[1]), lambda b, i: (b, i, 0)),
                   pl.BlockSpec((1, tm, wm.shape[1]), lambda b, i: (b, i, 0)),
                   pl.BlockSpec((1, tm, LANES), lambda b, i: (b, i, 0)),
                   pl.BlockSpec((1, tm, ML_HEADS * LANES), lambda b, i: (b, i, 0)),
                   pl.BlockSpec((1, ML_HEADS, 8, tm), lambda b, i: (b, 0, 0, i))],
        out_shape=[jax.ShapeDtypeStruct((bsz, l, wg.shape[1]), F32),
                   jax.ShapeDtypeStruct((bsz, l, wm.shape[1]), F32),
                   jax.ShapeDtypeStruct((bsz, l, LANES), F32),
                   jax.ShapeDtypeStruct((bsz, l, ML_HEADS * LANES), F32),
                   jax.ShapeDtypeStruct((bsz, ML_HEADS, 8, l), F32)],
        compiler_params=_cparams(("arbitrary", "arbitrary")),
        name="inproj",
    )(x, mods, norm_g, wg, wm, ws, wst, bcol, brow)


def _round_robin(chains):
    results = [None] * len(chains)
    live = list(enumerate(chains))
    while live:
        still = []
        for idx, chain in live:
            try:
                next(chain)
                still.append((idx, chain))
            except StopIteration as done:
                results[idx] = done.value
        live = still
    return results


def _visibility(c):
    r = lax.broadcasted_iota(jnp.int32, (c, c), 0)
    cc = lax.broadcasted_iota(jnp.int32, (c, c), 1)
    masks = [cc <= r, cc >= r]
    return masks, [jnp.where(m, 1.0, 0.0).astype(BF16) for m in masks]


def _gla_chunk(q, k, v, zs, wup, bup, state, direction, want_out, causal, tri):
    c = k.shape[0]
    logits = jnp.dot(zs.astype(BF16), wup, preferred_element_type=F32) + bup
    yield
    g = _log_sigmoid(logits) * (1.0 / GLA_TAU)
    b = _split_dot(tri, g)
    yield
    b_end = b[c - 1:c, :] if direction == 0 else b[0:1, :]
    kd = (k * jnp.exp(b_end - b)).astype(BF16)
    upd = lax.dot_general(v.astype(BF16), kd, (((0,), (0,)), ((), ())), preferred_element_type=F32)
    s = state[direction]
    state[direction] = jnp.exp(b_end) * s + upd
    if not want_out:
        return None
    b_mid = b[c // 2:c // 2 + 1, :]
    q_in = (q * jnp.exp(b - b_mid)).astype(BF16)
    k_in = (k * jnp.exp(b_mid - b)).astype(BF16)
    att = lax.dot_general(q_in, k_in, (((1,), (1,)), ((), ())), preferred_element_type=F32)
    inter = lax.dot_general((q * jnp.exp(b)).astype(BF16), s.astype(BF16),
                            (((1,), (1,)), ((), ())), preferred_element_type=F32)
    yield
    att = jnp.where(causal, att, 0.0)
    return jnp.dot(att.astype(BF16), v.astype(BF16), preferred_element_type=F32) + inter


def _scan_order(j, n, unroll):
    return [(d, j * unroll + u if d == 0 else n - 1 - (j * unroll + u)) for u in range(unroll) for d in range(2)]


def _gla_kernel(q_ref, k_ref, v_ref, gg_ref, zs_ref, kc_ref, vc_ref, zsc_ref,
                wup_ref, bup_ref, ng_ref, o_ref, s_ref, acc_ref):
    seq = q_ref.shape[1]
    ctx = kc_ref.shape[1]
    n = seq // GLA_C
    nc = ctx // GLA_C
    s_ref[...] = jnp.zeros_like(s_ref)

    def rows(i):
        return pl.ds(pl.multiple_of(i * GLA_C, GLA_C), GLA_C)

    masks, tris = _visibility(GLA_C)
    heads_per_slab = LANES // GLA_DK
    lo = lax.rem(pl.program_id(1), heads_per_slab) * GLA_DK
    lane = lax.broadcasted_iota(jnp.int32, (GLA_C, LANES), 1)
    mine = (lane >= lo) & (lane < lo + GLA_DK)

    def own(t):
        return jnp.where(mine, t, 0.0)

    def ctx_step(j, carry):
        order = _scan_order(j, nc, 1)
        ins = [(own(kc_ref[0, rows(i), :]), vc_ref[0, rows(i), :], zsc_ref[0, rows(i), :]) for _, i in order]
        s = [s_ref[0], s_ref[1]]
        _round_robin([_gla_chunk(None, k, v, zs, wup_ref[0, d], bup_ref[0, d], s, d, False, masks[d], tris[d])
                      for (d, _), (k, v, zs) in zip(order, ins)])
        s_ref[0] = s[0]
        s_ref[1] = s[1]
        return carry

    lax.fori_loop(0, nc, ctx_step, 0)

    def lat_step(j, carry, second):
        order = _scan_order(j, n, SCAN_UNROLL)
        ins = [(own(q_ref[0, rows(i), :]), own(k_ref[0, rows(i), :]), v_ref[0, rows(i), :], zs_ref[0, rows(i), :])
               for _, i in order]
        prev = [(acc_ref[rows(i), :], gg_ref[0, rows(i), :]) for _, i in order] if second else None
        s = [s_ref[0], s_ref[1]]
        outs = _round_robin([_gla_chunk(q * (GLA_DK ** -0.5), k, v, zs, wup_ref[0, d], bup_ref[0, d], s, d, True,
                                        masks[d], tris[d])
                             for (d, _), (q, k, v, zs) in zip(order, ins)])
        s_ref[0] = s[0]
        s_ref[1] = s[1]
        for idx, (_, i) in enumerate(order):
            if second:
                total = prev[idx][0] + outs[idx]
                y = total * lax.rsqrt(jnp.mean(total * total, axis=-1, keepdims=True) + EPS) * ng_ref[...]
                o_ref[0, rows(i), :] = (y * _silu(prev[idx][1])).astype(o_ref.dtype)
            else:
                acc_ref[rows(i), :] = outs[idx]
        return carry

    half = n // (2 * SCAN_UNROLL)
    lax.fori_loop(0, half, functools.partial(lat_step, second=False), 0)
    lax.fori_loop(half, 2 * half, functools.partial(lat_step, second=True), 0)


def _gla(zg_x, zs_x, zg_c, zs_c, wup, bup, norm_g):
    bsz, seq, _ = zg_x.shape
    ctx = zg_c.shape[1]
    assert seq % (2 * SCAN_UNROLL * GLA_C) == 0 and ctx % GLA_C == 0
    h = GLA_HEADS

    hps = LANES // GLA_DK
    qk_slabs = h // hps

    def qk(l, off):
        return pl.BlockSpec((1, l, LANES), lambda b, hd: (b, 0, off + hd // hps))

    def col(l, off):
        return pl.BlockSpec((1, l, LANES), lambda b, hd: (b, 0, off + hd))

    return pl.pallas_call(
        _gla_kernel,
        grid=(bsz, h),
        in_specs=[qk(seq, 0), qk(seq, qk_slabs), col(seq, 2 * qk_slabs), col(seq, 2 * qk_slabs + h),
                  pl.BlockSpec((1, seq, LANES), lambda b, hd: (b, 0, 0)),
                  qk(ctx, qk_slabs), col(ctx, 2 * qk_slabs),
                  pl.BlockSpec((1, ctx, LANES), lambda b, hd: (b, 0, 0)),
                  pl.BlockSpec((1, 2, LANES, LANES), lambda b, hd: (hd, 0, 0, 0)),
                  pl.BlockSpec((1, 2, 1, LANES), lambda b, hd: (hd, 0, 0, 0)),
                  pl.BlockSpec((1, LANES), lambda b, hd: (0, hd))],
        out_specs=pl.BlockSpec((1, seq, LANES), lambda b, hd: (b, 0, hd)),
        out_shape=jax.ShapeDtypeStruct((bsz, seq, h * GLA_DV), BF16),
        scratch_shapes=[pltpu.VMEM((2, LANES, LANES), F32), pltpu.VMEM((seq, LANES), F32)],
        compiler_params=_cparams(("arbitrary", "arbitrary")),
        name="gla",
    )(zg_x, zg_x, zg_x, zg_x, zs_x, zg_c, zg_c, zs_c, wup, bup, norm_g)


def _grid_conv_silu(src_ref, dst_ref, w_ref, b_ref, grid_w, scale):
    l = src_ref.shape[1]
    n_rows = l // grid_w
    col = lax.broadcasted_iota(jnp.int32, (grid_w, LANES), 0)

    def body(r, carry):
        acc = jnp.zeros((grid_w, LANES), F32) + b_ref[...]
        for dy in (-1, 0, 1):
            if n_rows == 1 and dy != 0:
                continue
            rr = jnp.clip(r + dy, 0, n_rows - 1)
            blk = src_ref[0, pl.ds(pl.multiple_of(rr * grid_w, grid_w), grid_w), :]
            valid = jnp.logical_and(r + dy >= 0, r + dy < n_rows)
            blk = jnp.where(valid, blk, 0.0)
            for dx in (-1, 0, 1):
                if dx == 0:
                    sh = blk
                else:
                    sh = pltpu.roll(blk, shift=(-dx) % grid_w, axis=0)
                    sh = jnp.where((col + dx >= 0) & (col + dx < grid_w), sh, 0.0)
                tap = (dy + 1) * 3 + (dx + 1)
                acc = acc + sh * w_ref[tap:tap + 1, :]
        dst_ref[pl.ds(pl.multiple_of(r * grid_w, grid_w), grid_w), :] = _silu(acc) * scale
        return carry

    lax.fori_loop(0, n_rows, body, 0)


def _ml_chunk(qb, k, vt, gcol, grow, state, mstate, direction, want_out, visible):
    c = k.shape[0]
    a_row = grow[direction:direction + 1, :]
    i_row = grow[2 + direction:3 + direction, :]
    a_end = a_row[:, c - 1:c] if direction == 0 else a_row[:, 0:1]
    g = a_end - a_row + i_row
    g_max = jnp.max(g, axis=-1, keepdims=True)
    head = 2 * SUBLANES
    pad_rows = jnp.zeros((LANES - head, c), BF16)
    first = lax.broadcasted_iota(jnp.int32, (head, c), 0) == 0
    kb = k.astype(BF16)
    if want_out:
        c_col = gcol[:, direction:direction + 1] - gcol[:, 2 + direction:3 + direction]
        dmat = jnp.where(visible, a_row - c_col, -jnp.inf)
        d_max = jnp.max(dmat, axis=0, keepdims=True)
        kq = jnp.dot(kb, qb, preferred_element_type=F32)
    yield
    s, m = state[direction], mstate[direction]
    m_new = jnp.maximum(a_end + m, g_max)
    decay = jnp.exp(a_end + m - m_new)
    w = jnp.exp(g - m_new)
    vw = jnp.concatenate([(vt * w).astype(BF16), jnp.where(first, w, 0.0).astype(BF16), pad_rows], axis=0)
    state[direction] = decay * s + jnp.dot(vw, kb, preferred_element_type=F32)
    mstate[direction] = m_new
    if not want_out:
        return None
    inter = a_row + m
    m_t = jnp.maximum(inter, d_max)
    w_inter = jnp.exp(inter - m_t)
    p = (kq * jnp.exp(dmat - m_t)).astype(BF16)
    vt_aug = jnp.concatenate([vt.astype(BF16), jnp.where(first, 1.0, 0.0).astype(BF16), pad_rows], axis=0)
    pv = jnp.dot(vt_aug, p, preferred_element_type=F32)
    sq = jnp.dot(s.astype(BF16), qb, preferred_element_type=F32)
    yield
    both = pv + w_inter * sq
    den = both[ML_DH:ML_DH + 1, :]
    return both[:ML_DH, :] / jnp.maximum(jnp.abs(den), jnp.exp(-m_t))


def _mlstm_kernel(q_ref, k_ref, v_ref, mo_ref, gcol_ref, grow_ref,
                  kc_ref, vc_ref, growc_ref,
                  wq_ref, wk_ref, bq_ref, bk_ref, ng_ref, o_ref,
                  cq_ref, ck_ref, ckc_ref, qt_ref, vt_ref, vct_ref, s_ref, m_ref, acc_ref):
    seq = q_ref.shape[1]
    ctx = kc_ref.shape[1]
    n = seq // ML_C
    nc = ctx // ML_C
    _grid_conv_silu(q_ref, cq_ref, wq_ref, bq_ref, GRID_W, 1.0)
    _grid_conv_silu(k_ref, ck_ref, wk_ref, bk_ref, GRID_W, ML_DH ** -0.5)
    _grid_conv_silu(kc_ref, ckc_ref, wk_ref, bk_ref, ctx, ML_DH ** -0.5)
    s_ref[...] = jnp.zeros_like(s_ref)
    m_ref[...] = jnp.zeros_like(m_ref)

    def rows(i):
        return pl.ds(pl.multiple_of(i * ML_C, ML_C), ML_C)

    def transpose_chunks(i, carry):
        qt_ref[:, rows(i)] = cq_ref[rows(i), :].T.astype(qt_ref.dtype)
        vt_ref[:, rows(i)] = v_ref[0, rows(i), :].T
        return carry

    lax.fori_loop(0, n, transpose_chunks, 0, unroll=2)
    for i in range(nc):
        vct_ref[:, i * ML_C:(i + 1) * ML_C] = vc_ref[0, i * ML_C:(i + 1) * ML_C, :].T

    def load_state():
        return [s_ref[0], s_ref[1]], [m_ref[0, :, 0:1], m_ref[1, :, 0:1]]

    def store_state(s, m):
        for d in range(2):
            s_ref[d] = s[d]
            m_ref[d] = jnp.broadcast_to(m[d], m_ref.shape[1:])

    masks, _ = _visibility(ML_C)
    visible = [masks[1], masks[0]]

    def ctx_step(j, carry):
        order = _scan_order(j, nc, 1)
        ins = [(ckc_ref[rows(i), :], vct_ref[:, rows(i)], growc_ref[0, 0, :, rows(i)]) for _, i in order]
        s, m = load_state()
        _round_robin([_ml_chunk(None, k, vt, None, grow, s, m, d, False, None)
                      for (d, _), (k, vt, grow) in zip(order, ins)])
        store_state(s, m)
        return carry

    lax.fori_loop(0, nc, ctx_step, 0)

    def lat_step(j, carry, second):
        order = _scan_order(j, n, SCAN_UNROLL)
        ins = [(qt_ref[:, rows(i)], ck_ref[rows(i), :], vt_ref[:, rows(i)], gcol_ref[0, rows(i), :],
                grow_ref[0, 0, :, rows(i)]) for _, i in order]
        prev = [(acc_ref[:, rows(i)], mo_ref[0, rows(i), :]) for _, i in order] if second else None
        s, m = load_state()
        outs = _round_robin([_ml_chunk(qb, k, vt, gcol, grow, s, m, d, True, visible[d])
                             for (d, _), (qb, k, vt, gcol, grow) in zip(order, ins)])
        store_state(s, m)
        for idx, (_, i) in enumerate(order):
            if second:
                total = prev[idx][0] + outs[idx]
                y = total * lax.rsqrt(jnp.mean(total * total, axis=0, keepdims=True) + EPS) * ng_ref[...]
                o_ref[0, rows(i), :] = (_sigmoid(prev[idx][1]) * y.T).astype(o_ref.dtype)
            else:
                acc_ref[:, rows(i)] = outs[idx]
        return carry

    half = n // (2 * SCAN_UNROLL)
    lax.fori_loop(0, half, functools.partial(lat_step, second=False), 0)
    lax.fori_loop(half, 2 * half, functools.partial(lat_step, second=True), 0)


def _mlstm(zm_x, gcol_x, grow_x, zm_c, grow_c, conv_w, conv_b, norm_g):
    bsz, seq, _ = zm_x.shape
    ctx = zm_c.shape[1]
    assert seq % (2 * SCAN_UNROLL * ML_C) == 0 and ctx % ML_C == 0 and seq % GRID_W == 0
    h = ML_HEADS

    def col(l, off):
        return pl.BlockSpec((1, l, LANES), lambda b, hd: (b, 0, off + hd))

    def gates(l):
        return [pl.BlockSpec((1, l, LANES), lambda b, hd: (b, 0, hd)),
                pl.BlockSpec((1, 1, 8, l), lambda b, hd: (b, hd, 0, 0))]

    return pl.pallas_call(
        _mlstm_kernel,
        grid=(bsz, h),
        in_specs=[col(seq, 0), col(seq, h), col(seq, 2 * h), col(seq, 3 * h)] + gates(seq)
                 + [col(ctx, h), col(ctx, 2 * h), gates(ctx)[1]]
                 + [pl.BlockSpec((9, LANES), lambda b, hd: (0, hd)),
                    pl.BlockSpec((9, LANES), lambda b, hd: (0, h + hd)),
                    pl.BlockSpec((1, LANES), lambda b, hd: (0, hd)),
                    pl.BlockSpec((1, LANES), lambda b, hd: (0, h + hd)),
                    pl.BlockSpec((LANES, 1), lambda b, hd: (hd, 0))],
        out_specs=pl.BlockSpec((1, seq, LANES), lambda b, hd: (b, 0, hd)),
        out_shape=jax.ShapeDtypeStruct((bsz, seq, h * ML_DH), BF16),
        scratch_shapes=[pltpu.VMEM((seq, LANES), F32), pltpu.VMEM((seq, LANES), F32),
                        pltpu.VMEM((ctx, LANES), F32),
                        pltpu.VMEM((LANES, seq), BF16), pltpu.VMEM((LANES, seq), F32),
                        pltpu.VMEM((LANES, ctx), F32),
                        pltpu.VMEM((2, 2 * LANES, LANES), F32), pltpu.VMEM((2, 1, LANES), F32),
                        pltpu.VMEM((LANES, seq), F32)],
        compiler_params=_cparams(("arbitrary", "arbitrary")),
        name="mlstm",
    )(zm_x, zm_x, zm_x, zm_x, gcol_x, grow_x, zm_c, zm_c, grow_c,
      conv_w, conv_w, conv_b, conv_b, norm_g.reshape(-1, 1))


_G0 = 0
_E0 = N_GROUPS
RANK_BITS = 16
RANK_SPAN = 1 << RANK_BITS
ROW_GROUP = 16
PIECE_COLS = 256
GATHER_RING = 3


SUBLANES = 8


def _store_token_tiles(ref2d, val):
    n, w = val.shape
    k = w // LANES
    for c in range(k):
        ref2d[pl.ds(c, n, stride=k), :] = val[:, c * LANES:(c + 1) * LANES]


def _load_token_tiles(ref2d, first, n, k, step):
    return jnp.concatenate([ref2d[pl.ds(first + c, n, stride=step), :] for c in range(k)], axis=1)


def _outproj_kernel(x_ref, ga_ref, ml_ref, mod_ref, wa_ref, wb_ref, g2_ref, wrh_ref, wrl_ref, br_ref,
                    x1_ref, h2_ref, ri_ref, rw_ref, cnt_ref, base_ref):
    tm = x_ref.shape[1]

    @pl.when((pl.program_id(0) == 0) & (pl.program_id(1) == 0))
    def _():
        base_ref[...] = jnp.zeros_like(base_ref)

    mix = (jnp.dot(ga_ref[0], wa_ref[...], preferred_element_type=F32)
           + jnp.dot(ml_ref[0], wb_ref[...], preferred_element_type=F32))
    x1 = x_ref[0] + mod_ref[0, 2:3, :] * mix
    x1_ref[0] = x1
    y = x1 * lax.rsqrt(jnp.mean(x1 * x1, axis=-1, keepdims=True) + EPS) * g2_ref[...]
    h2 = y * (1.0 + mod_ref[0, 4:5, :]) + mod_ref[0, 3:4, :]
    _store_token_tiles(h2_ref, h2)

    h_hi = h2.astype(BF16)
    h_lo = (h2 - h_hi.astype(F32)).astype(BF16)
    logits = (jnp.dot(h_hi, wrh_ref[...], preferred_element_type=F32)
              + jnp.dot(h_lo, wrh_ref[...], preferred_element_type=F32)
              + jnp.dot(h_hi, wrl_ref[...], preferred_element_type=F32)) + br_ref[...]

    lane = lax.broadcasted_iota(jnp.int32, (tm, LANES), 1).astype(F32)
    neg = -jnp.inf
    big = float(LANES)
    is_g = lane < float(_E0)
    lg = jnp.where(is_g, logits, neg)
    gmax = jnp.max(lg, axis=-1, keepdims=True)
    gidx = jnp.min(jnp.where(lg == gmax, lane, big), axis=-1, keepdims=True)
    gw = 1.0 / jnp.sum(jnp.where(is_g, jnp.exp(logits - gmax), 0.0), axis=-1, keepdims=True)
    lo = float(_E0) + float(EXP_PER_GROUP) * gidx
    le = jnp.where((lane >= lo) & (lane < lo + float(EXP_PER_GROUP)), logits, neg)
    v1 = jnp.max(le, axis=-1, keepdims=True)
    i1 = jnp.min(jnp.where(le == v1, lane, big), axis=-1, keepdims=True)
    le2 = jnp.where(lane == i1, neg, le)
    v2 = jnp.max(le2, axis=-1, keepdims=True)
    i2 = jnp.min(jnp.where(le2 == v2, lane, big), axis=-1, keepdims=True)
    t = jnp.exp(v2 - v1)
    w1 = gw / (1.0 + t)
    w2 = gw * t / (1.0 + t)
    e1 = i1 - float(_E0)
    e2 = i2 - float(_E0)

    oh1 = lane == e1
    oh2 = lane == e2
    oh = jnp.where(oh1 | oh2, 1.0, 0.0)
    r = lax.broadcasted_iota(jnp.int32, (tm, tm), 0)
    c = lax.broadcasted_iota(jnp.int32, (tm, tm), 1)
    strict = jnp.where(c < r, 1.0, 0.0).astype(BF16)
    before = jnp.dot(strict, oh.astype(BF16), preferred_element_type=F32) + base_ref[...]
    rank1 = jnp.sum(jnp.where(oh1, before, 0.0), axis=-1, keepdims=True)
    rank2 = jnp.sum(jnp.where(oh2, before, 0.0), axis=-1, keepdims=True)
    total = base_ref[...] + jnp.sum(oh, axis=0, keepdims=True)
    base_ref[...] = total
    cnt_ref[...] = total

    ids = jnp.where(lane == 0.0, e1 * float(RANK_SPAN) + rank1,
                    jnp.where(lane == 1.0, e2 * float(RANK_SPAN) + rank2, 0.0))
    ri_ref[...] = ids.astype(jnp.int32)
    rw_ref[...] = jnp.where(lane == 0.0, w1, jnp.where(lane == 1.0, w2, 0.0))


def _outproj(x, gla_o, ml_o, mods, wa, wb, g2, wrh, wrl, br, tm):
    bsz, seq, d = x.shape
    const = lambda shape: pl.BlockSpec(shape, lambda b, i: (0,) * len(shape))
    tile = lambda w: pl.BlockSpec((1, tm, w), lambda b, i: (b, i, 0))
    flat = lambda rows: pl.BlockSpec((rows, LANES), lambda b, i: (b * (seq // tm) + i, 0))
    return pl.pallas_call(
        _outproj_kernel,
        grid=(bsz, seq // tm),
        in_specs=[tile(d), tile(gla_o.shape[2]), tile(ml_o.shape[2]),
                  pl.BlockSpec((1, N_MOD, d), lambda b, i: (b, 0, 0)),
                  const(wa.shape), const(wb.shape), const((1, d)),
                  const(wrh.shape), const(wrl.shape), const((1, LANES))],
        out_specs=[tile(d),
                   pl.BlockSpec((tm * d // LANES, LANES), lambda b, i: (b * (seq // tm) + i, 0)),
                   flat(tm), flat(tm), const((1, LANES))],
        out_shape=[jax.ShapeDtypeStruct((bsz, seq, d), F32),
                   jax.ShapeDtypeStruct((bsz * seq * d // LANES, LANES), F32),
                   jax.ShapeDtypeStruct((bsz * seq, LANES), jnp.int32),
                   jax.ShapeDtypeStruct((bsz * seq, LANES), F32),
                   jax.ShapeDtypeStruct((1, LANES), F32)],
        scratch_shapes=[pltpu.VMEM((1, LANES), F32)],
        compiler_params=_cparams(("arbitrary", "arbitrary")),
        name="outproj",
    )(x, gla_o, ml_o, mods, wa, wb, g2, wrh, wrl, br)


def _experts_kernel(dest_ref, ps_ref, cnt_ref, be_ref, nv_ref, meta_ref, h_hbm, w1_ref, w2_ref, ytok_hbm,
                    src_ref, xbuf, ybuf, w1c_ref, w2c_ref, gsem, ssem):
    i = pl.program_id(0)
    n_steps = pl.num_programs(0)
    n_used = meta_ref[0]
    tr = SUBLANES
    n_tok = h_hbm.shape[0] // tr
    n_rows = src_ref.shape[0]
    blk = xbuf.shape[1] // tr
    n_x = xbuf.shape[0]
    slot = lax.rem(i, 2)
    xslot = lax.rem(i, n_x)

    def slab(j):
        return pl.ds(pl.multiple_of(j * tr, tr), tr)

    def group_rows(buf, s, g):
        span = ROW_GROUP * tr
        return buf.at[s, pl.ds(pl.multiple_of(g * span, span), span), :]

    def gather_copy(tok, s, g, u):
        return pltpu.make_async_copy(h_hbm.at[slab(tok), :], group_rows(xbuf, s, g).at[pl.ds(u * tr, tr), :],
                                     gsem.at[s])

    def scatter_copy(a, s, g, u):
        return pltpu.make_async_copy(group_rows(ybuf, s, g).at[pl.ds(u * tr, tr), :], ytok_hbm.at[slab(a), :],
                                     ssem.at[s])

    def groups(b):
        nv = jnp.where((b >= 0) & (b < n_steps), nv_ref[jnp.clip(b, 0, n_steps - 1)], 0)
        return lax.shift_right_logical(nv + (ROW_GROUP - 1), ROW_GROUP.bit_length() - 1)

    def rows_loop(b, body):
        def step(g, c):
            for u in range(ROW_GROUP):
                body(g, u)
            return c
        lax.fori_loop(0, groups(b), step, 0)

    def issue_gather(b, s):
        def one(g, u):
            tok = lax.shift_right_logical(src_ref[b * blk + g * ROW_GROUP + u], 1)
            gather_copy(jnp.minimum(tok, n_tok - 1), s, g, u).start(priority=u % 2)
        rows_loop(b, one)

    def wait_gather(b, s):
        rows_loop(b, lambda g, u: gather_copy(0, s, 0, 0).wait())

    def wait_scatter(b, s):
        rows_loop(b, lambda g, u: scatter_copy(0, s, 0, 0).wait())

    @pl.when(i == 0)
    def _():
        xbuf[...] = jnp.zeros_like(xbuf)
        ybuf[...] = jnp.zeros_like(ybuf)
        for s in range(2):
            tail = ytok_hbm.at[pl.ds((2 * n_tok + s * blk) * tr, blk * tr), :]
            cp = pltpu.make_async_copy(ybuf.at[s], tail, ssem.at[s])
            cp.start()
            cp.wait()

        def put(a, c):
            src_ref[dest_ref[a]] = a
            return c
        lax.fori_loop(0, 2 * n_tok, put, 0, unroll=16)

        def pad(j, c):
            src_ref[j] = 2 * n_tok + (j & (2 * blk - 1))
            return c

        def pad_expert(e, c):
            lax.fori_loop(ps_ref[e] + cnt_ref[e], ps_ref[e + 1], pad, 0)
            return c
        lax.fori_loop(0, cnt_ref.shape[0], pad_expert, 0)
        for b in range(n_x - 1):
            issue_gather(min(b, n_rows // blk - 1), b)

    @pl.when(i < n_used)
    def _():
        wait_gather(i, xslot)

        @pl.when((i == 0) | (be_ref[i] != be_ref[jnp.maximum(i - 1, 0)]))
        def _():
            w1c_ref[...] = w1_ref[0].astype(BF16)
            w2c_ref[...] = w2_ref[0].astype(BF16)

        @pl.when(i >= 2)
        def _():
            wait_scatter(i - 2, slot)

        prev_b, prev_slot = i - 1, 1 - slot
        next_b, next_slot = i + n_x - 1, lax.rem(i + n_x - 1, n_x)
        g_prev, g_next = groups(prev_b), groups(next_b)
        prev_on = [g < g_prev for g in range(blk // ROW_GROUP)]
        next_on = [g < g_next for g in range(blk // ROW_GROUP)]
        d, de2 = w1c_ref.shape
        rows_per_piece = blk // (de2 // PIECE_COLS + d // PIECE_COLS)
        yview = ybuf.at[slot]

        def move_rows(piece):
            for r in range(piece * rows_per_piece, (piece + 1) * rows_per_piece):
                g, u = divmod(r, ROW_GROUP)

                @pl.when(prev_on[g])
                def _():
                    scatter_copy(src_ref[prev_b * blk + r], prev_slot, g, u).start(priority=r % 2)

                @pl.when(next_on[g])
                def _():
                    tok = lax.shift_right_logical(src_ref[next_b * blk + r], 1)
                    gather_copy(jnp.minimum(tok, n_tok - 1), next_slot, g, u).start(priority=r % 2)

        row = lax.broadcasted_iota(jnp.int32, (blk, 1), 0)
        x = _load_token_tiles(xbuf.at[xslot], 0, blk, tr, tr)
        x = jnp.where(row < nv_ref[i], x, 0.0).astype(BF16)
        hs = []
        for c in range(de2 // PIECE_COLS):
            hs.append(jnp.dot(x, w1c_ref[:, c * PIECE_COLS:(c + 1) * PIECE_COLS], preferred_element_type=F32))
            move_rows(c)
        h = jnp.concatenate(hs, axis=1)
        a = (_silu(h[:, :D_EXPERT]) * h[:, D_EXPERT:]).astype(BF16)
        for c in range(d // PIECE_COLS):
            y = jnp.dot(a, w2c_ref[:, c * PIECE_COLS:(c + 1) * PIECE_COLS], preferred_element_type=F32)
            for t in range(PIECE_COLS // LANES):
                k = c * (PIECE_COLS // LANES) + t
                yview[pl.ds(k, blk, stride=tr), :] = y[:, t * LANES:(t + 1) * LANES]
            move_rows(de2 // PIECE_COLS + c)

    @pl.when(i == n_steps - 1)
    def _():
        last = n_used - 1
        rows_loop(last, lambda g, u: scatter_copy(src_ref[last * blk + g * ROW_GROUP + u], lax.rem(last, 2), g, u)
                  .start(priority=u % 2))
        for back in (2, 1):
            wait_scatter(n_used - back, lax.rem(n_used - back, 2))


def _experts(dest, pad_start, counts, block_e, block_nv, meta, h2, w_in, w_out, nb):
    d = w_in.shape[1]
    tr = d // LANES
    assert tr == SUBLANES, "a token row must fill exactly one (8, 128) tile"
    n_tok = h2.shape[0] // tr
    de2 = w_in.shape[2]
    n_rows = nb * MOE_BLK
    assert 2 * n_tok >= 2 * MOE_BLK
    assert MOE_BLK & (MOE_BLK - 1) == 0
    wmap = lambda i, pk, ps, cnt, be, nv, meta: (be[i], 0, 0)
    return pl.pallas_call(
        _experts_kernel,
        grid_spec=pltpu.PrefetchScalarGridSpec(
            num_scalar_prefetch=6, grid=(nb,),
            in_specs=[pl.BlockSpec(memory_space=pl.ANY),
                      pl.BlockSpec((1, d, de2), wmap),
                      pl.BlockSpec((1, de2 // 2, d), wmap)],
            out_specs=pl.BlockSpec(memory_space=pl.ANY),
            scratch_shapes=[pltpu.SMEM((n_rows,), jnp.int32),
                            pltpu.VMEM((GATHER_RING, MOE_BLK * tr, LANES), F32),
                            pltpu.VMEM((2, MOE_BLK * tr, LANES), F32),
                            pltpu.VMEM((d, de2), BF16), pltpu.VMEM((de2 // 2, d), BF16),
                            pltpu.SemaphoreType.DMA((GATHER_RING,)), pltpu.SemaphoreType.DMA((2,))]),
        out_shape=jax.ShapeDtypeStruct(((2 * n_tok + 2 * MOE_BLK) * tr, LANES), F32),
        compiler_params=_cparams(("arbitrary",)),
        name="experts",
    )(dest, pad_start, counts, block_e, block_nv, meta, h2, w_in, w_out)


def _combine_kernel(x1_ref, y_ref, rw_ref, mod_ref, fg_ref, o_ref):
    tc, d = x1_ref.shape
    tr = d // LANES
    y1 = _load_token_tiles(y_ref, 0, tc, tr, 2 * tr)
    y2 = _load_token_tiles(y_ref, tr, tc, tr, 2 * tr)
    moe = rw_ref[:, 0:1] * y1 + rw_ref[:, 1:2] * y2
    x2 = x1_ref[...] + mod_ref[0, 5:6, :] * moe
    o_ref[...] = x2 * lax.rsqrt(jnp.mean(x2 * x2, axis=-1, keepdims=True) + EPS) * fg_ref[...]


def _combine(x1, ytok, rw, mods, fg, tokens_per_batch, tc):
    n_tok, d = x1.shape
    tiles_per_batch = tokens_per_batch // tc
    return pl.pallas_call(
        _combine_kernel,
        grid=(n_tok // tc,),
        in_specs=[pl.BlockSpec((tc, d), lambda i: (i, 0)),
                  pl.BlockSpec((2 * tc * d // LANES, LANES), lambda i: (i, 0)),
                  pl.BlockSpec((tc, LANES), lambda i: (i, 0)),
                  pl.BlockSpec((1, N_MOD, d), lambda i: (i // tiles_per_batch, 0, 0)),
                  pl.BlockSpec((1, d), lambda i: (0, 0))],
        out_specs=pl.BlockSpec((tc, d), lambda i: (i, 0)),
        out_shape=jax.ShapeDtypeStruct((n_tok, d), F32),
        compiler_params=_cparams(("arbitrary",)),
        name="combine",
    )(x1, ytok, rw, mods, fg)


def _prep_inproj_weights(w_in, gla_up_w, gla_up_b, ml_i_b, ml_f_b):
    d = w_in.shape[0]
    o_gq, o_gk, o_gv, o_gg = 0, GLA_QK_W, 2 * GLA_QK_W, 2 * GLA_QK_W + GLA_V_W
    o_lr = o_gg + GLA_V_W
    o_mqk = o_lr + 2 * GLA_LR
    o_mi = o_mqk + 4 * ML_W
    o_mf = o_mi + 2 * ML_HEADS

    wg = w_in[:, o_gq:o_lr]
    wm = w_in[:, o_mqk:o_mi]
    ws = jnp.concatenate([w_in[:, o_lr:o_mqk], w_in[:, o_mi:o_mf + 2 * ML_HEADS],
                          jnp.zeros((d, LANES - 2 * GLA_LR - 4 * ML_HEADS), w_in.dtype)], axis=1)
    bias = jnp.zeros((LANES,), F32)
    bias = bias.at[_MI0:_MI0 + 2 * ML_HEADS].set(ml_i_b.reshape(-1))
    bias = bias.at[_MF0:_MF0 + 2 * ML_HEADS].set(ml_f_b.reshape(-1))
    up = gla_up_w.reshape(2, GLA_LR, GLA_HEADS, GLA_DK).transpose(2, 0, 1, 3)
    ub = gla_up_b.reshape(2, GLA_HEADS, GLA_DK).transpose(1, 0, 2)
    wup = jnp.zeros((GLA_HEADS, 2, LANES, LANES), F32)
    bup = jnp.zeros((GLA_HEADS, 2, 1, LANES), F32)
    for hd in range(GLA_HEADS):
        lo = (hd % (LANES // GLA_DK)) * GLA_DK
        for dr in range(2):
            wup = wup.at[hd, dr, dr * GLA_LR:(dr + 1) * GLA_LR, lo:lo + GLA_DK].set(up[hd, dr])
        bup = bup.at[hd, :, 0, lo:lo + GLA_DK].set(ub[hd])
    return (wg.astype(BF16), wm.astype(BF16), ws.astype(BF16), ws.T.astype(BF16),
            bias.reshape(1, LANES), bias.reshape(LANES, 1), wup.astype(BF16), bup)


def _layer(x, ctx, mods, norm1_g, w_in, gla_up_w, gla_up_b, gla_norm_g, ml_conv_w, ml_conv_b,
           ml_i_b, ml_f_b, ml_norm_g, w_out, norm2_g, rg_w, rg_b, re_w, re_b, e_w_in, e_w_out, final_g):
    bsz, seq, d = x.shape
    n_tok = bsz * seq
    wg, wm, ws, wst, bcol, brow, wup, bup = _prep_inproj_weights(w_in, gla_up_w, gla_up_b, ml_i_b, ml_f_b)
    g1 = norm1_g.reshape(1, d)
    zg_x, zm_x, zs_x, gcol_x, grow_x = _inproj(x, mods, lambda b: b, g1, wg, wm, ws, wst, bcol, brow, 256)
    zg_c, zm_c, zs_c, gcol_c, grow_c = _inproj(ctx, mods, lambda b: bsz, g1, wg, wm, ws, wst, bcol, brow,
                                               min(256, ctx.shape[1]))
    gla_o = _gla(zg_x, zs_x, zg_c, zs_c, wup, bup, gla_norm_g.reshape(1, -1))
    ml_o = _mlstm(zm_x, gcol_x, grow_x, zm_c, grow_c,
                  ml_conv_w.reshape(9, -1), ml_conv_b.reshape(1, -1), ml_norm_g.reshape(1, -1))

    wr = jnp.zeros((d, LANES), F32).at[:, _G0:_E0].set(rg_w).at[:, _E0:_E0 + N_EXPERTS].set(re_w)
    br = jnp.zeros((1, LANES), F32).at[0, _G0:_E0].set(rg_b).at[0, _E0:_E0 + N_EXPERTS].set(re_b)
    wrh = wr.astype(BF16)
    wrl = (wr - wrh.astype(F32)).astype(BF16)
    x1, h2, ri, rw, cnt = _outproj(x, gla_o, ml_o, mods, w_out[:GLA_V_W].astype(BF16),
                                   w_out[GLA_V_W:].astype(BF16), norm2_g.reshape(1, d), wrh, wrl, br, 256)

    counts = cnt[0, :N_EXPERTS].astype(jnp.int32)
    nblk = (counts + MOE_BLK - 1) // MOE_BLK
    blk_end = jnp.cumsum(nblk)
    blk_start = blk_end - nblk
    n_used = blk_end[-1]
    nb_max = (2 * n_tok) // MOE_BLK + N_EXPERTS
    blk = jnp.arange(nb_max, dtype=jnp.int32)
    blk_c = jnp.minimum(blk, n_used - 1)
    onehot = (blk_c[:, None] >= blk_start[None, :]) & (blk_c[:, None] < blk_end[None, :])
    pick = lambda v: jnp.sum(jnp.where(onehot, v[None, :], 0), axis=1)
    block_e = pick(jnp.arange(N_EXPERTS, dtype=jnp.int32)).astype(jnp.int32)
    block_nv = jnp.clip(pick(counts) - (blk_c - pick(blk_start)) * MOE_BLK, 0, MOE_BLK)
    block_nv = jnp.where(blk < n_used, block_nv, 0).astype(jnp.int32)
    pad_start = (jnp.concatenate([blk_start, blk_end[-1:]]) * MOE_BLK).astype(jnp.int32)
    packed = ri[:, 0:2].reshape(-1)
    e_of = lax.shift_right_logical(packed, RANK_BITS)
    start_of = jnp.sum(jnp.where(e_of[:, None] == jnp.arange(N_EXPERTS, dtype=jnp.int32)[None, :],
                                 pad_start[None, :N_EXPERTS], 0), axis=1)
    dest = (start_of + (packed & (RANK_SPAN - 1))).astype(jnp.int32)
    meta = jnp.stack([n_used, n_used]).astype(jnp.int32)

    ytok = _experts(dest, pad_start, counts, block_e, block_nv, meta, h2, e_w_in, e_w_out, nb_max)
    out = _combine(x1.reshape(n_tok, d), ytok, rw, mods, final_g.reshape(1, d), seq, 256)
    return out.reshape(bsz, seq, d)


def kernel(x, c, ctx, c_ctx, ada_w, ada_b, norm1_g, w_in, gla_up_w, gla_up_b, gla_norm_g, ml_conv_w, ml_conv_b,
           ml_i_b, ml_f_b, ml_norm_g, w_out, norm2_g, router_group_w, router_group_b, router_expert_w,
           router_expert_b, expert_w_in, expert_w_out, final_norm_g):
    assert ada_w.shape[0] == 1, "single-layer stack"
    bsz, d = c.shape
    cc = jnp.concatenate([c, c_ctx[None, :], jnp.zeros((8 - bsz - 1, d), F32)], axis=0)
    mods = _modulation(cc, ada_w[0], ada_b[0]).reshape(8, N_MOD, d)
    return _layer(x, ctx, mods, norm1_g[0], w_in[0], gla_up_w[0], gla_up_b[0], gla_norm_g[0],
                  ml_conv_w[0], ml_conv_b[0], ml_i_b[0], ml_f_b[0], ml_norm_g[0], w_out[0], norm2_g[0],
                  router_group_w[0], router_group_b[0], router_expert_w[0], router_expert_b[0],
                  expert_w_in[0], expert_w_out[0], final_norm_g)
```

```python
import functools

import jax
import jax.numpy as jnp
from jax import lax
from jax.experimental import pallas as pl
from jax.experimental.pallas import tpu as pltpu

F32 = jnp.float32
BF16 = jnp.bfloat16

D_MODEL = 1024
GRID_W = 64
N_MOD = 6
EPS = 1e-6

GLA_HEADS = 4
GLA_DK = 64
GLA_DV = 128
GLA_LR = 16
GLA_TAU = 16.0
GLA_C = 128
GLA_UNROLL = 8
SCAN_UNROLL = 4

ML_HEADS = 4
ML_DH = 128
ML_C = 128

N_GROUPS = 4
EXP_PER_GROUP = 8
N_EXPERTS = N_GROUPS * EXP_PER_GROUP
D_EXPERT = 512
MOE_BLK = 256

GLA_QK_W = GLA_HEADS * GLA_DK
GLA_V_W = GLA_HEADS * GLA_DV
ML_W = ML_HEADS * ML_DH
LANES = 128
VMEM_LIMIT = 56 * 1024 * 1024

_LR0 = 0
_MI0 = 2 * GLA_LR
_MF0 = _MI0 + 2 * ML_HEADS


def _cparams(sem):
    return pltpu.CompilerParams(dimension_semantics=sem, vmem_limit_bytes=VMEM_LIMIT)


def _sigmoid(x):
    return 1.0 / (1.0 + jnp.exp(-x))


def _silu(x):
    return x * _sigmoid(x)


def _log_sigmoid(x):
    return jnp.minimum(x, 0.0) - jnp.log1p(jnp.exp(-jnp.abs(x)))


def _split_dot(a_bf16_exact, x, dims=None):
    x_hi = x.astype(BF16)
    x_lo = (x - x_hi.astype(F32)).astype(BF16)
    if dims is None:
        f = lambda u: jnp.dot(a_bf16_exact, u, preferred_element_type=F32)
    else:
        f = lambda u: lax.dot_general(u, a_bf16_exact, dims, preferred_element_type=F32)
    return f(x_hi) + f(x_lo)


def _mod_kernel(c_ref, w_ref, b_ref, o_ref):
    c = c_ref[...]
    s = _silu(c).astype(BF16)
    o_ref[...] = jnp.dot(s, w_ref[...].astype(BF16), preferred_element_type=F32) + b_ref[...]


def _modulation(cc, ada_w, ada_b):
    rows, d = cc.shape
    n = ada_w.shape[1]
    tn = 1536
    return pl.pallas_call(
        _mod_kernel,
        grid=(n // tn,),
        in_specs=[pl.BlockSpec((rows, d), lambda j: (0, 0)),
                  pl.BlockSpec((d, tn), lambda j: (0, j)),
                  pl.BlockSpec((1, tn), lambda j: (0, j))],
        out_specs=pl.BlockSpec((rows, tn), lambda j: (0, j)),
        out_shape=jax.ShapeDtypeStruct((rows, n), F32),
        compiler_params=_cparams(("arbitrary",)),
        name="mod",
    )(cc, ada_w, ada_b.reshape(1, n))


def _inproj_kernel(x_ref, mod_ref, g_ref, wg_ref, wm_ref, ws_ref, wst_ref, bcol_ref, brow_ref,
                   zg_ref, zm_ref, zs_ref, gcol_ref, grow_ref):
    tm = x_ref.shape[1]
    x = x_ref[0]
    y = x * lax.rsqrt(jnp.mean(x * x, axis=-1, keepdims=True) + EPS) * g_ref[...]
    h = (y * (1.0 + mod_ref[0, 1:2, :]) + mod_ref[0, 0:1, :]).astype(BF16)
    zg_ref[0] = jnp.dot(h, wg_ref[...], preferred_element_type=F32)
    zm_ref[0] = jnp.dot(h, wm_ref[...], preferred_element_type=F32)
    zs = jnp.dot(h, ws_ref[...], preferred_element_type=F32) + bcol_ref[...]
    zst = lax.dot_general(wst_ref[...], h, (((1,), (1,)), ((), ())),
                          preferred_element_type=F32) + brow_ref[...]
    zs_ref[0] = zs

    r = lax.broadcasted_iota(jnp.int32, (tm, tm), 0)
    c = lax.broadcasted_iota(jnp.int32, (tm, tm), 1)
    shift = ML_C.bit_length() - 1
    same = jnp.right_shift(r, shift) == jnp.right_shift(c, shift)
    lower = jnp.where(same & (c <= r), 1.0, 0.0).astype(BF16)
    upper = jnp.where(same & (c >= r), 1.0, 0.0).astype(BF16)
    chunks = range(0, tm, ML_C)
    lsf = _log_sigmoid(zs)
    a_pre = _split_dot(lower, lsf)
    tot = jnp.concatenate([jnp.broadcast_to(a_pre[o + ML_C - 1:o + ML_C, :], (ML_C, LANES)) for o in chunks], axis=0)
    a_suf = tot - a_pre + lsf
    lsft = _log_sigmoid(zst)
    a_pre_t = _split_dot(upper, lsft, (((1,), (0,)), ((), ())))
    tot_t = jnp.concatenate([jnp.broadcast_to(a_pre_t[:, o + ML_C - 1:o + ML_C], (LANES, ML_C)) for o in chunks], axis=1)
    a_suf_t = tot_t - a_pre_t + lsft

    lane = lax.broadcasted_iota(jnp.int32, (tm, LANES), 1)
    for hd in range(ML_HEADS):
        cols = (a_pre[:, _MF0 + hd:_MF0 + hd + 1],
                a_suf[:, _MF0 + ML_HEADS + hd:_MF0 + ML_HEADS + hd + 1],
                zs[:, _MI0 + hd:_MI0 + hd + 1],
                zs[:, _MI0 + ML_HEADS + hd:_MI0 + ML_HEADS + hd + 1])
        slab = jnp.zeros((tm, LANES), F32)
        for j, col in enumerate(cols):
            slab = jnp.where(lane == j, col, slab)
        gcol_ref[0, :, hd * LANES:(hd + 1) * LANES] = slab
        rows = (a_pre_t[_MF0 + hd:_MF0 + hd + 1, :],
                a_suf_t[_MF0 + ML_HEADS + hd:_MF0 + ML_HEADS + hd + 1, :],
                zst[_MI0 + hd:_MI0 + hd + 1, :],
                zst[_MI0 + ML_HEADS + hd:_MI0 + ML_HEADS + hd + 1, :])
        for j, row in enumerate(rows):
            grow_ref[0, hd, j:j + 1, :] = row
        grow_ref[0, hd, 4:8, :] = jnp.zeros((4, tm), F32)


def _inproj(x, mods, mod_row_of_batch, norm_g, wg, wm, ws, wst, bcol, brow, tm):
    bsz, l, d = x.shape
    assert l % tm == 0 and tm % ML_C == 0
    const = lambda shape: pl.BlockSpec(shape, lambda b, i: (0,) * len(shape))
    return pl.pallas_call(
        _inproj_kernel,
        grid=(bsz, l // tm),
        in_specs=[pl.BlockSpec((1, tm, d), lambda b, i: (b, i, 0)),
                  pl.BlockSpec((1, N_MOD, d), lambda b, i: (mod_row_of_batch(b), 0, 0)),
                  const((1, d)), const(wg.shape), const(wm.shape), const(ws.shape), const(wst.shape),
                  const((1, LANES)), const((LANES, 1))],
        out_specs=[pl.BlockSpec((1, tm, wg.shape[1]), lambda b, i: (b, i, 0)),
                   pl.BlockSpec((1, tm, wm.shape[1]), lambda b, i: (b, i, 0)),
                   pl.BlockSpec((1, tm, LANES), lambda b, i: (b, i, 0)),
                   pl.BlockSpec((1, tm, ML_HEADS * LANES), lambda b, i: (b, i, 0)),
                   pl.BlockSpec((1, ML_HEADS, 8, tm), lambda b, i: (b, 0, 0, i))],
        out_shape=[jax.ShapeDtypeStruct((bsz, l, wg.shape[1]), F32),
                   jax.ShapeDtypeStruct((bsz, l, wm.shape[1]), F32),
                   jax.ShapeDtypeStruct((bsz, l, LANES), F32),
                   jax.ShapeDtypeStruct((bsz, l, ML_HEADS * LANES), F32),
                   jax.ShapeDtypeStruct((bsz, ML_HEADS, 8, l), F32)],
        compiler_params=_cparams(("arbitrary", "arbitrary")),
        name="inproj",
    )(x, mods, norm_g, wg, wm, ws, wst, bcol, brow)


def _round_robin(chains):
    results = [None] * len(chains)
    live = list(enumerate(chains))
    while live:
        still = []
        for idx, chain in live:
            try:
                next(chain)
                still.append((idx, chain))
            except StopIteration as done:
                results[idx] = done.value
        live = still
    return results


def _visibility(c):
    r = lax.broadcasted_iota(jnp.int32, (c, c), 0)
    cc = lax.broadcasted_iota(jnp.int32, (c, c), 1)
    masks = [cc <= r, cc >= r]
    return masks, [jnp.where(m, 1.0, 0.0).astype(BF16) for m in masks]


def _gla_chunk(q, k, v, zs, wup, bup, state, direction, want_out, causal, tri):
    c = k.shape[0]
    logits = jnp.dot(zs.astype(BF16), wup, preferred_element_type=F32) + bup
    yield
    g = _log_sigmoid(logits) * (1.0 / GLA_TAU)
    b = _split_dot(tri, g)
    yield
    b_end = b[c - 1:c, :] if direction == 0 else b[0:1, :]
    kd = (k * jnp.exp(b_end - b)).astype(BF16)
    upd = lax.dot_general(v.astype(BF16), kd, (((0,), (0,)), ((), ())), preferred_element_type=F32)
    s = state[direction]
    state[direction] = jnp.exp(b_end) * s + upd
    if not want_out:
        return None
    b_mid = b[c // 2:c // 2 + 1, :]
    q_in = (q * jnp.exp(b - b_mid)).astype(BF16)
    k_in = (k * jnp.exp(b_mid - b)).astype(BF16)
    att = lax.dot_general(q_in, k_in, (((1,), (1,)), ((), ())), preferred_element_type=F32)
    inter = lax.dot_general((q * jnp.exp(b)).astype(BF16), s.astype(BF16),
                            (((1,), (1,)), ((), ())), preferred_element_type=F32)
    yield
    att = jnp.where(causal, att, 0.0)
    return jnp.dot(att.astype(BF16), v.astype(BF16), preferred_element_type=F32) + inter


def _scan_order(j, n, unroll):
    return [(d, j * unroll + u if d == 0 else n - 1 - (j * unroll + u)) for u in range(unroll) for d in range(2)]


def _gla_kernel(q_ref, k_ref, v_ref, gg_ref, zs_ref, kc_ref, vc_ref, zsc_ref,
                wup_ref, bup_ref, ng_ref, o_ref, s_ref, acc_ref):
    seq = q_ref.shape[1]
    ctx = kc_ref.shape[1]
    n = seq // GLA_C
    nc = ctx // GLA_C
    s_ref[...] = jnp.zeros_like(s_ref)

    def rows(i):
        return pl.ds(pl.multiple_of(i * GLA_C, GLA_C), GLA_C)

    masks, tris = _visibility(GLA_C)
    heads_per_slab = LANES // GLA_DK
    lo = lax.rem(pl.program_id(1), heads_per_slab) * GLA_DK
    lane = lax.broadcasted_iota(jnp.int32, (GLA_C, LANES), 1)
    mine = (lane >= lo) & (lane < lo + GLA_DK)

    def own(t):
        return jnp.where(mine, t, 0.0)

    def ctx_step(j, carry):
        order = _scan_order(j, nc, 1)
        ins = [(own(kc_ref[0, rows(i), :]), vc_ref[0, rows(i), :], zsc_ref[0, rows(i), :]) for _, i in order]
        s = [s_ref[0], s_ref[1]]
        _round_robin([_gla_chunk(None, k, v, zs, wup_ref[0, d], bup_ref[0, d], s, d, False, masks[d], tris[d])
                      for (d, _), (k, v, zs) in zip(order, ins)])
        s_ref[0] = s[0]
        s_ref[1] = s[1]
        return carry

    lax.fori_loop(0, nc, ctx_step, 0)

    def lat_step(j, carry, second):
        order = _scan_order(j, n, GLA_UNROLL)
        ins = [(own(q_ref[0, rows(i), :]), own(k_ref[0, rows(i), :]), v_ref[0, rows(i), :], zs_ref[0, rows(i), :])
               for _, i in order]
        prev = [(acc_ref[rows(i), :], gg_ref[0, rows(i), :]) for _, i in order] if second else None
        s = [s_ref[0], s_ref[1]]
        outs = _round_robin([_gla_chunk(q * (GLA_DK ** -0.5), k, v, zs, wup_ref[0, d], bup_ref[0, d], s, d, True,
                                        masks[d], tris[d])
                             for (d, _), (q, k, v, zs) in zip(order, ins)])
        s_ref[0] = s[0]
        s_ref[1] = s[1]
        for idx, (_, i) in enumerate(order):
            if second:
                total = prev[idx][0] + outs[idx]
                y = total * lax.rsqrt(jnp.mean(total * total, axis=-1, keepdims=True) + EPS) * ng_ref[...]
                o_ref[0, rows(i), :] = (y * _silu(prev[idx][1])).astype(o_ref.dtype)
            else:
                acc_ref[rows(i), :] = outs[idx]
        return carry

    half = n // (2 * GLA_UNROLL)
    lax.fori_loop(0, half, functools.partial(lat_step, second=False), 0)
    lax.fori_loop(half, 2 * half, functools.partial(lat_step, second=True), 0)


def _gla(zg_x, zs_x, zg_c, zs_c, wup, bup, norm_g):
    bsz, seq, _ = zg_x.shape
    ctx = zg_c.shape[1]
    assert seq % (2 * GLA_UNROLL * GLA_C) == 0 and ctx % GLA_C == 0
    h = GLA_HEADS

    hps = LANES // GLA_DK
    qk_slabs = h // hps

    def qk(l, off):
        return pl.BlockSpec((1, l, LANES), lambda b, hd: (b, 0, off + hd // hps))

    def col(l, off):
        return pl.BlockSpec((1, l, LANES), lambda b, hd: (b, 0, off + hd))

    return pl.pallas_call(
        _gla_kernel,
        grid=(bsz, h),
        in_specs=[qk(seq, 0), qk(seq, qk_slabs), col(seq, 2 * qk_slabs), col(seq, 2 * qk_slabs + h),
                  pl.BlockSpec((1, seq, LANES), lambda b, hd: (b, 0, 0)),
                  qk(ctx, qk_slabs), col(ctx, 2 * qk_slabs),
                  pl.BlockSpec((1, ctx, LANES), lambda b, hd: (b, 0, 0)),
                  pl.BlockSpec((1, 2, LANES, LANES), lambda b, hd: (hd, 0, 0, 0)),
                  pl.BlockSpec((1, 2, 1, LANES), lambda b, hd: (hd, 0, 0, 0)),
                  pl.BlockSpec((1, LANES), lambda b, hd: (0, hd))],
        out_specs=pl.BlockSpec((1, seq, LANES), lambda b, hd: (b, 0, hd)),
        out_shape=jax.ShapeDtypeStruct((bsz, seq, h * GLA_DV), BF16),
        scratch_shapes=[pltpu.VMEM((2, LANES, LANES), F32), pltpu.VMEM((seq, LANES), F32)],
        compiler_params=_cparams(("arbitrary", "arbitrary")),
        name="gla",
    )(zg_x, zg_x, zg_x, zg_x, zs_x, zg_c, zg_c, zs_c, wup, bup, norm_g)


def _grid_conv_silu(src_ref, pad_ref, dst_ref, w_ref, b_ref, grid_w, scale):
    l = src_ref.shape[1]
    n_rows = l // grid_w
    margin = grid_w + SUBLANES
    assert pad_ref.shape[0] >= l + 2 * margin and margin % SUBLANES == 0
    pad_ref[0:margin, :] = jnp.zeros((margin, LANES), F32)
    pad_ref[margin + l:2 * margin + l, :] = jnp.zeros((margin, LANES), F32)

    def copy_row(r, carry):
        at = pl.ds(pl.multiple_of(r * grid_w, grid_w), grid_w)
        pad_ref[pl.ds(pl.multiple_of(margin + r * grid_w, SUBLANES), grid_w), :] = src_ref[0, at, :]
        return carry

    lax.fori_loop(0, n_rows, copy_row, 0, unroll=2)

    col = lax.broadcasted_iota(jnp.int32, (grid_w, LANES), 0)
    inside = {dx: (col + dx >= 0) & (col + dx < grid_w) for dx in (-1, 1)}
    rows_dy = (0,) if n_rows == 1 else (-1, 0, 1)

    def body(r, carry):
        base = pl.multiple_of(margin + r * grid_w, SUBLANES)
        acc = jnp.zeros((grid_w, LANES), F32) + b_ref[...]
        for dy in rows_dy:
            for dx in (-1, 0, 1):
                tap = (dy + 1) * 3 + (dx + 1)
                blk = pad_ref[pl.ds(base + dy * grid_w + dx, grid_w), :]
                if dx != 0:
                    blk = jnp.where(inside[dx], blk, 0.0)
                acc = acc + blk * w_ref[tap:tap + 1, :]
        dst_ref[pl.ds(pl.multiple_of(r * grid_w, grid_w), grid_w), :] = _silu(acc) * scale
        return carry

    lax.fori_loop(0, n_rows, body, 0)


def _ml_chunk(qb, k, vt, gcol, grow, state, mstate, direction, want_out, visible):
    c = k.shape[0]
    a_row = grow[direction:direction + 1, :]
    i_row = grow[2 + direction:3 + direction, :]
    a_end = a_row[:, c - 1:c] if direction == 0 else a_row[:, 0:1]
    g = a_end - a_row + i_row
    g_max = jnp.max(g, axis=-1, keepdims=True)
    head = 2 * SUBLANES
    pad_rows = jnp.zeros((LANES - head, c), BF16)
    first = lax.broadcasted_iota(jnp.int32, (head, c), 0) == 0
    kb = k.astype(BF16)
    if want_out:
        c_col = gcol[:, direction:direction + 1] - gcol[:, 2 + direction:3 + direction]
        dmat = jnp.where(visible, a_row - c_col, -jnp.inf)
        d_max = jnp.max(dmat, axis=0, keepdims=True)
        kq = jnp.dot(kb, qb, preferred_element_type=F32)
    yield
    s, m = state[direction], mstate[direction]
    m_new = jnp.maximum(a_end + m, g_max)
    decay = jnp.exp(a_end + m - m_new)
    w = jnp.exp(g - m_new)
    vw = jnp.concatenate([(vt * w).astype(BF16), jnp.where(first, w, 0.0).astype(BF16), pad_rows], axis=0)
    state[direction] = decay * s + jnp.dot(vw, kb, preferred_element_type=F32)
    mstate[direction] = m_new
    if not want_out:
        return None
    inter = a_row + m
    m_t = jnp.maximum(inter, d_max)
    w_inter = jnp.exp(inter - m_t)
    p = (kq * jnp.exp(dmat - m_t)).astype(BF16)
    vt_aug = jnp.concatenate([vt.astype(BF16), jnp.where(first, 1.0, 0.0).astype(BF16), pad_rows], axis=0)
    pv = jnp.dot(vt_aug, p, preferred_element_type=F32)
    sq = jnp.dot(s.astype(BF16), qb, preferred_element_type=F32)
    yield
    both = pv + w_inter * sq
    den = both[ML_DH:ML_DH + 1, :]
    return both[:ML_DH, :] / jnp.maximum(jnp.abs(den), jnp.exp(-m_t))


def _mlstm_kernel(q_ref, k_ref, v_ref, mo_ref, gcol_ref, grow_ref,
                  kc_ref, vc_ref, growc_ref,
                  wq_ref, wk_ref, bq_ref, bk_ref, ng_ref, o_ref,
                  cq_ref, ck_ref, ckc_ref, pad_ref, qt_ref, vt_ref, vct_ref, s_ref, m_ref, acc_ref):
    seq = q_ref.shape[1]
    ctx = kc_ref.shape[1]
    n = seq // ML_C
    nc = ctx // ML_C
    _grid_conv_silu(q_ref, pad_ref, cq_ref, wq_ref, bq_ref, GRID_W, 1.0)
    _grid_conv_silu(k_ref, pad_ref, ck_ref, wk_ref, bk_ref, GRID_W, ML_DH ** -0.5)
    _grid_conv_silu(kc_ref, pad_ref, ckc_ref, wk_ref, bk_ref, ctx, ML_DH ** -0.5)
    s_ref[...] = jnp.zeros_like(s_ref)
    m_ref[...] = jnp.zeros_like(m_ref)

    def rows(i):
        return pl.ds(pl.multiple_of(i * ML_C, ML_C), ML_C)

    def transpose_chunks(i, carry):
        qt_ref[:, rows(i)] = cq_ref[rows(i), :].T.astype(qt_ref.dtype)
        vt_ref[:, rows(i)] = v_ref[0, rows(i), :].T
        return carry

    lax.fori_loop(0, n, transpose_chunks, 0, unroll=2)
    for i in range(nc):
        vct_ref[:, i * ML_C:(i + 1) * ML_C] = vc_ref[0, i * ML_C:(i + 1) * ML_C, :].T

    def load_state():
        return [s_ref[0], s_ref[1]], [m_ref[0, :, 0:1], m_ref[1, :, 0:1]]

    def store_state(s, m):
        for d in range(2):
            s_ref[d] = s[d]
            m_ref[d] = jnp.broadcast_to(m[d], m_ref.shape[1:])

    masks, _ = _visibility(ML_C)
    visible = [masks[1], masks[0]]

    def ctx_step(j, carry):
        order = _scan_order(j, nc, 1)
        ins = [(ckc_ref[rows(i), :], vct_ref[:, rows(i)], growc_ref[0, 0, :, rows(i)]) for _, i in order]
        s, m = load_state()
        _round_robin([_ml_chunk(None, k, vt, None, grow, s, m, d, False, None)
                      for (d, _), (k, vt, grow) in zip(order, ins)])
        store_state(s, m)
        return carry

    lax.fori_loop(0, nc, ctx_step, 0)

    def lat_step(j, carry, second):
        order = _scan_order(j, n, SCAN_UNROLL)
        ins = [(qt_ref[:, rows(i)], ck_ref[rows(i), :], vt_ref[:, rows(i)], gcol_ref[0, rows(i), :],
                grow_ref[0, 0, :, rows(i)]) for _, i in order]
        prev = [(acc_ref[:, rows(i)], mo_ref[0, rows(i), :]) for _, i in order] if second else None
        s, m = load_state()
        outs = _round_robin([_ml_chunk(qb, k, vt, gcol, grow, s, m, d, True, visible[d])
                             for (d, _), (qb, k, vt, gcol, grow) in zip(order, ins)])
        store_state(s, m)
        for idx, (_, i) in enumerate(order):
            if second:
                total = prev[idx][0] + outs[idx]
                y = total * lax.rsqrt(jnp.mean(total * total, axis=0, keepdims=True) + EPS) * ng_ref[...]
                o_ref[0, rows(i), :] = (_sigmoid(prev[idx][1]) * y.T).astype(o_ref.dtype)
            else:
                acc_ref[:, rows(i)] = outs[idx]
        return carry

    half = n // (2 * SCAN_UNROLL)
    lax.fori_loop(0, half, functools.partial(lat_step, second=False), 0)
    lax.fori_loop(half, 2 * half, functools.partial(lat_step, second=True), 0)


def _mlstm(zm_x, gcol_x, grow_x, zm_c, grow_c, conv_w, conv_b, norm_g):
    bsz, seq, _ = zm_x.shape
    ctx = zm_c.shape[1]
    assert seq % (2 * SCAN_UNROLL * ML_C) == 0 and ctx % ML_C == 0 and seq % GRID_W == 0
    h = ML_HEADS

    def col(l, off):
        return pl.BlockSpec((1, l, LANES), lambda b, hd: (b, 0, off + hd))

    def gates(l):
        return [pl.BlockSpec((1, l, LANES), lambda b, hd: (b, 0, hd)),
                pl.BlockSpec((1, 1, 8, l), lambda b, hd: (b, hd, 0, 0))]

    return pl.pallas_call(
        _mlstm_kernel,
        grid=(bsz, h),
        in_specs=[col(seq, 0), col(seq, h), col(seq, 2 * h), col(seq, 3 * h)] + gates(seq)
                 + [col(ctx, h), col(ctx, 2 * h), gates(ctx)[1]]
                 + [pl.BlockSpec((9, LANES), lambda b, hd: (0, hd)),
                    pl.BlockSpec((9, LANES), lambda b, hd: (0, h + hd)),
                    pl.BlockSpec((1, LANES), lambda b, hd: (0, hd)),
                    pl.BlockSpec((1, LANES), lambda b, hd: (0, h + hd)),
                    pl.BlockSpec((LANES, 1), lambda b, hd: (hd, 0))],
        out_specs=pl.BlockSpec((1, seq, LANES), lambda b, hd: (b, 0, hd)),
        out_shape=jax.ShapeDtypeStruct((bsz, seq, h * ML_DH), BF16),
        scratch_shapes=[pltpu.VMEM((seq, LANES), F32), pltpu.VMEM((seq, LANES), F32),
                        pltpu.VMEM((ctx, LANES), F32),
                        pltpu.VMEM((max(seq + 2 * (GRID_W + SUBLANES), 3 * ctx + 2 * SUBLANES), LANES), F32),
                        pltpu.VMEM((LANES, seq), BF16), pltpu.VMEM((LANES, seq), F32),
                        pltpu.VMEM((LANES, ctx), F32),
                        pltpu.VMEM((2, 2 * LANES, LANES), F32), pltpu.VMEM((2, 1, LANES), F32),
                        pltpu.VMEM((LANES, seq), F32)],
        compiler_params=_cparams(("arbitrary", "arbitrary")),
        name="mlstm",
    )(zm_x, zm_x, zm_x, zm_x, gcol_x, grow_x, zm_c, zm_c, grow_c,
      conv_w, conv_w, conv_b, conv_b, norm_g.reshape(-1, 1))


_G0 = 0
_E0 = N_GROUPS
RANK_BITS = 16
RANK_SPAN = 1 << RANK_BITS
ROW_GROUP = 8
GATHER_RING = 3


SUBLANES = 8


def _store_token_tiles(ref2d, val):
    n, w = val.shape
    k = w // LANES
    for c in range(k):
        ref2d[pl.ds(c, n, stride=k), :] = val[:, c * LANES:(c + 1) * LANES]


def _load_token_tiles(ref2d, first, n, k, step):
    return jnp.concatenate([ref2d[pl.ds(first + c, n, stride=step), :] for c in range(k)], axis=1)


def _outproj_kernel(x_ref, ga_ref, ml_ref, mod_ref, wa_ref, wb_ref, g2_ref, wrh_ref, wrl_ref, br_ref,
                    x1_ref, h2_ref, ri_ref, rw_ref, cnt_ref, base_ref):
    tm = x_ref.shape[1]

    @pl.when((pl.program_id(0) == 0) & (pl.program_id(1) == 0))
    def _():
        base_ref[...] = jnp.zeros_like(base_ref)

    mix = (jnp.dot(ga_ref[0], wa_ref[...], preferred_element_type=F32)
           + jnp.dot(ml_ref[0], wb_ref[...], preferred_element_type=F32))
    x1 = x_ref[0] + mod_ref[0, 2:3, :] * mix
    x1_ref[0] = x1
    y = x1 * lax.rsqrt(jnp.mean(x1 * x1, axis=-1, keepdims=True) + EPS) * g2_ref[...]
    h2 = y * (1.0 + mod_ref[0, 4:5, :]) + mod_ref[0, 3:4, :]
    _store_token_tiles(h2_ref, h2)

    h_hi = h2.astype(BF16)
    h_lo = (h2 - h_hi.astype(F32)).astype(BF16)
    logits = (jnp.dot(h_hi, wrh_ref[...], preferred_element_type=F32)
              + jnp.dot(h_lo, wrh_ref[...], preferred_element_type=F32)
              + jnp.dot(h_hi, wrl_ref[...], preferred_element_type=F32)) + br_ref[...]

    lane = lax.broadcasted_iota(jnp.int32, (tm, LANES), 1).astype(F32)
    neg = -jnp.inf
    big = float(LANES)
    is_g = lane < float(_E0)
    lg = jnp.where(is_g, logits, neg)
    gmax = jnp.max(lg, axis=-1, keepdims=True)
    gidx = jnp.min(jnp.where(lg == gmax, lane, big), axis=-1, keepdims=True)
    gw = 1.0 / jnp.sum(jnp.where(is_g, jnp.exp(logits - gmax), 0.0), axis=-1, keepdims=True)
    lo = float(_E0) + float(EXP_PER_GROUP) * gidx
    le = jnp.where((lane >= lo) & (lane < lo + float(EXP_PER_GROUP)), logits, neg)
    v1 = jnp.max(le, axis=-1, keepdims=True)
    i1 = jnp.min(jnp.where(le == v1, lane, big), axis=-1, keepdims=True)
    le2 = jnp.where(lane == i1, neg, le)
    v2 = jnp.max(le2, axis=-1, keepdims=True)
    i2 = jnp.min(jnp.where(le2 == v2, lane, big), axis=-1, keepdims=True)
    t = jnp.exp(v2 - v1)
    w1 = gw / (1.0 + t)
    w2 = gw * t / (1.0 + t)
    e1 = i1 - float(_E0)
    e2 = i2 - float(_E0)

    oh1 = lane == e1
    oh2 = lane == e2
    oh = jnp.where(oh1 | oh2, 1.0, 0.0)
    r = lax.broadcasted_iota(jnp.int32, (tm, tm), 0)
    c = lax.broadcasted_iota(jnp.int32, (tm, tm), 1)
    strict = jnp.where(c < r, 1.0, 0.0).astype(BF16)
    before = jnp.dot(strict, oh.astype(BF16), preferred_element_type=F32) + base_ref[...]
    rank1 = jnp.sum(jnp.where(oh1, before, 0.0), axis=-1, keepdims=True)
    rank2 = jnp.sum(jnp.where(oh2, before, 0.0), axis=-1, keepdims=True)
    total = base_ref[...] + jnp.sum(oh, axis=0, keepdims=True)
    base_ref[...] = total
    cnt_ref[...] = total

    ids = jnp.where(lane == 0.0, e1 * float(RANK_SPAN) + rank1,
                    jnp.where(lane == 1.0, e2 * float(RANK_SPAN) + rank2, 0.0))
    ri_ref[...] = ids.astype(jnp.int32)
    rw_ref[...] = jnp.where(lane == 0.0, w1, jnp.where(lane == 1.0, w2, 0.0))


def _outproj(x, gla_o, ml_o, mods, wa, wb, g2, wrh, wrl, br, tm):
    bsz, seq, d = x.shape
    const = lambda shape: pl.BlockSpec(shape, lambda b, i: (0,) * len(shape))
    tile = lambda w: pl.BlockSpec((1, tm, w), lambda b, i: (b, i, 0))
    flat = lambda rows: pl.BlockSpec((rows, LANES), lambda b, i: (b * (seq // tm) + i, 0))
    return pl.pallas_call(
        _outproj_kernel,
        grid=(bsz, seq // tm),
        in_specs=[tile(d), tile(gla_o.shape[2]), tile(ml_o.shape[2]),
                  pl.BlockSpec((1, N_MOD, d), lambda b, i: (b, 0, 0)),
                  const(wa.shape), const(wb.shape), const((1, d)),
                  const(wrh.shape), const(wrl.shape), const((1, LANES))],
        out_specs=[tile(d),
                   pl.BlockSpec((tm * d // LANES, LANES), lambda b, i: (b * (seq // tm) + i, 0)),
                   flat(tm), flat(tm), const((1, LANES))],
        out_shape=[jax.ShapeDtypeStruct((bsz, seq, d), F32),
                   jax.ShapeDtypeStruct((bsz * seq * d // LANES, LANES), F32),
                   jax.ShapeDtypeStruct((bsz * seq, LANES), jnp.int32),
                   jax.ShapeDtypeStruct((bsz * seq, LANES), F32),
                   jax.ShapeDtypeStruct((1, LANES), F32)],
        scratch_shapes=[pltpu.VMEM((1, LANES), F32)],
        compiler_params=_cparams(("arbitrary", "arbitrary")),
        name="outproj",
    )(x, gla_o, ml_o, mods, wa, wb, g2, wrh, wrl, br)


def _experts_kernel(dest_ref, ps_ref, cnt_ref, be_ref, nv_ref, meta_ref, h_hbm, w1_ref, w2_ref, ytok_hbm,
                    src_ref, xbuf, ybuf, w1c_ref, w2c_ref, gsem, ssem):
    i = pl.program_id(0)
    n_steps = pl.num_programs(0)
    n_used = meta_ref[0]
    tr = SUBLANES
    n_tok = h_hbm.shape[0] // tr
    n_rows = src_ref.shape[0]
    blk = xbuf.shape[1] // tr
    n_x = xbuf.shape[0]
    slot = lax.rem(i, 2)
    xslot = lax.rem(i, n_x)

    def slab(j):
        return pl.ds(pl.multiple_of(j * tr, tr), tr)

    def group_rows(buf, s, g):
        span = ROW_GROUP * tr
        return buf.at[s, pl.ds(pl.multiple_of(g * span, span), span), :]

    def gather_copy(tok, s, g, u):
        return pltpu.make_async_copy(h_hbm.at[slab(tok), :], group_rows(xbuf, s, g).at[pl.ds(u * tr, tr), :],
                                     gsem.at[s])

    def scatter_copy(a, s, g, u):
        return pltpu.make_async_copy(group_rows(ybuf, s, g).at[pl.ds(u * tr, tr), :], ytok_hbm.at[slab(a), :],
                                     ssem.at[s])

    def groups(b):
        nv = jnp.where(b < n_steps, nv_ref[jnp.minimum(b, n_steps - 1)], 0)
        return lax.shift_right_logical(nv + (ROW_GROUP - 1), ROW_GROUP.bit_length() - 1)

    def rows_loop(b, body):
        def step(g, c):
            for u in range(ROW_GROUP):
                body(g, u)
            return c
        lax.fori_loop(0, groups(b), step, 0)

    def issue_gather(b, s):
        def one(g, u):
            tok = lax.shift_right_logical(src_ref[b * blk + g * ROW_GROUP + u], 1)
            gather_copy(jnp.minimum(tok, n_tok - 1), s, g, u).start(priority=u % 2)
        rows_loop(b, one)

    def wait_gather(b, s):
        rows_loop(b, lambda g, u: gather_copy(0, s, 0, 0).wait())

    def wait_scatter(b, s):
        rows_loop(b, lambda g, u: scatter_copy(0, s, 0, 0).wait())

    @pl.when(i == 0)
    def _():
        xbuf[...] = jnp.zeros_like(xbuf)
        ybuf[...] = jnp.zeros_like(ybuf)
        for s in range(2):
            tail = ytok_hbm.at[pl.ds((2 * n_tok + s * blk) * tr, blk * tr), :]
            cp = pltpu.make_async_copy(ybuf.at[s], tail, ssem.at[s])
            cp.start()
            cp.wait()

        def put(a, c):
            src_ref[dest_ref[a]] = a
            return c
        lax.fori_loop(0, 2 * n_tok, put, 0, unroll=16)

        def pad(j, c):
            src_ref[j] = 2 * n_tok + (j & (2 * blk - 1))
            return c

        def pad_expert(e, c):
            lax.fori_loop(ps_ref[e] + cnt_ref[e], ps_ref[e + 1], pad, 0)
            return c
        lax.fori_loop(0, cnt_ref.shape[0], pad_expert, 0)
        for b in range(n_x - 1):
            issue_gather(min(b, n_rows // blk - 1), b)

    @pl.when(i < n_used)
    def _():
        wait_gather(i, xslot)

        @pl.when((i == 0) | (be_ref[i] != be_ref[jnp.maximum(i - 1, 0)]))
        def _():
            w1c_ref[...] = w1_ref[0].astype(BF16)
            w2c_ref[...] = w2_ref[0].astype(BF16)

        @pl.when(i >= 2)
        def _():
            wait_scatter(i - 2, slot)

        row = lax.broadcasted_iota(jnp.int32, (blk, 1), 0)
        x = _load_token_tiles(xbuf.at[xslot], 0, blk, tr, tr)
        x = jnp.where(row < nv_ref[i], x, 0.0).astype(BF16)
        h = jnp.dot(x, w1c_ref[...], preferred_element_type=F32)
        a = (_silu(h[:, :D_EXPERT]) * h[:, D_EXPERT:]).astype(BF16)
        _store_token_tiles(ybuf.at[slot], jnp.dot(a, w2c_ref[...], preferred_element_type=F32))
        rows_loop(i, lambda g, u: scatter_copy(src_ref[i * blk + g * ROW_GROUP + u], slot, g, u)
                  .start(priority=u % 2))
        issue_gather(i + n_x - 1, lax.rem(i + n_x - 1, n_x))

    @pl.when(i == n_steps - 1)
    def _():
        for back in (2, 1):
            wait_scatter(n_used - back, lax.rem(n_used - back, 2))


def _experts(dest, pad_start, counts, block_e, block_nv, meta, h2, w_in, w_out, nb):
    d = w_in.shape[1]
    tr = d // LANES
    assert tr == SUBLANES, "a token row must fill exactly one (8, 128) tile"
    n_tok = h2.shape[0] // tr
    de2 = w_in.shape[2]
    n_rows = nb * MOE_BLK
    assert 2 * n_tok >= 2 * MOE_BLK
    assert MOE_BLK & (MOE_BLK - 1) == 0
    wmap = lambda i, pk, ps, cnt, be, nv, meta: (be[i], 0, 0)
    return pl.pallas_call(
        _experts_kernel,
        grid_spec=pltpu.PrefetchScalarGridSpec(
            num_scalar_prefetch=6, grid=(nb,),
            in_specs=[pl.BlockSpec(memory_space=pl.ANY),
                      pl.BlockSpec((1, d, de2), wmap),
                      pl.BlockSpec((1, de2 // 2, d), wmap)],
            out_specs=pl.BlockSpec(memory_space=pl.ANY),
            scratch_shapes=[pltpu.SMEM((n_rows,), jnp.int32),
                            pltpu.VMEM((GATHER_RING, MOE_BLK * tr, LANES), F32),
                            pltpu.VMEM((2, MOE_BLK * tr, LANES), F32),
                            pltpu.VMEM((d, de2), BF16), pltpu.VMEM((de2 // 2, d), BF16),
                            pltpu.SemaphoreType.DMA((GATHER_RING,)), pltpu.SemaphoreType.DMA((2,))]),
        out_shape=jax.ShapeDtypeStruct(((2 * n_tok + 2 * MOE_BLK) * tr, LANES), F32),
        compiler_params=_cparams(("arbitrary",)),
        name="experts",
    )(dest, pad_start, counts, block_e, block_nv, meta, h2, w_in, w_out)


def _combine_kernel(x1_ref, y_ref, rw_ref, mod_ref, fg_ref, o_ref):
    tc, d = x1_ref.shape
    tr = d // LANES
    y1 = _load_token_tiles(y_ref, 0, tc, tr, 2 * tr)
    y2 = _load_token_tiles(y_ref, tr, tc, tr, 2 * tr)
    moe = rw_ref[:, 0:1] * y1 + rw_ref[:, 1:2] * y2
    x2 = x1_ref[...] + mod_ref[0, 5:6, :] * moe
    o_ref[...] = x2 * lax.rsqrt(jnp.mean(x2 * x2, axis=-1, keepdims=True) + EPS) * fg_ref[...]


def _combine(x1, ytok, rw, mods, fg, tokens_per_batch, tc):
    n_tok, d = x1.shape
    tiles_per_batch = tokens_per_batch // tc
    return pl.pallas_call(
        _combine_kernel,
        grid=(n_tok // tc,),
        in_specs=[pl.BlockSpec((tc, d), lambda i: (i, 0)),
                  pl.BlockSpec((2 * tc * d // LANES, LANES), lambda i: (i, 0)),
                  pl.BlockSpec((tc, LANES), lambda i: (i, 0)),
                  pl.BlockSpec((1, N_MOD, d), lambda i: (i // tiles_per_batch, 0, 0)),
                  pl.BlockSpec((1, d), lambda i: (0, 0))],
        out_specs=pl.BlockSpec((tc, d), lambda i: (i, 0)),
        out_shape=jax.ShapeDtypeStruct((n_tok, d), F32),
        compiler_params=_cparams(("arbitrary",)),
        name="combine",
    )(x1, ytok, rw, mods, fg)


def _prep_inproj_weights(w_in, gla_up_w, gla_up_b, ml_i_b, ml_f_b):
    d = w_in.shape[0]
    o_gq, o_gk, o_gv, o_gg = 0, GLA_QK_W, 2 * GLA_QK_W, 2 * GLA_QK_W + GLA_V_W
    o_lr = o_gg + GLA_V_W
    o_mqk = o_lr + 2 * GLA_LR
    o_mi = o_mqk + 4 * ML_W
    o_mf = o_mi + 2 * ML_HEADS

    wg = w_in[:, o_gq:o_lr]
    wm = w_in[:, o_mqk:o_mi]
    ws = jnp.concatenate([w_in[:, o_lr:o_mqk], w_in[:, o_mi:o_mf + 2 * ML_HEADS],
                          jnp.zeros((d, LANES - 2 * GLA_LR - 4 * ML_HEADS), w_in.dtype)], axis=1)
    bias = jnp.zeros((LANES,), F32)
    bias = bias.at[_MI0:_MI0 + 2 * ML_HEADS].set(ml_i_b.reshape(-1))
    bias = bias.at[_MF0:_MF0 + 2 * ML_HEADS].set(ml_f_b.reshape(-1))
    up = gla_up_w.reshape(2, GLA_LR, GLA_HEADS, GLA_DK).transpose(2, 0, 1, 3)
    ub = gla_up_b.reshape(2, GLA_HEADS, GLA_DK).transpose(1, 0, 2)
    wup = jnp.zeros((GLA_HEADS, 2, LANES, LANES), F32)
    bup = jnp.zeros((GLA_HEADS, 2, 1, LANES), F32)
    for hd in range(GLA_HEADS):
        lo = (hd % (LANES // GLA_DK)) * GLA_DK
        for dr in range(2):
            wup = wup.at[hd, dr, dr * GLA_LR:(dr + 1) * GLA_LR, lo:lo + GLA_DK].set(up[hd, dr])
        bup = bup.at[hd, :, 0, lo:lo + GLA_DK].set(ub[hd])
    return (wg.astype(BF16), wm.astype(BF16), ws.astype(BF16), ws.T.astype(BF16),
            bias.reshape(1, LANES), bias.reshape(LANES, 1), wup.astype(BF16), bup)


def _layer(x, ctx, mods, norm1_g, w_in, gla_up_w, gla_up_b, gla_norm_g, ml_conv_w, ml_conv_b,
           ml_i_b, ml_f_b, ml_norm_g, w_out, norm2_g, rg_w, rg_b, re_w, re_b, e_w_in, e_w_out, final_g):
    bsz, seq, d = x.shape
    n_tok = bsz * seq
    wg, wm, ws, wst, bcol, brow, wup, bup = _prep_inproj_weights(w_in, gla_up_w, gla_up_b, ml_i_b, ml_f_b)
    g1 = norm1_g.reshape(1, d)
    zg_x, zm_x, zs_x, gcol_x, grow_x = _inproj(x, mods, lambda b: b, g1, wg, wm, ws, wst, bcol, brow, 256)
    zg_c, zm_c, zs_c, gcol_c, grow_c = _inproj(ctx, mods, lambda b: bsz, g1, wg, wm, ws, wst, bcol, brow,
                                               min(256, ctx.shape[1]))
    gla_o = _gla(zg_x, zs_x, zg_c, zs_c, wup, bup, gla_norm_g.reshape(1, -1))
    ml_o = _mlstm(zm_x, gcol_x, grow_x, zm_c, grow_c,
                  ml_conv_w.reshape(9, -1), ml_conv_b.reshape(1, -1), ml_norm_g.reshape(1, -1))

    wr = jnp.zeros((d, LANES), F32).at[:, _G0:_E0].set(rg_w).at[:, _E0:_E0 + N_EXPERTS].set(re_w)
    br = jnp.zeros((1, LANES), F32).at[0, _G0:_E0].set(rg_b).at[0, _E0:_E0 + N_EXPERTS].set(re_b)
    wrh = wr.astype(BF16)
    wrl = (wr - wrh.astype(F32)).astype(BF16)
    x1, h2, ri, rw, cnt = _outproj(x, gla_o, ml_o, mods, w_out[:GLA_V_W].astype(BF16),
                                   w_out[GLA_V_W:].astype(BF16), norm2_g.reshape(1, d), wrh, wrl, br, 256)

    counts = cnt[0, :N_EXPERTS].astype(jnp.int32)
    nblk = (counts + MOE_BLK - 1) // MOE_BLK
    blk_end = jnp.cumsum(nblk)
    blk_start = blk_end - nblk
    n_used = blk_end[-1]
    nb_max = (2 * n_tok) // MOE_BLK + N_EXPERTS
    blk = jnp.arange(nb_max, dtype=jnp.int32)
    blk_c = jnp.minimum(blk, n_used - 1)
    onehot = (blk_c[:, None] >= blk_start[None, :]) & (blk_c[:, None] < blk_end[None, :])
    pick = lambda v: jnp.sum(jnp.where(onehot, v[None, :], 0), axis=1)
    block_e = pick(jnp.arange(N_EXPERTS, dtype=jnp.int32)).astype(jnp.int32)
    block_nv = jnp.clip(pick(counts) - (blk_c - pick(blk_start)) * MOE_BLK, 0, MOE_BLK)
    block_nv = jnp.where(blk < n_used, block_nv, 0).astype(jnp.int32)
    pad_start = (jnp.concatenate([blk_start, blk_end[-1:]]) * MOE_BLK).astype(jnp.int32)
    packed = ri[:, 0:2].reshape(-1)
    e_of = lax.shift_right_logical(packed, RANK_BITS)
    start_of = jnp.sum(jnp.where(e_of[:, None] == jnp.arange(N_EXPERTS, dtype=jnp.int32)[None, :],
                                 pad_start[None, :N_EXPERTS], 0), axis=1)
    dest = (start_of + (packed & (RANK_SPAN - 1))).astype(jnp.int32)
    meta = jnp.stack([n_used, n_used]).astype(jnp.int32)

    ytok = _experts(dest, pad_start, counts, block_e, block_nv, meta, h2, e_w_in, e_w_out, nb_max)
    out = _combine(x1.reshape(n_tok, d), ytok, rw, mods, final_g.reshape(1, d), seq, 256)
    return out.reshape(bsz, seq, d)


def kernel(x, c, ctx, c_ctx, ada_w, ada_b, norm1_g, w_in, gla_up_w, gla_up_b, gla_norm_g, ml_conv_w, ml_conv_b,
           ml_i_b, ml_f_b, ml_norm_g, w_out, norm2_g, router_group_w, router_group_b, router_expert_w,
           router_expert_b, expert_w_in, expert_w_out, final_norm_g):
    assert ada_w.shape[0] == 1, "single-layer stack"
    bsz, d = c.shape
    cc = jnp.concatenate([c, c_ctx[None, :], jnp.zeros((8 - bsz - 1, d), F32)], axis=0)
    mods = _modulation(cc, ada_w[0], ada_b[0]).reshape(8, N_MOD, d)
    return _layer(x, ctx, mods, norm1_g[0], w_in[0], gla_up_w[0], gla_up_b[0], gla_norm_g[0],
                  ml_conv_w[0], ml_conv_b[0], ml_i_b[0], ml_f_b[0], ml_norm_g[0], w_out[0], norm2_g[0],
                  router_group_w[0], router_group_b[0], router_expert_w[0], router_expert_b[0],
                  expert_w_in[0], expert_w_out[0], final_norm_g)
```

```python
import functools

import jax
import jax.numpy as jnp
from jax import lax
from jax.experimental import pallas as pl
from jax.experimental.pallas import tpu as pltpu

F32 = jnp.float32
BF16 = jnp.bfloat16

D_MODEL = 1024
GRID_W = 64
N_MOD = 6
EPS = 1e-6

GLA_HEADS = 4
GLA_DK = 64
GLA_DV = 128
GLA_LR = 16
GLA_TAU = 16.0
GLA_C = 128
GLA_UNROLL = 8
SCAN_UNROLL = 4

ML_HEADS = 4
ML_DH = 128
ML_C = 128

N_GROUPS = 4
EXP_PER_GROUP = 8
N_EXPERTS = N_GROUPS * EXP_PER_GROUP
D_EXPERT = 512
MOE_BLK = 256

GLA_QK_W = GLA_HEADS * GLA_DK
GLA_V_W = GLA_HEADS * GLA_DV
ML_W = ML_HEADS * ML_DH
LANES = 128
VMEM_LIMIT = 56 * 1024 * 1024

_LR0 = 0
_MI0 = 2 * GLA_LR
_MF0 = _MI0 + 2 * ML_HEADS


def _cparams(sem):
    return pltpu.CompilerParams(dimension_semantics=sem, vmem_limit_bytes=VMEM_LIMIT)


def _sigmoid(x):
    return 1.0 / (1.0 + jnp.exp(-x))


def _silu(x):
    return x * _sigmoid(x)


def _log_sigmoid(x):
    return jnp.minimum(x, 0.0) - jnp.log1p(jnp.exp(-jnp.abs(x)))


def _split_dot(a_bf16_exact, x, dims=None):
    x_hi = x.astype(BF16)
    x_lo = (x - x_hi.astype(F32)).astype(BF16)
    if dims is None:
        f = lambda u: jnp.dot(a_bf16_exact, u, preferred_element_type=F32)
    else:
        f = lambda u: lax.dot_general(u, a_bf16_exact, dims, preferred_element_type=F32)
    return f(x_hi) + f(x_lo)


def _mod_kernel(c_ref, w_ref, b_ref, o_ref):
    c = c_ref[...]
    s = _silu(c).astype(BF16)
    o_ref[...] = jnp.dot(s, w_ref[...].astype(BF16), preferred_element_type=F32) + b_ref[...]


def _modulation(cc, ada_w, ada_b):
    rows, d = cc.shape
    n = ada_w.shape[1]
    tn = 1536
    return pl.pallas_call(
        _mod_kernel,
        grid=(n // tn,),
        in_specs=[pl.BlockSpec((rows, d), lambda j: (0, 0)),
                  pl.BlockSpec((d, tn), lambda j: (0, j)),
                  pl.BlockSpec((1, tn), lambda j: (0, j))],
        out_specs=pl.BlockSpec((rows, tn), lambda j: (0, j)),
        out_shape=jax.ShapeDtypeStruct((rows, n), F32),
        compiler_params=_cparams(("arbitrary",)),
        name="mod",
    )(cc, ada_w, ada_b.reshape(1, n))


def _inproj_kernel(x_ref, mod_ref, g_ref, wg_ref, wm_ref, ws_ref, wst_ref, bcol_ref, brow_ref,
                   zg_ref, zm_ref, zs_ref, gcol_ref, grow_ref):
    tm = x_ref.shape[1]
    x = x_ref[0]
    y = x * lax.rsqrt(jnp.mean(x * x, axis=-1, keepdims=True) + EPS) * g_ref[...]
    h = (y * (1.0 + mod_ref[0, 1:2, :]) + mod_ref[0, 0:1, :]).astype(BF16)
    zg_ref[0] = jnp.dot(h, wg_ref[...], preferred_element_type=F32)
    zm_ref[0] = jnp.dot(h, wm_ref[...], preferred_element_type=F32)
    zs = jnp.dot(h, ws_ref[...], preferred_element_type=F32) + bcol_ref[...]
    zst = lax.dot_general(wst_ref[...], h, (((1,), (1,)), ((), ())),
                          preferred_element_type=F32) + brow_ref[...]
    zs_ref[0] = zs

    r = lax.broadcasted_iota(jnp.int32, (tm, tm), 0)
    c = lax.broadcasted_iota(jnp.int32, (tm, tm), 1)
    shift = ML_C.bit_length() - 1
    same = jnp.right_shift(r, shift) == jnp.right_shift(c, shift)
    lower = jnp.where(same & (c <= r), 1.0, 0.0).astype(BF16)
    upper = jnp.where(same & (c >= r), 1.0, 0.0).astype(BF16)
    chunks = range(0, tm, ML_C)
    lsf = _log_sigmoid(zs)
    a_pre = _split_dot(lower, lsf)
    tot = jnp.concatenate([jnp.broadcast_to(a_pre[o + ML_C - 1:o + ML_C, :], (ML_C, LANES)) for o in chunks], axis=0)
    a_suf = tot - a_pre + lsf
    lsft = _log_sigmoid(zst)
    a_pre_t = _split_dot(upper, lsft, (((1,), (0,)), ((), ())))
    tot_t = jnp.concatenate([jnp.broadcast_to(a_pre_t[:, o + ML_C - 1:o + ML_C], (LANES, ML_C)) for o in chunks], axis=1)
    a_suf_t = tot_t - a_pre_t + lsft

    lane = lax.broadcasted_iota(jnp.int32, (tm, LANES), 1)
    for hd in range(ML_HEADS):
        cols = (a_pre[:, _MF0 + hd:_MF0 + hd + 1],
                a_suf[:, _MF0 + ML_HEADS + hd:_MF0 + ML_HEADS + hd + 1],
                zs[:, _MI0 + hd:_MI0 + hd + 1],
                zs[:, _MI0 + ML_HEADS + hd:_MI0 + ML_HEADS + hd + 1])
        slab = jnp.zeros((tm, LANES), F32)
        for j, col in enumerate(cols):
            slab = jnp.where(lane == j, col, slab)
        gcol_ref[0, :, hd * LANES:(hd + 1) * LANES] = slab
        rows = (a_pre_t[_MF0 + hd:_MF0 + hd + 1, :],
                a_suf_t[_MF0 + ML_HEADS + hd:_MF0 + ML_HEADS + hd + 1, :],
                zst[_MI0 + hd:_MI0 + hd + 1, :],
                zst[_MI0 + ML_HEADS + hd:_MI0 + ML_HEADS + hd + 1, :])
        for j, row in enumerate(rows):
            grow_ref[0, hd, j:j + 1, :] = row
        grow_ref[0, hd, 4:8, :] = jnp.zeros((4, tm), F32)


def _inproj(x, mods, mod_row_of_batch, norm_g, wg, wm, ws, wst, bcol, brow, tm):
    bsz, l, d = x.shape
    assert l % tm == 0 and tm % ML_C == 0
    const = lambda shape: pl.BlockSpec(shape, lambda b, i: (0,) * len(shape))
    return pl.pallas_call(
        _inproj_kernel,
        grid=(bsz, l // tm),
        in_specs=[pl.BlockSpec((1, tm, d), lambda b, i: (b, i, 0)),
                  pl.BlockSpec((1, N_MOD, d), lambda b, i: (mod_row_of_batch(b), 0, 0)),
                  const((1, d)), const(wg.shape), const(wm.shape), const(ws.shape), const(wst.shape),
                  const((1, LANES)), const((LANES, 1))],
        out_specs=[pl.BlockSpec((1, tm, wg.shape[1]), lambda b, i: (b, i, 0)),
                   pl.BlockSpec((1, tm, wm.shape[1]), lambda b, i: (b, i, 0)),
                   pl.BlockSpec((1, tm, LANES), lambda b, i: (b, i, 0)),
                   pl.BlockSpec((1, tm, ML_HEADS * LANES), lambda b, i: (b, i, 0)),
                   pl.BlockSpec((1, ML_HEADS, 8, tm), lambda b, i: (b, 0, 0, i))],
        out_shape=[jax.ShapeDtypeStruct((bsz, l, wg.shape[1]), F32),
                   jax.ShapeDtypeStruct((bsz, l, wm.shape[1]), F32),
                   jax.ShapeDtypeStruct((bsz, l, LANES), F32),
                   jax.ShapeDtypeStruct((bsz, l, ML_HEADS * LANES), F32),
                   jax.ShapeDtypeStruct((bsz, ML_HEADS, 8, l), F32)],
        compiler_params=_cparams(("arbitrary", "arbitrary")),
        name="inproj",
    )(x, mods, norm_g, wg, wm, ws, wst, bcol, brow)


def _round_robin(chains):
    results = [None] * len(chains)
    live = list(enumerate(chains))
    while live:
        still = []
        for idx, chain in live:
            try:
                next(chain)
                still.append((idx, chain))
            except StopIteration as done:
                results[idx] = done.value
        live = still
    return results


def _visibility(c):
    r = lax.broadcasted_iota(jnp.int32, (c, c), 0)
    cc = lax.broadcasted_iota(jnp.int32, (c, c), 1)
    masks = [cc <= r, cc >= r]
    return masks, [jnp.where(m, 1.0, 0.0).astype(BF16) for m in masks]


def _gla_chunk(q, k, v, zs, wup, bup, state, direction, want_out, causal, tri):
    c = k.shape[0]
    logits = jnp.dot(zs.astype(BF16), wup, preferred_element_type=F32) + bup
    yield
    g = _log_sigmoid(logits) * (1.0 / GLA_TAU)
    b = _split_dot(tri, g)
    yield
    b_end = b[c - 1:c, :] if direction == 0 else b[0:1, :]
    kd = (k * jnp.exp(b_end - b)).astype(BF16)
    upd = lax.dot_general(v.astype(BF16), kd, (((0,), (0,)), ((), ())), preferred_element_type=F32)
    s = state[direction]
    state[direction] = jnp.exp(b_end) * s + upd
    if not want_out:
        return None
    b_mid = b[c // 2:c // 2 + 1, :]
    q_in = (q * jnp.exp(b - b_mid)).astype(BF16)
    k_in = (k * jnp.exp(b_mid - b)).astype(BF16)
    att = lax.dot_general(q_in, k_in, (((1,), (1,)), ((), ())), preferred_element_type=F32)
    inter = lax.dot_general((q * jnp.exp(b)).astype(BF16), s.astype(BF16),
                            (((1,), (1,)), ((), ())), preferred_element_type=F32)
    yield
    att = jnp.where(causal, att, 0.0)
    return jnp.dot(att.astype(BF16), v.astype(BF16), preferred_element_type=F32) + inter


def _scan_order(j, n, unroll):
    return [(d, j * unroll + u if d == 0 else n - 1 - (j * unroll + u)) for u in range(unroll) for d in range(2)]


def _gla_kernel(q_ref, k_ref, v_ref, gg_ref, zs_ref, kc_ref, vc_ref, zsc_ref,
                wup_ref, bup_ref, ng_ref, o_ref, s_ref, acc_ref):
    seq = q_ref.shape[1]
    ctx = kc_ref.shape[1]
    n = seq // GLA_C
    nc = ctx // GLA_C
    s_ref[...] = jnp.zeros_like(s_ref)

    def rows(i):
        return pl.ds(pl.multiple_of(i * GLA_C, GLA_C), GLA_C)

    masks, tris = _visibility(GLA_C)
    heads_per_slab = LANES // GLA_DK
    lo = lax.rem(pl.program_id(1), heads_per_slab) * GLA_DK
    lane = lax.broadcasted_iota(jnp.int32, (GLA_C, LANES), 1)
    mine = (lane >= lo) & (lane < lo + GLA_DK)

    def own(t):
        return jnp.where(mine, t, 0.0)

    def ctx_step(j, carry):
        order = _scan_order(j, nc, 1)
        ins = [(own(kc_ref[0, rows(i), :]), vc_ref[0, rows(i), :], zsc_ref[0, rows(i), :]) for _, i in order]
        s = [s_ref[0], s_ref[1]]
        _round_robin([_gla_chunk(None, k, v, zs, wup_ref[0, d], bup_ref[0, d], s, d, False, masks[d], tris[d])
                      for (d, _), (k, v, zs) in zip(order, ins)])
        s_ref[0] = s[0]
        s_ref[1] = s[1]
        return carry

    lax.fori_loop(0, nc, ctx_step, 0)

    def lat_step(j, carry, second):
        order = _scan_order(j, n, GLA_UNROLL)
        ins = [(own(q_ref[0, rows(i), :]), own(k_ref[0, rows(i), :]), v_ref[0, rows(i), :], zs_ref[0, rows(i), :])
               for _, i in order]
        prev = [(acc_ref[rows(i), :], gg_ref[0, rows(i), :]) for _, i in order] if second else None
        s = [s_ref[0], s_ref[1]]
        outs = _round_robin([_gla_chunk(q * (GLA_DK ** -0.5), k, v, zs, wup_ref[0, d], bup_ref[0, d], s, d, True,
                                        masks[d], tris[d])
                             for (d, _), (q, k, v, zs) in zip(order, ins)])
        s_ref[0] = s[0]
        s_ref[1] = s[1]
        for idx, (_, i) in enumerate(order):
            if second:
                total = prev[idx][0] + outs[idx]
                y = total * lax.rsqrt(jnp.mean(total * total, axis=-1, keepdims=True) + EPS) * ng_ref[...]
                o_ref[0, rows(i), :] = (y * _silu(prev[idx][1])).astype(o_ref.dtype)
            else:
                acc_ref[rows(i), :] = outs[idx]
        return carry

    half = n // (2 * GLA_UNROLL)
    lax.fori_loop(0, half, functools.partial(lat_step, second=False), 0)
    lax.fori_loop(half, 2 * half, functools.partial(lat_step, second=True), 0)


def _gla(zg_x, zs_x, zg_c, zs_c, wup, bup, norm_g):
    bsz, seq, _ = zg_x.shape
    ctx = zg_c.shape[1]
    assert seq % (2 * GLA_UNROLL * GLA_C) == 0 and ctx % GLA_C == 0
    h = GLA_HEADS

    hps = LANES // GLA_DK
    qk_slabs = h // hps

    def qk(l, off):
        return pl.BlockSpec((1, l, LANES), lambda b, hd: (b, 0, off + hd // hps))

    def col(l, off):
        return pl.BlockSpec((1, l, LANES), lambda b, hd: (b, 0, off + hd))

    return pl.pallas_call(
        _gla_kernel,
        grid=(bsz, h),
        in_specs=[qk(seq, 0), qk(seq, qk_slabs), col(seq, 2 * qk_slabs), col(seq, 2 * qk_slabs + h),
                  pl.BlockSpec((1, seq, LANES), lambda b, hd: (b, 0, 0)),
                  qk(ctx, qk_slabs), col(ctx, 2 * qk_slabs),
                  pl.BlockSpec((1, ctx, LANES), lambda b, hd: (b, 0, 0)),
                  pl.BlockSpec((1, 2, LANES, LANES), lambda b, hd: (hd, 0, 0, 0)),
                  pl.BlockSpec((1, 2, 1, LANES), lambda b, hd: (hd, 0, 0, 0)),
                  pl.BlockSpec((1, LANES), lambda b, hd: (0, hd))],
        out_specs=pl.BlockSpec((1, seq, LANES), lambda b, hd: (b, 0, hd)),
        out_shape=jax.ShapeDtypeStruct((bsz, seq, h * GLA_DV), BF16),
        scratch_shapes=[pltpu.VMEM((2, LANES, LANES), F32), pltpu.VMEM((seq, LANES), F32)],
        compiler_params=_cparams(("arbitrary", "arbitrary")),
        name="gla",
    )(zg_x, zg_x, zg_x, zg_x, zs_x, zg_c, zg_c, zs_c, wup, bup, norm_g)


def _grid_conv_silu(src_ref, pad_ref, dst_ref, w_ref, b_ref, grid_w, scale):
    l = src_ref.shape[1]
    n_rows = l // grid_w
    margin = grid_w + SUBLANES
    assert pad_ref.shape[0] >= l + 2 * margin and margin % SUBLANES == 0
    pad_ref[0:margin, :] = jnp.zeros((margin, LANES), F32)
    pad_ref[margin + l:2 * margin + l, :] = jnp.zeros((margin, LANES), F32)

    def copy_row(r, carry):
        at = pl.ds(pl.multiple_of(r * grid_w, grid_w), grid_w)
        pad_ref[pl.ds(pl.multiple_of(margin + r * grid_w, SUBLANES), grid_w), :] = src_ref[0, at, :]
        return carry

    lax.fori_loop(0, n_rows, copy_row, 0, unroll=2)

    col = lax.broadcasted_iota(jnp.int32, (grid_w, LANES), 0)
    inside = {dx: (col + dx >= 0) & (col + dx < grid_w) for dx in (-1, 1)}
    rows_dy = (0,) if n_rows == 1 else (-1, 0, 1)

    def body(r, carry):
        base = pl.multiple_of(margin + r * grid_w, SUBLANES)
        acc = jnp.zeros((grid_w, LANES), F32) + b_ref[...]
        for dy in rows_dy:
            for dx in (-1, 0, 1):
                tap = (dy + 1) * 3 + (dx + 1)
                blk = pad_ref[pl.ds(base + dy * grid_w + dx, grid_w), :]
                if dx != 0:
                    blk = jnp.where(inside[dx], blk, 0.0)
                acc = acc + blk * w_ref[tap:tap + 1, :]
        dst_ref[pl.ds(pl.multiple_of(r * grid_w, grid_w), grid_w), :] = _silu(acc) * scale
        return carry

    lax.fori_loop(0, n_rows, body, 0)


def _ml_chunk(qb, k, vt, gcol, grow, state, mstate, direction, want_out, visible):
    c = k.shape[0]
    a_row = grow[direction:direction + 1, :]
    i_row = grow[2 + direction:3 + direction, :]
    a_end = a_row[:, c - 1:c] if direction == 0 else a_row[:, 0:1]
    g = a_end - a_row + i_row
    g_max = jnp.max(g, axis=-1, keepdims=True)
    head = 2 * SUBLANES
    pad_rows = jnp.zeros((LANES - head, c), BF16)
    first = lax.broadcasted_iota(jnp.int32, (head, c), 0) == 0
    kb = k.astype(BF16)
    if want_out:
        c_col = gcol[:, direction:direction + 1] - gcol[:, 2 + direction:3 + direction]
        dmat = jnp.where(visible, a_row - c_col, -jnp.inf)
        d_max = jnp.max(dmat, axis=0, keepdims=True)
        kq = jnp.dot(kb, qb, preferred_element_type=F32)
    yield
    s, m = state[direction], mstate[direction]
    m_new = jnp.maximum(a_end + m, g_max)
    decay = jnp.exp(a_end + m - m_new)
    w = jnp.exp(g - m_new)
    vw = jnp.concatenate([(vt * w).astype(BF16), jnp.where(first, w, 0.0).astype(BF16), pad_rows], axis=0)
    state[direction] = decay * s + jnp.dot(vw, kb, preferred_element_type=F32)
    mstate[direction] = m_new
    if not want_out:
        return None
    inter = a_row + m
    m_t = jnp.maximum(inter, d_max)
    w_inter = jnp.exp(inter - m_t)
    p = (kq * jnp.exp(dmat - m_t)).astype(BF16)
    vt_aug = jnp.concatenate([vt.astype(BF16), jnp.where(first, 1.0, 0.0).astype(BF16), pad_rows], axis=0)
    pv = jnp.dot(vt_aug, p, preferred_element_type=F32)
    sq = jnp.dot(s.astype(BF16), qb, preferred_element_type=F32)
    yield
    both = pv + w_inter * sq
    den = both[ML_DH:ML_DH + 1, :]
    return both[:ML_DH, :] / jnp.maximum(jnp.abs(den), jnp.exp(-m_t))


def _mlstm_kernel(q_ref, k_ref, v_ref, mo_ref, gcol_ref, grow_ref,
                  kc_ref, vc_ref, growc_ref,
                  wq_ref, wk_ref, bq_ref, bk_ref, ng_ref, o_ref,
                  cq_ref, ck_ref, ckc_ref, pad_ref, qt_ref, vt_ref, vct_ref, s_ref, m_ref, acc_ref):
    seq = q_ref.shape[1]
    ctx = kc_ref.shape[1]
    n = seq // ML_C
    nc = ctx // ML_C
    _grid_conv_silu(q_ref, pad_ref, cq_ref, wq_ref, bq_ref, GRID_W, 1.0)
    _grid_conv_silu(k_ref, pad_ref, ck_ref, wk_ref, bk_ref, GRID_W, ML_DH ** -0.5)
    _grid_conv_silu(kc_ref, pad_ref, ckc_ref, wk_ref, bk_ref, ctx, ML_DH ** -0.5)
    s_ref[...] = jnp.zeros_like(s_ref)
    m_ref[...] = jnp.zeros_like(m_ref)

    def rows(i):
        return pl.ds(pl.multiple_of(i * ML_C, ML_C), ML_C)

    def transpose_chunks(i, carry):
        qt_ref[:, rows(i)] = cq_ref[rows(i), :].T.astype(qt_ref.dtype)
        vt_ref[:, rows(i)] = v_ref[0, rows(i), :].T
        return carry

    lax.fori_loop(0, n, transpose_chunks, 0, unroll=2)
    for i in range(nc):
        vct_ref[:, i * ML_C:(i + 1) * ML_C] = vc_ref[0, i * ML_C:(i + 1) * ML_C, :].T

    def load_state():
        return [s_ref[0], s_ref[1]], [m_ref[0, :, 0:1], m_ref[1, :, 0:1]]

    def store_state(s, m):
        for d in range(2):
            s_ref[d] = s[d]
            m_ref[d] = jnp.broadcast_to(m[d], m_ref.shape[1:])

    masks, _ = _visibility(ML_C)
    visible = [masks[1], masks[0]]

    def ctx_step(j, carry):
        order = _scan_order(j, nc, 1)
        ins = [(ckc_ref[rows(i), :], vct_ref[:, rows(i)], growc_ref[0, 0, :, rows(i)]) for _, i in order]
        s, m = load_state()
        _round_robin([_ml_chunk(None, k, vt, None, grow, s, m, d, False, None)
                      for (d, _), (k, vt, grow) in zip(order, ins)])
        store_state(s, m)
        return carry

    lax.fori_loop(0, nc, ctx_step, 0)

    def lat_step(j, carry, second):
        order = _scan_order(j, n, SCAN_UNROLL)
        ins = [(qt_ref[:, rows(i)], ck_ref[rows(i), :], vt_ref[:, rows(i)], gcol_ref[0, rows(i), :],
                grow_ref[0, 0, :, rows(i)]) for _, i in order]
        prev = [(acc_ref[:, rows(i)], mo_ref[0, rows(i), :]) for _, i in order] if second else None
        s, m = load_state()
        outs = _round_robin([_ml_chunk(qb, k, vt, gcol, grow, s, m, d, True, visible[d])
                             for (d, _), (qb, k, vt, gcol, grow) in zip(order, ins)])
        store_state(s, m)
        for idx, (_, i) in enumerate(order):
            if second:
                total = prev[idx][0] + outs[idx]
                y = total * lax.rsqrt(jnp.mean(total * total, axis=0, keepdims=True) + EPS) * ng_ref[...]
                o_ref[0, rows(i), :] = (_sigmoid(prev[idx][1]) * y.T).astype(o_ref.dtype)
            else:
                acc_ref[:, rows(i)] = outs[idx]
        return carry

    half = n // (2 * SCAN_UNROLL)
    lax.fori_loop(0, half, functools.partial(lat_step, second=False), 0)
    lax.fori_loop(half, 2 * half, functools.partial(lat_step, second=True), 0)


def _mlstm(zm_x, gcol_x, grow_x, zm_c, grow_c, conv_w, conv_b, norm_g):
    bsz, seq, _ = zm_x.shape
    ctx = zm_c.shape[1]
    assert seq % (2 * SCAN_UNROLL * ML_C) == 0 and ctx % ML_C == 0 and seq % GRID_W == 0
    h = ML_HEADS

    def col(l, off):
        return pl.BlockSpec((1, l, LANES), lambda b, hd: (b, 0, off + hd))

    def gates(l):
        return [pl.BlockSpec((1, l, LANES), lambda b, hd: (b, 0, hd)),
                pl.BlockSpec((1, 1, 8, l), lambda b, hd: (b, hd, 0, 0))]

    return pl.pallas_call(
        _mlstm_kernel,
        grid=(bsz, h),
        in_specs=[col(seq, 0), col(seq, h), col(seq, 2 * h), col(seq, 3 * h)] + gates(seq)
                 + [col(ctx, h), col(ctx, 2 * h), gates(ctx)[1]]
                 + [pl.BlockSpec((9, LANES), lambda b, hd: (0, hd)),
                    pl.BlockSpec((9, LANES), lambda b, hd: (0, h + hd)),
                    pl.BlockSpec((1, LANES), lambda b, hd: (0, hd)),
                    pl.BlockSpec((1, LANES), lambda b, hd: (0, h + hd)),
                    pl.BlockSpec((LANES, 1), lambda b, hd: (hd, 0))],
        out_specs=pl.BlockSpec((1, seq, LANES), lambda b, hd: (b, 0, hd)),
        out_shape=jax.ShapeDtypeStruct((bsz, seq, h * ML_DH), BF16),
        scratch_shapes=[pltpu.VMEM((seq, LANES), F32), pltpu.VMEM((seq, LANES), F32),
                        pltpu.VMEM((ctx, LANES), F32),
                        pltpu.VMEM((max(seq + 2 * (GRID_W + SUBLANES), 3 * ctx + 2 * SUBLANES), LANES), F32),
                        pltpu.VMEM((LANES, seq), BF16), pltpu.VMEM((LANES, seq), F32),
                        pltpu.VMEM((LANES, ctx), F32),
                        pltpu.VMEM((2, 2 * LANES, LANES), F32), pltpu.VMEM((2, 1, LANES), F32),
                        pltpu.VMEM((LANES, seq), F32)],
        compiler_params=_cparams(("arbitrary", "arbitrary")),
        name="mlstm",
    )(zm_x, zm_x, zm_x, zm_x, gcol_x, grow_x, zm_c, zm_c, grow_c,
      conv_w, conv_w, conv_b, conv_b, norm_g.reshape(-1, 1))


_G0 = 0
_E0 = N_GROUPS
RANK_BITS = 16
RANK_SPAN = 1 << RANK_BITS
ROW_GROUP = 32
GATHER_RING = 3


SUBLANES = 8


def _store_token_tiles(ref2d, val):
    n, w = val.shape
    k = w // LANES
    for c in range(k):
        ref2d[pl.ds(c, n, stride=k), :] = val[:, c * LANES:(c + 1) * LANES]


def _load_token_tiles(ref2d, first, n, k, step):
    return jnp.concatenate([ref2d[pl.ds(first + c, n, stride=step), :] for c in range(k)], axis=1)


def _outproj_kernel(x_ref, ga_ref, ml_ref, mod_ref, wa_ref, wb_ref, g2_ref, wrh_ref, wrl_ref, br_ref,
                    x1_ref, h2_ref, ri_ref, rw_ref, cnt_ref, base_ref):
    tm = x_ref.shape[1]

    @pl.when((pl.program_id(0) == 0) & (pl.program_id(1) == 0))
    def _():
        base_ref[...] = jnp.zeros_like(base_ref)

    mix = (jnp.dot(ga_ref[0], wa_ref[...], preferred_element_type=F32)
           + jnp.dot(ml_ref[0], wb_ref[...], preferred_element_type=F32))
    x1 = x_ref[0] + mod_ref[0, 2:3, :] * mix
    x1_ref[0] = x1
    y = x1 * lax.rsqrt(jnp.mean(x1 * x1, axis=-1, keepdims=True) + EPS) * g2_ref[...]
    h2 = y * (1.0 + mod_ref[0, 4:5, :]) + mod_ref[0, 3:4, :]
    _store_token_tiles(h2_ref, h2)

    h_hi = h2.astype(BF16)
    h_lo = (h2 - h_hi.astype(F32)).astype(BF16)
    logits = (jnp.dot(h_hi, wrh_ref[...], preferred_element_type=F32)
              + jnp.dot(h_lo, wrh_ref[...], preferred_element_type=F32)
              + jnp.dot(h_hi, wrl_ref[...], preferred_element_type=F32)) + br_ref[...]

    lane = lax.broadcasted_iota(jnp.int32, (tm, LANES), 1).astype(F32)
    neg = -jnp.inf
    big = float(LANES)
    is_g = lane < float(_E0)
    lg = jnp.where(is_g, logits, neg)
    gmax = jnp.max(lg, axis=-1, keepdims=True)
    gidx = jnp.min(jnp.where(lg == gmax, lane, big), axis=-1, keepdims=True)
    gw = 1.0 / jnp.sum(jnp.where(is_g, jnp.exp(logits - gmax), 0.0), axis=-1, keepdims=True)
    lo = float(_E0) + float(EXP_PER_GROUP) * gidx
    le = jnp.where((lane >= lo) & (lane < lo + float(EXP_PER_GROUP)), logits, neg)
    v1 = jnp.max(le, axis=-1, keepdims=True)
    i1 = jnp.min(jnp.where(le == v1, lane, big), axis=-1, keepdims=True)
    le2 = jnp.where(lane == i1, neg, le)
    v2 = jnp.max(le2, axis=-1, keepdims=True)
    i2 = jnp.min(jnp.where(le2 == v2, lane, big), axis=-1, keepdims=True)
    t = jnp.exp(v2 - v1)
    w1 = gw / (1.0 + t)
    w2 = gw * t / (1.0 + t)
    e1 = i1 - float(_E0)
    e2 = i2 - float(_E0)

    oh1 = lane == e1
    oh2 = lane == e2
    oh = jnp.where(oh1 | oh2, 1.0, 0.0)
    r = lax.broadcasted_iota(jnp.int32, (tm, tm), 0)
    c = lax.broadcasted_iota(jnp.int32, (tm, tm), 1)
    strict = jnp.where(c < r, 1.0, 0.0).astype(BF16)
    before = jnp.dot(strict, oh.astype(BF16), preferred_element_type=F32) + base_ref[...]
    rank1 = jnp.sum(jnp.where(oh1, before, 0.0), axis=-1, keepdims=True)
    rank2 = jnp.sum(jnp.where(oh2, before, 0.0), axis=-1, keepdims=True)
    total = base_ref[...] + jnp.sum(oh, axis=0, keepdims=True)
    base_ref[...] = total
    cnt_ref[...] = total

    ids = jnp.where(lane == 0.0, e1 * float(RANK_SPAN) + rank1,
                    jnp.where(lane == 1.0, e2 * float(RANK_SPAN) + rank2, 0.0))
    ri_ref[...] = ids.astype(jnp.int32)
    rw_ref[...] = jnp.where(lane == 0.0, w1, jnp.where(lane == 1.0, w2, 0.0))


def _outproj(x, gla_o, ml_o, mods, wa, wb, g2, wrh, wrl, br, tm):
    bsz, seq, d = x.shape
    const = lambda shape: pl.BlockSpec(shape, lambda b, i: (0,) * len(shape))
    tile = lambda w: pl.BlockSpec((1, tm, w), lambda b, i: (b, i, 0))
    flat = lambda rows: pl.BlockSpec((rows, LANES), lambda b, i: (b * (seq // tm) + i, 0))
    return pl.pallas_call(
        _outproj_kernel,
        grid=(bsz, seq // tm),
        in_specs=[tile(d), tile(gla_o.shape[2]), tile(ml_o.shape[2]),
                  pl.BlockSpec((1, N_MOD, d), lambda b, i: (b, 0, 0)),
                  const(wa.shape), const(wb.shape), const((1, d)),
                  const(wrh.shape), const(wrl.shape), const((1, LANES))],
        out_specs=[tile(d),
                   pl.BlockSpec((tm * d // LANES, LANES), lambda b, i: (b * (seq // tm) + i, 0)),
                   flat(tm), flat(tm), const((1, LANES))],
        out_shape=[jax.ShapeDtypeStruct((bsz, seq, d), F32),
                   jax.ShapeDtypeStruct((bsz * seq * d // LANES, LANES), F32),
                   jax.ShapeDtypeStruct((bsz * seq, LANES), jnp.int32),
                   jax.ShapeDtypeStruct((bsz * seq, LANES), F32),
                   jax.ShapeDtypeStruct((1, LANES), F32)],
        scratch_shapes=[pltpu.VMEM((1, LANES), F32)],
        compiler_params=_cparams(("arbitrary", "arbitrary")),
        name="outproj",
    )(x, gla_o, ml_o, mods, wa, wb, g2, wrh, wrl, br)


def _experts_kernel(dest_ref, ps_ref, cnt_ref, be_ref, nv_ref, meta_ref, h_hbm, w1_ref, w2_ref, ytok_hbm,
                    src_ref, xbuf, ybuf, w1c_ref, w2c_ref, gsem, ssem):
    i = pl.program_id(0)
    n_steps = pl.num_programs(0)
    n_used = meta_ref[0]
    tr = SUBLANES
    n_tok = h_hbm.shape[0] // tr
    n_rows = src_ref.shape[0]
    blk = xbuf.shape[1] // tr
    n_x = xbuf.shape[0]
    slot = lax.rem(i, 2)
    xslot = lax.rem(i, n_x)

    def slab(j):
        return pl.ds(pl.multiple_of(j * tr, tr), tr)

    def group_rows(buf, s, g):
        span = ROW_GROUP * tr
        return buf.at[s, pl.ds(pl.multiple_of(g * span, span), span), :]

    def gather_copy(tok, s, g, u):
        return pltpu.make_async_copy(h_hbm.at[slab(tok), :], group_rows(xbuf, s, g).at[pl.ds(u * tr, tr), :],
                                     gsem.at[s])

    def scatter_copy(a, s, g, u):
        return pltpu.make_async_copy(group_rows(ybuf, s, g).at[pl.ds(u * tr, tr), :], ytok_hbm.at[slab(a), :],
                                     ssem.at[s])

    def groups(b):
        nv = jnp.where(b < n_steps, nv_ref[jnp.minimum(b, n_steps - 1)], 0)
        return lax.shift_right_logical(nv + (ROW_GROUP - 1), ROW_GROUP.bit_length() - 1)

    def rows_loop(b, body):
        def step(g, c):
            for u in range(ROW_GROUP):
                body(g, u)
            return c
        lax.fori_loop(0, groups(b), step, 0)

    def issue_gather(b, s):
        def one(g, u):
            tok = lax.shift_right_logical(src_ref[b * blk + g * ROW_GROUP + u], 1)
            gather_copy(jnp.minimum(tok, n_tok - 1), s, g, u).start(priority=u % 2)
        rows_loop(b, one)

    def wait_rows(b, copy):
        n_groups = groups(b)
        bit = blk // ROW_GROUP
        while bit:
            @pl.when((n_groups & bit) != 0)
            def _():
                for _ in range(bit * ROW_GROUP):
                    copy.wait()
            bit //= 2

    def wait_gather(b, s):
        wait_rows(b, gather_copy(0, s, 0, 0))

    def wait_scatter(b, s):
        wait_rows(b, scatter_copy(0, s, 0, 0))

    @pl.when(i == 0)
    def _():
        xbuf[...] = jnp.zeros_like(xbuf)
        ybuf[...] = jnp.zeros_like(ybuf)
        for s in range(2):
            tail = ytok_hbm.at[pl.ds((2 * n_tok + s * blk) * tr, blk * tr), :]
            cp = pltpu.make_async_copy(ybuf.at[s], tail, ssem.at[s])
            cp.start()
            cp.wait()

        def put(a, c):
            src_ref[dest_ref[a]] = a
            return c
        lax.fori_loop(0, 2 * n_tok, put, 0, unroll=16)

        def pad(j, c):
            src_ref[j] = 2 * n_tok + (j & (2 * blk - 1))
            return c

        def pad_expert(e, c):
            lax.fori_loop(ps_ref[e] + cnt_ref[e], ps_ref[e + 1], pad, 0)
            return c
        lax.fori_loop(0, cnt_ref.shape[0], pad_expert, 0)
        for b in range(n_x - 1):
            issue_gather(min(b, n_rows // blk - 1), b)

    @pl.when(i < n_used)
    def _():
        wait_gather(i, xslot)

        @pl.when((i == 0) | (be_ref[i] != be_ref[jnp.maximum(i - 1, 0)]))
        def _():
            w1c_ref[...] = w1_ref[0].astype(BF16)
            w2c_ref[...] = w2_ref[0].astype(BF16)

        @pl.when(i >= 2)
        def _():
            wait_scatter(i - 2, slot)

        row = lax.broadcasted_iota(jnp.int32, (blk, 1), 0)
        x = _load_token_tiles(xbuf.at[xslot], 0, blk, tr, tr)
        x = jnp.where(row < nv_ref[i], x, 0.0).astype(BF16)
        h = jnp.dot(x, w1c_ref[...], preferred_element_type=F32)
        a = (_silu(h[:, :D_EXPERT]) * h[:, D_EXPERT:]).astype(BF16)
        _store_token_tiles(ybuf.at[slot], jnp.dot(a, w2c_ref[...], preferred_element_type=F32))
        rows_loop(i, lambda g, u: scatter_copy(src_ref[i * blk + g * ROW_GROUP + u], slot, g, u)
                  .start(priority=u % 2))
        issue_gather(i + n_x - 1, lax.rem(i + n_x - 1, n_x))

    @pl.when(i == n_steps - 1)
    def _():
        for back in (2, 1):
            wait_scatter(n_used - back, lax.rem(n_used - back, 2))


def _experts(dest, pad_start, counts, block_e, block_nv, meta, h2, w_in, w_out, nb):
    d = w_in.shape[1]
    tr = d // LANES
    assert tr == SUBLANES, "a token row must fill exactly one (8, 128) tile"
    n_tok = h2.shape[0] // tr
    de2 = w_in.shape[2]
    n_rows = nb * MOE_BLK
    assert 2 * n_tok >= 2 * MOE_BLK
    assert MOE_BLK & (MOE_BLK - 1) == 0
    wmap = lambda i, pk, ps, cnt, be, nv, meta: (be[i], 0, 0)
    return pl.pallas_call(
        _experts_kernel,
        grid_spec=pltpu.PrefetchScalarGridSpec(
            num_scalar_prefetch=6, grid=(nb,),
            in_specs=[pl.BlockSpec(memory_space=pl.ANY),
                      pl.BlockSpec((1, d, de2), wmap),
                      pl.BlockSpec((1, de2 // 2, d), wmap)],
            out_specs=pl.BlockSpec(memory_space=pl.ANY),
            scratch_shapes=[pltpu.SMEM((n_rows,), jnp.int32),
                            pltpu.VMEM((GATHER_RING, MOE_BLK * tr, LANES), F32),
                            pltpu.VMEM((2, MOE_BLK * tr, LANES), F32),
                            pltpu.VMEM((d, de2), BF16), pltpu.VMEM((de2 // 2, d), BF16),
                            pltpu.SemaphoreType.DMA((GATHER_RING,)), pltpu.SemaphoreType.DMA((2,))]),
        out_shape=jax.ShapeDtypeStruct(((2 * n_tok + 2 * MOE_BLK) * tr, LANES), F32),
        compiler_params=_cparams(("arbitrary",)),
        name="experts",
    )(dest, pad_start, counts, block_e, block_nv, meta, h2, w_in, w_out)


def _combine_kernel(x1_ref, y_ref, rw_ref, mod_ref, fg_ref, o_ref):
    tc, d = x1_ref.shape
    tr = d // LANES
    y1 = _load_token_tiles(y_ref, 0, tc, tr, 2 * tr)
    y2 = _load_token_tiles(y_ref, tr, tc, tr, 2 * tr)
    moe = rw_ref[:, 0:1] * y1 + rw_ref[:, 1:2] * y2
    x2 = x1_ref[...] + mod_ref[0, 5:6, :] * moe
    o_ref[...] = x2 * lax.rsqrt(jnp.mean(x2 * x2, axis=-1, keepdims=True) + EPS) * fg_ref[...]


def _combine(x1, ytok, rw, mods, fg, tokens_per_batch, tc):
    n_tok, d = x1.shape
    tiles_per_batch = tokens_per_batch // tc
    return pl.pallas_call(
        _combine_kernel,
        grid=(n_tok // tc,),
        in_specs=[pl.BlockSpec((tc, d), lambda i: (i, 0)),
                  pl.BlockSpec((2 * tc * d // LANES, LANES), lambda i: (i, 0)),
                  pl.BlockSpec((tc, LANES), lambda i: (i, 0)),
                  pl.BlockSpec((1, N_MOD, d), lambda i: (i // tiles_per_batch, 0, 0)),
                  pl.BlockSpec((1, d), lambda i: (0, 0))],
        out_specs=pl.BlockSpec((tc, d), lambda i: (i, 0)),
        out_shape=jax.ShapeDtypeStruct((n_tok, d), F32),
        compiler_params=_cparams(("arbitrary",)),
        name="combine",
    )(x1, ytok, rw, mods, fg)


def _prep_inproj_weights(w_in, gla_up_w, gla_up_b, ml_i_b, ml_f_b):
    d = w_in.shape[0]
    o_gq, o_gk, o_gv, o_gg = 0, GLA_QK_W, 2 * GLA_QK_W, 2 * GLA_QK_W + GLA_V_W
    o_lr = o_gg + GLA_V_W
    o_mqk = o_lr + 2 * GLA_LR
    o_mi = o_mqk + 4 * ML_W
    o_mf = o_mi + 2 * ML_HEADS

    wg = w_in[:, o_gq:o_lr]
    wm = w_in[:, o_mqk:o_mi]
    ws = jnp.concatenate([w_in[:, o_lr:o_mqk], w_in[:, o_mi:o_mf + 2 * ML_HEADS],
                          jnp.zeros((d, LANES - 2 * GLA_LR - 4 * ML_HEADS), w_in.dtype)], axis=1)
    bias = jnp.zeros((LANES,), F32)
    bias = bias.at[_MI0:_MI0 + 2 * ML_HEADS].set(ml_i_b.reshape(-1))
    bias = bias.at[_MF0:_MF0 + 2 * ML_HEADS].set(ml_f_b.reshape(-1))
    up = gla_up_w.reshape(2, GLA_LR, GLA_HEADS, GLA_DK).transpose(2, 0, 1, 3)
    ub = gla_up_b.reshape(2, GLA_HEADS, GLA_DK).transpose(1, 0, 2)
    wup = jnp.zeros((GLA_HEADS, 2, LANES, LANES), F32)
    bup = jnp.zeros((GLA_HEADS, 2, 1, LANES), F32)
    for hd in range(GLA_HEADS):
        lo = (hd % (LANES // GLA_DK)) * GLA_DK
        for dr in range(2):
            wup = wup.at[hd, dr, dr * GLA_LR:(dr + 1) * GLA_LR, lo:lo + GLA_DK].set(up[hd, dr])
        bup = bup.at[hd, :, 0, lo:lo + GLA_DK].set(ub[hd])
    return (wg.astype(BF16), wm.astype(BF16), ws.astype(BF16), ws.T.astype(BF16),
            bias.reshape(1, LANES), bias.reshape(LANES, 1), wup.astype(BF16), bup)


def _layer(x, ctx, mods, norm1_g, w_in, gla_up_w, gla_up_b, gla_norm_g, ml_conv_w, ml_conv_b,
           ml_i_b, ml_f_b, ml_norm_g, w_out, norm2_g, rg_w, rg_b, re_w, re_b, e_w_in, e_w_out, final_g):
    bsz, seq, d = x.shape
    n_tok = bsz * seq
    wg, wm, ws, wst, bcol, brow, wup, bup = _prep_inproj_weights(w_in, gla_up_w, gla_up_b, ml_i_b, ml_f_b)
    g1 = norm1_g.reshape(1, d)
    zg_x, zm_x, zs_x, gcol_x, grow_x = _inproj(x, mods, lambda b: b, g1, wg, wm, ws, wst, bcol, brow, 256)
    zg_c, zm_c, zs_c, gcol_c, grow_c = _inproj(ctx, mods, lambda b: bsz, g1, wg, wm, ws, wst, bcol, brow,
                                               min(256, ctx.shape[1]))
    gla_o = _gla(zg_x, zs_x, zg_c, zs_c, wup, bup, gla_norm_g.reshape(1, -1))
    ml_o = _mlstm(zm_x, gcol_x, grow_x, zm_c, grow_c,
                  ml_conv_w.reshape(9, -1), ml_conv_b.reshape(1, -1), ml_norm_g.reshape(1, -1))

    wr = jnp.zeros((d, LANES), F32).at[:, _G0:_E0].set(rg_w).at[:, _E0:_E0 + N_EXPERTS].set(re_w)
    br = jnp.zeros((1, LANES), F32).at[0, _G0:_E0].set(rg_b).at[0, _E0:_E0 + N_EXPERTS].set(re_b)
    wrh = wr.astype(BF16)
    wrl = (wr - wrh.astype(F32)).astype(BF16)
    x1, h2, ri, rw, cnt = _outproj(x, gla_o, ml_o, mods, w_out[:GLA_V_W].astype(BF16),
                                   w_out[GLA_V_W:].astype(BF16), norm2_g.reshape(1, d), wrh, wrl, br, 256)

    counts = cnt[0, :N_EXPERTS].astype(jnp.int32)
    nblk = (counts + MOE_BLK - 1) // MOE_BLK
    blk_end = jnp.cumsum(nblk)
    blk_start = blk_end - nblk
    n_used = blk_end[-1]
    nb_max = (2 * n_tok) // MOE_BLK + N_EXPERTS
    blk = jnp.arange(nb_max, dtype=jnp.int32)
    blk_c = jnp.minimum(blk, n_used - 1)
    onehot = (blk_c[:, None] >= blk_start[None, :]) & (blk_c[:, None] < blk_end[None, :])
    pick = lambda v: jnp.sum(jnp.where(onehot, v[None, :], 0), axis=1)
    block_e = pick(jnp.arange(N_EXPERTS, dtype=jnp.int32)).astype(jnp.int32)
    block_nv = jnp.clip(pick(counts) - (blk_c - pick(blk_start)) * MOE_BLK, 0, MOE_BLK)
    block_nv = jnp.where(blk < n_used, block_nv, 0).astype(jnp.int32)
    pad_start = (jnp.concatenate([blk_start, blk_end[-1:]]) * MOE_BLK).astype(jnp.int32)
    packed = ri[:, 0:2].reshape(-1)
    e_of = lax.shift_right_logical(packed, RANK_BITS)
    start_of = jnp.sum(jnp.where(e_of[:, None] == jnp.arange(N_EXPERTS, dtype=jnp.int32)[None, :],
                                 pad_start[None, :N_EXPERTS], 0), axis=1)
    dest = (start_of + (packed & (RANK_SPAN - 1))).astype(jnp.int32)
    meta = jnp.stack([n_used, n_used]).astype(jnp.int32)

    ytok = _experts(dest, pad_start, counts, block_e, block_nv, meta, h2, e_w_in, e_w_out, nb_max)
    out = _combine(x1.reshape(n_tok, d), ytok, rw, mods, final_g.reshape(1, d), seq, 256)
    return out.reshape(bsz, seq, d)


def kernel(x, c, ctx, c_ctx, ada_w, ada_b, norm1_g, w_in, gla_up_w, gla_up_b, gla_norm_g, ml_conv_w, ml_conv_b,
           ml_i_b, ml_f_b, ml_norm_g, w_out, norm2_g, router_group_w, router_group_b, router_expert_w,
           router_expert_b, expert_w_in, expert_w_out, final_norm_g):
    assert ada_w.shape[0] == 1, "single-layer stack"
    bsz, d = c.shape
    cc = jnp.concatenate([c, c_ctx[None, :], jnp.zeros((8 - bsz - 1, d), F32)], axis=0)
    mods = _modulation(cc, ada_w[0], ada_b[0]).reshape(8, N_MOD, d)
    return _layer(x, ctx, mods, norm1_g[0], w_in[0], gla_up_w[0], gla_up_b[0], gla_norm_g[0],
                  ml_conv_w[0], ml_conv_b[0], ml_i_b[0], ml_f_b[0], ml_norm_g[0], w_out[0], norm2_g[0],
                  router_group_w[0], router_group_b[0], router_expert_w[0], router_expert_b[0],
                  expert_w_in[0], expert_w_out[0], final_norm_g)
```

```python
import functools

import jax
import jax.numpy as jnp
from jax import lax
from jax.experimental import pallas as pl
from jax.experimental.pallas import tpu as pltpu

F32 = jnp.float32
BF16 = jnp.bfloat16

D_MODEL = 1024
GRID_W = 64
N_MOD = 6
EPS = 1e-6

GLA_HEADS = 4
GLA_DK = 64
GLA_DV = 128
GLA_LR = 16
GLA_TAU = 16.0
GLA_C = 128
GLA_UNROLL = 8
SCAN_UNROLL = 4

ML_HEADS = 4
ML_DH = 128
ML_C = 128

N_GROUPS = 4
EXP_PER_GROUP = 8
N_EXPERTS = N_GROUPS * EXP_PER_GROUP
D_EXPERT = 512
MOE_BLK = 256

GLA_QK_W = GLA_HEADS * GLA_DK
GLA_V_W = GLA_HEADS * GLA_DV
ML_W = ML_HEADS * ML_DH
LANES = 128
VMEM_LIMIT = 56 * 1024 * 1024

_LR0 = 0
_MI0 = 2 * GLA_LR
_MF0 = _MI0 + 2 * ML_HEADS


def _cparams(sem):
    return pltpu.CompilerParams(dimension_semantics=sem, vmem_limit_bytes=VMEM_LIMIT)


def _sigmoid(x):
    return 1.0 / (1.0 + jnp.exp(-x))


def _silu(x):
    return x * _sigmoid(x)


def _log_sigmoid(x):
    return jnp.minimum(x, 0.0) - jnp.log1p(jnp.exp(-jnp.abs(x)))


def _split_dot(a_bf16_exact, x, dims=None):
    x_hi = x.astype(BF16)
    x_lo = (x - x_hi.astype(F32)).astype(BF16)
    if dims is None:
        f = lambda u: jnp.dot(a_bf16_exact, u, preferred_element_type=F32)
    else:
        f = lambda u: lax.dot_general(u, a_bf16_exact, dims, preferred_element_type=F32)
    return f(x_hi) + f(x_lo)


def _mod_kernel(c_ref, w_ref, b_ref, o_ref):
    c = c_ref[...]
    s = _silu(c).astype(BF16)
    o_ref[...] = jnp.dot(s, w_ref[...].astype(BF16), preferred_element_type=F32) + b_ref[...]


def _modulation(cc, ada_w, ada_b):
    rows, d = cc.shape
    n = ada_w.shape[1]
    tn = 1536
    return pl.pallas_call(
        _mod_kernel,
        grid=(n // tn,),
        in_specs=[pl.BlockSpec((rows, d), lambda j: (0, 0)),
                  pl.BlockSpec((d, tn), lambda j: (0, j)),
                  pl.BlockSpec((1, tn), lambda j: (0, j))],
        out_specs=pl.BlockSpec((rows, tn), lambda j: (0, j)),
        out_shape=jax.ShapeDtypeStruct((rows, n), F32),
        compiler_params=_cparams(("arbitrary",)),
        name="mod",
    )(cc, ada_w, ada_b.reshape(1, n))


def _inproj_kernel(x_ref, mod_ref, g_ref, wg_ref, wm_ref, ws_ref, wst_ref, bcol_ref, brow_ref,
                   zg_ref, zm_ref, zs_ref, gcol_ref, grow_ref):
    tm = x_ref.shape[1]
    x = x_ref[0]
    y = x * lax.rsqrt(jnp.mean(x * x, axis=-1, keepdims=True) + EPS) * g_ref[...]
    h = (y * (1.0 + mod_ref[0, 1:2, :]) + mod_ref[0, 0:1, :]).astype(BF16)
    zg_ref[0] = jnp.dot(h, wg_ref[...], preferred_element_type=F32)
    zm_ref[0] = jnp.dot(h, wm_ref[...], preferred_element_type=F32)
    zs = jnp.dot(h, ws_ref[...], preferred_element_type=F32) + bcol_ref[...]
    zst = lax.dot_general(wst_ref[...], h, (((1,), (1,)), ((), ())),
                          preferred_element_type=F32) + brow_ref[...]
    zs_ref[0] = zs

    r = lax.broadcasted_iota(jnp.int32, (tm, tm), 0)
    c = lax.broadcasted_iota(jnp.int32, (tm, tm), 1)
    shift = ML_C.bit_length() - 1
    same = jnp.right_shift(r, shift) == jnp.right_shift(c, shift)
    lower = jnp.where(same & (c <= r), 1.0, 0.0).astype(BF16)
    upper = jnp.where(same & (c >= r), 1.0, 0.0).astype(BF16)
    chunks = range(0, tm, ML_C)
    lsf = _log_sigmoid(zs)
    a_pre = _split_dot(lower, lsf)
    tot = jnp.concatenate([jnp.broadcast_to(a_pre[o + ML_C - 1:o + ML_C, :], (ML_C, LANES)) for o in chunks], axis=0)
    a_suf = tot - a_pre + lsf
    lsft = _log_sigmoid(zst)
    a_pre_t = _split_dot(upper, lsft, (((1,), (0,)), ((), ())))
    tot_t = jnp.concatenate([jnp.broadcast_to(a_pre_t[:, o + ML_C - 1:o + ML_C], (LANES, ML_C)) for o in chunks], axis=1)
    a_suf_t = tot_t - a_pre_t + lsft

    lane = lax.broadcasted_iota(jnp.int32, (tm, LANES), 1)
    for hd in range(ML_HEADS):
        cols = (a_pre[:, _MF0 + hd:_MF0 + hd + 1],
                a_suf[:, _MF0 + ML_HEADS + hd:_MF0 + ML_HEADS + hd + 1],
                zs[:, _MI0 + hd:_MI0 + hd + 1],
                zs[:, _MI0 + ML_HEADS + hd:_MI0 + ML_HEADS + hd + 1])
        slab = jnp.zeros((tm, LANES), F32)
        for j, col in enumerate(cols):
            slab = jnp.where(lane == j, col, slab)
        gcol_ref[0, :, hd * LANES:(hd + 1) * LANES] = slab
        rows = (a_pre_t[_MF0 + hd:_MF0 + hd + 1, :],
                a_suf_t[_MF0 + ML_HEADS + hd:_MF0 + ML_HEADS + hd + 1, :],
                zst[_MI0 + hd:_MI0 + hd + 1, :],
                zst[_MI0 + ML_HEADS + hd:_MI0 + ML_HEADS + hd + 1, :])
        for j, row in enumerate(rows):
            grow_ref[0, hd, j:j + 1, :] = row
        grow_ref[0, hd, 4:8, :] = jnp.zeros((4, tm), F32)


def _inproj(x, mods, mod_row_of_batch, norm_g, wg, wm, ws, wst, bcol, brow, tm):
    bsz, l, d = x.shape
    assert l % tm == 0 and tm % ML_C == 0
    const = lambda shape: pl.BlockSpec(shape, lambda b, i: (0,) * len(shape))
    return pl.pallas_call(
        _inproj_kernel,
        grid=(bsz, l // tm),
        in_specs=[pl.BlockSpec((1, tm, d), lambda b, i: (b, i, 0)),
                  pl.BlockSpec((1, N_MOD, d), lambda b, i: (mod_row_of_batch(b), 0, 0)),
                  const((1, d)), const(wg.shape), const(wm.shape), const(ws.shape), const(wst.shape),
                  const((1, LANES)), const((LANES, 1))],
        out_specs=[pl.BlockSpec((1, tm, wg.shape[1]), lambda b, i: (b, i, 0)),
                   pl.BlockSpec((1, tm, wm.shape[1]), lambda b, i: (b, i, 0)),
                   pl.BlockSpec((1, tm, LANES), lambda b, i: (b, i, 0)),
                   pl.BlockSpec((1, tm, ML_HEADS * LANES), lambda b, i: (b, i, 0)),
                   pl.BlockSpec((1, ML_HEADS, 8, tm), lambda b, i: (b, 0, 0, i))],
        out_shape=[jax.ShapeDtypeStruct((bsz, l, wg.shape[1]), F32),
                   jax.ShapeDtypeStruct((bsz, l, wm.shape[1]), F32),
                   jax.ShapeDtypeStruct((bsz, l, LANES), F32),
                   jax.ShapeDtypeStruct((bsz, l, ML_HEADS * LANES), F32),
                   jax.ShapeDtypeStruct((bsz, ML_HEADS, 8, l), F32)],
        compiler_params=_cparams(("arbitrary", "arbitrary")),
        name="inproj",
    )(x, mods, norm_g, wg, wm, ws, wst, bcol, brow)


def _round_robin(chains):
    results = [None] * len(chains)
    live = list(enumerate(chains))
    while live:
        still = []
        for idx, chain in live:
            try:
                next(chain)
                still.append((idx, chain))
            except StopIteration as done:
                results[idx] = done.value
        live = still
    return results


def _visibility(c):
    r = lax.broadcasted_iota(jnp.int32, (c, c), 0)
    cc = lax.broadcasted_iota(jnp.int32, (c, c), 1)
    masks = [cc <= r, cc >= r]
    return masks, [jnp.where(m, 1.0, 0.0).astype(BF16) for m in masks]


def _gla_chunk(q, k, v, zs, wup, bup, state, direction, want_out, causal, tri):
    c = k.shape[0]
    logits = jnp.dot(zs.astype(BF16), wup, preferred_element_type=F32) + bup
    yield
    g = _log_sigmoid(logits) * (1.0 / GLA_TAU)
    b = _split_dot(tri, g)
    yield
    b_end = b[c - 1:c, :] if direction == 0 else b[0:1, :]
    kd = (k * jnp.exp(b_end - b)).astype(BF16)
    upd = lax.dot_general(v.astype(BF16), kd, (((0,), (0,)), ((), ())), preferred_element_type=F32)
    s = state[direction]
    state[direction] = jnp.exp(b_end) * s + upd
    if not want_out:
        return None
    b_mid = b[c // 2:c // 2 + 1, :]
    q_in = (q * jnp.exp(b - b_mid)).astype(BF16)
    k_in = (k * jnp.exp(b_mid - b)).astype(BF16)
    att = lax.dot_general(q_in, k_in, (((1,), (1,)), ((), ())), preferred_element_type=F32)
    inter = lax.dot_general((q * jnp.exp(b)).astype(BF16), s.astype(BF16),
                            (((1,), (1,)), ((), ())), preferred_element_type=F32)
    yield
    att = jnp.where(causal, att, 0.0)
    return jnp.dot(att.astype(BF16), v.astype(BF16), preferred_element_type=F32) + inter


def _scan_order(j, n, unroll):
    return [(d, j * unroll + u if d == 0 else n - 1 - (j * unroll + u)) for u in range(unroll) for d in range(2)]


def _gla_kernel(q_ref, k_ref, v_ref, gg_ref, zs_ref, kc_ref, vc_ref, zsc_ref,
                wup_ref, bup_ref, ng_ref, o_ref, s_ref, acc_ref):
    seq = q_ref.shape[1]
    ctx = kc_ref.shape[1]
    n = seq // GLA_C
    nc = ctx // GLA_C
    s_ref[...] = jnp.zeros_like(s_ref)

    def rows(i):
        return pl.ds(pl.multiple_of(i * GLA_C, GLA_C), GLA_C)

    masks, tris = _visibility(GLA_C)
    heads_per_slab = LANES // GLA_DK
    lo = lax.rem(pl.program_id(1), heads_per_slab) * GLA_DK
    lane = lax.broadcasted_iota(jnp.int32, (GLA_C, LANES), 1)
    mine = (lane >= lo) & (lane < lo + GLA_DK)

    def own(t):
        return jnp.where(mine, t, 0.0)

    def ctx_step(j, carry):
        order = _scan_order(j, nc, 1)
        ins = [(own(kc_ref[0, rows(i), :]), vc_ref[0, rows(i), :], zsc_ref[0, rows(i), :]) for _, i in order]
        s = [s_ref[0], s_ref[1]]
        _round_robin([_gla_chunk(None, k, v, zs, wup_ref[0, d], bup_ref[0, d], s, d, False, masks[d], tris[d])
                      for (d, _), (k, v, zs) in zip(order, ins)])
        s_ref[0] = s[0]
        s_ref[1] = s[1]
        return carry

    lax.fori_loop(0, nc, ctx_step, 0)

    def lat_step(j, carry, second):
        order = _scan_order(j, n, GLA_UNROLL)
        ins = [(own(q_ref[0, rows(i), :]), own(k_ref[0, rows(i), :]), v_ref[0, rows(i), :], zs_ref[0, rows(i), :])
               for _, i in order]
        prev = [(acc_ref[rows(i), :], gg_ref[0, rows(i), :]) for _, i in order] if second else None
        s = [s_ref[0], s_ref[1]]
        outs = _round_robin([_gla_chunk(q * (GLA_DK ** -0.5), k, v, zs, wup_ref[0, d], bup_ref[0, d], s, d, True,
                                        masks[d], tris[d])
                             for (d, _), (q, k, v, zs) in zip(order, ins)])
        s_ref[0] = s[0]
        s_ref[1] = s[1]
        for idx, (_, i) in enumerate(order):
            if second:
                total = prev[idx][0] + outs[idx]
                y = total * lax.rsqrt(jnp.mean(total * total, axis=-1, keepdims=True) + EPS) * ng_ref[...]
                o_ref[0, rows(i), :] = (y * _silu(prev[idx][1])).astype(o_ref.dtype)
            else:
                acc_ref[rows(i), :] = outs[idx]
        return carry

    half = n // (2 * GLA_UNROLL)
    lax.fori_loop(0, half, functools.partial(lat_step, second=False), 0)
    lax.fori_loop(half, 2 * half, functools.partial(lat_step, second=True), 0)


def _gla(zg_x, zs_x, zg_c, zs_c, wup, bup, norm_g):
    bsz, seq, _ = zg_x.shape
    ctx = zg_c.shape[1]
    assert seq % (2 * GLA_UNROLL * GLA_C) == 0 and ctx % GLA_C == 0
    h = GLA_HEADS

    hps = LANES // GLA_DK
    qk_slabs = h // hps

    def qk(l, off):
        return pl.BlockSpec((1, l, LANES), lambda b, hd: (b, 0, off + hd // hps))

    def col(l, off):
        return pl.BlockSpec((1, l, LANES), lambda b, hd: (b, 0, off + hd))

    return pl.pallas_call(
        _gla_kernel,
        grid=(bsz, h),
        in_specs=[qk(seq, 0), qk(seq, qk_slabs), col(seq, 2 * qk_slabs), col(seq, 2 * qk_slabs + h),
                  pl.BlockSpec((1, seq, LANES), lambda b, hd: (b, 0, 0)),
                  qk(ctx, qk_slabs), col(ctx, 2 * qk_slabs),
                  pl.BlockSpec((1, ctx, LANES), lambda b, hd: (b, 0, 0)),
                  pl.BlockSpec((1, 2, LANES, LANES), lambda b, hd: (hd, 0, 0, 0)),
                  pl.BlockSpec((1, 2, 1, LANES), lambda b, hd: (hd, 0, 0, 0)),
                  pl.BlockSpec((1, LANES), lambda b, hd: (0, hd))],
        out_specs=pl.BlockSpec((1, seq, LANES), lambda b, hd: (b, 0, hd)),
        out_shape=jax.ShapeDtypeStruct((bsz, seq, h * GLA_DV), BF16),
        scratch_shapes=[pltpu.VMEM((2, LANES, LANES), F32), pltpu.VMEM((seq, LANES), F32)],
        compiler_params=_cparams(("arbitrary", "arbitrary")),
        name="gla",
    )(zg_x, zg_x, zg_x, zg_x, zs_x, zg_c, zg_c, zs_c, wup, bup, norm_g)


def _grid_conv_silu(src_ref, pad_ref, dst_ref, w_ref, b_ref, grid_w, scale):
    l = src_ref.shape[1]
    n_rows = l // grid_w
    margin = grid_w + SUBLANES
    assert pad_ref.shape[0] >= l + 2 * margin and margin % SUBLANES == 0
    pad_ref[0:margin, :] = jnp.zeros((margin, LANES), F32)
    pad_ref[margin + l:2 * margin + l, :] = jnp.zeros((margin, LANES), F32)

    def copy_row(r, carry):
        at = pl.ds(pl.multiple_of(r * grid_w, grid_w), grid_w)
        pad_ref[pl.ds(pl.multiple_of(margin + r * grid_w, SUBLANES), grid_w), :] = src_ref[0, at, :]
        return carry

    lax.fori_loop(0, n_rows, copy_row, 0, unroll=2)

    col = lax.broadcasted_iota(jnp.int32, (grid_w, LANES), 0)
    inside = {dx: (col + dx >= 0) & (col + dx < grid_w) for dx in (-1, 1)}
    rows_dy = (0,) if n_rows == 1 else (-1, 0, 1)

    def body(r, carry):
        base = pl.multiple_of(margin + r * grid_w, SUBLANES)
        acc = jnp.zeros((grid_w, LANES), F32) + b_ref[...]
        for dy in rows_dy:
            for dx in (-1, 0, 1):
                tap = (dy + 1) * 3 + (dx + 1)
                blk = pad_ref[pl.ds(base + dy * grid_w + dx, grid_w), :]
                if dx != 0:
                    blk = jnp.where(inside[dx], blk, 0.0)
                acc = acc + blk * w_ref[tap:tap + 1, :]
        dst_ref[pl.ds(pl.multiple_of(r * grid_w, grid_w), grid_w), :] = _silu(acc) * scale
        return carry

    lax.fori_loop(0, n_rows, body, 0)


def _ml_chunk(qb, k, vt, gcol, grow, state, mstate, direction, want_out, visible):
    c = k.shape[0]
    a_row = grow[direction:direction + 1, :]
    i_row = grow[2 + direction:3 + direction, :]
    a_end = a_row[:, c - 1:c] if direction == 0 else a_row[:, 0:1]
    g = a_end - a_row + i_row
    g_max = jnp.max(g, axis=-1, keepdims=True)
    head = 2 * SUBLANES
    pad_rows = jnp.zeros((LANES - head, c), BF16)
    first = lax.broadcasted_iota(jnp.int32, (head, c), 0) == 0
    kb = k.astype(BF16)
    if want_out:
        c_col = gcol[:, direction:direction + 1] - gcol[:, 2 + direction:3 + direction]
        dmat = jnp.where(visible, a_row - c_col, -jnp.inf)
        d_max = jnp.max(dmat, axis=0, keepdims=True)
        kq = jnp.dot(kb, qb, preferred_element_type=F32)
    yield
    s, m = state[direction], mstate[direction]
    m_new = jnp.maximum(a_end + m, g_max)
    decay = jnp.exp(a_end + m - m_new)
    w = jnp.exp(g - m_new)
    vw = jnp.concatenate([(vt * w).astype(BF16), jnp.where(first, w, 0.0).astype(BF16), pad_rows], axis=0)
    state[direction] = decay * s + jnp.dot(vw, kb, preferred_element_type=F32)
    mstate[direction] = m_new
    if not want_out:
        return None
    inter = a_row + m
    m_t = jnp.maximum(inter, d_max)
    w_inter = jnp.exp(inter - m_t)
    p = (kq * jnp.exp(dmat - m_t)).astype(BF16)
    vt_aug = jnp.concatenate([vt.astype(BF16), jnp.where(first, 1.0, 0.0).astype(BF16), pad_rows], axis=0)
    pv = jnp.dot(vt_aug, p, preferred_element_type=F32)
    sq = jnp.dot(s.astype(BF16), qb, preferred_element_type=F32)
    yield
    both = pv + w_inter * sq
    den = both[ML_DH:ML_DH + 1, :]
    return both[:ML_DH, :] / jnp.maximum(jnp.abs(den), jnp.exp(-m_t))


def _mlstm_kernel(q_ref, k_ref, v_ref, mo_ref, gcol_ref, grow_ref,
                  kc_ref, vc_ref, growc_ref,
                  wq_ref, wk_ref, bq_ref, bk_ref, ng_ref, o_ref,
                  cq_ref, ck_ref, ckc_ref, pad_ref, qt_ref, vt_ref, vct_ref, s_ref, m_ref, acc_ref):
    seq = q_ref.shape[1]
    ctx = kc_ref.shape[1]
    n = seq // ML_C
    nc = ctx // ML_C
    _grid_conv_silu(q_ref, pad_ref, cq_ref, wq_ref, bq_ref, GRID_W, 1.0)
    _grid_conv_silu(k_ref, pad_ref, ck_ref, wk_ref, bk_ref, GRID_W, ML_DH ** -0.5)
    _grid_conv_silu(kc_ref, pad_ref, ckc_ref, wk_ref, bk_ref, ctx, ML_DH ** -0.5)
    s_ref[...] = jnp.zeros_like(s_ref)
    m_ref[...] = jnp.zeros_like(m_ref)

    def rows(i):
        return pl.ds(pl.multiple_of(i * ML_C, ML_C), ML_C)

    def transpose_chunks(i, carry):
        qt_ref[:, rows(i)] = cq_ref[rows(i), :].T.astype(qt_ref.dtype)
        vt_ref[:, rows(i)] = v_ref[0, rows(i), :].T
        return carry

    lax.fori_loop(0, n, transpose_chunks, 0, unroll=2)
    for i in range(nc):
        vct_ref[:, i * ML_C:(i + 1) * ML_C] = vc_ref[0, i * ML_C:(i + 1) * ML_C, :].T

    def load_state():
        return [s_ref[0], s_ref[1]], [m_ref[0, :, 0:1], m_ref[1, :, 0:1]]

    def store_state(s, m):
        for d in range(2):
            s_ref[d] = s[d]
            m_ref[d] = jnp.broadcast_to(m[d], m_ref.shape[1:])

    masks, _ = _visibility(ML_C)
    visible = [masks[1], masks[0]]

    def ctx_step(j, carry):
        order = _scan_order(j, nc, 1)
        ins = [(ckc_ref[rows(i), :], vct_ref[:, rows(i)], growc_ref[0, 0, :, rows(i)]) for _, i in order]
        s, m = load_state()
        _round_robin([_ml_chunk(None, k, vt, None, grow, s, m, d, False, None)
                      for (d, _), (k, vt, grow) in zip(order, ins)])
        store_state(s, m)
        return carry

    lax.fori_loop(0, nc, ctx_step, 0)

    def lat_step(j, carry, second):
        order = _scan_order(j, n, SCAN_UNROLL)
        ins = [(qt_ref[:, rows(i)], ck_ref[rows(i), :], vt_ref[:, rows(i)], gcol_ref[0, rows(i), :],
                grow_ref[0, 0, :, rows(i)]) for _, i in order]
        prev = [(acc_ref[:, rows(i)], mo_ref[0, rows(i), :]) for _, i in order] if second else None
        s, m = load_state()
        outs = _round_robin([_ml_chunk(qb, k, vt, gcol, grow, s, m, d, True, visible[d])
                             for (d, _), (qb, k, vt, gcol, grow) in zip(order, ins)])
        store_state(s, m)
        for idx, (_, i) in enumerate(order):
            if second:
                total = prev[idx][0] + outs[idx]
                y = total * lax.rsqrt(jnp.mean(total * total, axis=0, keepdims=True) + EPS) * ng_ref[...]
                o_ref[0, rows(i), :] = (_sigmoid(prev[idx][1]) * y.T).astype(o_ref.dtype)
            else:
                acc_ref[:, rows(i)] = outs[idx]
        return carry

    half = n // (2 * SCAN_UNROLL)
    lax.fori_loop(0, half, functools.partial(lat_step, second=False), 0)
    lax.fori_loop(half, 2 * half, functools.partial(lat_step, second=True), 0)


def _mlstm(zm_x, gcol_x, grow_x, zm_c, grow_c, conv_w, conv_b, norm_g):
    bsz, seq, _ = zm_x.shape
    ctx = zm_c.shape[1]
    assert seq % (2 * SCAN_UNROLL * ML_C) == 0 and ctx % ML_C == 0 and seq % GRID_W == 0
    h = ML_HEADS

    def col(l, off):
        return pl.BlockSpec((1, l, LANES), lambda b, hd: (b, 0, off + hd))

    def gates(l):
        return [pl.BlockSpec((1, l, LANES), lambda b, hd: (b, 0, hd)),
                pl.BlockSpec((1, 1, 8, l), lambda b, hd: (b, hd, 0, 0))]

    return pl.pallas_call(
        _mlstm_kernel,
        grid=(bsz, h),
        in_specs=[col(seq, 0), col(seq, h), col(seq, 2 * h), col(seq, 3 * h)] + gates(seq)
                 + [col(ctx, h), col(ctx, 2 * h), gates(ctx)[1]]
                 + [pl.BlockSpec((9, LANES), lambda b, hd: (0, hd)),
                    pl.BlockSpec((9, LANES), lambda b, hd: (0, h + hd)),
                    pl.BlockSpec((1, LANES), lambda b, hd: (0, hd)),
                    pl.BlockSpec((1, LANES), lambda b, hd: (0, h + hd)),
                    pl.BlockSpec((LANES, 1), lambda b, hd: (hd, 0))],
        out_specs=pl.BlockSpec((1, seq, LANES), lambda b, hd: (b, 0, hd)),
        out_shape=jax.ShapeDtypeStruct((bsz, seq, h * ML_DH), BF16),
        scratch_shapes=[pltpu.VMEM((seq, LANES), F32), pltpu.VMEM((seq, LANES), F32),
                        pltpu.VMEM((ctx, LANES), F32),
                        pltpu.VMEM((max(seq + 2 * (GRID_W + SUBLANES), 3 * ctx + 2 * SUBLANES), LANES), F32),
                        pltpu.VMEM((LANES, seq), BF16), pltpu.VMEM((LANES, seq), F32),
                        pltpu.VMEM((LANES, ctx), F32),
                        pltpu.VMEM((2, 2 * LANES, LANES), F32), pltpu.VMEM((2, 1, LANES), F32),
                        pltpu.VMEM((LANES, seq), F32)],
        compiler_params=_cparams(("arbitrary", "arbitrary")),
        name="mlstm",
    )(zm_x, zm_x, zm_x, zm_x, gcol_x, grow_x, zm_c, zm_c, grow_c,
      conv_w, conv_w, conv_b, conv_b, norm_g.reshape(-1, 1))


_G0 = 0
_E0 = N_GROUPS
RANK_BITS = 16
RANK_SPAN = 1 << RANK_BITS
ROW_GROUP = 64
GATHER_RING = 3


SUBLANES = 8


def _store_token_tiles(ref2d, val):
    n, w = val.shape
    k = w // LANES
    for c in range(k):
        ref2d[pl.ds(c, n, stride=k), :] = val[:, c * LANES:(c + 1) * LANES]


def _load_token_tiles(ref2d, first, n, k, step):
    return jnp.concatenate([ref2d[pl.ds(first + c, n, stride=step), :] for c in range(k)], axis=1)


def _outproj_kernel(x_ref, ga_ref, ml_ref, mod_ref, wa_ref, wb_ref, g2_ref, wrh_ref, wrl_ref, br_ref,
                    x1_ref, h2_ref, ri_ref, rw_ref, cnt_ref, base_ref):
    tm = x_ref.shape[1]

    @pl.when((pl.program_id(0) == 0) & (pl.program_id(1) == 0))
    def _():
        base_ref[...] = jnp.zeros_like(base_ref)

    mix = (jnp.dot(ga_ref[0], wa_ref[...], preferred_element_type=F32)
           + jnp.dot(ml_ref[0], wb_ref[...], preferred_element_type=F32))
    x1 = x_ref[0] + mod_ref[0, 2:3, :] * mix
    x1_ref[0] = x1
    y = x1 * lax.rsqrt(jnp.mean(x1 * x1, axis=-1, keepdims=True) + EPS) * g2_ref[...]
    h2 = y * (1.0 + mod_ref[0, 4:5, :]) + mod_ref[0, 3:4, :]
    _store_token_tiles(h2_ref, h2)

    h_hi = h2.astype(BF16)
    h_lo = (h2 - h_hi.astype(F32)).astype(BF16)
    logits = (jnp.dot(h_hi, wrh_ref[...], preferred_element_type=F32)
              + jnp.dot(h_lo, wrh_ref[...], preferred_element_type=F32)
              + jnp.dot(h_hi, wrl_ref[...], preferred_element_type=F32)) + br_ref[...]

    lane = lax.broadcasted_iota(jnp.int32, (tm, LANES), 1).astype(F32)
    neg = -jnp.inf
    big = float(LANES)
    is_g = lane < float(_E0)
    lg = jnp.where(is_g, logits, neg)
    gmax = jnp.max(lg, axis=-1, keepdims=True)
    gidx = jnp.min(jnp.where(lg == gmax, lane, big), axis=-1, keepdims=True)
    gw = 1.0 / jnp.sum(jnp.where(is_g, jnp.exp(logits - gmax), 0.0), axis=-1, keepdims=True)
    lo = float(_E0) + float(EXP_PER_GROUP) * gidx
    le = jnp.where((lane >= lo) & (lane < lo + float(EXP_PER_GROUP)), logits, neg)
    v1 = jnp.max(le, axis=-1, keepdims=True)
    i1 = jnp.min(jnp.where(le == v1, lane, big), axis=-1, keepdims=True)
    le2 = jnp.where(lane == i1, neg, le)
    v2 = jnp.max(le2, axis=-1, keepdims=True)
    i2 = jnp.min(jnp.where(le2 == v2, lane, big), axis=-1, keepdims=True)
    t = jnp.exp(v2 - v1)
    w1 = gw / (1.0 + t)
    w2 = gw * t / (1.0 + t)
    e1 = i1 - float(_E0)
    e2 = i2 - float(_E0)

    oh1 = lane == e1
    oh2 = lane == e2
    oh = jnp.where(oh1 | oh2, 1.0, 0.0)
    r = lax.broadcasted_iota(jnp.int32, (tm, tm), 0)
    c = lax.broadcasted_iota(jnp.int32, (tm, tm), 1)
    strict = jnp.where(c < r, 1.0, 0.0).astype(BF16)
    before = jnp.dot(strict, oh.astype(BF16), preferred_element_type=F32) + base_ref[...]
    rank1 = jnp.sum(jnp.where(oh1, before, 0.0), axis=-1, keepdims=True)
    rank2 = jnp.sum(jnp.where(oh2, before, 0.0), axis=-1, keepdims=True)
    total = base_ref[...] + jnp.sum(oh, axis=0, keepdims=True)
    base_ref[...] = total
    cnt_ref[...] = total

    ids = jnp.where(lane == 0.0, e1 * float(RANK_SPAN) + rank1,
                    jnp.where(lane == 1.0, e2 * float(RANK_SPAN) + rank2, 0.0))
    ri_ref[...] = ids.astype(jnp.int32)
    rw_ref[...] = jnp.where(lane == 0.0, w1, jnp.where(lane == 1.0, w2, 0.0))


def _outproj(x, gla_o, ml_o, mods, wa, wb, g2, wrh, wrl, br, tm):
    bsz, seq, d = x.shape
    const = lambda shape: pl.BlockSpec(shape, lambda b, i: (0,) * len(shape))
    tile = lambda w: pl.BlockSpec((1, tm, w), lambda b, i: (b, i, 0))
    flat = lambda rows: pl.BlockSpec((rows, LANES), lambda b, i: (b * (seq // tm) + i, 0))
    return pl.pallas_call(
        _outproj_kernel,
        grid=(bsz, seq // tm),
        in_specs=[tile(d), tile(gla_o.shape[2]), tile(ml_o.shape[2]),
                  pl.BlockSpec((1, N_MOD, d), lambda b, i: (b, 0, 0)),
                  const(wa.shape), const(wb.shape), const((1, d)),
                  const(wrh.shape), const(wrl.shape), const((1, LANES))],
        out_specs=[tile(d),
                   pl.BlockSpec((tm * d // LANES, LANES), lambda b, i: (b * (seq // tm) + i, 0)),
                   flat(tm), flat(tm), const((1, LANES))],
        out_shape=[jax.ShapeDtypeStruct((bsz, seq, d), F32),
                   jax.ShapeDtypeStruct((bsz * seq * d // LANES, LANES), F32),
                   jax.ShapeDtypeStruct((bsz * seq, LANES), jnp.int32),
                   jax.ShapeDtypeStruct((bsz * seq, LANES), F32),
                   jax.ShapeDtypeStruct((1, LANES), F32)],
        scratch_shapes=[pltpu.VMEM((1, LANES), F32)],
        compiler_params=_cparams(("arbitrary", "arbitrary")),
        name="outproj",
    )(x, gla_o, ml_o, mods, wa, wb, g2, wrh, wrl, br)


def _experts_kernel(dest_ref, ps_ref, cnt_ref, be_ref, nv_ref, meta_ref, h_hbm, w1_ref, w2_ref, ytok_hbm,
                    src_ref, xbuf, ybuf, w1c_ref, w2c_ref, gsem, ssem):
    i = pl.program_id(0)
    n_steps = pl.num_programs(0)
    n_used = meta_ref[0]
    tr = SUBLANES
    n_tok = h_hbm.shape[0] // tr
    n_rows = src_ref.shape[0]
    blk = xbuf.shape[1] // tr
    n_x = xbuf.shape[0]
    slot = lax.rem(i, 2)
    xslot = lax.rem(i, n_x)

    def slab(j):
        return pl.ds(pl.multiple_of(j * tr, tr), tr)

    def group_rows(buf, s, g):
        span = ROW_GROUP * tr
        return buf.at[s, pl.ds(pl.multiple_of(g * span, span), span), :]

    def gather_copy(tok, s, g, u):
        return pltpu.make_async_copy(h_hbm.at[slab(tok), :], group_rows(xbuf, s, g).at[pl.ds(u * tr, tr), :],
                                     gsem.at[s])

    def scatter_copy(a, s, g, u):
        return pltpu.make_async_copy(group_rows(ybuf, s, g).at[pl.ds(u * tr, tr), :], ytok_hbm.at[slab(a), :],
                                     ssem.at[s])

    def groups(b):
        nv = jnp.where(b < n_steps, nv_ref[jnp.minimum(b, n_steps - 1)], 0)
        return lax.shift_right_logical(nv + (ROW_GROUP - 1), ROW_GROUP.bit_length() - 1)

    def rows_loop(b, body):
        def step(g, c):
            for u in range(ROW_GROUP):
                body(g, u)
            return c
        lax.fori_loop(0, groups(b), step, 0)

    def issue_gather(b, s):
        def one(g, u):
            tok = lax.shift_right_logical(src_ref[b * blk + g * ROW_GROUP + u], 1)
            gather_copy(jnp.minimum(tok, n_tok - 1), s, g, u).start(priority=u % 2)
        rows_loop(b, one)

    def wait_rows(b, copy):
        n_groups = groups(b)
        bit = blk // ROW_GROUP
        while bit:
            @pl.when((n_groups & bit) != 0)
            def _():
                for _ in range(bit * ROW_GROUP):
                    copy.wait()
            bit //= 2

    def wait_gather(b, s):
        wait_rows(b, gather_copy(0, s, 0, 0))

    def wait_scatter(b, s):
        wait_rows(b, scatter_copy(0, s, 0, 0))

    @pl.when(i == 0)
    def _():
        xbuf[...] = jnp.zeros_like(xbuf)
        ybuf[...] = jnp.zeros_like(ybuf)
        for s in range(2):
            tail = ytok_hbm.at[pl.ds((2 * n_tok + s * blk) * tr, blk * tr), :]
            cp = pltpu.make_async_copy(ybuf.at[s], tail, ssem.at[s])
            cp.start()
            cp.wait()

        def put(a, c):
            src_ref[dest_ref[a]] = a
            return c
        lax.fori_loop(0, 2 * n_tok, put, 0, unroll=16)

        def pad(j, c):
            src_ref[j] = 2 * n_tok + (j & (2 * blk - 1))
            return c

        def pad_expert(e, c):
            lax.fori_loop(ps_ref[e] + cnt_ref[e], ps_ref[e + 1], pad, 0)
            return c
        lax.fori_loop(0, cnt_ref.shape[0], pad_expert, 0)
        for b in range(n_x - 1):
            issue_gather(min(b, n_rows // blk - 1), b)

    @pl.when(i < n_used)
    def _():
        wait_gather(i, xslot)

        @pl.when((i == 0) | (be_ref[i] != be_ref[jnp.maximum(i - 1, 0)]))
        def _():
            w1c_ref[...] = w1_ref[0].astype(BF16)
            w2c_ref[...] = w2_ref[0].astype(BF16)

        @pl.when(i >= 2)
        def _():
            wait_scatter(i - 2, slot)

        row = lax.broadcasted_iota(jnp.int32, (blk, 1), 0)
        x = _load_token_tiles(xbuf.at[xslot], 0, blk, tr, tr)
        x = jnp.where(row < nv_ref[i], x, 0.0).astype(BF16)
        h = jnp.dot(x, w1c_ref[...], preferred_element_type=F32)
        a = (_silu(h[:, :D_EXPERT]) * h[:, D_EXPERT:]).astype(BF16)
        _store_token_tiles(ybuf.at[slot], jnp.dot(a, w2c_ref[...], preferred_element_type=F32))
        rows_loop(i, lambda g, u: scatter_copy(src_ref[i * blk + g * ROW_GROUP + u], slot, g, u)
                  .start(priority=u % 2))
        issue_gather(i + n_x - 1, lax.rem(i + n_x - 1, n_x))

    @pl.when(i == n_steps - 1)
    def _():
        for back in (2, 1):
            wait_scatter(n_used - back, lax.rem(n_used - back, 2))


def _experts(dest, pad_start, counts, block_e, block_nv, meta, h2, w_in, w_out, nb):
    d = w_in.shape[1]
    tr = d // LANES
    assert tr == SUBLANES, "a token row must fill exactly one (8, 128) tile"
    n_tok = h2.shape[0] // tr
    de2 = w_in.shape[2]
    n_rows = nb * MOE_BLK
    assert 2 * n_tok >= 2 * MOE_BLK
    assert MOE_BLK & (MOE_BLK - 1) == 0
    wmap = lambda i, pk, ps, cnt, be, nv, meta: (be[i], 0, 0)
    return pl.pallas_call(
        _experts_kernel,
        grid_spec=pltpu.PrefetchScalarGridSpec(
            num_scalar_prefetch=6, grid=(nb,),
            in_specs=[pl.BlockSpec(memory_space=pl.ANY),
                      pl.BlockSpec((1, d, de2), wmap),
                      pl.BlockSpec((1, de2 // 2, d), wmap)],
            out_specs=pl.BlockSpec(memory_space=pl.ANY),
            scratch_shapes=[pltpu.SMEM((n_rows,), jnp.int32),
                            pltpu.VMEM((GATHER_RING, MOE_BLK * tr, LANES), F32),
                            pltpu.VMEM((2, MOE_BLK * tr, LANES), F32),
                            pltpu.VMEM((d, de2), BF16), pltpu.VMEM((de2 // 2, d), BF16),
                            pltpu.SemaphoreType.DMA((GATHER_RING,)), pltpu.SemaphoreType.DMA((2,))]),
        out_shape=jax.ShapeDtypeStruct(((2 * n_tok + 2 * MOE_BLK) * tr, LANES), F32),
        compiler_params=_cparams(("arbitrary",)),
        name="experts",
    )(dest, pad_start, counts, block_e, block_nv, meta, h2, w_in, w_out)


def _combine_kernel(x1_ref, y_ref, rw_ref, mod_ref, fg_ref, o_ref):
    tc, d = x1_ref.shape
    tr = d // LANES
    y1 = _load_token_tiles(y_ref, 0, tc, tr, 2 * tr)
    y2 = _load_token_tiles(y_ref, tr, tc, tr, 2 * tr)
    moe = rw_ref[:, 0:1] * y1 + rw_ref[:, 1:2] * y2
    x2 = x1_ref[...] + mod_ref[0, 5:6, :] * moe
    o_ref[...] = x2 * lax.rsqrt(jnp.mean(x2 * x2, axis=-1, keepdims=True) + EPS) * fg_ref[...]


def _combine(x1, ytok, rw, mods, fg, tokens_per_batch, tc):
    n_tok, d = x1.shape
    tiles_per_batch = tokens_per_batch // tc
    return pl.pallas_call(
        _combine_kernel,
        grid=(n_tok // tc,),
        in_specs=[pl.BlockSpec((tc, d), lambda i: (i, 0)),
                  pl.BlockSpec((2 * tc * d // LANES, LANES), lambda i: (i, 0)),
                  pl.BlockSpec((tc, LANES), lambda i: (i, 0)),
                  pl.BlockSpec((1, N_MOD, d), lambda i: (i // tiles_per_batch, 0, 0)),
                  pl.BlockSpec((1, d), lambda i: (0, 0))],
        out_specs=pl.BlockSpec((tc, d), lambda i: (i, 0)),
        out_shape=jax.ShapeDtypeStruct((n_tok, d), F32),
        compiler_params=_cparams(("arbitrary",)),
        name="combine",
    )(x1, ytok, rw, mods, fg)


def _prep_inproj_weights(w_in, gla_up_w, gla_up_b, ml_i_b, ml_f_b):
    d = w_in.shape[0]
    o_gq, o_gk, o_gv, o_gg = 0, GLA_QK_W, 2 * GLA_QK_W, 2 * GLA_QK_W + GLA_V_W
    o_lr = o_gg + GLA_V_W
    o_mqk = o_lr + 2 * GLA_LR
    o_mi = o_mqk + 4 * ML_W
    o_mf = o_mi + 2 * ML_HEADS

    wg = w_in[:, o_gq:o_lr]
    wm = w_in[:, o_mqk:o_mi]
    ws = jnp.concatenate([w_in[:, o_lr:o_mqk], w_in[:, o_mi:o_mf + 2 * ML_HEADS],
                          jnp.zeros((d, LANES - 2 * GLA_LR - 4 * ML_HEADS), w_in.dtype)], axis=1)
    bias = jnp.zeros((LANES,), F32)
    bias = bias.at[_MI0:_MI0 + 2 * ML_HEADS].set(ml_i_b.reshape(-1))
    bias = bias.at[_MF0:_MF0 + 2 * ML_HEADS].set(ml_f_b.reshape(-1))
    up = gla_up_w.reshape(2, GLA_LR, GLA_HEADS, GLA_DK).transpose(2, 0, 1, 3)
    ub = gla_up_b.reshape(2, GLA_HEADS, GLA_DK).transpose(1, 0, 2)
    wup = jnp.zeros((GLA_HEADS, 2, LANES, LANES), F32)
    bup = jnp.zeros((GLA_HEADS, 2, 1, LANES), F32)
    for hd in range(GLA_HEADS):
        lo = (hd % (LANES // GLA_DK)) * GLA_DK
        for dr in range(2):
            wup = wup.at[hd, dr, dr * GLA_LR:(dr + 1) * GLA_LR, lo:lo + GLA_DK].set(up[hd, dr])
        bup = bup.at[hd, :, 0, lo:lo + GLA_DK].set(ub[hd])
    return (wg.astype(BF16), wm.astype(BF16), ws.astype(BF16), ws.T.astype(BF16),
            bias.reshape(1, LANES), bias.reshape(LANES, 1), wup.astype(BF16), bup)


def _layer(x, ctx, mods, norm1_g, w_in, gla_up_w, gla_up_b, gla_norm_g, ml_conv_w, ml_conv_b,
           ml_i_b, ml_f_b, ml_norm_g, w_out, norm2_g, rg_w, rg_b, re_w, re_b, e_w_in, e_w_out, final_g):
    bsz, seq, d = x.shape
    n_tok = bsz * seq
    wg, wm, ws, wst, bcol, brow, wup, bup = _prep_inproj_weights(w_in, gla_up_w, gla_up_b, ml_i_b, ml_f_b)
    g1 = norm1_g.reshape(1, d)
    zg_x, zm_x, zs_x, gcol_x, grow_x = _inproj(x, mods, lambda b: b, g1, wg, wm, ws, wst, bcol, brow, 256)
    zg_c, zm_c, zs_c, gcol_c, grow_c = _inproj(ctx, mods, lambda b: bsz, g1, wg, wm, ws, wst, bcol, brow,
                                               min(256, ctx.shape[1]))
    gla_o = _gla(zg_x, zs_x, zg_c, zs_c, wup, bup, gla_norm_g.reshape(1, -1))
    ml_o = _mlstm(zm_x, gcol_x, grow_x, zm_c, grow_c,
                  ml_conv_w.reshape(9, -1), ml_conv_b.reshape(1, -1), ml_norm_g.reshape(1, -1))

    wr = jnp.zeros((d, LANES), F32).at[:, _G0:_E0].set(rg_w).at[:, _E0:_E0 + N_EXPERTS].set(re_w)
    br = jnp.zeros((1, LANES), F32).at[0, _G0:_E0].set(rg_b).at[0, _E0:_E0 + N_EXPERTS].set(re_b)
    wrh = wr.astype(BF16)
    wrl = (wr - wrh.astype(F32)).astype(BF16)
    x1, h2, ri, rw, cnt = _outproj(x, gla_o, ml_o, mods, w_out[:GLA_V_W].astype(BF16),
                                   w_out[GLA_V_W:].astype(BF16), norm2_g.reshape(1, d), wrh, wrl, br, 256)

    counts = cnt[0, :N_EXPERTS].astype(jnp.int32)
    nblk = (counts + MOE_BLK - 1) // MOE_BLK
    blk_end = jnp.cumsum(nblk)
    blk_start = blk_end - nblk
    n_used = blk_end[-1]
    nb_max = (2 * n_tok) // MOE_BLK + N_EXPERTS
    blk = jnp.arange(nb_max, dtype=jnp.int32)
    blk_c = jnp.minimum(blk, n_used - 1)
    onehot = (blk_c[:, None] >= blk_start[None, :]) & (blk_c[:, None] < blk_end[None, :])
    pick = lambda v: jnp.sum(jnp.where(onehot, v[None, :], 0), axis=1)
    block_e = pick(jnp.arange(N_EXPERTS, dtype=jnp.int32)).astype(jnp.int32)
    block_nv = jnp.clip(pick(counts) - (blk_c - pick(blk_start)) * MOE_BLK, 0, MOE_BLK)
    block_nv = jnp.where(blk < n_used, block_nv, 0).astype(jnp.int32)
    pad_start = (jnp.concatenate([blk_start, blk_end[-1:]]) * MOE_BLK).astype(jnp.int32)
    packed = ri[:, 0:2].reshape(-1)
    e_of = lax.shift_right_logical(packed, RANK_BITS)
    start_of = jnp.sum(jnp.where(e_of[:, None] == jnp.arange(N_EXPERTS, dtype=jnp.int32)[None, :],
                                 pad_start[None, :N_EXPERTS], 0), axis=1)
    dest = (start_of + (packed & (RANK_SPAN - 1))).astype(jnp.int32)
    meta = jnp.stack([n_used, n_used]).astype(jnp.int32)

    ytok = _experts(dest, pad_start, counts, block_e, block_nv, meta, h2, e_w_in, e_w_out, nb_max)
    out = _combine(x1.reshape(n_tok, d), ytok, rw, mods, final_g.reshape(1, d), seq, 256)
    return out.reshape(bsz, seq, d)


def kernel(x, c, ctx, c_ctx, ada_w, ada_b, norm1_g, w_in, gla_up_w, gla_up_b, gla_norm_g, ml_conv_w, ml_conv_b,
           ml_i_b, ml_f_b, ml_norm_g, w_out, norm2_g, router_group_w, router_group_b, router_expert_w,
           router_expert_b, expert_w_in, expert_w_out, final_norm_g):
    assert ada_w.shape[0] == 1, "single-layer stack"
    bsz, d = c.shape
    cc = jnp.concatenate([c, c_ctx[None, :], jnp.zeros((8 - bsz - 1, d), F32)], axis=0)
    mods = _modulation(cc, ada_w[0], ada_b[0]).reshape(8, N_MOD, d)
    return _layer(x, ctx, mods, norm1_g[0], w_in[0], gla_up_w[0], gla_up_b[0], gla_norm_g[0],
                  ml_conv_w[0], ml_conv_b[0], ml_i_b[0], ml_f_b[0], ml_norm_g[0], w_out[0], norm2_g[0],
                  router_group_w[0], router_group_b[0], router_expert_w[0], router_expert_b[0],
                  expert_w_in[0], expert_w_out[0], final_norm_g)
```

```python
import functools

import jax
import jax.numpy as jnp
from jax import lax
from jax.experimental import pallas as pl
from jax.experimental.pallas import tpu as pltpu

F32 = jnp.float32
BF16 = jnp.bfloat16

D_MODEL = 1024
GRID_W = 64
N_MOD = 6
EPS = 1e-6

GLA_HEADS = 4
GLA_DK = 64
GLA_DV = 128
GLA_LR = 16
GLA_TAU = 16.0
GLA_C = 128
GLA_UNROLL = 8
SCAN_UNROLL = 4

ML_HEADS = 4
ML_DH = 128
ML_C = 128

N_GROUPS = 4
EXP_PER_GROUP = 8
N_EXPERTS = N_GROUPS * EXP_PER_GROUP
D_EXPERT = 512
MOE_BLK = 256

GLA_QK_W = GLA_HEADS * GLA_DK
GLA_V_W = GLA_HEADS * GLA_DV
ML_W = ML_HEADS * ML_DH
LANES = 128
VMEM_LIMIT = 56 * 1024 * 1024

_LR0 = 0
_MI0 = 2 * GLA_LR
_MF0 = _MI0 + 2 * ML_HEADS


def _cparams(sem):
    return pltpu.CompilerParams(dimension_semantics=sem, vmem_limit_bytes=VMEM_LIMIT)


def _sigmoid(x):
    return 1.0 / (1.0 + jnp.exp(-x))


def _silu(x):
    return x * _sigmoid(x)


def _log_sigmoid(x):
    return jnp.minimum(x, 0.0) - jnp.log1p(jnp.exp(-jnp.abs(x)))


def _split_dot(a_bf16_exact, x, dims=None):
    x_hi = x.astype(BF16)
    x_lo = (x - x_hi.astype(F32)).astype(BF16)
    if dims is None:
        f = lambda u: jnp.dot(a_bf16_exact, u, preferred_element_type=F32)
    else:
        f = lambda u: lax.dot_general(u, a_bf16_exact, dims, preferred_element_type=F32)
    return f(x_hi) + f(x_lo)


def _mod_kernel(c_ref, w_ref, b_ref, o_ref):
    c = c_ref[...]
    s = _silu(c).astype(BF16)
    o_ref[...] = jnp.dot(s, w_ref[...].astype(BF16), preferred_element_type=F32) + b_ref[...]


def _modulation(cc, ada_w, ada_b):
    rows, d = cc.shape
    n = ada_w.shape[1]
    tn = 1536
    return pl.pallas_call(
        _mod_kernel,
        grid=(n // tn,),
        in_specs=[pl.BlockSpec((rows, d), lambda j: (0, 0)),
                  pl.BlockSpec((d, tn), lambda j: (0, j)),
                  pl.BlockSpec((1, tn), lambda j: (0, j))],
        out_specs=pl.BlockSpec((rows, tn), lambda j: (0, j)),
        out_shape=jax.ShapeDtypeStruct((rows, n), F32),
        compiler_params=_cparams(("arbitrary",)),
        name="mod",
    )(cc, ada_w, ada_b.reshape(1, n))


def _inproj_kernel(x_ref, mod_ref, g_ref, wg_ref, wm_ref, ws_ref, wst_ref, bcol_ref, brow_ref,
                   zg_ref, zm_ref, zs_ref, gcol_ref, grow_ref):
    tm = x_ref.shape[1]
    x = x_ref[0]
    y = x * lax.rsqrt(jnp.mean(x * x, axis=-1, keepdims=True) + EPS) * g_ref[...]
    h = (y * (1.0 + mod_ref[0, 1:2, :]) + mod_ref[0, 0:1, :]).astype(BF16)
    zg_ref[0] = jnp.dot(h, wg_ref[...], preferred_element_type=F32)
    zm_ref[0] = jnp.dot(h, wm_ref[...], preferred_element_type=F32)
    zs = jnp.dot(h, ws_ref[...], preferred_element_type=F32) + bcol_ref[...]
    zst = lax.dot_general(wst_ref[...], h, (((1,), (1,)), ((), ())),
                          preferred_element_type=F32) + brow_ref[...]
    zs_ref[0] = zs

    r = lax.broadcasted_iota(jnp.int32, (tm, tm), 0)
    c = lax.broadcasted_iota(jnp.int32, (tm, tm), 1)
    shift = ML_C.bit_length() - 1
    same = jnp.right_shift(r, shift) == jnp.right_shift(c, shift)
    lower = jnp.where(same & (c <= r), 1.0, 0.0).astype(BF16)
    upper = jnp.where(same & (c >= r), 1.0, 0.0).astype(BF16)
    chunks = range(0, tm, ML_C)
    lsf = _log_sigmoid(zs)
    a_pre = _split_dot(lower, lsf)
    tot = jnp.concatenate([jnp.broadcast_to(a_pre[o + ML_C - 1:o + ML_C, :], (ML_C, LANES)) for o in chunks], axis=0)
    a_suf = tot - a_pre + lsf
    lsft = _log_sigmoid(zst)
    a_pre_t = _split_dot(upper, lsft, (((1,), (0,)), ((), ())))
    tot_t = jnp.concatenate([jnp.broadcast_to(a_pre_t[:, o + ML_C - 1:o + ML_C], (LANES, ML_C)) for o in chunks], axis=1)
    a_suf_t = tot_t - a_pre_t + lsft

    lane = lax.broadcasted_iota(jnp.int32, (tm, LANES), 1)
    for hd in range(ML_HEADS):
        cols = (a_pre[:, _MF0 + hd:_MF0 + hd + 1],
                a_suf[:, _MF0 + ML_HEADS + hd:_MF0 + ML_HEADS + hd + 1],
                zs[:, _MI0 + hd:_MI0 + hd + 1],
                zs[:, _MI0 + ML_HEADS + hd:_MI0 + ML_HEADS + hd + 1])
        slab = jnp.zeros((tm, LANES), F32)
        for j, col in enumerate(cols):
            slab = jnp.where(lane == j, col, slab)
        gcol_ref[0, :, hd * LANES:(hd + 1) * LANES] = slab
        rows = (a_pre_t[_MF0 + hd:_MF0 + hd + 1, :],
                a_suf_t[_MF0 + ML_HEADS + hd:_MF0 + ML_HEADS + hd + 1, :],
                zst[_MI0 + hd:_MI0 + hd + 1, :],
                zst[_MI0 + ML_HEADS + hd:_MI0 + ML_HEADS + hd + 1, :])
        for j, row in enumerate(rows):
            grow_ref[0, hd, j:j + 1, :] = row
        grow_ref[0, hd, 4:8, :] = jnp.zeros((4, tm), F32)


def _inproj(x, mods, mod_row_of_batch, norm_g, wg, wm, ws, wst, bcol, brow, tm):
    bsz, l, d = x.shape
    assert l % tm == 0 and tm % ML_C == 0
    const = lambda shape: pl.BlockSpec(shape, lambda b, i: (0,) * len(shape))
    return pl.pallas_call(
        _inproj_kernel,
        grid=(bsz, l // tm),
        in_specs=[pl.BlockSpec((1, tm, d), lambda b, i: (b, i, 0)),
                  pl.BlockSpec((1, N_MOD, d), lambda b, i: (mod_row_of_batch(b), 0, 0)),
                  const((1, d)), const(wg.shape), const(wm.shape), const(ws.shape), const(wst.shape),
                  const((1, LANES)), const((LANES, 1))],
        out_specs=[pl.BlockSpec((1, tm, wg.shape[1]), lambda b, i: (b, i, 0)),
                   pl.BlockSpec((1, tm, wm.shape[1]), lambda b, i: (b, i, 0)),
                   pl.BlockSpec((1, tm, LANES), lambda b, i: (b, i, 0)),
                   pl.BlockSpec((1, tm, ML_HEADS * LANES), lambda b, i: (b, i, 0)),
                   pl.BlockSpec((1, ML_HEADS, 8, tm), lambda b, i: (b, 0, 0, i))],
        out_shape=[jax.ShapeDtypeStruct((bsz, l, wg.shape[1]), F32),
                   jax.ShapeDtypeStruct((bsz, l, wm.shape[1]), F32),
                   jax.ShapeDtypeStruct((bsz, l, LANES), F32),
                   jax.ShapeDtypeStruct((bsz, l, ML_HEADS * LANES), F32),
                   jax.ShapeDtypeStruct((bsz, ML_HEADS, 8, l), F32)],
        compiler_params=_cparams(("arbitrary", "arbitrary")),
        name="inproj",
    )(x, mods, norm_g, wg, wm, ws, wst, bcol, brow)


def _round_robin(chains):
    results = [None] * len(chains)
    live = list(enumerate(chains))
    while live:
        still = []
        for idx, chain in live:
            try:
                next(chain)
                still.append((idx, chain))
            except StopIteration as done:
                results[idx] = done.value
        live = still
    return results


def _visibility(c):
    r = lax.broadcasted_iota(jnp.int32, (c, c), 0)
    cc = lax.broadcasted_iota(jnp.int32, (c, c), 1)
    masks = [cc <= r, cc >= r]
    return masks, [jnp.where(m, 1.0, 0.0).astype(BF16) for m in masks]


def _gla_chunk(q, k, v, zs, wup, bup, state, direction, want_out, causal, tri):
    c = k.shape[0]
    logits = jnp.dot(zs.astype(BF16), wup, preferred_element_type=F32) + bup
    yield
    g = _log_sigmoid(logits) * (1.0 / GLA_TAU)
    b = _split_dot(tri, g)
    yield
    b_end = b[c - 1:c, :] if direction == 0 else b[0:1, :]
    kd = (k * jnp.exp(b_end - b)).astype(BF16)
    upd = lax.dot_general(v.astype(BF16), kd, (((0,), (0,)), ((), ())), preferred_element_type=F32)
    s = state[direction]
    state[direction] = jnp.exp(b_end) * s + upd
    if not want_out:
        return None
    b_mid = b[c // 2:c // 2 + 1, :]
    q_in = (q * jnp.exp(b - b_mid)).astype(BF16)
    k_in = (k * jnp.exp(b_mid - b)).astype(BF16)
    att = lax.dot_general(q_in, k_in, (((1,), (1,)), ((), ())), preferred_element_type=F32)
    inter = lax.dot_general((q * jnp.exp(b)).astype(BF16), s.astype(BF16),
                            (((1,), (1,)), ((), ())), preferred_element_type=F32)
    yield
    att = jnp.where(causal, att, 0.0)
    return jnp.dot(att.astype(BF16), v.astype(BF16), preferred_element_type=F32) + inter


def _scan_order(j, n, unroll):
    return [(d, j * unroll + u if d == 0 else n - 1 - (j * unroll + u)) for u in range(unroll) for d in range(2)]


def _gla_kernel(q_ref, k_ref, v_ref, gg_ref, zs_ref, kc_ref, vc_ref, zsc_ref,
                wup_ref, bup_ref, ng_ref, o_ref, s_ref, acc_ref):
    seq = q_ref.shape[1]
    ctx = kc_ref.shape[1]
    n = seq // GLA_C
    nc = ctx // GLA_C
    s_ref[...] = jnp.zeros_like(s_ref)

    def rows(i):
        return pl.ds(pl.multiple_of(i * GLA_C, GLA_C), GLA_C)

    masks, tris = _visibility(GLA_C)
    heads_per_slab = LANES // GLA_DK
    lo = lax.rem(pl.program_id(1), heads_per_slab) * GLA_DK
    lane = lax.broadcasted_iota(jnp.int32, (GLA_C, LANES), 1)
    mine = (lane >= lo) & (lane < lo + GLA_DK)

    def own(t):
        return jnp.where(mine, t, 0.0)

    def ctx_step(j, carry):
        order = _scan_order(j, nc, 1)
        ins = [(own(kc_ref[0, rows(i), :]), vc_ref[0, rows(i), :], zsc_ref[0, rows(i), :]) for _, i in order]
        s = [s_ref[0], s_ref[1]]
        _round_robin([_gla_chunk(None, k, v, zs, wup_ref[0, d], bup_ref[0, d], s, d, False, masks[d], tris[d])
                      for (d, _), (k, v, zs) in zip(order, ins)])
        s_ref[0] = s[0]
        s_ref[1] = s[1]
        return carry

    lax.fori_loop(0, nc, ctx_step, 0)

    def lat_step(j, carry, second):
        order = _scan_order(j, n, GLA_UNROLL)
        ins = [(own(q_ref[0, rows(i), :]), own(k_ref[0, rows(i), :]), v_ref[0, rows(i), :], zs_ref[0, rows(i), :])
               for _, i in order]
        prev = [(acc_ref[rows(i), :], gg_ref[0, rows(i), :]) for _, i in order] if second else None
        s = [s_ref[0], s_ref[1]]
        outs = _round_robin([_gla_chunk(q * (GLA_DK ** -0.5), k, v, zs, wup_ref[0, d], bup_ref[0, d], s, d, True,
                                        masks[d], tris[d])
                             for (d, _), (q, k, v, zs) in zip(order, ins)])
        s_ref[0] = s[0]
        s_ref[1] = s[1]
        for idx, (_, i) in enumerate(order):
            if second:
                total = prev[idx][0] + outs[idx]
                y = total * lax.rsqrt(jnp.mean(total * total, axis=-1, keepdims=True) + EPS) * ng_ref[...]
                o_ref[0, rows(i), :] = (y * _silu(prev[idx][1])).astype(o_ref.dtype)
            else:
                acc_ref[rows(i), :] = outs[idx]
        return carry

    half = n // (2 * GLA_UNROLL)
    lax.fori_loop(0, half, functools.partial(lat_step, second=False), 0)
    lax.fori_loop(half, 2 * half, functools.partial(lat_step, second=True), 0)


def _gla(zg_x, zs_x, zg_c, zs_c, wup, bup, norm_g):
    bsz, seq, _ = zg_x.shape
    ctx = zg_c.shape[1]
    assert seq % (2 * GLA_UNROLL * GLA_C) == 0 and ctx % GLA_C == 0
    h = GLA_HEADS

    hps = LANES // GLA_DK
    qk_slabs = h // hps

    def qk(l, off):
        return pl.BlockSpec((1, l, LANES), lambda b, hd: (b, 0, off + hd // hps))

    def col(l, off):
        return pl.BlockSpec((1, l, LANES), lambda b, hd: (b, 0, off + hd))

    return pl.pallas_call(
        _gla_kernel,
        grid=(bsz, h),
        in_specs=[qk(seq, 0), qk(seq, qk_slabs), col(seq, 2 * qk_slabs), col(seq, 2 * qk_slabs + h),
                  pl.BlockSpec((1, seq, LANES), lambda b, hd: (b, 0, 0)),
                  qk(ctx, qk_slabs), col(ctx, 2 * qk_slabs),
                  pl.BlockSpec((1, ctx, LANES), lambda b, hd: (b, 0, 0)),
                  pl.BlockSpec((1, 2, LANES, LANES), lambda b, hd: (hd, 0, 0, 0)),
                  pl.BlockSpec((1, 2, 1, LANES), lambda b, hd: (hd, 0, 0, 0)),
                  pl.BlockSpec((1, LANES), lambda b, hd: (0, hd))],
        out_specs=pl.BlockSpec((1, seq, LANES), lambda b, hd: (b, 0, hd)),
        out_shape=jax.ShapeDtypeStruct((bsz, seq, h * GLA_DV), BF16),
        scratch_shapes=[pltpu.VMEM((2, LANES, LANES), F32), pltpu.VMEM((seq, LANES), F32)],
        compiler_params=_cparams(("arbitrary", "arbitrary")),
        name="gla",
    )(zg_x, zg_x, zg_x, zg_x, zs_x, zg_c, zg_c, zs_c, wup, bup, norm_g)


def _grid_conv_silu(src_ref, pad_ref, dst_ref, w_ref, b_ref, grid_w, scale):
    l = src_ref.shape[1]
    n_rows = l // grid_w
    margin = grid_w + SUBLANES
    assert pad_ref.shape[0] >= l + 2 * margin and margin % SUBLANES == 0
    pad_ref[0:margin, :] = jnp.zeros((margin, LANES), F32)
    pad_ref[margin + l:2 * margin + l, :] = jnp.zeros((margin, LANES), F32)

    def copy_row(r, carry):
        at = pl.ds(pl.multiple_of(r * grid_w, grid_w), grid_w)
        pad_ref[pl.ds(pl.multiple_of(margin + r * grid_w, SUBLANES), grid_w), :] = src_ref[0, at, :]
        return carry

    lax.fori_loop(0, n_rows, copy_row, 0, unroll=2)

    col = lax.broadcasted_iota(jnp.int32, (grid_w, LANES), 0)
    inside = {dx: (col + dx >= 0) & (col + dx < grid_w) for dx in (-1, 1)}
    rows_dy = (0,) if n_rows == 1 else (-1, 0, 1)

    def body(r, carry):
        base = pl.multiple_of(margin + r * grid_w, SUBLANES)
        acc = jnp.zeros((grid_w, LANES), F32) + b_ref[...]
        for dy in rows_dy:
            for dx in (-1, 0, 1):
                tap = (dy + 1) * 3 + (dx + 1)
                blk = pad_ref[pl.ds(base + dy * grid_w + dx, grid_w), :]
                if dx != 0:
                    blk = jnp.where(inside[dx], blk, 0.0)
                acc = acc + blk * w_ref[tap:tap + 1, :]
        dst_ref[pl.ds(pl.multiple_of(r * grid_w, grid_w), grid_w), :] = _silu(acc) * scale
        return carry

    lax.fori_loop(0, n_rows, body, 0)


def _ml_chunk(qb, k, vt, gcol, grow, state, mstate, direction, want_out, visible):
    c = k.shape[0]
    a_row = grow[direction:direction + 1, :]
    i_row = grow[2 + direction:3 + direction, :]
    a_end = a_row[:, c - 1:c] if direction == 0 else a_row[:, 0:1]
    g = a_end - a_row + i_row
    g_max = jnp.max(g, axis=-1, keepdims=True)
    head = 2 * SUBLANES
    pad_rows = jnp.zeros((LANES - head, c), BF16)
    first = lax.broadcasted_iota(jnp.int32, (head, c), 0) == 0
    kb = k.astype(BF16)
    if want_out:
        c_col = gcol[:, direction:direction + 1] - gcol[:, 2 + direction:3 + direction]
        dmat = jnp.where(visible, a_row - c_col, -jnp.inf)
        d_max = jnp.max(dmat, axis=0, keepdims=True)
        kq = jnp.dot(kb, qb, preferred_element_type=F32)
    yield
    s, m = state[direction], mstate[direction]
    m_new = jnp.maximum(a_end + m, g_max)
    decay = jnp.exp(a_end + m - m_new)
    w = jnp.exp(g - m_new)
    vw = jnp.concatenate([(vt * w).astype(BF16), jnp.where(first, w, 0.0).astype(BF16), pad_rows], axis=0)
    state[direction] = decay * s + jnp.dot(vw, kb, preferred_element_type=F32)
    mstate[direction] = m_new
    if not want_out:
        return None
    inter = a_row + m
    m_t = jnp.maximum(inter, d_max)
    w_inter = jnp.exp(inter - m_t)
    p = (kq * jnp.exp(dmat - m_t)).astype(BF16)
    vt_aug = jnp.concatenate([vt.astype(BF16), jnp.where(first, 1.0, 0.0).astype(BF16), pad_rows], axis=0)
    pv = jnp.dot(vt_aug, p, preferred_element_type=F32)
    sq = jnp.dot(s.astype(BF16), qb, preferred_element_type=F32)
    yield
    both = pv + w_inter * sq
    den = both[ML_DH:ML_DH + 1, :]
    return both[:ML_DH, :] / jnp.maximum(jnp.abs(den), jnp.exp(-m_t))


def _mlstm_kernel(q_ref, k_ref, v_ref, mo_ref, gcol_ref, grow_ref,
                  kc_ref, vc_ref, growc_ref,
                  wq_ref, wk_ref, bq_ref, bk_ref, ng_ref, o_ref,
                  cq_ref, ck_ref, ckc_ref, pad_ref, qt_ref, vt_ref, vct_ref, s_ref, m_ref, acc_ref):
    seq = q_ref.shape[1]
    ctx = kc_ref.shape[1]
    n = seq // ML_C
    nc = ctx // ML_C
    _grid_conv_silu(q_ref, pad_ref, cq_ref, wq_ref, bq_ref, GRID_W, 1.0)
    _grid_conv_silu(k_ref, pad_ref, ck_ref, wk_ref, bk_ref, GRID_W, ML_DH ** -0.5)
    _grid_conv_silu(kc_ref, pad_ref, ckc_ref, wk_ref, bk_ref, ctx, ML_DH ** -0.5)
    s_ref[...] = jnp.zeros_like(s_ref)
    m_ref[...] = jnp.zeros_like(m_ref)

    def rows(i):
        return pl.ds(pl.multiple_of(i * ML_C, ML_C), ML_C)

    def transpose_chunks(i, carry):
        qt_ref[:, rows(i)] = cq_ref[rows(i), :].T.astype(qt_ref.dtype)
        vt_ref[:, rows(i)] = v_ref[0, rows(i), :].T
        return carry

    lax.fori_loop(0, n, transpose_chunks, 0, unroll=2)
    for i in range(nc):
        vct_ref[:, i * ML_C:(i + 1) * ML_C] = vc_ref[0, i * ML_C:(i + 1) * ML_C, :].T

    def load_state():
        return [s_ref[0], s_ref[1]], [m_ref[0, :, 0:1], m_ref[1, :, 0:1]]

    def store_state(s, m):
        for d in range(2):
            s_ref[d] = s[d]
            m_ref[d] = jnp.broadcast_to(m[d], m_ref.shape[1:])

    masks, _ = _visibility(ML_C)
    visible = [masks[1], masks[0]]

    def ctx_step(j, carry):
        order = _scan_order(j, nc, 1)
        ins = [(ckc_ref[rows(i), :], vct_ref[:, rows(i)], growc_ref[0, 0, :, rows(i)]) for _, i in order]
        s, m = load_state()
        _round_robin([_ml_chunk(None, k, vt, None, grow, s, m, d, False, None)
                      for (d, _), (k, vt, grow) in zip(order, ins)])
        store_state(s, m)
        return carry

    lax.fori_loop(0, nc, ctx_step, 0)

    def lat_step(j, carry, second):
        order = _scan_order(j, n, SCAN_UNROLL)
        ins = [(qt_ref[:, rows(i)], ck_ref[rows(i), :], vt_ref[:, rows(i)], gcol_ref[0, rows(i), :],
                grow_ref[0, 0, :, rows(i)]) for _, i in order]
        prev = [(acc_ref[:, rows(i)], mo_ref[0, rows(i), :]) for _, i in order] if second else None
        s, m = load_state()
        outs = _round_robin([_ml_chunk(qb, k, vt, gcol, grow, s, m, d, True, visible[d])
                             for (d, _), (qb, k, vt, gcol, grow) in zip(order, ins)])
        store_state(s, m)
        for idx, (_, i) in enumerate(order):
            if second:
                total = prev[idx][0] + outs[idx]
                y = total * lax.rsqrt(jnp.mean(total * total, axis=0, keepdims=True) + EPS) * ng_ref[...]
                o_ref[0, rows(i), :] = (_sigmoid(prev[idx][1]) * y.T).astype(o_ref.dtype)
            else:
                acc_ref[:, rows(i)] = outs[idx]
        return carry

    half = n // (2 * SCAN_UNROLL)
    lax.fori_loop(0, half, functools.partial(lat_step, second=False), 0)
    lax.fori_loop(half, 2 * half, functools.partial(lat_step, second=True), 0)


def _mlstm(zm_x, gcol_x, grow_x, zm_c, grow_c, conv_w, conv_b, norm_g):
    bsz, seq, _ = zm_x.shape
    ctx = zm_c.shape[1]
    assert seq % (2 * SCAN_UNROLL * ML_C) == 0 and ctx % ML_C == 0 and seq % GRID_W == 0
    h = ML_HEADS

    def col(l, off):
        return pl.BlockSpec((1, l, LANES), lambda b, hd: (b, 0, off + hd))

    def gates(l):
        return [pl.BlockSpec((1, l, LANES), lambda b, hd: (b, 0, hd)),
                pl.BlockSpec((1, 1, 8, l), lambda b, hd: (b, hd, 0, 0))]

    return pl.pallas_call(
        _mlstm_kernel,
        grid=(bsz, h),
        in_specs=[col(seq, 0), col(seq, h), col(seq, 2 * h), col(seq, 3 * h)] + gates(seq)
                 + [col(ctx, h), col(ctx, 2 * h), gates(ctx)[1]]
                 + [pl.BlockSpec((9, LANES), lambda b, hd: (0, hd)),
                    pl.BlockSpec((9, LANES), lambda b, hd: (0, h + hd)),
                    pl.BlockSpec((1, LANES), lambda b, hd: (0, hd)),
                    pl.BlockSpec((1, LANES), lambda b, hd: (0, h + hd)),
                    pl.BlockSpec((LANES, 1), lambda b, hd: (hd, 0))],
        out_specs=pl.BlockSpec((1, seq, LANES), lambda b, hd: (b, 0, hd)),
        out_shape=jax.ShapeDtypeStruct((bsz, seq, h * ML_DH), BF16),
        scratch_shapes=[pltpu.VMEM((seq, LANES), F32), pltpu.VMEM((seq, LANES), F32),
                        pltpu.VMEM((ctx, LANES), F32),
                        pltpu.VMEM((max(seq + 2 * (GRID_W + SUBLANES), 3 * ctx + 2 * SUBLANES), LANES), F32),
                        pltpu.VMEM((LANES, seq), BF16), pltpu.VMEM((LANES, seq), F32),
                        pltpu.VMEM((LANES, ctx), F32),
                        pltpu.VMEM((2, 2 * LANES, LANES), F32), pltpu.VMEM((2, 1, LANES), F32),
                        pltpu.VMEM((LANES, seq), F32)],
        compiler_params=_cparams(("arbitrary", "arbitrary")),
        name="mlstm",
    )(zm_x, zm_x, zm_x, zm_x, gcol_x, grow_x, zm_c, zm_c, grow_c,
      conv_w, conv_w, conv_b, conv_b, norm_g.reshape(-1, 1))


_G0 = 0
_E0 = N_GROUPS
RANK_BITS = 16
RANK_SPAN = 1 << RANK_BITS
ROW_GROUP = 32
GATHER_RING = 3


SUBLANES = 8


def _store_token_tiles(ref2d, val):
    n, w = val.shape
    k = w // LANES
    for c in range(k):
        ref2d[pl.ds(c, n, stride=k), :] = val[:, c * LANES:(c + 1) * LANES]


def _load_token_tiles(ref2d, first, n, k, step):
    return jnp.concatenate([ref2d[pl.ds(first + c, n, stride=step), :] for c in range(k)], axis=1)


def _outproj_kernel(x_ref, ga_ref, ml_ref, mod_ref, wa_ref, wb_ref, g2_ref, wrh_ref, wrl_ref, br_ref,
                    x1_ref, h2_ref, ri_ref, rw_ref, cnt_ref, base_ref):
    tm = x_ref.shape[1]

    @pl.when((pl.program_id(0) == 0) & (pl.program_id(1) == 0))
    def _():
        base_ref[...] = jnp.zeros_like(base_ref)

    mix = (jnp.dot(ga_ref[0], wa_ref[...], preferred_element_type=F32)
           + jnp.dot(ml_ref[0], wb_ref[...], preferred_element_type=F32))
    x1 = x_ref[0] + mod_ref[0, 2:3, :] * mix
    x1_ref[0] = x1
    y = x1 * lax.rsqrt(jnp.mean(x1 * x1, axis=-1, keepdims=True) + EPS) * g2_ref[...]
    h2 = y * (1.0 + mod_ref[0, 4:5, :]) + mod_ref[0, 3:4, :]
    _store_token_tiles(h2_ref, h2)

    h_hi = h2.astype(BF16)
    h_lo = (h2 - h_hi.astype(F32)).astype(BF16)
    logits = (jnp.dot(h_hi, wrh_ref[...], preferred_element_type=F32)
              + jnp.dot(h_lo, wrh_ref[...], preferred_element_type=F32)
              + jnp.dot(h_hi, wrl_ref[...], preferred_element_type=F32)) + br_ref[...]

    lane = lax.broadcasted_iota(jnp.int32, (tm, LANES), 1).astype(F32)
    neg = -jnp.inf
    big = float(LANES)
    is_g = lane < float(_E0)
    lg = jnp.where(is_g, logits, neg)
    gmax = jnp.max(lg, axis=-1, keepdims=True)
    gidx = jnp.min(jnp.where(lg == gmax, lane, big), axis=-1, keepdims=True)
    gw = 1.0 / jnp.sum(jnp.where(is_g, jnp.exp(logits - gmax), 0.0), axis=-1, keepdims=True)
    lo = float(_E0) + float(EXP_PER_GROUP) * gidx
    le = jnp.where((lane >= lo) & (lane < lo + float(EXP_PER_GROUP)), logits, neg)
    v1 = jnp.max(le, axis=-1, keepdims=True)
    i1 = jnp.min(jnp.where(le == v1, lane, big), axis=-1, keepdims=True)
    le2 = jnp.where(lane == i1, neg, le)
    v2 = jnp.max(le2, axis=-1, keepdims=True)
    i2 = jnp.min(jnp.where(le2 == v2, lane, big), axis=-1, keepdims=True)
    t = jnp.exp(v2 - v1)
    w1 = gw / (1.0 + t)
    w2 = gw * t / (1.0 + t)
    e1 = i1 - float(_E0)
    e2 = i2 - float(_E0)

    oh1 = lane == e1
    oh2 = lane == e2
    oh = jnp.where(oh1 | oh2, 1.0, 0.0)
    r = lax.broadcasted_iota(jnp.int32, (tm, tm), 0)
    c = lax.broadcasted_iota(jnp.int32, (tm, tm), 1)
    strict = jnp.where(c < r, 1.0, 0.0).astype(BF16)
    before = jnp.dot(strict, oh.astype(BF16), preferred_element_type=F32) + base_ref[...]
    rank1 = jnp.sum(jnp.where(oh1, before, 0.0), axis=-1, keepdims=True)
    rank2 = jnp.sum(jnp.where(oh2, before, 0.0), axis=-1, keepdims=True)
    total = base_ref[...] + jnp.sum(oh, axis=0, keepdims=True)
    base_ref[...] = total
    cnt_ref[...] = total

    ids = jnp.where(lane == 0.0, e1 * float(RANK_SPAN) + rank1,
                    jnp.where(lane == 1.0, e2 * float(RANK_SPAN) + rank2, 0.0))
    ri_ref[...] = ids.astype(jnp.int32)
    rw_ref[...] = jnp.where(lane == 0.0, w1, jnp.where(lane == 1.0, w2, 0.0))


def _outproj(x, gla_o, ml_o, mods, wa, wb, g2, wrh, wrl, br, tm):
    bsz, seq, d = x.shape
    const = lambda shape: pl.BlockSpec(shape, lambda b, i: (0,) * len(shape))
    tile = lambda w: pl.BlockSpec((1, tm, w), lambda b, i: (b, i, 0))
    flat = lambda rows: pl.BlockSpec((rows, LANES), lambda b, i: (b * (seq // tm) + i, 0))
    return pl.pallas_call(
        _outproj_kernel,
        grid=(bsz, seq // tm),
        in_specs=[tile(d), tile(gla_o.shape[2]), tile(ml_o.shape[2]),
                  pl.BlockSpec((1, N_MOD, d), lambda b, i: (b, 0, 0)),
                  const(wa.shape), const(wb.shape), const((1, d)),
                  const(wrh.shape), const(wrl.shape), const((1, LANES))],
        out_specs=[tile(d),
                   pl.BlockSpec((tm * d // LANES, LANES), lambda b, i: (b * (seq // tm) + i, 0)),
                   flat(tm), flat(tm), const((1, LANES))],
        out_shape=[jax.ShapeDtypeStruct((bsz, seq, d), F32),
                   jax.ShapeDtypeStruct((bsz * seq * d // LANES, LANES), F32),
                   jax.ShapeDtypeStruct((bsz * seq, LANES), jnp.int32),
                   jax.ShapeDtypeStruct((bsz * seq, LANES), F32),
                   jax.ShapeDtypeStruct((1, LANES), F32)],
        scratch_shapes=[pltpu.VMEM((1, LANES), F32)],
        compiler_params=_cparams(("arbitrary", "arbitrary")),
        name="outproj",
    )(x, gla_o, ml_o, mods, wa, wb, g2, wrh, wrl, br)


def _experts_kernel(dest_ref, ps_ref, cnt_ref, be_ref, nv_ref, meta_ref, h_hbm, w1_ref, w2_ref, ytok_hbm,
                    src_ref, xbuf, ybuf, w1c_ref, w2c_ref, gsem, ssem):
    i = pl.program_id(0)
    n_steps = pl.num_programs(0)
    n_used = meta_ref[0]
    tr = SUBLANES
    n_tok = h_hbm.shape[0] // tr
    n_rows = src_ref.shape[0]
    blk = xbuf.shape[1] // tr
    n_x = xbuf.shape[0]
    slot = lax.rem(i, 2)
    xslot = lax.rem(i, n_x)

    def slab(j):
        return pl.ds(pl.multiple_of(j * tr, tr), tr)

    def group_rows(buf, s, g):
        span = ROW_GROUP * tr
        return buf.at[s, pl.ds(pl.multiple_of(g * span, span), span), :]

    def gather_copy(tok, s, g, u):
        return pltpu.make_async_copy(h_hbm.at[slab(tok), :], group_rows(xbuf, s, g).at[pl.ds(u * tr, tr), :],
                                     gsem.at[s])

    def scatter_copy(a, s, g, u):
        return pltpu.make_async_copy(group_rows(ybuf, s, g).at[pl.ds(u * tr, tr), :], ytok_hbm.at[slab(a), :],
                                     ssem.at[s])

    def groups(b):
        nv = jnp.where(b < n_steps, nv_ref[jnp.minimum(b, n_steps - 1)], 0)
        return lax.shift_right_logical(nv + (ROW_GROUP - 1), ROW_GROUP.bit_length() - 1)

    def rows_loop(b, body):
        def step(g, c):
            for u in range(ROW_GROUP):
                body(g, u)
            return c
        lax.fori_loop(0, groups(b), step, 0)

    def issue_gather(b, s):
        def one(g, u):
            tok = lax.shift_right_logical(src_ref[b * blk + g * ROW_GROUP + u], 1)
            gather_copy(jnp.minimum(tok, n_tok - 1), s, g, u).start(priority=u % 2)
        rows_loop(b, one)

    def wait_rows(b, copy):
        n_groups = groups(b)
        bit = blk // ROW_GROUP
        while bit:
            @pl.when((n_groups & bit) != 0)
            def _():
                for _ in range(bit * ROW_GROUP):
                    copy.wait()
            bit //= 2

    def wait_gather(b, s):
        wait_rows(b, gather_copy(0, s, 0, 0))

    def wait_scatter(b, s):
        wait_rows(b, scatter_copy(0, s, 0, 0))

    @pl.when(i == 0)
    def _():
        xbuf[...] = jnp.zeros_like(xbuf)
        ybuf[...] = jnp.zeros_like(ybuf)
        for s in range(2):
            tail = ytok_hbm.at[pl.ds((2 * n_tok + s * blk) * tr, blk * tr), :]
            cp = pltpu.make_async_copy(ybuf.at[s], tail, ssem.at[s])
            cp.start()
            cp.wait()

        def put(a, c):
            src_ref[dest_ref[a]] = a
            return c
        lax.fori_loop(0, 2 * n_tok, put, 0, unroll=16)

        def pad(j, c):
            src_ref[j] = 2 * n_tok + (j & (2 * blk - 1))
            return c

        def pad_expert(e, c):
            lax.fori_loop(ps_ref[e] + cnt_ref[e], ps_ref[e + 1], pad, 0)
            return c
        lax.fori_loop(0, cnt_ref.shape[0], pad_expert, 0)
        for b in range(n_x - 1):
            issue_gather(min(b, n_rows // blk - 1), b)

    @pl.when(i < n_used)
    def _():
        wait_gather(i, xslot)

        @pl.when((i == 0) | (be_ref[i] != be_ref[jnp.maximum(i - 1, 0)]))
        def _():
            w1c_ref[...] = w1_ref[0].astype(BF16)
            w2c_ref[...] = w2_ref[0].astype(BF16)

        @pl.when(i >= 2)
        def _():
            wait_scatter(i - 2, slot)

        row = lax.broadcasted_iota(jnp.int32, (blk, 1), 0)
        x = _load_token_tiles(xbuf.at[xslot], 0, blk, tr, tr)
        x = jnp.where(row < nv_ref[i], x, 0.0).astype(BF16)
        h = jnp.dot(x, w1c_ref[...], preferred_element_type=F32)
        a = (_silu(h[:, :D_EXPERT]) * h[:, D_EXPERT:]).astype(BF16)
        _store_token_tiles(ybuf.at[slot], jnp.dot(a, w2c_ref[...], preferred_element_type=F32))
        rows_loop(i, lambda g, u: scatter_copy(src_ref[i * blk + g * ROW_GROUP + u], slot, g, u)
                  .start(priority=u % 2))
        issue_gather(i + n_x - 1, lax.rem(i + n_x - 1, n_x))

    @pl.when(i == n_steps - 1)
    def _():
        for back in (2, 1):
            wait_scatter(n_used - back, lax.rem(n_used - back, 2))


def _experts(dest, pad_start, counts, block_e, block_nv, meta, h2, w_in, w_out, nb):
    d = w_in.shape[1]
    tr = d // LANES
    assert tr == SUBLANES, "a token row must fill exactly one (8, 128) tile"
    n_tok = h2.shape[0] // tr
    de2 = w_in.shape[2]
    n_rows = nb * MOE_BLK
    assert 2 * n_tok >= 2 * MOE_BLK
    assert MOE_BLK & (MOE_BLK - 1) == 0
    wmap = lambda i, pk, ps, cnt, be, nv, meta: (be[i], 0, 0)
    return pl.pallas_call(
        _experts_kernel,
        grid_spec=pltpu.PrefetchScalarGridSpec(
            num_scalar_prefetch=6, grid=(nb,),
            in_specs=[pl.BlockSpec(memory_space=pl.ANY),
                      pl.BlockSpec((1, d, de2), wmap),
                      pl.BlockSpec((1, de2 // 2, d), wmap)],
            out_specs=pl.BlockSpec(memory_space=pl.ANY),
            scratch_shapes=[pltpu.SMEM((n_rows,), jnp.int32),
                            pltpu.VMEM((GATHER_RING, MOE_BLK * tr, LANES), F32),
                            pltpu.VMEM((2, MOE_BLK * tr, LANES), F32),
                            pltpu.VMEM((d, de2), BF16), pltpu.VMEM((de2 // 2, d), BF16),
                            pltpu.SemaphoreType.DMA((GATHER_RING,)), pltpu.SemaphoreType.DMA((2,))]),
        out_shape=jax.ShapeDtypeStruct(((2 * n_tok + 2 * MOE_BLK) * tr, LANES), F32),
        compiler_params=_cparams(("arbitrary",)),
        name="experts",
    )(dest, pad_start, counts, block_e, block_nv, meta, h2, w_in, w_out)


def _combine_kernel(x1_ref, y_ref, rw_ref, mod_ref, fg_ref, o_ref):
    tc, d = x1_ref.shape
    tr = d // LANES
    y1 = _load_token_tiles(y_ref, 0, tc, tr, 2 * tr)
    y2 = _load_token_tiles(y_ref, tr, tc, tr, 2 * tr)
    moe = rw_ref[:, 0:1] * y1 + rw_ref[:, 1:2] * y2
    x2 = x1_ref[...] + mod_ref[0, 5:6, :] * moe
    o_ref[...] = x2 * lax.rsqrt(jnp.mean(x2 * x2, axis=-1, keepdims=True) + EPS) * fg_ref[...]


def _combine(x1, ytok, rw, mods, fg, tokens_per_batch, tc):
    n_tok, d = x1.shape
    tiles_per_batch = tokens_per_batch // tc
    return pl.pallas_call(
        _combine_kernel,
        grid=(n_tok // tc,),
        in_specs=[pl.BlockSpec((tc, d), lambda i: (i, 0)),
                  pl.BlockSpec((2 * tc * d // LANES, LANES), lambda i: (i, 0)),
                  pl.BlockSpec((tc, LANES), lambda i: (i, 0)),
                  pl.BlockSpec((1, N_MOD, d), lambda i: (i // tiles_per_batch, 0, 0)),
                  pl.BlockSpec((1, d), lambda i: (0, 0))],
        out_specs=pl.BlockSpec((tc, d), lambda i: (i, 0)),
        out_shape=jax.ShapeDtypeStruct((n_tok, d), F32),
        compiler_params=_cparams(("arbitrary",)),
        name="combine",
    )(x1, ytok, rw, mods, fg)


def _prep_inproj_weights(w_in, gla_up_w, gla_up_b, ml_i_b, ml_f_b):
    d = w_in.shape[0]
    o_gq, o_gk, o_gv, o_gg = 0, GLA_QK_W, 2 * GLA_QK_W, 2 * GLA_QK_W + GLA_V_W
    o_lr = o_gg + GLA_V_W
    o_mqk = o_lr + 2 * GLA_LR
    o_mi = o_mqk + 4 * ML_W
    o_mf = o_mi + 2 * ML_HEADS

    wg = w_in[:, o_gq:o_lr]
    wm = w_in[:, o_mqk:o_mi]
    ws = jnp.concatenate([w_in[:, o_lr:o_mqk], w_in[:, o_mi:o_mf + 2 * ML_HEADS],
                          jnp.zeros((d, LANES - 2 * GLA_LR - 4 * ML_HEADS), w_in.dtype)], axis=1)
    bias = jnp.zeros((LANES,), F32)
    bias = bias.at[_MI0:_MI0 + 2 * ML_HEADS].set(ml_i_b.reshape(-1))
    bias = bias.at[_MF0:_MF0 + 2 * ML_HEADS].set(ml_f_b.reshape(-1))
    up = gla_up_w.reshape(2, GLA_LR, GLA_HEADS, GLA_DK).transpose(2, 0, 1, 3)
    ub = gla_up_b.reshape(2, GLA_HEADS, GLA_DK).transpose(1, 0, 2)
    wup = jnp.zeros((GLA_HEADS, 2, LANES, LANES), F32)
    bup = jnp.zeros((GLA_HEADS, 2, 1, LANES), F32)
    for hd in range(GLA_HEADS):
        lo = (hd % (LANES // GLA_DK)) * GLA_DK
        for dr in range(2):
            wup = wup.at[hd, dr, dr * GLA_LR:(dr + 1) * GLA_LR, lo:lo + GLA_DK].set(up[hd, dr])
        bup = bup.at[hd, :, 0, lo:lo + GLA_DK].set(ub[hd])
    return (wg.astype(BF16), wm.astype(BF16), ws.astype(BF16), ws.T.astype(BF16),
            bias.reshape(1, LANES), bias.reshape(LANES, 1), wup.astype(BF16), bup)


def _layer(x, ctx, mods, norm1_g, w_in, gla_up_w, gla_up_b, gla_norm_g, ml_conv_w, ml_conv_b,
           ml_i_b, ml_f_b, ml_norm_g, w_out, norm2_g, rg_w, rg_b, re_w, re_b, e_w_in, e_w_out, final_g):
    bsz, seq, d = x.shape
    n_tok = bsz * seq
    wg, wm, ws, wst, bcol, brow, wup, bup = _prep_inproj_weights(w_in, gla_up_w, gla_up_b, ml_i_b, ml_f_b)
    g1 = norm1_g.reshape(1, d)
    zg_x, zm_x, zs_x, gcol_x, grow_x = _inproj(x, mods, lambda b: b, g1, wg, wm, ws, wst, bcol, brow, 256)
    zg_c, zm_c, zs_c, gcol_c, grow_c = _inproj(ctx, mods, lambda b: bsz, g1, wg, wm, ws, wst, bcol, brow,
                                               min(256, ctx.shape[1]))
    gla_o = _gla(zg_x, zs_x, zg_c, zs_c, wup, bup, gla_norm_g.reshape(1, -1))
    ml_o = _mlstm(zm_x, gcol_x, grow_x, zm_c, grow_c,
                  ml_conv_w.reshape(9, -1), ml_conv_b.reshape(1, -1), ml_norm_g.reshape(1, -1))

    wr = jnp.zeros((d, LANES), F32).at[:, _G0:_E0].set(rg_w).at[:, _E0:_E0 + N_EXPERTS].set(re_w)
    br = jnp.zeros((1, LANES), F32).at[0, _G0:_E0].set(rg_b).at[0, _E0:_E0 + N_EXPERTS].set(re_b)
    wrh = wr.astype(BF16)
    wrl = (wr - wrh.astype(F32)).astype(BF16)
    x1, h2, ri, rw, cnt = _outproj(x, gla_o, ml_o, mods, w_out[:GLA_V_W].astype(BF16),
                                   w_out[GLA_V_W:].astype(BF16), norm2_g.reshape(1, d), wrh, wrl, br, 256)

    counts = cnt[0, :N_EXPERTS].astype(jnp.int32)
    nblk = (counts + MOE_BLK - 1) // MOE_BLK
    blk_end = jnp.cumsum(nblk)
    blk_start = blk_end - nblk
    n_used = blk_end[-1]
    nb_max = (2 * n_tok) // MOE_BLK + N_EXPERTS
    blk = jnp.arange(nb_max, dtype=jnp.int32)
    blk_c = jnp.minimum(blk, n_used - 1)
    onehot = (blk_c[:, None] >= blk_start[None, :]) & (blk_c[:, None] < blk_end[None, :])
    pick = lambda v: jnp.sum(jnp.where(onehot, v[None, :], 0), axis=1)
    block_e = pick(jnp.arange(N_EXPERTS, dtype=jnp.int32)).astype(jnp.int32)
    block_nv = jnp.clip(pick(counts) - (blk_c - pick(blk_start)) * MOE_BLK, 0, MOE_BLK)
    block_nv = jnp.where(blk < n_used, block_nv, 0).astype(jnp.int32)
    pad_start = (jnp.concatenate([blk_start, blk_end[-1:]]) * MOE_BLK).astype(jnp.int32)
    packed = ri[:, 0:2].reshape(-1)
    e_of = lax.shift_right_logical(packed, RANK_BITS)
    start_of = jnp.sum(jnp.where(e_of[:, None] == jnp.arange(N_EXPERTS, dtype=jnp.int32)[None, :],
                                 pad_start[None, :N_EXPERTS], 0), axis=1)
    dest = (start_of + (packed & (RANK_SPAN - 1))).astype(jnp.int32)
    meta = jnp.stack([n_used, n_used]).astype(jnp.int32)

    ytok = _experts(dest, pad_start, counts, block_e, block_nv, meta, h2, e_w_in, e_w_out, nb_max)
    out = _combine(x1.reshape(n_tok, d), ytok, rw, mods, final_g.reshape(1, d), seq, 512)
    return out.reshape(bsz, seq, d)


def kernel(x, c, ctx, c_ctx, ada_w, ada_b, norm1_g, w_in, gla_up_w, gla_up_b, gla_norm_g, ml_conv_w, ml_conv_b,
           ml_i_b, ml_f_b, ml_norm_g, w_out, norm2_g, router_group_w, router_group_b, router_expert_w,
           router_expert_b, expert_w_in, expert_w_out, final_norm_g):
    assert ada_w.shape[0] == 1, "single-layer stack"
    bsz, d = c.shape
    cc = jnp.concatenate([c, c_ctx[None, :], jnp.zeros((8 - bsz - 1, d), F32)], axis=0)
    mods = _modulation(cc, ada_w[0], ada_b[0]).reshape(8, N_MOD, d)
    return _layer(x, ctx, mods, norm1_g[0], w_in[0], gla_up_w[0], gla_up_b[0], gla_norm_g[0],
                  ml_conv_w[0], ml_conv_b[0], ml_i_b[0], ml_f_b[0], ml_norm_g[0], w_out[0], norm2_g[0],
                  router_group_w[0], router_group_b[0], router_expert_w[0], router_expert_b[0],
                  expert_w_in[0], expert_w_out[0], final_norm_g)
```

```python
import functools

import jax
import jax.numpy as jnp
from jax import lax
from jax.experimental import pallas as pl
from jax.experimental.pallas import tpu as pltpu

F32 = jnp.float32
BF16 = jnp.bfloat16

D_MODEL = 1024
GRID_W = 64
N_MOD = 6
EPS = 1e-6

GLA_HEADS = 4
GLA_DK = 64
GLA_DV = 128
GLA_LR = 16
GLA_TAU = 16.0
GLA_C = 128
GLA_UNROLL = 8
SCAN_UNROLL = 4

ML_HEADS = 4
ML_DH = 128
ML_C = 128

N_GROUPS = 4
EXP_PER_GROUP = 8
N_EXPERTS = N_GROUPS * EXP_PER_GROUP
D_EXPERT = 512
MOE_BLK = 256

GLA_QK_W = GLA_HEADS * GLA_DK
GLA_V_W = GLA_HEADS * GLA_DV
ML_W = ML_HEADS * ML_DH
LANES = 128
VMEM_LIMIT = 56 * 1024 * 1024

_LR0 = 0
_MI0 = 2 * GLA_LR
_MF0 = _MI0 + 2 * ML_HEADS


def _cparams(sem):
    return pltpu.CompilerParams(dimension_semantics=sem, vmem_limit_bytes=VMEM_LIMIT)


def _sigmoid(x):
    return 1.0 / (1.0 + jnp.exp(-x))


def _silu(x):
    return x * _sigmoid(x)


def _log_sigmoid(x):
    return jnp.minimum(x, 0.0) - jnp.log1p(jnp.exp(-jnp.abs(x)))


def _split_dot(a_bf16_exact, x, dims=None):
    x_hi = x.astype(BF16)
    x_lo = (x - x_hi.astype(F32)).astype(BF16)
    if dims is None:
        f = lambda u: jnp.dot(a_bf16_exact, u, preferred_element_type=F32)
    else:
        f = lambda u: lax.dot_general(u, a_bf16_exact, dims, preferred_element_type=F32)
    return f(x_hi) + f(x_lo)


def _mod_kernel(c_ref, w_ref, b_ref, o_ref):
    c = c_ref[...]
    s = _silu(c).astype(BF16)
    o_ref[...] = jnp.dot(s, w_ref[...].astype(BF16), preferred_element_type=F32) + b_ref[...]


def _modulation(cc, ada_w, ada_b):
    rows, d = cc.shape
    n = ada_w.shape[1]
    tn = 1536
    return pl.pallas_call(
        _mod_kernel,
        grid=(n // tn,),
        in_specs=[pl.BlockSpec((rows, d), lambda j: (0, 0)),
                  pl.BlockSpec((d, tn), lambda j: (0, j)),
                  pl.BlockSpec((1, tn), lambda j: (0, j))],
        out_specs=pl.BlockSpec((rows, tn), lambda j: (0, j)),
        out_shape=jax.ShapeDtypeStruct((rows, n), F32),
        compiler_params=_cparams(("arbitrary",)),
        name="mod",
    )(cc, ada_w, ada_b.reshape(1, n))


def _inproj_kernel(x_ref, mod_ref, g_ref, wg_ref, wm_ref, ws_ref, wst_ref, bcol_ref, brow_ref,
                   zg_ref, zm_ref, zs_ref, gcol_ref, grow_ref):
    tm = x_ref.shape[1]
    x = x_ref[0]
    y = x * lax.rsqrt(jnp.mean(x * x, axis=-1, keepdims=True) + EPS) * g_ref[...]
    h = (y * (1.0 + mod_ref[0, 1:2, :]) + mod_ref[0, 0:1, :]).astype(BF16)
    zg_ref[0] = jnp.dot(h, wg_ref[...], preferred_element_type=F32)
    zm_ref[0] = jnp.dot(h, wm_ref[...], preferred_element_type=F32)
    zs = jnp.dot(h, ws_ref[...], preferred_element_type=F32) + bcol_ref[...]
    zst = lax.dot_general(wst_ref[...], h, (((1,), (1,)), ((), ())),
                          preferred_element_type=F32) + brow_ref[...]
    zs_ref[0] = zs

    r = lax.broadcasted_iota(jnp.int32, (tm, tm), 0)
    c = lax.broadcasted_iota(jnp.int32, (tm, tm), 1)
    shift = ML_C.bit_length() - 1
    same = jnp.right_shift(r, shift) == jnp.right_shift(c, shift)
    lower = jnp.where(same & (c <= r), 1.0, 0.0).astype(BF16)
    upper = jnp.where(same & (c >= r), 1.0, 0.0).astype(BF16)
    chunks = range(0, tm, ML_C)
    lsf = _log_sigmoid(zs)
    a_pre = _split_dot(lower, lsf)
    tot = jnp.concatenate([jnp.broadcast_to(a_pre[o + ML_C - 1:o + ML_C, :], (ML_C, LANES)) for o in chunks], axis=0)
    a_suf = tot - a_pre + lsf
    lsft = _log_sigmoid(zst)
    a_pre_t = _split_dot(upper, lsft, (((1,), (0,)), ((), ())))
    tot_t = jnp.concatenate([jnp.broadcast_to(a_pre_t[:, o + ML_C - 1:o + ML_C], (LANES, ML_C)) for o in chunks], axis=1)
    a_suf_t = tot_t - a_pre_t + lsft

    lane = lax.broadcasted_iota(jnp.int32, (tm, LANES), 1)
    for hd in range(ML_HEADS):
        cols = (a_pre[:, _MF0 + hd:_MF0 + hd + 1],
                a_suf[:, _MF0 + ML_HEADS + hd:_MF0 + ML_HEADS + hd + 1],
                zs[:, _MI0 + hd:_MI0 + hd + 1],
                zs[:, _MI0 + ML_HEADS + hd:_MI0 + ML_HEADS + hd + 1])
        slab = jnp.zeros((tm, LANES), F32)
        for j, col in enumerate(cols):
            slab = jnp.where(lane == j, col, slab)
        gcol_ref[0, :, hd * LANES:(hd + 1) * LANES] = slab
        rows = (a_pre_t[_MF0 + hd:_MF0 + hd + 1, :],
                a_suf_t[_MF0 + ML_HEADS + hd:_MF0 + ML_HEADS + hd + 1, :],
                zst[_MI0 + hd:_MI0 + hd + 1, :],
                zst[_MI0 + ML_HEADS + hd:_MI0 + ML_HEADS + hd + 1, :])
        for j, row in enumerate(rows):
            grow_ref[0, hd, j:j + 1, :] = row
        grow_ref[0, hd, 4:8, :] = jnp.zeros((4, tm), F32)


def _inproj(x, mods, mod_row_of_batch, norm_g, wg, wm, ws, wst, bcol, brow, tm):
    bsz, l, d = x.shape
    assert l % tm == 0 and tm % ML_C == 0
    const = lambda shape: pl.BlockSpec(shape, lambda b, i: (0,) * len(shape))
    return pl.pallas_call(
        _inproj_kernel,
        grid=(bsz, l // tm),
        in_specs=[pl.BlockSpec((1, tm, d), lambda b, i: (b, i, 0)),
                  pl.BlockSpec((1, N_MOD, d), lambda b, i: (mod_row_of_batch(b), 0, 0)),
                  const((1, d)), const(wg.shape), const(wm.shape), const(ws.shape), const(wst.shape),
                  const((1, LANES)), const((LANES, 1))],
        out_specs=[pl.BlockSpec((1, tm, wg.shape[1]), lambda b, i: (b, i, 0)),
                   pl.BlockSpec((1, tm, wm.shape[1]), lambda b, i: (b, i, 0)),
                   pl.BlockSpec((1, tm, LANES), lambda b, i: (b, i, 0)),
                   pl.BlockSpec((1, tm, ML_HEADS * LANES), lambda b, i: (b, i, 0)),
                   pl.BlockSpec((1, ML_HEADS, 8, tm), lambda b, i: (b, 0, 0, i))],
        out_shape=[jax.ShapeDtypeStruct((bsz, l, wg.shape[1]), F32),
                   jax.ShapeDtypeStruct((bsz, l, wm.shape[1]), F32),
                   jax.ShapeDtypeStruct((bsz, l, LANES), F32),
                   jax.ShapeDtypeStruct((bsz, l, ML_HEADS * LANES), F32),
                   jax.ShapeDtypeStruct((bsz, ML_HEADS, 8, l), F32)],
        compiler_params=_cparams(("arbitrary", "arbitrary")),
        name="inproj",
    )(x, mods, norm_g, wg, wm, ws, wst, bcol, brow)


def _round_robin(chains):
    results = [None] * len(chains)
    live = list(enumerate(chains))
    while live:
        still = []
        for idx, chain in live:
            try:
                next(chain)
                still.append((idx, chain))
            except StopIteration as done:
                results[idx] = done.value
        live = still
    return results


def _visibility(c):
    r = lax.broadcasted_iota(jnp.int32, (c, c), 0)
    cc = lax.broadcasted_iota(jnp.int32, (c, c), 1)
    masks = [cc <= r, cc >= r]
    return masks, [jnp.where(m, 1.0, 0.0).astype(BF16) for m in masks]


def _gla_chunk(q, k, v, zs, wup, bup, state, direction, want_out, causal, tri):
    c = k.shape[0]
    logits = jnp.dot(zs.astype(BF16), wup, preferred_element_type=F32) + bup
    yield
    g = _log_sigmoid(logits) * (1.0 / GLA_TAU)
    b = _split_dot(tri, g)
    yield
    b_end = b[c - 1:c, :] if direction == 0 else b[0:1, :]
    kd = (k * jnp.exp(b_end - b)).astype(BF16)
    upd = lax.dot_general(v.astype(BF16), kd, (((0,), (0,)), ((), ())), preferred_element_type=F32)
    s = state[direction]
    state[direction] = jnp.exp(b_end) * s + upd
    if not want_out:
        return None
    b_mid = b[c // 2:c // 2 + 1, :]
    q_in = (q * jnp.exp(b - b_mid)).astype(BF16)
    k_in = (k * jnp.exp(b_mid - b)).astype(BF16)
    att = lax.dot_general(q_in, k_in, (((1,), (1,)), ((), ())), preferred_element_type=F32)
    inter = lax.dot_general((q * jnp.exp(b)).astype(BF16), s.astype(BF16),
                            (((1,), (1,)), ((), ())), preferred_element_type=F32)
    yield
    att = jnp.where(causal, att, 0.0)
    return jnp.dot(att.astype(BF16), v.astype(BF16), preferred_element_type=F32) + inter


def _scan_order(j, n, unroll):
    return [(d, j * unroll + u if d == 0 else n - 1 - (j * unroll + u)) for u in range(unroll) for d in range(2)]


def _gla_kernel(q_ref, k_ref, v_ref, gg_ref, zs_ref, kc_ref, vc_ref, zsc_ref,
                wup_ref, bup_ref, ng_ref, o_ref, s_ref, acc_ref):
    seq = q_ref.shape[1]
    ctx = kc_ref.shape[1]
    n = seq // GLA_C
    nc = ctx // GLA_C
    s_ref[...] = jnp.zeros_like(s_ref)

    def rows(i):
        return pl.ds(pl.multiple_of(i * GLA_C, GLA_C), GLA_C)

    masks, tris = _visibility(GLA_C)
    heads_per_slab = LANES // GLA_DK
    lo = lax.rem(pl.program_id(1), heads_per_slab) * GLA_DK
    lane = lax.broadcasted_iota(jnp.int32, (GLA_C, LANES), 1)
    mine = (lane >= lo) & (lane < lo + GLA_DK)

    def own(t):
        return jnp.where(mine, t, 0.0)

    def ctx_step(j, carry):
        order = _scan_order(j, nc, 1)
        ins = [(own(kc_ref[0, rows(i), :]), vc_ref[0, rows(i), :], zsc_ref[0, rows(i), :]) for _, i in order]
        s = [s_ref[0], s_ref[1]]
        _round_robin([_gla_chunk(None, k, v, zs, wup_ref[0, d], bup_ref[0, d], s, d, False, masks[d], tris[d])
                      for (d, _), (k, v, zs) in zip(order, ins)])
        s_ref[0] = s[0]
        s_ref[1] = s[1]
        return carry

    lax.fori_loop(0, nc, ctx_step, 0)

    def lat_step(j, carry, second):
        order = _scan_order(j, n, GLA_UNROLL)
        ins = [(own(q_ref[0, rows(i), :]), own(k_ref[0, rows(i), :]), v_ref[0, rows(i), :], zs_ref[0, rows(i), :])
               for _, i in order]
        prev = [(acc_ref[rows(i), :], gg_ref[0, rows(i), :]) for _, i in order] if second else None
        s = [s_ref[0], s_ref[1]]
        outs = _round_robin([_gla_chunk(q * (GLA_DK ** -0.5), k, v, zs, wup_ref[0, d], bup_ref[0, d], s, d, True,
                                        masks[d], tris[d])
                             for (d, _), (q, k, v, zs) in zip(order, ins)])
        s_ref[0] = s[0]
        s_ref[1] = s[1]
        for idx, (_, i) in enumerate(order):
            if second:
                total = prev[idx][0] + outs[idx]
                y = total * lax.rsqrt(jnp.mean(total * total, axis=-1, keepdims=True) + EPS) * ng_ref[...]
                o_ref[0, rows(i), :] = (y * _silu(prev[idx][1])).astype(o_ref.dtype)
            else:
                acc_ref[rows(i), :] = outs[idx]
        return carry

    half = n // (2 * GLA_UNROLL)
    lax.fori_loop(0, half, functools.partial(lat_step, second=False), 0)
    lax.fori_loop(half, 2 * half, functools.partial(lat_step, second=True), 0)


def _gla(zg_x, zs_x, zg_c, zs_c, wup, bup, norm_g):
    bsz, seq, _ = zg_x.shape
    ctx = zg_c.shape[1]
    assert seq % (2 * GLA_UNROLL * GLA_C) == 0 and ctx % GLA_C == 0
    h = GLA_HEADS

    hps = LANES // GLA_DK
    qk_slabs = h // hps

    def qk(l, off):
        return pl.BlockSpec((1, l, LANES), lambda b, hd: (b, 0, off + hd // hps))

    def col(l, off):
        return pl.BlockSpec((1, l, LANES), lambda b, hd: (b, 0, off + hd))

    return pl.pallas_call(
        _gla_kernel,
        grid=(bsz, h),
        in_specs=[qk(seq, 0), qk(seq, qk_slabs), col(seq, 2 * qk_slabs), col(seq, 2 * qk_slabs + h),
                  pl.BlockSpec((1, seq, LANES), lambda b, hd: (b, 0, 0)),
                  qk(ctx, qk_slabs), col(ctx, 2 * qk_slabs),
                  pl.BlockSpec((1, ctx, LANES), lambda b, hd: (b, 0, 0)),
                  pl.BlockSpec((1, 2, LANES, LANES), lambda b, hd: (hd, 0, 0, 0)),
                  pl.BlockSpec((1, 2, 1, LANES), lambda b, hd: (hd, 0, 0, 0)),
                  pl.BlockSpec((1, LANES), lambda b, hd: (0, hd))],
        out_specs=pl.BlockSpec((1, seq, LANES), lambda b, hd: (b, 0, hd)),
        out_shape=jax.ShapeDtypeStruct((bsz, seq, h * GLA_DV), BF16),
        scratch_shapes=[pltpu.VMEM((2, LANES, LANES), F32), pltpu.VMEM((seq, LANES), F32)],
        compiler_params=_cparams(("arbitrary", "arbitrary")),
        name="gla",
    )(zg_x, zg_x, zg_x, zg_x, zs_x, zg_c, zg_c, zs_c, wup, bup, norm_g)


def _grid_conv_silu(src_ref, pad_ref, dst_ref, w_ref, b_ref, grid_w, scale):
    l = src_ref.shape[1]
    n_rows = l // grid_w
    margin = grid_w + SUBLANES
    assert pad_ref.shape[0] >= l + 2 * margin and margin % SUBLANES == 0
    pad_ref[0:margin, :] = jnp.zeros((margin, LANES), F32)
    pad_ref[margin + l:2 * margin + l, :] = jnp.zeros((margin, LANES), F32)

    def copy_row(r, carry):
        at = pl.ds(pl.multiple_of(r * grid_w, grid_w), grid_w)
        pad_ref[pl.ds(pl.multiple_of(margin + r * grid_w, SUBLANES), grid_w), :] = src_ref[0, at, :]
        return carry

    lax.fori_loop(0, n_rows, copy_row, 0, unroll=2)

    col = lax.broadcasted_iota(jnp.int32, (grid_w, LANES), 0)
    inside = {dx: (col + dx >= 0) & (col + dx < grid_w) for dx in (-1, 1)}
    rows_dy = (0,) if n_rows == 1 else (-1, 0, 1)

    def body(r, carry):
        base = pl.multiple_of(margin + r * grid_w, SUBLANES)
        acc = jnp.zeros((grid_w, LANES), F32) + b_ref[...]
        for dy in rows_dy:
            for dx in (-1, 0, 1):
                tap = (dy + 1) * 3 + (dx + 1)
                blk = pad_ref[pl.ds(base + dy * grid_w + dx, grid_w), :]
                if dx != 0:
                    blk = jnp.where(inside[dx], blk, 0.0)
                acc = acc + blk * w_ref[tap:tap + 1, :]
        dst_ref[pl.ds(pl.multiple_of(r * grid_w, grid_w), grid_w), :] = _silu(acc) * scale
        return carry

    lax.fori_loop(0, n_rows, body, 0)


def _ml_chunk(qb, k, vt, gcol, grow, state, mstate, direction, want_out, visible):
    c = k.shape[0]
    a_row = grow[direction:direction + 1, :]
    i_row = grow[2 + direction:3 + direction, :]
    a_end = a_row[:, c - 1:c] if direction == 0 else a_row[:, 0:1]
    g = a_end - a_row + i_row
    g_max = jnp.max(g, axis=-1, keepdims=True)
    head = 2 * SUBLANES
    pad_rows = jnp.zeros((LANES - head, c), BF16)
    first = lax.broadcasted_iota(jnp.int32, (head, c), 0) == 0
    kb = k.astype(BF16)
    if want_out:
        c_col = gcol[:, direction:direction + 1] - gcol[:, 2 + direction:3 + direction]
        dmat = jnp.where(visible, a_row - c_col, -jnp.inf)
        d_max = jnp.max(dmat, axis=0, keepdims=True)
        kq = jnp.dot(kb, qb, preferred_element_type=F32)
    yield
    s, m = state[direction], mstate[direction]
    m_new = jnp.maximum(a_end + m, g_max)
    decay = jnp.exp(a_end + m - m_new)
    w = jnp.exp(g - m_new)
    vw = jnp.concatenate([(vt * w).astype(BF16), jnp.where(first, w, 0.0).astype(BF16), pad_rows], axis=0)
    state[direction] = decay * s + jnp.dot(vw, kb, preferred_element_type=F32)
    mstate[direction] = m_new
    if not want_out:
        return None
    inter = a_row + m
    m_t = jnp.maximum(inter, d_max)
    w_inter = jnp.exp(inter - m_t)
    p = (kq * jnp.exp(dmat - m_t)).astype(BF16)
    vt_aug = jnp.concatenate([vt.astype(BF16), jnp.where(first, 1.0, 0.0).astype(BF16), pad_rows], axis=0)
    pv = jnp.dot(vt_aug, p, preferred_element_type=F32)
    sq = jnp.dot(s.astype(BF16), qb, preferred_element_type=F32)
    yield
    both = pv + w_inter * sq
    den = both[ML_DH:ML_DH + 1, :]
    return both[:ML_DH, :] / jnp.maximum(jnp.abs(den), jnp.exp(-m_t))


def _mlstm_kernel(q_ref, k_ref, v_ref, mo_ref, gcol_ref, grow_ref,
                  kc_ref, vc_ref, growc_ref,
                  wq_ref, wk_ref, bq_ref, bk_ref, ng_ref, o_ref,
                  cq_ref, ck_ref, ckc_ref, pad_ref, qt_ref, vt_ref, vct_ref, s_ref, m_ref, acc_ref):
    seq = q_ref.shape[1]
    ctx = kc_ref.shape[1]
    n = seq // ML_C
    nc = ctx // ML_C
    _grid_conv_silu(q_ref, pad_ref, cq_ref, wq_ref, bq_ref, GRID_W, 1.0)
    _grid_conv_silu(k_ref, pad_ref, ck_ref, wk_ref, bk_ref, GRID_W, ML_DH ** -0.5)
    _grid_conv_silu(kc_ref, pad_ref, ckc_ref, wk_ref, bk_ref, ctx, ML_DH ** -0.5)
    s_ref[...] = jnp.zeros_like(s_ref)
    m_ref[...] = jnp.zeros_like(m_ref)

    def rows(i):
        return pl.ds(pl.multiple_of(i * ML_C, ML_C), ML_C)

    def transpose_chunks(i, carry):
        qt_ref[:, rows(i)] = cq_ref[rows(i), :].T.astype(qt_ref.dtype)
        vt_ref[:, rows(i)] = v_ref[0, rows(i), :].T
        return carry

    lax.fori_loop(0, n, transpose_chunks, 0, unroll=2)
    for i in range(nc):
        vct_ref[:, i * ML_C:(i + 1) * ML_C] = vc_ref[0, i * ML_C:(i + 1) * ML_C, :].T

    def load_state():
        return [s_ref[0], s_ref[1]], [m_ref[0, :, 0:1], m_ref[1, :, 0:1]]

    def store_state(s, m):
        for d in range(2):
            s_ref[d] = s[d]
            m_ref[d] = jnp.broadcast_to(m[d], m_ref.shape[1:])

    masks, _ = _visibility(ML_C)
    visible = [masks[1], masks[0]]

    def ctx_step(j, carry):
        order = _scan_order(j, nc, 1)
        ins = [(ckc_ref[rows(i), :], vct_ref[:, rows(i)], growc_ref[0, 0, :, rows(i)]) for _, i in order]
        s, m = load_state()
        _round_robin([_ml_chunk(None, k, vt, None, grow, s, m, d, False, None)
                      for (d, _), (k, vt, grow) in zip(order, ins)])
        store_state(s, m)
        return carry

    lax.fori_loop(0, nc, ctx_step, 0)

    def lat_step(j, carry, second):
        order = _scan_order(j, n, SCAN_UNROLL)
        ins = [(qt_ref[:, rows(i)], ck_ref[rows(i), :], vt_ref[:, rows(i)], gcol_ref[0, rows(i), :],
                grow_ref[0, 0, :, rows(i)]) for _, i in order]
        prev = [(acc_ref[:, rows(i)], mo_ref[0, rows(i), :]) for _, i in order] if second else None
        s, m = load_state()
        outs = _round_robin([_ml_chunk(qb, k, vt, gcol, grow, s, m, d, True, visible[d])
                             for (d, _), (qb, k, vt, gcol, grow) in zip(order, ins)])
        store_state(s, m)
        for idx, (_, i) in enumerate(order):
            if second:
                total = prev[idx][0] + outs[idx]
                y = total * lax.rsqrt(jnp.mean(total * total, axis=0, keepdims=True) + EPS) * ng_ref[...]
                o_ref[0, rows(i), :] = (_sigmoid(prev[idx][1]) * y.T).astype(o_ref.dtype)
            else:
                acc_ref[:, rows(i)] = outs[idx]
        return carry

    half = n // (2 * SCAN_UNROLL)
    lax.fori_loop(0, half, functools.partial(lat_step, second=False), 0)
    lax.fori_loop(half, 2 * half, functools.partial(lat_step, second=True), 0)


def _mlstm(zm_x, gcol_x, grow_x, zm_c, grow_c, conv_w, conv_b, norm_g):
    bsz, seq, _ = zm_x.shape
    ctx = zm_c.shape[1]
    assert seq % (2 * SCAN_UNROLL * ML_C) == 0 and ctx % ML_C == 0 and seq % GRID_W == 0
    h = ML_HEADS

    def col(l, off):
        return pl.BlockSpec((1, l, LANES), lambda b, hd: (b, 0, off + hd))

    def gates(l):
        return [pl.BlockSpec((1, l, LANES), lambda b, hd: (b, 0, hd)),
                pl.BlockSpec((1, 1, 8, l), lambda b, hd: (b, hd, 0, 0))]

    return pl.pallas_call(
        _mlstm_kernel,
        grid=(bsz, h),
        in_specs=[col(seq, 0), col(seq, h), col(seq, 2 * h), col(seq, 3 * h)] + gates(seq)
                 + [col(ctx, h), col(ctx, 2 * h), gates(ctx)[1]]
                 + [pl.BlockSpec((9, LANES), lambda b, hd: (0, hd)),
                    pl.BlockSpec((9, LANES), lambda b, hd: (0, h + hd)),
                    pl.BlockSpec((1, LANES), lambda b, hd: (0, hd)),
                    pl.BlockSpec((1, LANES), lambda b, hd: (0, h + hd)),
                    pl.BlockSpec((LANES, 1), lambda b, hd: (hd, 0))],
        out_specs=pl.BlockSpec((1, seq, LANES), lambda b, hd: (b, 0, hd)),
        out_shape=jax.ShapeDtypeStruct((bsz, seq, h * ML_DH), BF16),
        scratch_shapes=[pltpu.VMEM((seq, LANES), F32), pltpu.VMEM((seq, LANES), F32),
                        pltpu.VMEM((ctx, LANES), F32),
                        pltpu.VMEM((max(seq + 2 * (GRID_W + SUBLANES), 3 * ctx + 2 * SUBLANES), LANES), F32),
                        pltpu.VMEM((LANES, seq), BF16), pltpu.VMEM((LANES, seq), F32),
                        pltpu.VMEM((LANES, ctx), F32),
                        pltpu.VMEM((2, 2 * LANES, LANES), F32), pltpu.VMEM((2, 1, LANES), F32),
                        pltpu.VMEM((LANES, seq), F32)],
        compiler_params=_cparams(("arbitrary", "arbitrary")),
        name="mlstm",
    )(zm_x, zm_x, zm_x, zm_x, gcol_x, grow_x, zm_c, zm_c, grow_c,
      conv_w, conv_w, conv_b, conv_b, norm_g.reshape(-1, 1))


_G0 = 0
_E0 = N_GROUPS
RANK_BITS = 16
RANK_SPAN = 1 << RANK_BITS
ROW_GROUP = 32
GATHER_RING = 3


SUBLANES = 8


def _store_token_tiles(ref2d, val):
    n, w = val.shape
    k = w // LANES
    for c in range(k):
        ref2d[pl.ds(c, n, stride=k), :] = val[:, c * LANES:(c + 1) * LANES]


def _load_token_tiles(ref2d, first, n, k, step):
    return jnp.concatenate([ref2d[pl.ds(first + c, n, stride=step), :] for c in range(k)], axis=1)


def _outproj_kernel(x_ref, ga_ref, ml_ref, mod_ref, wa_ref, wb_ref, g2_ref, wrh_ref, wrl_ref, br_ref,
                    x1_ref, h2_ref, ri_ref, rw_ref, cnt_ref, base_ref):
    tm = x_ref.shape[1]

    @pl.when((pl.program_id(0) == 0) & (pl.program_id(1) == 0))
    def _():
        base_ref[...] = jnp.zeros_like(base_ref)

    mix = (jnp.dot(ga_ref[0], wa_ref[...], preferred_element_type=F32)
           + jnp.dot(ml_ref[0], wb_ref[...], preferred_element_type=F32))
    x1 = x_ref[0] + mod_ref[0, 2:3, :] * mix
    x1_ref[0] = x1
    y = x1 * lax.rsqrt(jnp.mean(x1 * x1, axis=-1, keepdims=True) + EPS) * g2_ref[...]
    h2 = y * (1.0 + mod_ref[0, 4:5, :]) + mod_ref[0, 3:4, :]
    _store_token_tiles(h2_ref, h2)

    h_hi = h2.astype(BF16)
    h_lo = (h2 - h_hi.astype(F32)).astype(BF16)
    logits = (jnp.dot(h_hi, wrh_ref[...], preferred_element_type=F32)
              + jnp.dot(h_lo, wrh_ref[...], preferred_element_type=F32)
              + jnp.dot(h_hi, wrl_ref[...], preferred_element_type=F32)) + br_ref[...]

    lane = lax.broadcasted_iota(jnp.int32, (tm, LANES), 1).astype(F32)
    neg = -jnp.inf
    big = float(LANES)
    is_g = lane < float(_E0)
    lg = jnp.where(is_g, logits, neg)
    gmax = jnp.max(lg, axis=-1, keepdims=True)
    gidx = jnp.min(jnp.where(lg == gmax, lane, big), axis=-1, keepdims=True)
    gw = 1.0 / jnp.sum(jnp.where(is_g, jnp.exp(logits - gmax), 0.0), axis=-1, keepdims=True)
    lo = float(_E0) + float(EXP_PER_GROUP) * gidx
    le = jnp.where((lane >= lo) & (lane < lo + float(EXP_PER_GROUP)), logits, neg)
    v1 = jnp.max(le, axis=-1, keepdims=True)
    i1 = jnp.min(jnp.where(le == v1, lane, big), axis=-1, keepdims=True)
    le2 = jnp.where(lane == i1, neg, le)
    v2 = jnp.max(le2, axis=-1, keepdims=True)
    i2 = jnp.min(jnp.where(le2 == v2, lane, big), axis=-1, keepdims=True)
    t = jnp.exp(v2 - v1)
    w1 = gw / (1.0 + t)
    w2 = gw * t / (1.0 + t)
    e1 = i1 - float(_E0)
    e2 = i2 - float(_E0)

    oh1 = lane == e1
    oh2 = lane == e2
    oh = jnp.where(oh1 | oh2, 1.0, 0.0)
    r = lax.broadcasted_iota(jnp.int32, (tm, tm), 0)
    c = lax.broadcasted_iota(jnp.int32, (tm, tm), 1)
    strict = jnp.where(c < r, 1.0, 0.0).astype(BF16)
    before = jnp.dot(strict, oh.astype(BF16), preferred_element_type=F32) + base_ref[...]
    rank1 = jnp.sum(jnp.where(oh1, before, 0.0), axis=-1, keepdims=True)
    rank2 = jnp.sum(jnp.where(oh2, before, 0.0), axis=-1, keepdims=True)
    total = base_ref[...] + jnp.sum(oh, axis=0, keepdims=True)
    base_ref[...] = total
    cnt_ref[...] = total

    ids = jnp.where(lane == 0.0, e1 * float(RANK_SPAN) + rank1,
                    jnp.where(lane == 1.0, e2 * float(RANK_SPAN) + rank2, 0.0))
    ri_ref[...] = ids.astype(jnp.int32)
    rw_ref[...] = jnp.where(lane == 0.0, w1, jnp.where(lane == 1.0, w2, 0.0))


def _outproj(x, gla_o, ml_o, mods, wa, wb, g2, wrh, wrl, br, tm):
    bsz, seq, d = x.shape
    const = lambda shape: pl.BlockSpec(shape, lambda b, i: (0,) * len(shape))
    tile = lambda w: pl.BlockSpec((1, tm, w), lambda b, i: (b, i, 0))
    flat = lambda rows: pl.BlockSpec((rows, LANES), lambda b, i: (b * (seq // tm) + i, 0))
    return pl.pallas_call(
        _outproj_kernel,
        grid=(bsz, seq // tm),
        in_specs=[tile(d), tile(gla_o.shape[2]), tile(ml_o.shape[2]),
                  pl.BlockSpec((1, N_MOD, d), lambda b, i: (b, 0, 0)),
                  const(wa.shape), const(wb.shape), const((1, d)),
                  const(wrh.shape), const(wrl.shape), const((1, LANES))],
        out_specs=[tile(d),
                   pl.BlockSpec((tm * d // LANES, LANES), lambda b, i: (b * (seq // tm) + i, 0)),
                   flat(tm), flat(tm), const((1, LANES))],
        out_shape=[jax.ShapeDtypeStruct((bsz, seq, d), F32),
                   jax.ShapeDtypeStruct((bsz * seq * d // LANES, LANES), F32),
                   jax.ShapeDtypeStruct((bsz * seq, LANES), jnp.int32),
                   jax.ShapeDtypeStruct((bsz * seq, LANES), F32),
                   jax.ShapeDtypeStruct((1, LANES), F32)],
        scratch_shapes=[pltpu.VMEM((1, LANES), F32)],
        compiler_params=_cparams(("arbitrary", "arbitrary")),
        name="outproj",
    )(x, gla_o, ml_o, mods, wa, wb, g2, wrh, wrl, br)


def _experts_kernel(dest_ref, ps_ref, cnt_ref, be_ref, nv_ref, meta_ref, h_hbm, w1_ref, w2_ref, ytok_hbm,
                    src_ref, xbuf, ybuf, w1c_ref, w2c_ref, gsem, ssem):
    i = pl.program_id(0)
    n_steps = pl.num_programs(0)
    n_used = meta_ref[0]
    tr = SUBLANES
    n_tok = h_hbm.shape[0] // tr
    n_rows = src_ref.shape[0]
    blk = xbuf.shape[1] // tr
    n_x = xbuf.shape[0]
    slot = lax.rem(i, 2)
    xslot = lax.rem(i, n_x)

    def slab(j):
        return pl.ds(pl.multiple_of(j * tr, tr), tr)

    def group_rows(buf, s, g):
        span = ROW_GROUP * tr
        return buf.at[s, pl.ds(pl.multiple_of(g * span, span), span), :]

    def gather_copy(tok, s, g, u):
        return pltpu.make_async_copy(h_hbm.at[slab(tok), :], group_rows(xbuf, s, g).at[pl.ds(u * tr, tr), :],
                                     gsem.at[s])

    def scatter_copy(a, s, g, u):
        return pltpu.make_async_copy(group_rows(ybuf, s, g).at[pl.ds(u * tr, tr), :], ytok_hbm.at[slab(a), :],
                                     ssem.at[s])

    def groups(b):
        nv = jnp.where(b < n_steps, nv_ref[jnp.minimum(b, n_steps - 1)], 0)
        return lax.shift_right_logical(nv + (ROW_GROUP - 1), ROW_GROUP.bit_length() - 1)

    def rows_loop(b, body):
        def step(g, c):
            for u in range(ROW_GROUP):
                body(g, u)
            return c
        lax.fori_loop(0, groups(b), step, 0)

    def issue_gather(b, s):
        def one(g, u):
            tok = lax.shift_right_logical(src_ref[b * blk + g * ROW_GROUP + u], 1)
            gather_copy(jnp.minimum(tok, n_tok - 1), s, g, u).start(priority=u % 2)
        rows_loop(b, one)

    def wait_rows(b, copy):
        n_groups = groups(b)
        bit = blk // ROW_GROUP
        while bit:
            @pl.when((n_groups & bit) != 0)
            def _():
                for _ in range(bit * ROW_GROUP):
                    copy.wait()
            bit //= 2

    def wait_gather(b, s):
        wait_rows(b, gather_copy(0, s, 0, 0))

    def wait_scatter(b, s):
        wait_rows(b, scatter_copy(0, s, 0, 0))

    @pl.when(i == 0)
    def _():
        xbuf[...] = jnp.zeros_like(xbuf)
        ybuf[...] = jnp.zeros_like(ybuf)
        for s in range(2):
            tail = ytok_hbm.at[pl.ds((2 * n_tok + s * blk) * tr, blk * tr), :]
            cp = pltpu.make_async_copy(ybuf.at[s], tail, ssem.at[s])
            cp.start()
            cp.wait()

        def put(a, c):
            src_ref[dest_ref[a]] = a
            return c
        lax.fori_loop(0, 2 * n_tok, put, 0, unroll=16)

        def pad(j, c):
            src_ref[j] = 2 * n_tok + (j & (2 * blk - 1))
            return c

        def pad_expert(e, c):
            lax.fori_loop(ps_ref[e] + cnt_ref[e], ps_ref[e + 1], pad, 0)
            return c
        lax.fori_loop(0, cnt_ref.shape[0], pad_expert, 0)
        for b in range(n_x - 1):
            issue_gather(min(b, n_rows // blk - 1), b)

    @pl.when(i < n_used)
    def _():
        wait_gather(i, xslot)

        @pl.when((i == 0) | (be_ref[i] != be_ref[jnp.maximum(i - 1, 0)]))
        def _():
            w1c_ref[...] = w1_ref[0].astype(BF16)
            w2c_ref[...] = w2_ref[0].astype(BF16)

        @pl.when(i >= 2)
        def _():
            wait_scatter(i - 2, slot)

        row = lax.broadcasted_iota(jnp.int32, (blk, 1), 0)
        x = _load_token_tiles(xbuf.at[xslot], 0, blk, tr, tr)
        x = jnp.where(row < nv_ref[i], x, 0.0).astype(BF16)
        h = jnp.dot(x, w1c_ref[...], preferred_element_type=F32)
        a = (_silu(h[:, :D_EXPERT]) * h[:, D_EXPERT:]).astype(BF16)
        _store_token_tiles(ybuf.at[slot], jnp.dot(a, w2c_ref[...], preferred_element_type=F32))
        rows_loop(i, lambda g, u: scatter_copy(src_ref[i * blk + g * ROW_GROUP + u], slot, g, u)
                  .start(priority=u % 2))
        issue_gather(i + n_x - 1, lax.rem(i + n_x - 1, n_x))

    @pl.when(i == n_steps - 1)
    def _():
        for back in (2, 1):
            wait_scatter(n_used - back, lax.rem(n_used - back, 2))


def _experts(dest, pad_start, counts, block_e, block_nv, meta, h2, w_in, w_out, nb):
    d = w_in.shape[1]
    tr = d // LANES
    assert tr == SUBLANES, "a token row must fill exactly one (8, 128) tile"
    n_tok = h2.shape[0] // tr
    de2 = w_in.shape[2]
    n_rows = nb * MOE_BLK
    assert 2 * n_tok >= 2 * MOE_BLK
    assert MOE_BLK & (MOE_BLK - 1) == 0
    wmap = lambda i, pk, ps, cnt, be, nv, meta: (be[i], 0, 0)
    return pl.pallas_call(
        _experts_kernel,
        grid_spec=pltpu.PrefetchScalarGridSpec(
            num_scalar_prefetch=6, grid=(nb,),
            in_specs=[pl.BlockSpec(memory_space=pl.ANY),
                      pl.BlockSpec((1, d, de2), wmap),
                      pl.BlockSpec((1, de2 // 2, d), wmap)],
            out_specs=pl.BlockSpec(memory_space=pl.ANY),
            scratch_shapes=[pltpu.SMEM((n_rows,), jnp.int32),
                            pltpu.VMEM((GATHER_RING, MOE_BLK * tr, LANES), F32),
                            pltpu.VMEM((2, MOE_BLK * tr, LANES), F32),
                            pltpu.VMEM((d, de2), BF16), pltpu.VMEM((de2 // 2, d), BF16),
                            pltpu.SemaphoreType.DMA((GATHER_RING,)), pltpu.SemaphoreType.DMA((2,))]),
        out_shape=jax.ShapeDtypeStruct(((2 * n_tok + 2 * MOE_BLK) * tr, LANES), F32),
        compiler_params=_cparams(("arbitrary",)),
        name="experts",
    )(dest, pad_start, counts, block_e, block_nv, meta, h2, w_in, w_out)


def _combine_kernel(x1_ref, y_ref, rw_ref, mod_ref, fg_ref, o_ref):
    tc, d = x1_ref.shape
    tr = d // LANES
    y1 = _load_token_tiles(y_ref, 0, tc, tr, 2 * tr)
    y2 = _load_token_tiles(y_ref, tr, tc, tr, 2 * tr)
    moe = rw_ref[:, 0:1] * y1 + rw_ref[:, 1:2] * y2
    x2 = x1_ref[...] + mod_ref[0, 5:6, :] * moe
    o_ref[...] = x2 * lax.rsqrt(jnp.mean(x2 * x2, axis=-1, keepdims=True) + EPS) * fg_ref[...]


def _combine(x1, ytok, rw, mods, fg, tokens_per_batch, tc):
    n_tok, d = x1.shape
    tiles_per_batch = tokens_per_batch // tc
    return pl.pallas_call(
        _combine_kernel,
        grid=(n_tok // tc,),
        in_specs=[pl.BlockSpec((tc, d), lambda i: (i, 0)),
                  pl.BlockSpec((2 * tc * d // LANES, LANES), lambda i: (i, 0)),
                  pl.BlockSpec((tc, LANES), lambda i: (i, 0)),
                  pl.BlockSpec((1, N_MOD, d), lambda i: (i // tiles_per_batch, 0, 0)),
                  pl.BlockSpec((1, d), lambda i: (0, 0))],
        out_specs=pl.BlockSpec((tc, d), lambda i: (i, 0)),
        out_shape=jax.ShapeDtypeStruct((n_tok, d), F32),
        compiler_params=_cparams(("arbitrary",)),
        name="combine",
    )(x1, ytok, rw, mods, fg)


def _prep_inproj_weights(w_in, gla_up_w, gla_up_b, ml_i_b, ml_f_b):
    d = w_in.shape[0]
    o_gq, o_gk, o_gv, o_gg = 0, GLA_QK_W, 2 * GLA_QK_W, 2 * GLA_QK_W + GLA_V_W
    o_lr = o_gg + GLA_V_W
    o_mqk = o_lr + 2 * GLA_LR
    o_mi = o_mqk + 4 * ML_W
    o_mf = o_mi + 2 * ML_HEADS

    wg = w_in[:, o_gq:o_lr]
    wm = w_in[:, o_mqk:o_mi]
    ws = jnp.concatenate([w_in[:, o_lr:o_mqk], w_in[:, o_mi:o_mf + 2 * ML_HEADS],
                          jnp.zeros((d, LANES - 2 * GLA_LR - 4 * ML_HEADS), w_in.dtype)], axis=1)
    bias = jnp.zeros((LANES,), F32)
    bias = bias.at[_MI0:_MI0 + 2 * ML_HEADS].set(ml_i_b.reshape(-1))
    bias = bias.at[_MF0:_MF0 + 2 * ML_HEADS].set(ml_f_b.reshape(-1))
    up = gla_up_w.reshape(2, GLA_LR, GLA_HEADS, GLA_DK).transpose(2, 0, 1, 3)
    ub = gla_up_b.reshape(2, GLA_HEADS, GLA_DK).transpose(1, 0, 2)
    wup = jnp.zeros((GLA_HEADS, 2, LANES, LANES), F32)
    bup = jnp.zeros((GLA_HEADS, 2, 1, LANES), F32)
    for hd in range(GLA_HEADS):
        lo = (hd % (LANES // GLA_DK)) * GLA_DK
        for dr in range(2):
            wup = wup.at[hd, dr, dr * GLA_LR:(dr + 1) * GLA_LR, lo:lo + GLA_DK].set(up[hd, dr])
        bup = bup.at[hd, :, 0, lo:lo + GLA_DK].set(ub[hd])
    return (wg.astype(BF16), wm.astype(BF16), ws.astype(BF16), ws.T.astype(BF16),
            bias.reshape(1, LANES), bias.reshape(LANES, 1), wup.astype(BF16), bup)


def _layer(x, ctx, mods, norm1_g, w_in, gla_up_w, gla_up_b, gla_norm_g, ml_conv_w, ml_conv_b,
           ml_i_b, ml_f_b, ml_norm_g, w_out, norm2_g, rg_w, rg_b, re_w, re_b, e_w_in, e_w_out, final_g):
    bsz, seq, d = x.shape
    n_tok = bsz * seq
    wg, wm, ws, wst, bcol, brow, wup, bup = _prep_inproj_weights(w_in, gla_up_w, gla_up_b, ml_i_b, ml_f_b)
    g1 = norm1_g.reshape(1, d)
    zg_x, zm_x, zs_x, gcol_x, grow_x = _inproj(x, mods, lambda b: b, g1, wg, wm, ws, wst, bcol, brow, 256)
    zg_c, zm_c, zs_c, gcol_c, grow_c = _inproj(ctx, mods, lambda b: bsz, g1, wg, wm, ws, wst, bcol, brow,
                                               min(256, ctx.shape[1]))
    gla_o = _gla(zg_x, zs_x, zg_c, zs_c, wup, bup, gla_norm_g.reshape(1, -1))
    ml_o = _mlstm(zm_x, gcol_x, grow_x, zm_c, grow_c,
                  ml_conv_w.reshape(9, -1), ml_conv_b.reshape(1, -1), ml_norm_g.reshape(1, -1))

    wr = jnp.zeros((d, LANES), F32).at[:, _G0:_E0].set(rg_w).at[:, _E0:_E0 + N_EXPERTS].set(re_w)
    br = jnp.zeros((1, LANES), F32).at[0, _G0:_E0].set(rg_b).at[0, _E0:_E0 + N_EXPERTS].set(re_b)
    wrh = wr.astype(BF16)
    wrl = (wr - wrh.astype(F32)).astype(BF16)
    x1, h2, ri, rw, cnt = _outproj(x, gla_o, ml_o, mods, w_out[:GLA_V_W].astype(BF16),
                                   w_out[GLA_V_W:].astype(BF16), norm2_g.reshape(1, d), wrh, wrl, br, 256)

    counts = cnt[0, :N_EXPERTS].astype(jnp.int32)
    nblk = (counts + MOE_BLK - 1) // MOE_BLK
    blk_end = jnp.cumsum(nblk)
    blk_start = blk_end - nblk
    n_used = blk_end[-1]
    nb_max = (2 * n_tok) // MOE_BLK + N_EXPERTS
    blk = jnp.arange(nb_max, dtype=jnp.int32)
    blk_c = jnp.minimum(blk, n_used - 1)
    onehot = (blk_c[:, None] >= blk_start[None, :]) & (blk_c[:, None] < blk_end[None, :])
    pick = lambda v: jnp.sum(jnp.where(onehot, v[None, :], 0), axis=1)
    block_e = pick(jnp.arange(N_EXPERTS, dtype=jnp.int32)).astype(jnp.int32)
    block_nv = jnp.clip(pick(counts) - (blk_c - pick(blk_start)) * MOE_BLK, 0, MOE_BLK)
    block_nv = jnp.where(blk < n_used, block_nv, 0).astype(jnp.int32)
    pad_start = (jnp.concatenate([blk_start, blk_end[-1:]]) * MOE_BLK).astype(jnp.int32)
    packed = ri[:, 0:2].reshape(-1)
    e_of = lax.shift_right_logical(packed, RANK_BITS)
    start_of = jnp.sum(jnp.where(e_of[:, None] == jnp.arange(N_EXPERTS, dtype=jnp.int32)[None, :],
                                 pad_start[None, :N_EXPERTS], 0), axis=1)
    dest = (start_of + (packed & (RANK_SPAN - 1))).astype(jnp.int32)
    meta = jnp.stack([n_used, n_used]).astype(jnp.int32)

    ytok = _experts(dest, pad_start, counts, block_e, block_nv, meta, h2, e_w_in, e_w_out, nb_max)
    out = _combine(x1.reshape(n_tok, d), ytok, rw, mods, final_g.reshape(1, d), seq, 1024)
    return out.reshape(bsz, seq, d)


def kernel(x, c, ctx, c_ctx, ada_w, ada_b, norm1_g, w_in, gla_up_w, gla_up_b, gla_norm_g, ml_conv_w, ml_conv_b,
           ml_i_b, ml_f_b, ml_norm_g, w_out, norm2_g, router_group_w, router_group_b, router_expert_w,
           router_expert_b, expert_w_in, expert_w_out, final_norm_g):
    assert ada_w.shape[0] == 1, "single-layer stack"
    bsz, d = c.shape
    cc = jnp.concatenate([c, c_ctx[None, :], jnp.zeros((8 - bsz - 1, d), F32)], axis=0)
    mods = _modulation(cc, ada_w[0], ada_b[0]).reshape(8, N_MOD, d)
    return _layer(x, ctx, mods, norm1_g[0], w_in[0], gla_up_w[0], gla_up_b[0], gla_norm_g[0],
                  ml_conv_w[0], ml_conv_b[0], ml_i_b[0], ml_f_b[0], ml_norm_g[0], w_out[0], norm2_g[0],
                  router_group_w[0], router_group_b[0], router_expert_w[0], router_expert_b[0],
                  expert_w_in[0], expert_w_out[0], final_norm_g)
```

```python
import functools

import jax
import jax.numpy as jnp
from jax import lax
from jax.experimental import pallas as pl
from jax.experimental.pallas import tpu as pltpu

F32 = jnp.float32
BF16 = jnp.bfloat16

D_MODEL = 1024
GRID_W = 64
N_MOD = 6
EPS = 1e-6

GLA_HEADS = 4
GLA_DK = 64
GLA_DV = 128
GLA_LR = 16
GLA_TAU = 16.0
GLA_C = 128
GLA_UNROLL = 8
SCAN_UNROLL = 4

ML_HEADS = 4
ML_DH = 128
ML_C = 128

N_GROUPS = 4
EXP_PER_GROUP = 8
N_EXPERTS = N_GROUPS * EXP_PER_GROUP
D_EXPERT = 512
MOE_BLK = 512

GLA_QK_W = GLA_HEADS * GLA_DK
GLA_V_W = GLA_HEADS * GLA_DV
ML_W = ML_HEADS * ML_DH
LANES = 128
VMEM_LIMIT = 56 * 1024 * 1024

_LR0 = 0
_MI0 = 2 * GLA_LR
_MF0 = _MI0 + 2 * ML_HEADS


def _cparams(sem):
    return pltpu.CompilerParams(dimension_semantics=sem, vmem_limit_bytes=VMEM_LIMIT)


def _sigmoid(x):
    return 1.0 / (1.0 + jnp.exp(-x))


def _silu(x):
    return x * _sigmoid(x)


def _log_sigmoid(x):
    return jnp.minimum(x, 0.0) - jnp.log1p(jnp.exp(-jnp.abs(x)))


def _split_dot(a_bf16_exact, x, dims=None):
    x_hi = x.astype(BF16)
    x_lo = (x - x_hi.astype(F32)).astype(BF16)
    if dims is None:
        f = lambda u: jnp.dot(a_bf16_exact, u, preferred_element_type=F32)
    else:
        f = lambda u: lax.dot_general(u, a_bf16_exact, dims, preferred_element_type=F32)
    return f(x_hi) + f(x_lo)


def _mod_kernel(c_ref, w_ref, b_ref, o_ref):
    c = c_ref[...]
    s = _silu(c).astype(BF16)
    o_ref[...] = jnp.dot(s, w_ref[...].astype(BF16), preferred_element_type=F32) + b_ref[...]


def _modulation(cc, ada_w, ada_b):
    rows, d = cc.shape
    n = ada_w.shape[1]
    tn = 1536
    return pl.pallas_call(
        _mod_kernel,
        grid=(n // tn,),
        in_specs=[pl.BlockSpec((rows, d), lambda j: (0, 0)),
                  pl.BlockSpec((d, tn), lambda j: (0, j)),
                  pl.BlockSpec((1, tn), lambda j: (0, j))],
        out_specs=pl.BlockSpec((rows, tn), lambda j: (0, j)),
        out_shape=jax.ShapeDtypeStruct((rows, n), F32),
        compiler_params=_cparams(("arbitrary",)),
        name="mod",
    )(cc, ada_w, ada_b.reshape(1, n))


def _inproj_kernel(x_ref, mod_ref, g_ref, wg_ref, wm_ref, ws_ref, wst_ref, bcol_ref, brow_ref,
                   zg_ref, zm_ref, zs_ref, gcol_ref, grow_ref):
    tm = x_ref.shape[1]
    x = x_ref[0]
    y = x * lax.rsqrt(jnp.mean(x * x, axis=-1, keepdims=True) + EPS) * g_ref[...]
    h = (y * (1.0 + mod_ref[0, 1:2, :]) + mod_ref[0, 0:1, :]).astype(BF16)
    zg_ref[0] = jnp.dot(h, wg_ref[...], preferred_element_type=F32)
    zm_ref[0] = jnp.dot(h, wm_ref[...], preferred_element_type=F32)
    zs = jnp.dot(h, ws_ref[...], preferred_element_type=F32) + bcol_ref[...]
    zst = lax.dot_general(wst_ref[...], h, (((1,), (1,)), ((), ())),
                          preferred_element_type=F32) + brow_ref[...]
    zs_ref[0] = zs

    r = lax.broadcasted_iota(jnp.int32, (tm, tm), 0)
    c = lax.broadcasted_iota(jnp.int32, (tm, tm), 1)
    shift = ML_C.bit_length() - 1
    same = jnp.right_shift(r, shift) == jnp.right_shift(c, shift)
    lower = jnp.where(same & (c <= r), 1.0, 0.0).astype(BF16)
    upper = jnp.where(same & (c >= r), 1.0, 0.0).astype(BF16)
    chunks = range(0, tm, ML_C)
    lsf = _log_sigmoid(zs)
    a_pre = _split_dot(lower, lsf)
    tot = jnp.concatenate([jnp.broadcast_to(a_pre[o + ML_C - 1:o + ML_C, :], (ML_C, LANES)) for o in chunks], axis=0)
    a_suf = tot - a_pre + lsf
    lsft = _log_sigmoid(zst)
    a_pre_t = _split_dot(upper, lsft, (((1,), (0,)), ((), ())))
    tot_t = jnp.concatenate([jnp.broadcast_to(a_pre_t[:, o + ML_C - 1:o + ML_C], (LANES, ML_C)) for o in chunks], axis=1)
    a_suf_t = tot_t - a_pre_t + lsft

    lane = lax.broadcasted_iota(jnp.int32, (tm, LANES), 1)
    for hd in range(ML_HEADS):
        cols = (a_pre[:, _MF0 + hd:_MF0 + hd + 1],
                a_suf[:, _MF0 + ML_HEADS + hd:_MF0 + ML_HEADS + hd + 1],
                zs[:, _MI0 + hd:_MI0 + hd + 1],
                zs[:, _MI0 + ML_HEADS + hd:_MI0 + ML_HEADS + hd + 1])
        slab = jnp.zeros((tm, LANES), F32)
        for j, col in enumerate(cols):
            slab = jnp.where(lane == j, col, slab)
        gcol_ref[0, :, hd * LANES:(hd + 1) * LANES] = slab
        rows = (a_pre_t[_MF0 + hd:_MF0 + hd + 1, :],
                a_suf_t[_MF0 + ML_HEADS + hd:_MF0 + ML_HEADS + hd + 1, :],
                zst[_MI0 + hd:_MI0 + hd + 1, :],
                zst[_MI0 + ML_HEADS + hd:_MI0 + ML_HEADS + hd + 1, :])
        for j, row in enumerate(rows):
            grow_ref[0, hd, j:j + 1, :] = row
        grow_ref[0, hd, 4:8, :] = jnp.zeros((4, tm), F32)


def _inproj(x, mods, mod_row_of_batch, norm_g, wg, wm, ws, wst, bcol, brow, tm):
    bsz, l, d = x.shape
    assert l % tm == 0 and tm % ML_C == 0
    const = lambda shape: pl.BlockSpec(shape, lambda b, i: (0,) * len(shape))
    return pl.pallas_call(
        _inproj_kernel,
        grid=(bsz, l // tm),
        in_specs=[pl.BlockSpec((1, tm, d), lambda b, i: (b, i, 0)),
                  pl.BlockSpec((1, N_MOD, d), lambda b, i: (mod_row_of_batch(b), 0, 0)),
                  const((1, d)), const(wg.shape), const(wm.shape), const(ws.shape), const(wst.shape),
                  const((1, LANES)), const((LANES, 1))],
        out_specs=[pl.BlockSpec((1, tm, wg.shape[1]), lambda b, i: (b, i, 0)),
                   pl.BlockSpec((1, tm, wm.shape[1]), lambda b, i: (b, i, 0)),
                   pl.BlockSpec((1, tm, LANES), lambda b, i: (b, i, 0)),
                   pl.BlockSpec((1, tm, ML_HEADS * LANES), lambda b, i: (b, i, 0)),
                   pl.BlockSpec((1, ML_HEADS, 8, tm), lambda b, i: (b, 0, 0, i))],
        out_shape=[jax.ShapeDtypeStruct((bsz, l, wg.shape[1]), F32),
                   jax.ShapeDtypeStruct((bsz, l, wm.shape[1]), F32),
                   jax.ShapeDtypeStruct((bsz, l, LANES), F32),
                   jax.ShapeDtypeStruct((bsz, l, ML_HEADS * LANES), F32),
                   jax.ShapeDtypeStruct((bsz, ML_HEADS, 8, l), F32)],
        compiler_params=_cparams(("arbitrary", "arbitrary")),
        name="inproj",
    )(x, mods, norm_g, wg, wm, ws, wst, bcol, brow)


def _round_robin(chains):
    results = [None] * len(chains)
    live = list(enumerate(chains))
    while live:
        still = []
        for idx, chain in live:
            try:
                next(chain)
                still.append((idx, chain))
            except StopIteration as done:
                results[idx] = done.value
        live = still
    return results


def _visibility(c):
    r = lax.broadcasted_iota(jnp.int32, (c, c), 0)
    cc = lax.broadcasted_iota(jnp.int32, (c, c), 1)
    masks = [cc <= r, cc >= r]
    return masks, [jnp.where(m, 1.0, 0.0).astype(BF16) for m in masks]


def _gla_chunk(q, k, v, zs, wup, bup, state, direction, want_out, causal, tri):
    c = k.shape[0]
    logits = jnp.dot(zs.astype(BF16), wup, preferred_element_type=F32) + bup
    yield
    g = _log_sigmoid(logits) * (1.0 / GLA_TAU)
    b = _split_dot(tri, g)
    yield
    b_end = b[c - 1:c, :] if direction == 0 else b[0:1, :]
    kd = (k * jnp.exp(b_end - b)).astype(BF16)
    upd = lax.dot_general(v.astype(BF16), kd, (((0,), (0,)), ((), ())), preferred_element_type=F32)
    s = state[direction]
    state[direction] = jnp.exp(b_end) * s + upd
    if not want_out:
        return None
    b_mid = b[c // 2:c // 2 + 1, :]
    q_in = (q * jnp.exp(b - b_mid)).astype(BF16)
    k_in = (k * jnp.exp(b_mid - b)).astype(BF16)
    att = lax.dot_general(q_in, k_in, (((1,), (1,)), ((), ())), preferred_element_type=F32)
    inter = lax.dot_general((q * jnp.exp(b)).astype(BF16), s.astype(BF16),
                            (((1,), (1,)), ((), ())), preferred_element_type=F32)
    yield
    att = jnp.where(causal, att, 0.0)
    return jnp.dot(att.astype(BF16), v.astype(BF16), preferred_element_type=F32) + inter


def _scan_order(j, n, unroll):
    return [(d, j * unroll + u if d == 0 else n - 1 - (j * unroll + u)) for u in range(unroll) for d in range(2)]


def _gla_kernel(q_ref, k_ref, v_ref, gg_ref, zs_ref, kc_ref, vc_ref, zsc_ref,
                wup_ref, bup_ref, ng_ref, o_ref, s_ref, acc_ref):
    seq = q_ref.shape[1]
    ctx = kc_ref.shape[1]
    n = seq // GLA_C
    nc = ctx // GLA_C
    s_ref[...] = jnp.zeros_like(s_ref)

    def rows(i):
        return pl.ds(pl.multiple_of(i * GLA_C, GLA_C), GLA_C)

    masks, tris = _visibility(GLA_C)
    heads_per_slab = LANES // GLA_DK
    lo = lax.rem(pl.program_id(1), heads_per_slab) * GLA_DK
    lane = lax.broadcasted_iota(jnp.int32, (GLA_C, LANES), 1)
    mine = (lane >= lo) & (lane < lo + GLA_DK)

    def own(t):
        return jnp.where(mine, t, 0.0)

    def ctx_step(j, carry):
        order = _scan_order(j, nc, 1)
        ins = [(own(kc_ref[0, rows(i), :]), vc_ref[0, rows(i), :], zsc_ref[0, rows(i), :]) for _, i in order]
        s = [s_ref[0], s_ref[1]]
        _round_robin([_gla_chunk(None, k, v, zs, wup_ref[0, d], bup_ref[0, d], s, d, False, masks[d], tris[d])
                      for (d, _), (k, v, zs) in zip(order, ins)])
        s_ref[0] = s[0]
        s_ref[1] = s[1]
        return carry

    lax.fori_loop(0, nc, ctx_step, 0)

    def lat_step(j, carry, second):
        order = _scan_order(j, n, GLA_UNROLL)
        ins = [(own(q_ref[0, rows(i), :]), own(k_ref[0, rows(i), :]), v_ref[0, rows(i), :], zs_ref[0, rows(i), :])
               for _, i in order]
        prev = [(acc_ref[rows(i), :], gg_ref[0, rows(i), :]) for _, i in order] if second else None
        s = [s_ref[0], s_ref[1]]
        outs = _round_robin([_gla_chunk(q * (GLA_DK ** -0.5), k, v, zs, wup_ref[0, d], bup_ref[0, d], s, d, True,
                                        masks[d], tris[d])
                             for (d, _), (q, k, v, zs) in zip(order, ins)])
        s_ref[0] = s[0]
        s_ref[1] = s[1]
        for idx, (_, i) in enumerate(order):
            if second:
                total = prev[idx][0] + outs[idx]
                y = total * lax.rsqrt(jnp.mean(total * total, axis=-1, keepdims=True) + EPS) * ng_ref[...]
                o_ref[0, rows(i), :] = (y * _silu(prev[idx][1])).astype(o_ref.dtype)
            else:
                acc_ref[rows(i), :] = outs[idx]
        return carry

    half = n // (2 * GLA_UNROLL)
    lax.fori_loop(0, half, functools.partial(lat_step, second=False), 0)
    lax.fori_loop(half, 2 * half, functools.partial(lat_step, second=True), 0)


def _gla(zg_x, zs_x, zg_c, zs_c, wup, bup, norm_g):
    bsz, seq, _ = zg_x.shape
    ctx = zg_c.shape[1]
    assert seq % (2 * GLA_UNROLL * GLA_C) == 0 and ctx % GLA_C == 0
    h = GLA_HEADS

    hps = LANES // GLA_DK
    qk_slabs = h // hps

    def qk(l, off):
        return pl.BlockSpec((1, l, LANES), lambda b, hd: (b, 0, off + hd // hps))

    def col(l, off):
        return pl.BlockSpec((1, l, LANES), lambda b, hd: (b, 0, off + hd))

    return pl.pallas_call(
        _gla_kernel,
        grid=(bsz, h),
        in_specs=[qk(seq, 0), qk(seq, qk_slabs), col(seq, 2 * qk_slabs), col(seq, 2 * qk_slabs + h),
                  pl.BlockSpec((1, seq, LANES), lambda b, hd: (b, 0, 0)),
                  qk(ctx, qk_slabs), col(ctx, 2 * qk_slabs),
                  pl.BlockSpec((1, ctx, LANES), lambda b, hd: (b, 0, 0)),
                  pl.BlockSpec((1, 2, LANES, LANES), lambda b, hd: (hd, 0, 0, 0)),
                  pl.BlockSpec((1, 2, 1, LANES), lambda b, hd: (hd, 0, 0, 0)),
                  pl.BlockSpec((1, LANES), lambda b, hd: (0, hd))],
        out_specs=pl.BlockSpec((1, seq, LANES), lambda b, hd: (b, 0, hd)),
        out_shape=jax.ShapeDtypeStruct((bsz, seq, h * GLA_DV), BF16),
        scratch_shapes=[pltpu.VMEM((2, LANES, LANES), F32), pltpu.VMEM((seq, LANES), F32)],
        compiler_params=_cparams(("arbitrary", "arbitrary")),
        name="gla",
    )(zg_x, zg_x, zg_x, zg_x, zs_x, zg_c, zg_c, zs_c, wup, bup, norm_g)


def _grid_conv_silu(src_ref, pad_ref, dst_ref, w_ref, b_ref, grid_w, scale):
    l = src_ref.shape[1]
    n_rows = l // grid_w
    margin = grid_w + SUBLANES
    assert pad_ref.shape[0] >= l + 2 * margin and margin % SUBLANES == 0
    pad_ref[0:margin, :] = jnp.zeros((margin, LANES), F32)
    pad_ref[margin + l:2 * margin + l, :] = jnp.zeros((margin, LANES), F32)

    def copy_row(r, carry):
        at = pl.ds(pl.multiple_of(r * grid_w, grid_w), grid_w)
        pad_ref[pl.ds(pl.multiple_of(margin + r * grid_w, SUBLANES), grid_w), :] = src_ref[0, at, :]
        return carry

    lax.fori_loop(0, n_rows, copy_row, 0, unroll=2)

    col = lax.broadcasted_iota(jnp.int32, (grid_w, LANES), 0)
    inside = {dx: (col + dx >= 0) & (col + dx < grid_w) for dx in (-1, 1)}
    rows_dy = (0,) if n_rows == 1 else (-1, 0, 1)

    def body(r, carry):
        base = pl.multiple_of(margin + r * grid_w, SUBLANES)
        acc = jnp.zeros((grid_w, LANES), F32) + b_ref[...]
        for dy in rows_dy:
            for dx in (-1, 0, 1):
                tap = (dy + 1) * 3 + (dx + 1)
                blk = pad_ref[pl.ds(base + dy * grid_w + dx, grid_w), :]
                if dx != 0:
                    blk = jnp.where(inside[dx], blk, 0.0)
                acc = acc + blk * w_ref[tap:tap + 1, :]
        dst_ref[pl.ds(pl.multiple_of(r * grid_w, grid_w), grid_w), :] = _silu(acc) * scale
        return carry

    lax.fori_loop(0, n_rows, body, 0)


def _ml_chunk(qb, k, vt, gcol, grow, state, mstate, direction, want_out, visible):
    c = k.shape[0]
    a_row = grow[direction:direction + 1, :]
    i_row = grow[2 + direction:3 + direction, :]
    a_end = a_row[:, c - 1:c] if direction == 0 else a_row[:, 0:1]
    g = a_end - a_row + i_row
    g_max = jnp.max(g, axis=-1, keepdims=True)
    head = 2 * SUBLANES
    pad_rows = jnp.zeros((LANES - head, c), BF16)
    first = lax.broadcasted_iota(jnp.int32, (head, c), 0) == 0
    kb = k.astype(BF16)
    if want_out:
        c_col = gcol[:, direction:direction + 1] - gcol[:, 2 + direction:3 + direction]
        dmat = jnp.where(visible, a_row - c_col, -jnp.inf)
        d_max = jnp.max(dmat, axis=0, keepdims=True)
        kq = jnp.dot(kb, qb, preferred_element_type=F32)
    yield
    s, m = state[direction], mstate[direction]
    m_new = jnp.maximum(a_end + m, g_max)
    decay = jnp.exp(a_end + m - m_new)
    w = jnp.exp(g - m_new)
    vw = jnp.concatenate([(vt * w).astype(BF16), jnp.where(first, w, 0.0).astype(BF16), pad_rows], axis=0)
    state[direction] = decay * s + jnp.dot(vw, kb, preferred_element_type=F32)
    mstate[direction] = m_new
    if not want_out:
        return None
    inter = a_row + m
    m_t = jnp.maximum(inter, d_max)
    w_inter = jnp.exp(inter - m_t)
    p = (kq * jnp.exp(dmat - m_t)).astype(BF16)
    vt_aug = jnp.concatenate([vt.astype(BF16), jnp.where(first, 1.0, 0.0).astype(BF16), pad_rows], axis=0)
    pv = jnp.dot(vt_aug, p, preferred_element_type=F32)
    sq = jnp.dot(s.astype(BF16), qb, preferred_element_type=F32)
    yield
    both = pv + w_inter * sq
    den = both[ML_DH:ML_DH + 1, :]
    return both[:ML_DH, :] / jnp.maximum(jnp.abs(den), jnp.exp(-m_t))


def _mlstm_kernel(q_ref, k_ref, v_ref, mo_ref, gcol_ref, grow_ref,
                  kc_ref, vc_ref, growc_ref,
                  wq_ref, wk_ref, bq_ref, bk_ref, ng_ref, o_ref,
                  cq_ref, ck_ref, ckc_ref, pad_ref, qt_ref, vt_ref, vct_ref, s_ref, m_ref, acc_ref):
    seq = q_ref.shape[1]
    ctx = kc_ref.shape[1]
    n = seq // ML_C
    nc = ctx // ML_C
    _grid_conv_silu(q_ref, pad_ref, cq_ref, wq_ref, bq_ref, GRID_W, 1.0)
    _grid_conv_silu(k_ref, pad_ref, ck_ref, wk_ref, bk_ref, GRID_W, ML_DH ** -0.5)
    _grid_conv_silu(kc_ref, pad_ref, ckc_ref, wk_ref, bk_ref, ctx, ML_DH ** -0.5)
    s_ref[...] = jnp.zeros_like(s_ref)
    m_ref[...] = jnp.zeros_like(m_ref)

    def rows(i):
        return pl.ds(pl.multiple_of(i * ML_C, ML_C), ML_C)

    def transpose_chunks(i, carry):
        qt_ref[:, rows(i)] = cq_ref[rows(i), :].T.astype(qt_ref.dtype)
        vt_ref[:, rows(i)] = v_ref[0, rows(i), :].T
        return carry

    lax.fori_loop(0, n, transpose_chunks, 0, unroll=2)
    for i in range(nc):
        vct_ref[:, i * ML_C:(i + 1) * ML_C] = vc_ref[0, i * ML_C:(i + 1) * ML_C, :].T

    def load_state():
        return [s_ref[0], s_ref[1]], [m_ref[0, :, 0:1], m_ref[1, :, 0:1]]

    def store_state(s, m):
        for d in range(2):
            s_ref[d] = s[d]
            m_ref[d] = jnp.broadcast_to(m[d], m_ref.shape[1:])

    masks, _ = _visibility(ML_C)
    visible = [masks[1], masks[0]]

    def ctx_step(j, carry):
        order = _scan_order(j, nc, 1)
        ins = [(ckc_ref[rows(i), :], vct_ref[:, rows(i)], growc_ref[0, 0, :, rows(i)]) for _, i in order]
        s, m = load_state()
        _round_robin([_ml_chunk(None, k, vt, None, grow, s, m, d, False, None)
                      for (d, _), (k, vt, grow) in zip(order, ins)])
        store_state(s, m)
        return carry

    lax.fori_loop(0, nc, ctx_step, 0)

    def lat_step(j, carry, second):
        order = _scan_order(j, n, SCAN_UNROLL)
        ins = [(qt_ref[:, rows(i)], ck_ref[rows(i), :], vt_ref[:, rows(i)], gcol_ref[0, rows(i), :],
                grow_ref[0, 0, :, rows(i)]) for _, i in order]
        prev = [(acc_ref[:, rows(i)], mo_ref[0, rows(i), :]) for _, i in order] if second else None
        s, m = load_state()
        outs = _round_robin([_ml_chunk(qb, k, vt, gcol, grow, s, m, d, True, visible[d])
                             for (d, _), (qb, k, vt, gcol, grow) in zip(order, ins)])
        store_state(s, m)
        for idx, (_, i) in enumerate(order):
            if second:
                total = prev[idx][0] + outs[idx]
                y = total * lax.rsqrt(jnp.mean(total * total, axis=0, keepdims=True) + EPS) * ng_ref[...]
                o_ref[0, rows(i), :] = (_sigmoid(prev[idx][1]) * y.T).astype(o_ref.dtype)
            else:
                acc_ref[:, rows(i)] = outs[idx]
        return carry

    half = n // (2 * SCAN_UNROLL)
    lax.fori_loop(0, half, functools.partial(lat_step, second=False), 0)
    lax.fori_loop(half, 2 * half, functools.partial(lat_step, second=True), 0)


def _mlstm(zm_x, gcol_x, grow_x, zm_c, grow_c, conv_w, conv_b, norm_g):
    bsz, seq, _ = zm_x.shape
    ctx = zm_c.shape[1]
    assert seq % (2 * SCAN_UNROLL * ML_C) == 0 and ctx % ML_C == 0 and seq % GRID_W == 0
    h = ML_HEADS

    def col(l, off):
        return pl.BlockSpec((1, l, LANES), lambda b, hd: (b, 0, off + hd))

    def gates(l):
        return [pl.BlockSpec((1, l, LANES), lambda b, hd: (b, 0, hd)),
                pl.BlockSpec((1, 1, 8, l), lambda b, hd: (b, hd, 0, 0))]

    return pl.pallas_call(
        _mlstm_kernel,
        grid=(bsz, h),
        in_specs=[col(seq, 0), col(seq, h), col(seq, 2 * h), col(seq, 3 * h)] + gates(seq)
                 + [col(ctx, h), col(ctx, 2 * h), gates(ctx)[1]]
                 + [pl.BlockSpec((9, LANES), lambda b, hd: (0, hd)),
                    pl.BlockSpec((9, LANES), lambda b, hd: (0, h + hd)),
                    pl.BlockSpec((1, LANES), lambda b, hd: (0, hd)),
                    pl.BlockSpec((1, LANES), lambda b, hd: (0, h + hd)),
                    pl.BlockSpec((LANES, 1), lambda b, hd: (hd, 0))],
        out_specs=pl.BlockSpec((1, seq, LANES), lambda b, hd: (b, 0, hd)),
        out_shape=jax.ShapeDtypeStruct((bsz, seq, h * ML_DH), BF16),
        scratch_shapes=[pltpu.VMEM((seq, LANES), F32), pltpu.VMEM((seq, LANES), F32),
                        pltpu.VMEM((ctx, LANES), F32),
                        pltpu.VMEM((max(seq + 2 * (GRID_W + SUBLANES), 3 * ctx + 2 * SUBLANES), LANES), F32),
                        pltpu.VMEM((LANES, seq), BF16), pltpu.VMEM((LANES, seq), F32),
                        pltpu.VMEM((LANES, ctx), F32),
                        pltpu.VMEM((2, 2 * LANES, LANES), F32), pltpu.VMEM((2, 1, LANES), F32),
                        pltpu.VMEM((LANES, seq), F32)],
        compiler_params=_cparams(("arbitrary", "arbitrary")),
        name="mlstm",
    )(zm_x, zm_x, zm_x, zm_x, gcol_x, grow_x, zm_c, zm_c, grow_c,
      conv_w, conv_w, conv_b, conv_b, norm_g.reshape(-1, 1))


_G0 = 0
_E0 = N_GROUPS
RANK_BITS = 16
RANK_SPAN = 1 << RANK_BITS
ROW_GROUP = 32
GATHER_RING = 3


SUBLANES = 8


def _store_token_tiles(ref2d, val):
    n, w = val.shape
    k = w // LANES
    for c in range(k):
        ref2d[pl.ds(c, n, stride=k), :] = val[:, c * LANES:(c + 1) * LANES]


def _load_token_tiles(ref2d, first, n, k, step):
    return jnp.concatenate([ref2d[pl.ds(first + c, n, stride=step), :] for c in range(k)], axis=1)


def _outproj_kernel(x_ref, ga_ref, ml_ref, mod_ref, wa_ref, wb_ref, g2_ref, wrh_ref, wrl_ref, br_ref,
                    x1_ref, h2_ref, ri_ref, rw_ref, cnt_ref, base_ref):
    tm = x_ref.shape[1]

    @pl.when((pl.program_id(0) == 0) & (pl.program_id(1) == 0))
    def _():
        base_ref[...] = jnp.zeros_like(base_ref)

    mix = (jnp.dot(ga_ref[0], wa_ref[...], preferred_element_type=F32)
           + jnp.dot(ml_ref[0], wb_ref[...], preferred_element_type=F32))
    x1 = x_ref[0] + mod_ref[0, 2:3, :] * mix
    x1_ref[0] = x1
    y = x1 * lax.rsqrt(jnp.mean(x1 * x1, axis=-1, keepdims=True) + EPS) * g2_ref[...]
    h2 = y * (1.0 + mod_ref[0, 4:5, :]) + mod_ref[0, 3:4, :]
    _store_token_tiles(h2_ref, h2)

    h_hi = h2.astype(BF16)
    h_lo = (h2 - h_hi.astype(F32)).astype(BF16)
    logits = (jnp.dot(h_hi, wrh_ref[...], preferred_element_type=F32)
              + jnp.dot(h_lo, wrh_ref[...], preferred_element_type=F32)
              + jnp.dot(h_hi, wrl_ref[...], preferred_element_type=F32)) + br_ref[...]

    lane = lax.broadcasted_iota(jnp.int32, (tm, LANES), 1).astype(F32)
    neg = -jnp.inf
    big = float(LANES)
    is_g = lane < float(_E0)
    lg = jnp.where(is_g, logits, neg)
    gmax = jnp.max(lg, axis=-1, keepdims=True)
    gidx = jnp.min(jnp.where(lg == gmax, lane, big), axis=-1, keepdims=True)
    gw = 1.0 / jnp.sum(jnp.where(is_g, jnp.exp(logits - gmax), 0.0), axis=-1, keepdims=True)
    lo = float(_E0) + float(EXP_PER_GROUP) * gidx
    le = jnp.where((lane >= lo) & (lane < lo + float(EXP_PER_GROUP)), logits, neg)
    v1 = jnp.max(le, axis=-1, keepdims=True)
    i1 = jnp.min(jnp.where(le == v1, lane, big), axis=-1, keepdims=True)
    le2 = jnp.where(lane == i1, neg, le)
    v2 = jnp.max(le2, axis=-1, keepdims=True)
    i2 = jnp.min(jnp.where(le2 == v2, lane, big), axis=-1, keepdims=True)
    t = jnp.exp(v2 - v1)
    w1 = gw / (1.0 + t)
    w2 = gw * t / (1.0 + t)
    e1 = i1 - float(_E0)
    e2 = i2 - float(_E0)

    oh1 = lane == e1
    oh2 = lane == e2
    oh = jnp.where(oh1 | oh2, 1.0, 0.0)
    r = lax.broadcasted_iota(jnp.int32, (tm, tm), 0)
    c = lax.broadcasted_iota(jnp.int32, (tm, tm), 1)
    strict = jnp.where(c < r, 1.0, 0.0).astype(BF16)
    before = jnp.dot(strict, oh.astype(BF16), preferred_element_type=F32) + base_ref[...]
    rank1 = jnp.sum(jnp.where(oh1, before, 0.0), axis=-1, keepdims=True)
    rank2 = jnp.sum(jnp.where(oh2, before, 0.0), axis=-1, keepdims=True)
    total = base_ref[...] + jnp.sum(oh, axis=0, keepdims=True)
    base_ref[...] = total
    cnt_ref[...] = total

    ids = jnp.where(lane == 0.0, e1 * float(RANK_SPAN) + rank1,
                    jnp.where(lane == 1.0, e2 * float(RANK_SPAN) + rank2, 0.0))
    ri_ref[...] = ids.astype(jnp.int32)
    rw_ref[...] = jnp.where(lane == 0.0, w1, jnp.where(lane == 1.0, w2, 0.0))


def _outproj(x, gla_o, ml_o, mods, wa, wb, g2, wrh, wrl, br, tm):
    bsz, seq, d = x.shape
    const = lambda shape: pl.BlockSpec(shape, lambda b, i: (0,) * len(shape))
    tile = lambda w: pl.BlockSpec((1, tm, w), lambda b, i: (b, i, 0))
    flat = lambda rows: pl.BlockSpec((rows, LANES), lambda b, i: (b * (seq // tm) + i, 0))
    return pl.pallas_call(
        _outproj_kernel,
        grid=(bsz, seq // tm),
        in_specs=[tile(d), tile(gla_o.shape[2]), tile(ml_o.shape[2]),
                  pl.BlockSpec((1, N_MOD, d), lambda b, i: (b, 0, 0)),
                  const(wa.shape), const(wb.shape), const((1, d)),
                  const(wrh.shape), const(wrl.shape), const((1, LANES))],
        out_specs=[tile(d),
                   pl.BlockSpec((tm * d // LANES, LANES), lambda b, i: (b * (seq // tm) + i, 0)),
                   flat(tm), flat(tm), const((1, LANES))],
        out_shape=[jax.ShapeDtypeStruct((bsz, seq, d), F32),
                   jax.ShapeDtypeStruct((bsz * seq * d // LANES, LANES), F32),
                   jax.ShapeDtypeStruct((bsz * seq, LANES), jnp.int32),
                   jax.ShapeDtypeStruct((bsz * seq, LANES), F32),
                   jax.ShapeDtypeStruct((1, LANES), F32)],
        scratch_shapes=[pltpu.VMEM((1, LANES), F32)],
        compiler_params=_cparams(("arbitrary", "arbitrary")),
        name="outproj",
    )(x, gla_o, ml_o, mods, wa, wb, g2, wrh, wrl, br)


def _experts_kernel(dest_ref, ps_ref, cnt_ref, be_ref, nv_ref, meta_ref, h_hbm, w1_ref, w2_ref, ytok_hbm,
                    src_ref, xbuf, ybuf, w1c_ref, w2c_ref, gsem, ssem):
    i = pl.program_id(0)
    n_steps = pl.num_programs(0)
    n_used = meta_ref[0]
    tr = SUBLANES
    n_tok = h_hbm.shape[0] // tr
    n_rows = src_ref.shape[0]
    blk = xbuf.shape[1] // tr
    n_x = xbuf.shape[0]
    slot = lax.rem(i, 2)
    xslot = lax.rem(i, n_x)

    def slab(j):
        return pl.ds(pl.multiple_of(j * tr, tr), tr)

    def group_rows(buf, s, g):
        span = ROW_GROUP * tr
        return buf.at[s, pl.ds(pl.multiple_of(g * span, span), span), :]

    def gather_copy(tok, s, g, u):
        return pltpu.make_async_copy(h_hbm.at[slab(tok), :], group_rows(xbuf, s, g).at[pl.ds(u * tr, tr), :],
                                     gsem.at[s])

    def scatter_copy(a, s, g, u):
        return pltpu.make_async_copy(group_rows(ybuf, s, g).at[pl.ds(u * tr, tr), :], ytok_hbm.at[slab(a), :],
                                     ssem.at[s])

    def groups(b):
        nv = jnp.where(b < n_steps, nv_ref[jnp.minimum(b, n_steps - 1)], 0)
        return lax.shift_right_logical(nv + (ROW_GROUP - 1), ROW_GROUP.bit_length() - 1)

    def rows_loop(b, body):
        def step(g, c):
            for u in range(ROW_GROUP):
                body(g, u)
            return c
        lax.fori_loop(0, groups(b), step, 0)

    def issue_gather(b, s):
        def one(g, u):
            tok = lax.shift_right_logical(src_ref[b * blk + g * ROW_GROUP + u], 1)
            gather_copy(jnp.minimum(tok, n_tok - 1), s, g, u).start(priority=u % 2)
        rows_loop(b, one)

    def wait_rows(b, copy):
        n_groups = groups(b)
        bit = blk // ROW_GROUP
        while bit:
            @pl.when((n_groups & bit) != 0)
            def _():
                for _ in range(bit * ROW_GROUP):
                    copy.wait()
            bit //= 2

    def wait_gather(b, s):
        wait_rows(b, gather_copy(0, s, 0, 0))

    def wait_scatter(b, s):
        wait_rows(b, scatter_copy(0, s, 0, 0))

    @pl.when(i == 0)
    def _():
        xbuf[...] = jnp.zeros_like(xbuf)
        ybuf[...] = jnp.zeros_like(ybuf)
        for s in range(2):
            tail = ytok_hbm.at[pl.ds((2 * n_tok + s * blk) * tr, blk * tr), :]
            cp = pltpu.make_async_copy(ybuf.at[s], tail, ssem.at[s])
            cp.start()
            cp.wait()

        def put(a, c):
            src_ref[dest_ref[a]] = a
            return c
        lax.fori_loop(0, 2 * n_tok, put, 0, unroll=16)

        def pad(j, c):
            src_ref[j] = 2 * n_tok + (j & (2 * blk - 1))
            return c

        def pad_expert(e, c):
            lax.fori_loop(ps_ref[e] + cnt_ref[e], ps_ref[e + 1], pad, 0)
            return c
        lax.fori_loop(0, cnt_ref.shape[0], pad_expert, 0)
        for b in range(n_x - 1):
            issue_gather(min(b, n_rows // blk - 1), b)

    @pl.when(i < n_used)
    def _():
        wait_gather(i, xslot)

        @pl.when((i == 0) | (be_ref[i] != be_ref[jnp.maximum(i - 1, 0)]))
        def _():
            w1c_ref[...] = w1_ref[0].astype(BF16)
            w2c_ref[...] = w2_ref[0].astype(BF16)

        @pl.when(i >= 2)
        def _():
            wait_scatter(i - 2, slot)

        row = lax.broadcasted_iota(jnp.int32, (blk, 1), 0)
        x = _load_token_tiles(xbuf.at[xslot], 0, blk, tr, tr)
        x = jnp.where(row < nv_ref[i], x, 0.0).astype(BF16)
        h = jnp.dot(x, w1c_ref[...], preferred_element_type=F32)
        a = (_silu(h[:, :D_EXPERT]) * h[:, D_EXPERT:]).astype(BF16)
        _store_token_tiles(ybuf.at[slot], jnp.dot(a, w2c_ref[...], preferred_element_type=F32))
        rows_loop(i, lambda g, u: scatter_copy(src_ref[i * blk + g * ROW_GROUP + u], slot, g, u)
                  .start(priority=u % 2))
        issue_gather(i + n_x - 1, lax.rem(i + n_x - 1, n_x))

    @pl.when(i == n_steps - 1)
    def _():
        for back in (2, 1):
            wait_scatter(n_used - back, lax.rem(n_used - back, 2))


def _experts(dest, pad_start, counts, block_e, block_nv, meta, h2, w_in, w_out, nb):
    d = w_in.shape[1]
    tr = d // LANES
    assert tr == SUBLANES, "a token row must fill exactly one (8, 128) tile"
    n_tok = h2.shape[0] // tr
    de2 = w_in.shape[2]
    n_rows = nb * MOE_BLK
    assert 2 * n_tok >= 2 * MOE_BLK
    assert MOE_BLK & (MOE_BLK - 1) == 0
    wmap = lambda i, pk, ps, cnt, be, nv, meta: (be[i], 0, 0)
    return pl.pallas_call(
        _experts_kernel,
        grid_spec=pltpu.PrefetchScalarGridSpec(
            num_scalar_prefetch=6, grid=(nb,),
            in_specs=[pl.BlockSpec(memory_space=pl.ANY),
                      pl.BlockSpec((1, d, de2), wmap),
                      pl.BlockSpec((1, de2 // 2, d), wmap)],
            out_specs=pl.BlockSpec(memory_space=pl.ANY),
            scratch_shapes=[pltpu.SMEM((n_rows,), jnp.int32),
                            pltpu.VMEM((GATHER_RING, MOE_BLK * tr, LANES), F32),
                            pltpu.VMEM((2, MOE_BLK * tr, LANES), F32),
                            pltpu.VMEM((d, de2), BF16), pltpu.VMEM((de2 // 2, d), BF16),
                            pltpu.SemaphoreType.DMA((GATHER_RING,)), pltpu.SemaphoreType.DMA((2,))]),
        out_shape=jax.ShapeDtypeStruct(((2 * n_tok + 2 * MOE_BLK) * tr, LANES), F32),
        compiler_params=_cparams(("arbitrary",)),
        name="experts",
    )(dest, pad_start, counts, block_e, block_nv, meta, h2, w_in, w_out)


def _combine_kernel(x1_ref, y_ref, rw_ref, mod_ref, fg_ref, o_ref):
    tc, d = x1_ref.shape
    tr = d // LANES
    y1 = _load_token_tiles(y_ref, 0, tc, tr, 2 * tr)
    y2 = _load_token_tiles(y_ref, tr, tc, tr, 2 * tr)
    moe = rw_ref[:, 0:1] * y1 + rw_ref[:, 1:2] * y2
    x2 = x1_ref[...] + mod_ref[0, 5:6, :] * moe
    o_ref[...] = x2 * lax.rsqrt(jnp.mean(x2 * x2, axis=-1, keepdims=True) + EPS) * fg_ref[...]


def _combine(x1, ytok, rw, mods, fg, tokens_per_batch, tc):
    n_tok, d = x1.shape
    tiles_per_batch = tokens_per_batch // tc
    return pl.pallas_call(
        _combine_kernel,
        grid=(n_tok // tc,),
        in_specs=[pl.BlockSpec((tc, d), lambda i: (i, 0)),
                  pl.BlockSpec((2 * tc * d // LANES, LANES), lambda i: (i, 0)),
                  pl.BlockSpec((tc, LANES), lambda i: (i, 0)),
                  pl.BlockSpec((1, N_MOD, d), lambda i: (i // tiles_per_batch, 0, 0)),
                  pl.BlockSpec((1, d), lambda i: (0, 0))],
        out_specs=pl.BlockSpec((tc, d), lambda i: (i, 0)),
        out_shape=jax.ShapeDtypeStruct((n_tok, d), F32),
        compiler_params=_cparams(("arbitrary",)),
        name="combine",
    )(x1, ytok, rw, mods, fg)


def _prep_inproj_weights(w_in, gla_up_w, gla_up_b, ml_i_b, ml_f_b):
    d = w_in.shape[0]
    o_gq, o_gk, o_gv, o_gg = 0, GLA_QK_W, 2 * GLA_QK_W, 2 * GLA_QK_W + GLA_V_W
    o_lr = o_gg + GLA_V_W
    o_mqk = o_lr + 2 * GLA_LR
    o_mi = o_mqk + 4 * ML_W
    o_mf = o_mi + 2 * ML_HEADS

    wg = w_in[:, o_gq:o_lr]
    wm = w_in[:, o_mqk:o_mi]
    ws = jnp.concatenate([w_in[:, o_lr:o_mqk], w_in[:, o_mi:o_mf + 2 * ML_HEADS],
                          jnp.zeros((d, LANES - 2 * GLA_LR - 4 * ML_HEADS), w_in.dtype)], axis=1)
    bias = jnp.zeros((LANES,), F32)
    bias = bias.at[_MI0:_MI0 + 2 * ML_HEADS].set(ml_i_b.reshape(-1))
    bias = bias.at[_MF0:_MF0 + 2 * ML_HEADS].set(ml_f_b.reshape(-1))
    up = gla_up_w.reshape(2, GLA_LR, GLA_HEADS, GLA_DK).transpose(2, 0, 1, 3)
    ub = gla_up_b.reshape(2, GLA_HEADS, GLA_DK).transpose(1, 0, 2)
    wup = jnp.zeros((GLA_HEADS, 2, LANES, LANES), F32)
    bup = jnp.zeros((GLA_HEADS, 2, 1, LANES), F32)
    for hd in range(GLA_HEADS):
        lo = (hd % (LANES // GLA_DK)) * GLA_DK
        for dr in range(2):
            wup = wup.at[hd, dr, dr * GLA_LR:(dr + 1) * GLA_LR, lo:lo + GLA_DK].set(up[hd, dr])
        bup = bup.at[hd, :, 0, lo:lo + GLA_DK].set(ub[hd])
    return (wg.astype(BF16), wm.astype(BF16), ws.astype(BF16), ws.T.astype(BF16),
            bias.reshape(1, LANES), bias.reshape(LANES, 1), wup.astype(BF16), bup)


def _layer(x, ctx, mods, norm1_g, w_in, gla_up_w, gla_up_b, gla_norm_g, ml_conv_w, ml_conv_b,
           ml_i_b, ml_f_b, ml_norm_g, w_out, norm2_g, rg_w, rg_b, re_w, re_b, e_w_in, e_w_out, final_g):
    bsz, seq, d = x.shape
    n_tok = bsz * seq
    wg, wm, ws, wst, bcol, brow, wup, bup = _prep_inproj_weights(w_in, gla_up_w, gla_up_b, ml_i_b, ml_f_b)
    g1 = norm1_g.reshape(1, d)
    zg_x, zm_x, zs_x, gcol_x, grow_x = _inproj(x, mods, lambda b: b, g1, wg, wm, ws, wst, bcol, brow, 256)
    zg_c, zm_c, zs_c, gcol_c, grow_c = _inproj(ctx, mods, lambda b: bsz, g1, wg, wm, ws, wst, bcol, brow,
                                               min(256, ctx.shape[1]))
    gla_o = _gla(zg_x, zs_x, zg_c, zs_c, wup, bup, gla_norm_g.reshape(1, -1))
    ml_o = _mlstm(zm_x, gcol_x, grow_x, zm_c, grow_c,
                  ml_conv_w.reshape(9, -1), ml_conv_b.reshape(1, -1), ml_norm_g.reshape(1, -1))

    wr = jnp.zeros((d, LANES), F32).at[:, _G0:_E0].set(rg_w).at[:, _E0:_E0 + N_EXPERTS].set(re_w)
    br = jnp.zeros((1, LANES), F32).at[0, _G0:_E0].set(rg_b).at[0, _E0:_E0 + N_EXPERTS].set(re_b)
    wrh = wr.astype(BF16)
    wrl = (wr - wrh.astype(F32)).astype(BF16)
    x1, h2, ri, rw, cnt = _outproj(x, gla_o, ml_o, mods, w_out[:GLA_V_W].astype(BF16),
                                   w_out[GLA_V_W:].astype(BF16), norm2_g.reshape(1, d), wrh, wrl, br, 256)

    counts = cnt[0, :N_EXPERTS].astype(jnp.int32)
    nblk = (counts + MOE_BLK - 1) // MOE_BLK
    blk_end = jnp.cumsum(nblk)
    blk_start = blk_end - nblk
    n_used = blk_end[-1]
    nb_max = (2 * n_tok) // MOE_BLK + N_EXPERTS
    blk = jnp.arange(nb_max, dtype=jnp.int32)
    blk_c = jnp.minimum(blk, n_used - 1)
    onehot = (blk_c[:, None] >= blk_start[None, :]) & (blk_c[:, None] < blk_end[None, :])
    pick = lambda v: jnp.sum(jnp.where(onehot, v[None, :], 0), axis=1)
    block_e = pick(jnp.arange(N_EXPERTS, dtype=jnp.int32)).astype(jnp.int32)
    block_nv = jnp.clip(pick(counts) - (blk_c - pick(blk_start)) * MOE_BLK, 0, MOE_BLK)
    block_nv = jnp.where(blk < n_used, block_nv, 0).astype(jnp.int32)
    pad_start = (jnp.concatenate([blk_start, blk_end[-1:]]) * MOE_BLK).astype(jnp.int32)
    packed = ri[:, 0:2].reshape(-1)
    e_of = lax.shift_right_logical(packed, RANK_BITS)
    start_of = jnp.sum(jnp.where(e_of[:, None] == jnp.arange(N_EXPERTS, dtype=jnp.int32)[None, :],
                                 pad_start[None, :N_EXPERTS], 0), axis=1)
    dest = (start_of + (packed & (RANK_SPAN - 1))).astype(jnp.int32)
    meta = jnp.stack([n_used, n_used]).astype(jnp.int32)

    ytok = _experts(dest, pad_start, counts, block_e, block_nv, meta, h2, e_w_in, e_w_out, nb_max)
    out = _combine(x1.reshape(n_tok, d), ytok, rw, mods, final_g.reshape(1, d), seq, 1024)
    return out.reshape(bsz, seq, d)


def kernel(x, c, ctx, c_ctx, ada_w, ada_b, norm1_g, w_in, gla_up_w, gla_up_b, gla_norm_g, ml_conv_w, ml_conv_b,
           ml_i_b, ml_f_b, ml_norm_g, w_out, norm2_g, router_group_w, router_group_b, router_expert_w,
           router_expert_b, expert_w_in, expert_w_out, final_norm_g):
    assert ada_w.shape[0] == 1, "single-layer stack"
    bsz, d = c.shape
    cc = jnp.concatenate([c, c_ctx[None, :], jnp.zeros((8 - bsz - 1, d), F32)], axis=0)
    mods = _modulation(cc, ada_w[0], ada_b[0]).reshape(8, N_MOD, d)
    return _layer(x, ctx, mods, norm1_g[0], w_in[0], gla_up_w[0], gla_up_b[0], gla_norm_g[0],
                  ml_conv_w[0], ml_conv_b[0], ml_i_b[0], ml_f_b[0], ml_norm_g[0], w_out[0], norm2_g[0],
                  router_group_w[0], router_group_b[0], router_expert_w[0], router_expert_b[0],
                  expert_w_in[0], expert_w_out[0], final_norm_g)
```
